```python
import math
import jax, jax.numpy as jnp
from jax import lax
import numpy as np

D_MODEL = 4096
BATCH = 4
SEQ = 2048
DEPTH = 1
DEC_BATCH = 128
DEC_SEQ = 8
PAST_LEN = 16384
PAGE_SIZE = 128

RET_HEADS = 16
RET_HD = 128
RET_W = RET_HEADS * RET_HD
RET_IN = 4 * RET_W
RET_CHUNK = 128
ROPE_BASE = 10000.0
RWKV_HEADS = 32
RWKV_HD = 64
RWKV_W = RWKV_HEADS * RWKV_HD
W_LORA = 128
A_LORA = 128
G_LORA = 480
RWKV_IN = 3 * RWKV_W + W_LORA + A_LORA + G_LORA
IN_WIDTH = RET_IN + RWKV_IN
MIX_W = RET_W + RWKV_W
LNX_EPS = 64e-5
PEER_HEADS = 8
N_KEYS = 128
N_EXPERTS = N_KEYS * N_KEYS
PEER_QDIM = 256
PEER_KDIM = PEER_QDIM // 2
PEER_TOPK = 16
PEER_BLOCK = 64
NORM_EPS = 1e-6

kernel_name = 'hybrid_retention_rwkv7_peer_step'


def _rmsnorm(x, g):
    xf = x.astype(jnp.float32)
    return xf * lax.rsqrt(jnp.mean(xf * xf, axis=-1, keepdims=True) + NORM_EPS) * g.astype(jnp.float32)


def _rope(x, pos):
    half = x.shape[-1] // 2
    inv = ROPE_BASE ** (-jnp.arange(half, dtype=jnp.float32) / half)
    ang = pos.astype(jnp.float32)[:, None] * inv[None, :]
    cos = jnp.cos(ang)[None, :, None, :]
    sin = jnp.sin(ang)[None, :, None, :]
    x1, x2 = x[..., :half], x[..., half:]
    return jnp.concatenate([x1 * cos - x2 * sin, x1 * sin + x2 * cos], axis=-1)


def _retention(q, k, v, s0, log_gamma):
    B, T, H, D = q.shape
    C = RET_CHUNK if T % RET_CHUNK == 0 else T
    n = T // C
    idx = jnp.arange(C, dtype=jnp.float32)
    diff = idx[:, None] - idx[None, :]
    causal = diff >= 0
    mask = jnp.where(causal[None], jnp.exp(jnp.where(causal, diff, 0.0)[None] * log_gamma[:, None, None]), 0.0)
    q_dec = jnp.exp((idx[:, None] + 1.0) * log_gamma[None, :])
    k_dec = jnp.exp((C - 1.0 - idx[:, None]) * log_gamma[None, :])
    c_dec = jnp.exp(C * log_gamma)

    def to_chunks(t):
        return t.reshape(B, n, C, H, D).transpose(1, 0, 2, 3, 4)

    def step(S, inp):
        qc, kc, vc = inp
        att = jnp.einsum('bihd,bjhd->bhij', qc, kc) * mask[None]
        inner = jnp.einsum('bhij,bjhd->bihd', att, vc)
        cross = jnp.einsum('bihd,bhde->bihe', qc * q_dec[None, :, :, None], S)
        S = S * c_dec[None, :, None, None] + jnp.einsum('bjhd,bjhe->bhde', kc * k_dec[None, :, :, None], vc)
        return S, inner + cross

    S, out = lax.scan(step, s0, (to_chunks(q), to_chunks(k), to_chunks(v)))
    return out.transpose(1, 0, 2, 3, 4).reshape(B, T, H, D), S


def _rwkv7_scan(r, w, k, v, a, b, s0):
    def step(S, inp):
        rt, wt, kt, vt, at, bt = inp
        sa = jnp.einsum('bhvk,bhk->bhv', S, at)
        S = S * wt[:, :, None, :] + sa[..., :, None] * bt[..., None, :] + vt[..., :, None] * kt[..., None, :]
        return S, jnp.einsum('bhvk,bhk->bhv', S, rt)

    xs = tuple(t.transpose(1, 0, 2, 3) for t in (r, w, k, v, a, b))
    S, out = lax.scan(step, s0, xs)
    return out.transpose(1, 0, 2, 3), S


def _token_mix(h, s_ret, s_rwkv, s_shift, pos0, p):
    B, T, _ = h.shape
    proj = h @ p['w_in']

    pr = proj[..., :RET_IN].astype(jnp.float32)
    q, k, v, gate = jnp.split(pr, 4, axis=-1)

    def to_ret(t):
        return t.reshape(B, T, RET_HEADS, RET_HD)

    pos = pos0 + jnp.arange(T, dtype=jnp.int32)
    log_gamma = jnp.log1p(-jnp.exp2(-5.0 - jnp.arange(RET_HEADS, dtype=jnp.float32)))
    q = _rope(to_ret(q), pos)
    k = _rope(to_ret(k), pos) * (RET_HD ** -0.5)
    o_ret, s_ret_new = _retention(q, k, to_ret(v), s_ret.astype(jnp.float32), log_gamma)
    o_ret = o_ret * lax.rsqrt(jnp.mean(o_ret * o_ret, axis=-1, keepdims=True) + NORM_EPS)
    o_ret = o_ret.reshape(B, T, RET_W) * jax.nn.silu(gate)

    pw = proj[..., RET_IN:].astype(jnp.float32)
    prev = jnp.concatenate([s_shift[:, None, :].astype(jnp.float32), pw[:, :-1]], axis=1)
    pm = pw + p['shift_mu'] * (prev - pw)
    o1, o2, o3 = RWKV_W, 2 * RWKV_W, 3 * RWKV_W
    o4, o5 = o3 + W_LORA, o3 + W_LORA + A_LORA
    r, kw, vw = pm[..., :o1], pm[..., o1:o2], pm[..., o2:o3]
    wl, al, gl = pm[..., o3:o4], pm[..., o4:o5], pm[..., o5:]
    w_log = -jax.nn.softplus(-(p['w0'] + jnp.tanh(wl) @ p['w2'])) - 0.5
    decay = jnp.exp(-jnp.exp(w_log))
    a = jax.nn.sigmoid(p['a0'] + al @ p['a2'])
    g = jax.nn.sigmoid(gl) @ p['g2']

    def to_rw(t):
        return t.reshape(B, T, RWKV_HEADS, RWKV_HD)

    kk = to_rw(kw * p['k_k'])
    kk = kk / jnp.maximum(jnp.sqrt(jnp.sum(kk * kk, axis=-1, keepdims=True)), 1e-12)
    kw = kw * (1.0 + (a - 1.0) * p['k_a'])
    r4, k4, v4, a4 = to_rw(r), to_rw(kw), to_rw(vw), to_rw(a)
    o_rw, s_rwkv_new = _rwkv7_scan(r4, to_rw(decay), k4, v4, -kk, kk * a4, s_rwkv.astype(jnp.float32))
    mean = jnp.mean(o_rw, axis=-1, keepdims=True)
    var = jnp.mean(jnp.square(o_rw - mean), axis=-1, keepdims=True)
    o_rw = ((o_rw - mean) * lax.rsqrt(var + LNX_EPS)).reshape(B, T, RWKV_W) * p['lnx_g'] + p['lnx_b']
    bonus = (jnp.sum(r4 * k4 * p['r_k'], axis=-1, keepdims=True) * v4).reshape(B, T, RWKV_W)
    o_rw = (o_rw + bonus) * g

    mix = jnp.concatenate([o_ret, o_rw], axis=-1).astype(h.dtype) @ p['w_out']
    return mix, s_ret_new, s_rwkv_new, pw[:, -1]


def _peer(h, p):
    B, T, D = h.shape
    n = B * T
    pad = (-n) % PEER_BLOCK
    blocks = jnp.pad(h.reshape(n, D), ((0, pad), (0, 0))).reshape(-1, PEER_BLOCK, D)
    w_q, sub_keys, eu, ev = p['peer_wq'], p['peer_sub_keys'], p['peer_u'], p['peer_v']
    n_cand = PEER_TOPK * PEER_TOPK

    def block(hb):
        q = (hb @ w_q).astype(jnp.float32).reshape(PEER_BLOCK, PEER_HEADS, 2, PEER_KDIM)
        s = jnp.einsum('phcd,hcnd->phcn', q, sub_keys.astype(jnp.float32))
        top_s, top_i = lax.top_k(s, PEER_TOPK)
        cand_s = (top_s[:, :, 0, :, None] + top_s[:, :, 1, None, :]).reshape(PEER_BLOCK, PEER_HEADS, n_cand)
        cand_i = (top_i[:, :, 0, :, None] * N_KEYS + top_i[:, :, 1, None, :]).reshape(PEER_BLOCK, PEER_HEADS, n_cand)
        best_s, best_pos = lax.top_k(cand_s, PEER_TOPK)
        idx = jnp.take_along_axis(cand_i, best_pos, axis=-1)
        gates = jax.nn.softmax(best_s, axis=-1)
        act = jax.nn.gelu(jnp.einsum('pd,phkd->phk', hb, eu[idx]).astype(jnp.float32), approximate=False)
        coef = (gates * act).astype(ev.dtype)
        return jnp.einsum('phk,phkd->pd', coef, ev[idx])

    out = lax.map(block, blocks)
    return out.reshape(-1, D)[:n].reshape(B, T, D)


def _layer(x, c, s_ret, s_rwkv, s_shift, pos0, p):
    mod = jax.nn.silu(c.astype(jnp.float32)) @ p['ada_w'].astype(jnp.float32) + p['ada_b'].astype(jnp.float32)
    shift1, scale1, gate1, shift2, scale2, gate2 = jnp.split(mod[:, None, :], 6, axis=-1)
    h = (_rmsnorm(x, p['pre_mix_g']) * (1.0 + scale1) + shift1).astype(x.dtype)
    mix, s_ret_new, s_rwkv_new, s_shift_new = _token_mix(h, s_ret, s_rwkv, s_shift, pos0, p)
    x = (x.astype(jnp.float32) + gate1 * _rmsnorm(mix, p['post_mix_g'])).astype(x.dtype)
    h = (_rmsnorm(x, p['pre_ffn_g']) * (1.0 + scale2) + shift2).astype(x.dtype)
    x = (x.astype(jnp.float32) + gate2 * _rmsnorm(_peer(h, p), p['post_ffn_g'])).astype(x.dtype)
    return x, s_ret_new, s_rwkv_new, s_shift_new


def setup_inputs(seed: int = 0) -> dict:
    key = jax.random.key(seed)
    ks = jax.random.split(key, 32)
    f32 = jnp.float32

    def nrm(k, shape, scale):
        return jax.random.normal(k, shape, f32) * scale

    L = DEPTH
    return {
        'x_prompt': nrm(ks[0], (BATCH, SEQ, D_MODEL), 1.0),
        'x_sample': nrm(ks[1], (DEC_BATCH, DEC_SEQ, D_MODEL), 1.0),
        'state_ret': nrm(ks[2], (L, DEC_BATCH, RET_HEADS, RET_HD, RET_HD), 1.0),
        'state_rwkv': nrm(ks[3], (L, DEC_BATCH, RWKV_HEADS, RWKV_HD, RWKV_HD), 0.3),
        'state_shift': nrm(ks[4], (L, DEC_BATCH, RWKV_IN), 1.0),
        'c_prompt': nrm(ks[5], (BATCH, D_MODEL), 1.0),
        'c_sample': nrm(ks[6], (DEC_BATCH, D_MODEL), 1.0),
        'ada_w': nrm(ks[7], (L, D_MODEL, 6 * D_MODEL), 0.5 * D_MODEL ** -0.5),
        'ada_b': nrm(ks[8], (L, 6 * D_MODEL), 0.02),
        'pre_mix_g': 1.0 + nrm(ks[9], (L, D_MODEL), 0.1),
        'post_mix_g': 1.0 + nrm(ks[10], (L, D_MODEL), 0.1),
        'pre_ffn_g': 1.0 + nrm(ks[11], (L, D_MODEL), 0.1),
        'post_ffn_g': 1.0 + nrm(ks[12], (L, D_MODEL), 0.1),
        'w_in': nrm(ks[13], (L, D_MODEL, IN_WIDTH), D_MODEL ** -0.5),
        'shift_mu': jax.random.uniform(ks[14], (L, RWKV_IN), f32),
        'w0': jax.random.uniform(ks[15], (L, RWKV_W), f32, -6.0, -1.0),
        'w2': nrm(ks[16], (L, W_LORA, RWKV_W), 0.5 * W_LORA ** -0.5),
        'a0': nrm(ks[17], (L, RWKV_W), 0.1),
        'a2': nrm(ks[18], (L, A_LORA, RWKV_W), A_LORA ** -0.5),
        'g2': nrm(ks[19], (L, G_LORA, RWKV_W), G_LORA ** -0.5),
        'k_k': 0.85 + nrm(ks[20], (L, RWKV_W), 0.05),
        'k_a': 1.0 + nrm(ks[21], (L, RWKV_W), 0.05),
        'r_k': nrm(ks[22], (L, RWKV_HEADS, RWKV_HD), 0.1),
        'lnx_g': 1.0 + nrm(ks[23], (L, RWKV_W), 0.1),
        'lnx_b': nrm(ks[24], (L, RWKV_W), 0.02),
        'w_out': nrm(ks[25], (L, MIX_W, D_MODEL), MIX_W ** -0.5),
        'peer_wq': nrm(ks[26], (L, D_MODEL, PEER_HEADS * PEER_QDIM), D_MODEL ** -0.5),
        'peer_sub_keys': nrm(ks[27], (L, PEER_HEADS, 2, N_KEYS, PEER_KDIM), PEER_KDIM ** -0.5),
        'peer_u': nrm(ks[28], (L, N_EXPERTS, D_MODEL), D_MODEL ** -0.5),
        'peer_v': nrm(ks[29], (L, N_EXPERTS, D_MODEL), (PEER_HEADS * PEER_TOPK) ** -0.5),
    }


def reference(x_prompt, x_sample, state_ret, state_rwkv, state_shift, c_prompt, c_sample,
              ada_w, ada_b, pre_mix_g, post_mix_g, pre_ffn_g, post_ffn_g, w_in, shift_mu,
              w0, w2, a0, a2, g2, k_k, k_a, r_k, lnx_g, lnx_b, w_out,
              peer_wq, peer_sub_keys, peer_u, peer_v):
    yp, ys = x_prompt, x_sample
    nb = x_prompt.shape[0]
    ret_p, ret_s, rw_p, rw_s, sh_p, sh_s = [], [], [], [], [], []
    for l in range(DEPTH):
        prm = dict(ada_w=ada_w[l], ada_b=ada_b[l], pre_mix_g=pre_mix_g[l], post_mix_g=post_mix_g[l],
                   pre_ffn_g=pre_ffn_g[l], post_ffn_g=post_ffn_g[l], w_in=w_in[l], shift_mu=shift_mu[l],
                   w0=w0[l], w2=w2[l], a0=a0[l], a2=a2[l], g2=g2[l], k_k=k_k[l], k_a=k_a[l], r_k=r_k[l],
                   lnx_g=lnx_g[l], lnx_b=lnx_b[l], w_out=w_out[l], peer_wq=peer_wq[l],
                   peer_sub_keys=peer_sub_keys[l], peer_u=peer_u[l], peer_v=peer_v[l])
        z_ret = jnp.zeros((nb, RET_HEADS, RET_HD, RET_HD), jnp.float32)
        z_rw = jnp.zeros((nb, RWKV_HEADS, RWKV_HD, RWKV_HD), jnp.float32)
        z_sh = jnp.zeros((nb, RWKV_IN), jnp.float32)
        yp, sr_p, sw_p, ss_p = _layer(yp, c_prompt, z_ret, z_rw, z_sh, 0, prm)
        ys, sr_s, sw_s, ss_s = _layer(ys, c_sample, state_ret[l], state_rwkv[l], state_shift[l], PAST_LEN, prm)
        ret_p.append(sr_p)
        ret_s.append(sr_s)
        rw_p.append(sw_p)
        rw_s.append(sw_s)
        sh_p.append(ss_p)
        sh_s.append(ss_s)
    return (yp, ys,
            jnp.stack(ret_p).astype(state_ret.dtype), jnp.stack(ret_s).astype(state_ret.dtype),
            jnp.stack(rw_p).astype(state_rwkv.dtype), jnp.stack(rw_s).astype(state_rwkv.dtype),
            jnp.stack(sh_p).astype(state_shift.dtype), jnp.stack(sh_s).astype(state_shift.dtype))
```

```python
import functools
import math

import numpy as np
import jax
import jax.numpy as jnp
from jax import lax
from jax.experimental import pallas as pl
from jax.experimental.pallas import tpu as pltpu

PAST_LEN = 16384
RET_CHUNK = 128
ROPE_BASE = 10000.0
LNX_EPS = 64e-5
NORM_EPS = 1e-6
PEER_TOPK = 16

LANES = 128
ROW_BLOCK = 128
VMEM_LIMIT = 56 * 1024 * 1024

HI = lax.Precision.HIGHEST
F32 = jnp.float32
BF16 = jnp.bfloat16


def _pick(n, target, mult=LANES):
    best = None
    for c in range(mult, min(n, target) + 1, mult):
        if n % c == 0:
            best = c
    assert best is not None, (n, target, mult)
    return best


def _params(sem, vmem=VMEM_LIMIT):
    return pltpu.CompilerParams(dimension_semantics=sem, vmem_limit_bytes=vmem)


def _dot(a, b, precision=None):
    return jnp.dot(a, b, preferred_element_type=F32, precision=precision)


def _dot_nt(a, b, precision=None):
    return lax.dot_general(a, b, (((1,), (1,)), ((), ())), preferred_element_type=F32, precision=precision)


def _dot_tn(a, b, precision=None):
    return lax.dot_general(a, b, (((0,), (0,)), ((), ())), preferred_element_type=F32, precision=precision)


def _block_ones(n, seg, dtype):
    r = lax.broadcasted_iota(jnp.int32, (n, n), 0) // seg
    c = lax.broadcasted_iota(jnp.int32, (n, n), 1) // seg
    return (r == c).astype(dtype)


def _ada_kernel(c_ref, w_ref, b_ref, o_ref):
    c = c_ref[...]
    sc = (c * jax.nn.sigmoid(c)).astype(BF16)
    o_ref[...] = _dot(sc, w_ref[...].astype(BF16)) + b_ref[...]


def _ada(c_all, ada_w, ada_b, tn=512):
    m, d = c_all.shape
    n = ada_w.shape[1]
    return pl.pallas_call(
        _ada_kernel,
        grid=(n // tn,),
        in_specs=[pl.BlockSpec((m, d), lambda j: (0, 0)),
                  pl.BlockSpec((d, tn), lambda j: (0, j)),
                  pl.BlockSpec((1, tn), lambda j: (0, j))],
        out_specs=pl.BlockSpec((m, tn), lambda j: (0, j)),
        out_shape=jax.ShapeDtypeStruct((m, n), F32),
        compiler_params=_params(("arbitrary",)),
        name="ada",
    )(c_all, ada_w, ada_b.reshape(1, n))


def _expand_slots(m, rows):
    slots, _, d = m.shape
    return jnp.broadcast_to(m, (slots, rows // slots, d)).reshape(rows, d)


def _rms(x, g):
    return x * lax.rsqrt(jnp.mean(x * x, axis=-1, keepdims=True) + NORM_EPS) * g


def _prenorm_kernel(x_ref, g_ref, sh_ref, sc_ref, h_ref):
    x = x_ref[...]
    rows = x.shape[0]
    y = _rms(x, g_ref[...])
    h_ref[...] = (y * (1.0 + _expand_slots(sc_ref[...], rows)) + _expand_slots(sh_ref[...], rows)).astype(h_ref.dtype)


def _postmix_kernel(x_ref, m_ref, gpost_ref, gpre_ref, gate_ref, sh_ref, sc_ref, x1_ref, h_ref):
    x = x_ref[...]
    rows = x.shape[0]
    x1 = x + _expand_slots(gate_ref[...], rows) * _rms(m_ref[...], gpost_ref[...])
    x1_ref[...] = x1
    y = _rms(x1, gpre_ref[...])
    h_ref[...] = (y * (1.0 + _expand_slots(sc_ref[...], rows)) + _expand_slots(sh_ref[...], rows)).astype(h_ref.dtype)


def _final_kernel(x_ref, m_ref, gpost_ref, gate_ref, y_ref):
    x = x_ref[...]
    rows = x.shape[0]
    y_ref[...] = x + _expand_slots(gate_ref[...], rows) * _rms(m_ref[...], gpost_ref[...])


class _Rows:
    def __init__(self, nb, t, db, dt):
        self.n_prompt = nb * t
        self.n_sample = db * dt
        self.n = self.n_prompt + self.n_sample
        self.rb = ROW_BLOCK
        assert t % self.rb == 0 and self.rb % dt == 0 and self.n_sample % self.rb == 0
        self.slots = self.rb // dt
        self.prompt_blocks = self.n_prompt // self.rb
        self.blocks_per_seq = t // self.rb
        self.n_blocks = self.n // self.rb
        self.nb, self.t, self.db, self.dt = nb, t, db, dt

    def slot_block(self, i):
        return jnp.where(i < self.prompt_blocks, i // self.blocks_per_seq, i - self.prompt_blocks + self.nb)

    def extend(self, m):
        mp = jnp.repeat(m[: self.nb], self.slots, axis=0)
        return jnp.concatenate([mp, m[self.nb:]], axis=0)[:, None, :]


def _mod_spec(rows, d, chunk):
    return pl.BlockSpec((rows.slots, 1, d), lambda i: (rows.slot_block(i), 0, chunk))


def _prenorm(rows, x, g, mod_ext, d):
    row = pl.BlockSpec((rows.rb, d), lambda i: (i, 0))
    vec = pl.BlockSpec((1, d), lambda i: (0, 0))
    return pl.pallas_call(
        _prenorm_kernel,
        grid=(rows.n_blocks,),
        in_specs=[row, vec, _mod_spec(rows, d, 0), _mod_spec(rows, d, 1)],
        out_specs=row,
        out_shape=jax.ShapeDtypeStruct((rows.n, d), BF16),
        compiler_params=_params(("parallel",)),
        name="prenorm",
    )(x, g.reshape(1, d), mod_ext, mod_ext)


def _postmix(rows, x, mix, gpost, gpre, mod_ext, d):
    row = pl.BlockSpec((rows.rb, d), lambda i: (i, 0))
    vec = pl.BlockSpec((1, d), lambda i: (0, 0))
    return pl.pallas_call(
        _postmix_kernel,
        grid=(rows.n_blocks,),
        in_specs=[row, row, vec, vec, _mod_spec(rows, d, 2), _mod_spec(rows, d, 3), _mod_spec(rows, d, 4)],
        out_specs=[row, row],
        out_shape=[jax.ShapeDtypeStruct((rows.n, d), F32), jax.ShapeDtypeStruct((rows.n, d), BF16)],
        compiler_params=_params(("parallel",)),
        name="postmix",
    )(x, mix, gpost.reshape(1, d), gpre.reshape(1, d), mod_ext, mod_ext, mod_ext)


def _final(rows, x1, peer, gpost, mod_ext, d):
    row = pl.BlockSpec((rows.rb, d), lambda i: (i, 0))
    vec = pl.BlockSpec((1, d), lambda i: (0, 0))
    return pl.pallas_call(
        _final_kernel,
        grid=(rows.n_blocks,),
        in_specs=[row, row, vec, _mod_spec(rows, d, 5)],
        out_specs=row,
        out_shape=jax.ShapeDtypeStruct((rows.n, d), F32),
        compiler_params=_params(("parallel",)),
        name="final",
    )(x1, peer, gpost.reshape(1, d), mod_ext)


def _mm_kernel(a_ref, w_ref, o_ref):
    o_ref[...] = _dot(a_ref[...], w_ref[...].astype(BF16)).astype(o_ref.dtype)


def _matmul(a, w, *, col_block0, n_out, tm, tn, name):
    m, k = a.shape
    return pl.pallas_call(
        _mm_kernel,
        grid=(pl.cdiv(n_out, tn), m // tm),
        in_specs=[pl.BlockSpec((tm, k), lambda j, i: (i, 0)),
                  pl.BlockSpec((k, tn), lambda j, i: (0, j + col_block0))],
        out_specs=pl.BlockSpec((tm, tn), lambda j, i: (i, j)),
        out_shape=jax.ShapeDtypeStruct((m, n_out), F32),
        compiler_params=_params(("parallel", "parallel")),
        name=name,
    )(a, w)


def _mm2_kernel(a_ref, b_ref, wa_ref, wb_ref, o_ref):
    o_ref[...] = _dot(a_ref[...], wa_ref[...].astype(BF16)) + _dot(b_ref[...], wb_ref[...].astype(BF16))


def _matmul_out(o_ret, o_rw, w_out, tn=512):
    m, ka = o_ret.shape
    tm = _pick(m, 1024)
    kb = o_rw.shape[1]
    n = w_out.shape[1]
    tn = _pick(n, tn)
    assert ka == kb
    return pl.pallas_call(
        _mm2_kernel,
        grid=(n // tn, m // tm),
        in_specs=[pl.BlockSpec((tm, ka), lambda j, i: (i, 0)),
                  pl.BlockSpec((tm, kb), lambda j, i: (i, 0)),
                  pl.BlockSpec((ka, tn), lambda j, i: (0, j)),
                  pl.BlockSpec((kb, tn), lambda j, i: (1, j))],
        out_specs=pl.BlockSpec((tm, tn), lambda j, i: (i, j)),
        out_shape=jax.ShapeDtypeStruct((m, n), F32),
        compiler_params=_params(("parallel", "parallel")),
        name="out_proj",
    )(o_ret, o_rw, w_out, w_out)


def _rope_tables(pos, hd):
    half = hd // 2
    inv = ROPE_BASE ** (-jnp.arange(half, dtype=F32) / half)
    ang = pos.astype(F32)[:, None] * inv[None, :]
    cos, sin = jnp.cos(ang), jnp.sin(ang)
    return jnp.concatenate([cos, cos], axis=-1), jnp.concatenate([-sin, sin], axis=-1)


def _rope(x, cos, sin_signed):
    half = x.shape[-1] // 2
    return x * cos + pltpu.roll(x, half, 1) * sin_signed


def _ret_finish(o, gate):
    o = o * lax.rsqrt(jnp.mean(o * o, axis=-1, keepdims=True) + NORM_EPS)
    return o * (gate * jax.nn.sigmoid(gate))


def _ret_prompt_kernel(q_ref, k_ref, v_ref, g_ref, cos_ref, sin_ref, lg_ref, o_ref, s_ref, *, chunk):
    t, hd = q_ref.shape
    lg = lg_ref[0, 0:1, :]
    ri = lax.broadcasted_iota(jnp.int32, (chunk, chunk), 0)
    ci = lax.broadcasted_iota(jnp.int32, (chunk, chunk), 1)
    causal = ri >= ci
    diff = jnp.where(causal, ri - ci, 0).astype(F32)
    mask = jnp.where(causal, jnp.exp(diff * lg), 0.0)
    rowf = lax.broadcasted_iota(jnp.int32, (chunk, hd), 0).astype(F32)
    q_dec = jnp.exp((rowf + 1.0) * lg)
    k_dec = jnp.exp((chunk - 1.0 - rowf) * lg)
    c_dec = jnp.exp(float(chunk) * lg)

    def step(c, s):
        r0 = pl.multiple_of(c * chunk, chunk)
        cos = cos_ref[pl.ds(r0, chunk), :]
        sin = sin_ref[pl.ds(r0, chunk), :]
        qc = _rope(q_ref[pl.ds(r0, chunk), :], cos, sin)
        kc = _rope(k_ref[pl.ds(r0, chunk), :], cos, sin) * (hd ** -0.5)
        vc = v_ref[pl.ds(r0, chunk), :]
        att = _dot_nt(qc, kc, HI) * mask
        o = _dot(att, vc, HI) + _dot(qc * q_dec, s, HI)
        s = s * c_dec + _dot_tn(kc * k_dec, vc, HI)
        o_ref[pl.ds(r0, chunk), :] = _ret_finish(o, g_ref[pl.ds(r0, chunk), :]).astype(o_ref.dtype)
        return s

    s_ref[0, 0] = lax.fori_loop(0, t // chunk, step, jnp.zeros((hd, hd), F32))


def _ret_sample_kernel(q_ref, k_ref, v_ref, g_ref, cos_ref, sin_ref, lg_ref, s0_ref, o_ref, s_ref, *, dt):
    rows, hd = q_ref.shape
    nseq = rows // dt
    lg = lg_ref[0, 0:1, :]
    ri = lax.broadcasted_iota(jnp.int32, (rows, rows), 0)
    ci = lax.broadcasted_iota(jnp.int32, (rows, rows), 1)
    ok = (ri >= ci) & ((ri // dt) == (ci // dt))
    diff = jnp.where(ok, ri - ci, 0).astype(F32)
    mask = jnp.where(ok, jnp.exp(diff * lg), 0.0)
    rown = lax.broadcasted_iota(jnp.int32, (rows, hd), 0)
    posf = (rown % dt).astype(F32)
    q_dec = jnp.exp((posf + 1.0) * lg)
    k_dec = jnp.exp((dt - 1.0 - posf) * lg)
    c_dec = jnp.exp(float(dt) * lg)
    cos, sin = cos_ref[...], sin_ref[...]
    qc = _rope(q_ref[...], cos, sin)
    kc = _rope(k_ref[...], cos, sin) * (hd ** -0.5)
    vc = v_ref[...]
    inner = _dot(_dot_nt(qc, kc, HI) * mask, vc, HI)
    qd = qc * q_dec
    kd = kc * k_dec
    seq = rown // dt
    cross = jnp.zeros((rows, hd), F32)
    for s in range(nseq):
        s0 = s0_ref[s, 0]
        mine = seq == s
        cross = cross + _dot(jnp.where(mine, qd, 0.0), s0, HI)
        s_ref[s, 0] = s0 * c_dec + _dot_tn(jnp.where(mine, kd, 0.0), vc, HI)
    o_ref[...] = _ret_finish(inner + cross, g_ref[...]).astype(o_ref.dtype)


def _retention(rows, proj, state_ret, ret_heads, hd):
    nb, t, db, dt = rows.nb, rows.t, rows.db, rows.dt
    lg = jnp.log1p(-jnp.exp2(-5.0 - jnp.arange(ret_heads, dtype=F32)))
    lg_tab = jnp.broadcast_to(lg[:, None, None], (ret_heads, 8, hd))
    lg_spec = pl.BlockSpec((1, 8, hd), lambda b, h: (h, 0, 0))
    assert hd == RET_CHUNK and t % RET_CHUNK == 0 and dt % RET_CHUNK != 0

    cos_p, sin_p = _rope_tables(jnp.arange(t, dtype=jnp.int32), hd)
    col = lambda part: pl.BlockSpec((t, hd), lambda b, h: (b, h + part * ret_heads))
    tab = pl.BlockSpec((t, hd), lambda b, h: (0, 0))
    o_p, s_p = pl.pallas_call(
        functools.partial(_ret_prompt_kernel, chunk=RET_CHUNK),
        grid=(nb, ret_heads),
        in_specs=[col(0), col(1), col(2), col(3), tab, tab, lg_spec],
        out_specs=[pl.BlockSpec((t, hd), lambda b, h: (b, h)),
                   pl.BlockSpec((1, 1, hd, hd), lambda b, h: (b, h, 0, 0))],
        out_shape=[jax.ShapeDtypeStruct((nb * t, ret_heads * hd), BF16),
                   jax.ShapeDtypeStruct((nb, ret_heads, hd, hd), F32)],
        compiler_params=_params(("parallel", "parallel")),
        name="ret_prompt",
    )(proj, proj, proj, proj, cos_p, sin_p, lg_tab)

    rb = RET_CHUNK
    nseq = rb // dt
    cos_s, sin_s = _rope_tables(PAST_LEN + jnp.arange(dt, dtype=jnp.int32), hd)
    cos_s, sin_s = jnp.tile(cos_s, (nseq, 1)), jnp.tile(sin_s, (nseq, 1))
    off = (nb * t) // rb
    col = lambda part: pl.BlockSpec((rb, hd), lambda b, h: (b + off, h + part * ret_heads))
    tab = pl.BlockSpec((rb, hd), lambda b, h: (0, 0))
    st = pl.BlockSpec((nseq, 1, hd, hd), lambda b, h: (b, h, 0, 0))
    o_s, s_s = pl.pallas_call(
        functools.partial(_ret_sample_kernel, dt=dt),
        grid=(db // nseq, ret_heads),
        in_specs=[col(0), col(1), col(2), col(3), tab, tab, lg_spec, st],
        out_specs=[pl.BlockSpec((rb, hd), lambda b, h: (b, h)), st],
        out_shape=[jax.ShapeDtypeStruct((db * dt, ret_heads * hd), BF16),
                   jax.ShapeDtypeStruct((db, ret_heads, hd, hd), F32)],
        compiler_params=_params(("parallel", "parallel")),
        name="ret_sample",
    )(proj, proj, proj, proj, cos_s, sin_s, lg_tab, state_ret)
    return jnp.concatenate([o_p, o_s], axis=0), s_p, s_s


def _shifted(x, first, tseq_mask):
    prev = pltpu.roll(x, 1, 0)
    return jnp.where(tseq_mask, first, prev)


def _rwkv_prep_kernel(pr_ref, pk_ref, pv_ref, pt_ref, fr_ref, fk_ref, fv_ref, ft_ref,
                      mur_ref, muk_ref, muv_ref, mut_ref, w0_ref, w2_ref, a0_ref, a2_ref, g2_ref,
                      kk_ref, ka_ref, rk_ref,
                      r_out, w_out, k_out, v_out, a_out, b_out, g_out, bonus_out,
                      *, dt, prompt_blocks, w_lora, a_lora, g_lora, hd):
    i = pl.program_id(0)
    rows, cn = pr_ref.shape
    rown = lax.broadcasted_iota(jnp.int32, (rows, 1), 0)
    sample_block = jnp.zeros((rows, 1), jnp.int32) + (i >= prompt_blocks).astype(jnp.int32)
    is_first = (rown == 0) | ((rown % dt == 0) & (sample_block == 1))

    def lerp(p_ref, f_ref, mu_ref):
        p = p_ref[...]
        prev = _shifted(p, _expand_slots(f_ref[...], rows), is_first)
        return p + mu_ref[...] * (prev - p)

    r = lerp(pr_ref, fr_ref, mur_ref)
    kw = lerp(pk_ref, fk_ref, muk_ref)
    vw = lerp(pv_ref, fv_ref, muv_ref)
    tail = lerp(pt_ref, ft_ref, mut_ref)
    tw = tail.shape[1]
    wl = tail[:, :w_lora]
    al = tail[:, w_lora:w_lora + a_lora]
    glane = lax.broadcasted_iota(jnp.int32, (rows, tw - w_lora - a_lora), 1)
    gl = jnp.where(glane < g_lora, jax.nn.sigmoid(tail[:, w_lora + a_lora:]), 0.0)

    w_log = -jax.nn.softplus(-(w0_ref[...] + _dot(jnp.tanh(wl), w2_ref[...], HI))) - 0.5
    decay = jnp.exp(-jnp.exp(w_log))
    a = jax.nn.sigmoid(a0_ref[...] + _dot(al, a2_ref[...], HI))
    g = _dot(gl, g2_ref[...], HI)

    seg = _block_ones(LANES, hd, F32)

    def segsum(x):
        return jnp.concatenate([_dot(x[:, c:c + LANES], seg, HI) for c in range(0, cn, LANES)], axis=1)

    kk = kw * kk_ref[...]
    kk = kk / jnp.maximum(jnp.sqrt(segsum(kk * kk)), 1e-12)
    kn = kw * (1.0 + (a - 1.0) * ka_ref[...])
    r_out[...] = r
    w_out[...] = decay
    k_out[...] = kn
    v_out[...] = vw
    a_out[...] = -kk
    b_out[...] = kk * a
    g_out[...] = g
    bonus_out[...] = segsum(r * kn * rk_ref[...]) * vw


def _rwkv_prep(rows, proj_rkv, proj_tail, first_rkv, first_tail, mu_rkv, mu_tail, w0, w2, a0, a2, g2p,
               k_k, k_a, r_k_flat, rw, hd, g_lora, cn=512):
    n = rows.n
    cn = _pick(rw, cn)
    tw = proj_tail.shape[1]
    nblk = (rows.n_blocks, rw // cn)
    part = lambda p: pl.BlockSpec((rows.rb, cn), lambda i, j: (i, j + p * (rw // cn)))
    fpart = lambda p: pl.BlockSpec((rows.slots, 1, cn), lambda i, j: (i, 0, j + p * (rw // cn)))
    mupart = lambda p: pl.BlockSpec((1, cn), lambda i, j: (0, j + p * (rw // cn)))
    vec = pl.BlockSpec((1, cn), lambda i, j: (0, j))
    lora = lambda k: pl.BlockSpec((k, cn), lambda i, j: (0, j))
    out = pl.BlockSpec((rows.rb, cn), lambda i, j: (i, j))
    kern = functools.partial(_rwkv_prep_kernel, dt=rows.dt, prompt_blocks=rows.prompt_blocks,
                             w_lora=w2.shape[0], a_lora=a2.shape[0], g_lora=g_lora, hd=hd)
    return pl.pallas_call(
        kern,
        grid=nblk,
        in_specs=[part(0), part(1), part(2), pl.BlockSpec((rows.rb, tw), lambda i, j: (i, 0)),
                  fpart(0), fpart(1), fpart(2), pl.BlockSpec((rows.slots, 1, tw), lambda i, j: (i, 0, 0)),
                  mupart(0), mupart(1), mupart(2), pl.BlockSpec((1, tw), lambda i, j: (0, 0)),
                  vec, lora(w2.shape[0]), vec, lora(a2.shape[0]), lora(g2p.shape[0]),
                  vec, vec, vec],
        out_specs=[out] * 8,
        out_shape=[jax.ShapeDtypeStruct((n, rw), F32)] * 8,
        compiler_params=_params(("parallel", "arbitrary")),
        name="rwkv_prep",
    )(proj_rkv, proj_rkv, proj_rkv, proj_tail, first_rkv, first_rkv, first_rkv, first_tail,
      mu_rkv, mu_rkv, mu_rkv, mu_tail, w0, w2, a0, a2, g2p, k_k, k_a, r_k_flat)


def _rwkv_scan_kernel(r_ref, w_ref, k_ref, v_ref, a_ref, b_ref, *rest, hd, has_state):
    if has_state:
        s0_ref, o_ref, sout_ref, st_ref, ob_ref = rest
    else:
        o_ref, sout_ref, st_ref, ob_ref = rest
        s0_ref = None
    tb = pl.program_id(1)
    tc, width = r_ref.shape
    pw = 2 * hd
    npairs = width // pw

    @pl.when(tb == 0)
    def _():
        for p in range(npairs):
            if has_state:
                st_ref[p] = jnp.concatenate([s0_ref[0, 2 * p], s0_ref[0, 2 * p + 1]], axis=1)
            else:
                st_ref[p] = jnp.zeros((hd, pw), F32)

    ones_bf = _block_ones(pw, hd, BF16)
    ones_f = _block_ones(pw, hd, F32)
    sub = lax.broadcasted_iota(jnp.int32, (hd, pw), 0)
    lane = lax.broadcasted_iota(jnp.int32, (hd, pw), 1)
    diag = (lane % hd) == sub
    row8 = lax.broadcasted_iota(jnp.int32, (8, pw), 0)
    lane8 = lax.broadcasted_iota(jnp.int32, (8, pw), 1)
    sel = (row8 == (lane8 // hd))

    def group(tg, carry):
        r0 = pl.multiple_of(tg * 8, 8)
        for p in range(npairs):
            cols = pl.ds(p * pw, pw)
            a8, w8, k8 = a_ref[pl.ds(r0, 8), cols], w_ref[pl.ds(r0, 8), cols], k_ref[pl.ds(r0, 8), cols]
            v8, b8, r8 = v_ref[pl.ds(r0, 8), cols], b_ref[pl.ds(r0, 8), cols], r_ref[pl.ds(r0, 8), cols]
            s = st_ref[p]
            for j in range(8):
                row = slice(j, j + 1)
                sa = _dot((s * a8[row]).astype(BF16), ones_bf)
                vcol = _dot(jnp.where(diag, v8[row], 0.0), ones_f, HI)
                s = s * w8[row] + sa * b8[row] + vcol * k8[row]
                rsel = jnp.where(sel, r8[row], 0.0).astype(BF16)
                o8 = _dot_nt(rsel, s.astype(BF16))
                ob_ref[p, row, 0:hd] = o8[0:1, :]
                ob_ref[p, row, hd:pw] = o8[1:2, :]
            st_ref[p] = s
            o_ref[pl.ds(r0, 8), cols] = ob_ref[p]
        return carry

    lax.fori_loop(0, tc // 8, group, 0)

    @pl.when(tb == pl.num_programs(1) - 1)
    def _():
        for p in range(npairs):
            s = st_ref[p]
            sout_ref[0, 2 * p] = s[:, :hd]
            sout_ref[0, 2 * p + 1] = s[:, hd:]


def _rwkv_scan(ops, nseq, t, row0, tc, heads, hd, state):
    width = heads * hd
    assert t % tc == 0 and row0 % tc == 0
    nt = t // tc
    blk = pl.BlockSpec((tc, width), lambda s, j: (row0 // tc + s * nt + j, 0))
    st = pl.BlockSpec((1, heads, hd, hd), lambda s, j: (s, 0, 0, 0))
    in_specs = [blk] * 6
    args = list(ops)
    if state is not None:
        in_specs.append(st)
        args.append(state)
    return pl.pallas_call(
        functools.partial(_rwkv_scan_kernel, hd=hd, has_state=state is not None),
        grid=(nseq, nt),
        in_specs=in_specs,
        out_specs=[pl.BlockSpec((tc, width), lambda s, j: (s * nt + j, 0)), st],
        out_shape=[jax.ShapeDtypeStruct((nseq * t, width), F32),
                   jax.ShapeDtypeStruct((nseq, heads, hd, hd), F32)],
        scratch_shapes=[pltpu.VMEM((heads // 2, hd, 2 * hd), F32), pltpu.VMEM((heads // 2, 8, 2 * hd), F32)],
        compiler_params=_params(("parallel", "arbitrary")),
        name="rwkv_scan_state" if state is not None else "rwkv_scan",
    )(*args)


def _rwkv_post_kernel(o_ref, bonus_ref, g_ref, lg_ref, lb_ref, out_ref, *, hd):
    o = o_ref[...]
    cn = o.shape[1]
    seg = _block_ones(LANES, hd, F32)

    def segmean(x):
        return jnp.concatenate([_dot(x[:, c:c + LANES], seg, HI) for c in range(0, cn, LANES)], axis=1) * (1.0 / hd)

    mean = segmean(o)
    cen = o - mean
    var = segmean(cen * cen)
    y = cen * lax.rsqrt(var + LNX_EPS) * lg_ref[...] + lb_ref[...]
    out_ref[...] = ((y + bonus_ref[...]) * g_ref[...]).astype(out_ref.dtype)


def _rwkv_post(o_raw, bonus, g, lnx_g, lnx_b, hd, rb=256, cn=512):
    n, rw = o_raw.shape
    rb, cn = _pick(n, rb), _pick(rw, cn)
    blk = pl.BlockSpec((rb, cn), lambda i, j: (i, j))
    vec = pl.BlockSpec((1, cn), lambda i, j: (0, j))
    return pl.pallas_call(
        functools.partial(_rwkv_post_kernel, hd=hd),
        grid=(n // rb, rw // cn),
        in_specs=[blk, blk, blk, vec, vec],
        out_specs=blk,
        out_shape=jax.ShapeDtypeStruct((n, rw), BF16),
        compiler_params=_params(("parallel", "parallel")),
        name="rwkv_post",
    )(o_raw, bonus, g, lnx_g, lnx_b)


def _take_top(s, k):
    n = s.shape[1]
    lane = lax.broadcasted_iota(jnp.int32, s.shape, 1).astype(F32)
    out = []
    for _ in range(k):
        m = jnp.max(s, axis=1, keepdims=True)
        first = jnp.min(jnp.where(s == m, lane, float(n)), axis=1, keepdims=True)
        s = jnp.where(lane == first, -jnp.inf, s)
        out.append(m)
    return out


def _peer_topk_kernel(q_ref, keys_ref, s0_ref, s1_ref, e0_ref, e1_ref, thr_ref, *, topk):
    rows = q_ref.shape[0]
    nk = keys_ref.shape[2]
    kd = keys_ref.shape[3]
    s0 = _dot_nt(q_ref[:, :kd], keys_ref[0, 0], HI)
    s1 = _dot_nt(q_ref[:, kd:], keys_ref[0, 1], HI)
    top0 = _take_top(s0, topk)
    top1 = _take_top(s1, topk)
    lane = lax.broadcasted_iota(jnp.int32, (rows, topk * topk), 1)
    c0 = jnp.zeros((rows, topk * topk), F32)
    c1 = jnp.zeros((rows, topk * topk), F32)
    for a in range(topk):
        c0 = jnp.where(lane // topk == a, top0[a], c0)
        c1 = jnp.where(lane % topk == a, top1[a], c1)
    best = _take_top(c0 + c1, topk)
    z = jnp.zeros((rows, 1), F32)
    for b in best:
        z = z + jnp.exp(b - best[0])
    s0_ref[...] = s0
    s1_ref[...] = s1
    e0_ref[...] = jnp.exp(s0 - top0[0]) / z
    e1_ref[...] = jnp.exp(s1 - top1[0])
    thr_ref[0] = jnp.broadcast_to(best[topk - 1], (rows, thr_ref.shape[2]))


def _peer_topk(q, sub_keys, rb=256):
    n = q.shape[0]
    rb = _pick(n, rb)
    heads, _, nk, kd = sub_keys.shape
    blk = pl.BlockSpec((rb, nk), lambda i, h: (i, h))
    shp = jax.ShapeDtypeStruct((n, heads * nk), F32)
    return pl.pallas_call(
        functools.partial(_peer_topk_kernel, topk=PEER_TOPK),
        grid=(n // rb, heads),
        in_specs=[pl.BlockSpec((rb, 2 * kd), lambda i, h: (i, h)),
                  pl.BlockSpec((1, 2, nk, kd), lambda i, h: (h, 0, 0, 0))],
        out_specs=[blk] * 4 + [pl.BlockSpec((1, rb, 8), lambda i, h: (h, i, 0))],
        out_shape=[shp] * 4 + [jax.ShapeDtypeStruct((heads, n, 8), F32)],
        compiler_params=_params(("parallel", "parallel")),
        name="peer_topk",
    )(q, sub_keys)


def _gelu(x):
    return 0.5 * x * (1.0 + lax.erf(x * (2.0 ** -0.5)))


def _peer_dense_kernel(h_ref, ut_ref, v_ref, s0_ref, e0_ref, s1_ref, e1_ref, thr_ref, o_ref, *, heads, nk):
    e = pl.program_id(1)
    eb = ut_ref.shape[1]
    ni = eb // nk

    @pl.when(e == 0)
    def _():
        o_ref[...] = jnp.zeros_like(o_ref)

    act = _gelu(_dot(h_ref[...], ut_ref[...]))
    s0 = s0_ref[0]
    e0 = e0_ref[0]
    thr = thr_ref[...]
    parts = []
    for ii in range(ni):
        g = None
        for h in range(heads):
            c = ii * heads + h
            s1 = s1_ref[:, h * nk:(h + 1) * nk]
            hit = (s0[:, c:c + 1] + s1) >= thr[:, h:h + 1]
            term = jnp.where(hit, e0[:, c:c + 1] * e1_ref[:, h * nk:(h + 1) * nk], 0.0)
            g = term if g is None else g + term
        parts.append(g * act[:, ii * nk:(ii + 1) * nk])
    coef = jnp.concatenate(parts, axis=1).astype(BF16)
    o_ref[...] += _dot(coef, v_ref[...])


def _peer_dense(h2, ut, v, s0r, e0r, s1, e1, thr, heads, nk, tp=512, eb=512):
    n, d = h2.shape
    tp = _pick(n, tp)
    ne = ut.shape[1]
    ni = eb // nk
    once = pl.Buffered(1)
    tok = pl.BlockSpec((tp, heads * nk), lambda i, e: (i, 0), pipeline_mode=once)
    sl = pl.BlockSpec((1, tp, ni * heads), lambda i, e: (e, i, 0))
    return pl.pallas_call(
        functools.partial(_peer_dense_kernel, heads=heads, nk=nk),
        grid=(n // tp, ne // eb),
        in_specs=[pl.BlockSpec((tp, d), lambda i, e: (i, 0), pipeline_mode=once),
                  pl.BlockSpec((d, eb), lambda i, e: (0, e)),
                  pl.BlockSpec((eb, d), lambda i, e: (e, 0)),
                  sl, sl, tok, tok, pl.BlockSpec((tp, heads), lambda i, e: (i, 0))],
        out_specs=pl.BlockSpec((tp, d), lambda i, e: (i, 0)),
        out_shape=jax.ShapeDtypeStruct((n, d), F32),
        compiler_params=_params(("parallel", "arbitrary")),
        name="peer_dense",
    )(h2, ut, v, s0r, e0r, s1, e1, thr)


def _by_expert_block(x, heads, nk, ni):
    n = x.shape[0]
    return x.reshape(n, heads, nk // ni, ni).transpose(2, 0, 3, 1).reshape(nk // ni, n, ni * heads)


def _layer(xp, xs, s_ret, s_rwkv, s_shift, cp, cs, p):
    nb, t, d = xp.shape
    db, dt, _ = xs.shape
    rows = _Rows(nb, t, db, dt)
    ret_heads, ret_hd = s_ret.shape[1], s_ret.shape[2]
    rw_heads, rw_hd = s_rwkv.shape[1], s_rwkv.shape[2]
    ret_w, rw = ret_heads * ret_hd, rw_heads * rw_hd
    w_lora, a_lora, g_lora = p['w2'].shape[0], p['a2'].shape[0], p['g2'].shape[0]
    rwkv_in = s_shift.shape[-1]
    ret_in = 4 * ret_w
    assert p['w_in'].shape[1] == ret_in + rwkv_in and rwkv_in == 3 * rw + w_lora + a_lora + g_lora

    x = jnp.concatenate([xp.reshape(nb * t, d), xs.reshape(db * dt, d)], axis=0)
    mod = _ada(jnp.concatenate([cp, cs], axis=0), p['ada_w'], p['ada_b'])
    mod_ext = rows.extend(mod)

    h = _prenorm(rows, x, p['pre_mix_g'], mod_ext, d)

    tm = _pick(rows.n, 1024)
    proj_ret = _matmul(h, p['w_in'], col_block0=0, n_out=ret_in, tm=tm, tn=512, name="in_proj_ret")
    proj_rkv = _matmul(h, p['w_in'], col_block0=ret_in // 512, n_out=3 * rw, tm=tm, tn=512, name="in_proj_rkv")
    tail0 = ret_in + 3 * rw
    tail_w = -(-(w_lora + a_lora + g_lora) // LANES) * LANES
    proj_tail = _matmul(h, p['w_in'], col_block0=tail0 // LANES, n_out=tail_w, tm=tm, tn=LANES, name="in_proj_tail")
    tail_valid = w_lora + a_lora + g_lora

    o_ret, sr_p, sr_s = _retention(rows, proj_ret, s_ret, ret_heads, ret_hd)

    def firsts(proj, shift_cols):
        w = proj.shape[1]
        starts = np.arange(rows.prompt_blocks) * rows.rb
        prev = jnp.where((starts % t == 0)[:, None], 0.0, proj[np.maximum(starts - 1, 0)])
        fp = jnp.zeros((rows.prompt_blocks, rows.slots, w), F32).at[:, 0].set(prev)
        return jnp.concatenate([fp.reshape(-1, w), shift_cols], axis=0)[:, None, :]

    shift_tail = jnp.pad(s_shift[:, 3 * rw:], ((0, 0), (0, tail_w - tail_valid)))
    first_rkv = firsts(proj_rkv, s_shift[:, :3 * rw])
    first_tail = firsts(proj_tail, shift_tail)
    mu = p['shift_mu']
    mu_tail = jnp.pad(mu[3 * rw:], (0, tail_w - tail_valid)).reshape(1, tail_w)
    g2p = jnp.pad(p['g2'], ((0, tail_w - w_lora - a_lora - g_lora), (0, 0)))
    vec = lambda a: a.reshape(1, rw)
    r_, w_, k_, v_, a_, b_, g_, bonus = _rwkv_prep(
        rows, proj_rkv, proj_tail, first_rkv, first_tail, mu[:3 * rw].reshape(1, 3 * rw), mu_tail,
        vec(p['w0']), p['w2'], vec(p['a0']), p['a2'], g2p, vec(p['k_k']), vec(p['k_a']), vec(p['r_k']),
        rw, rw_hd, g_lora)
    ops = (r_, w_, k_, v_, a_, b_)
    o_rw_p, sw_p = _rwkv_scan(ops, nb, t, 0, 256, rw_heads, rw_hd, None)
    o_rw_s, sw_s = _rwkv_scan(ops, db, dt, nb * t, dt, rw_heads, rw_hd, s_rwkv)
    o_rw = _rwkv_post(jnp.concatenate([o_rw_p, o_rw_s], axis=0), bonus, g_, vec(p['lnx_g']), vec(p['lnx_b']), rw_hd)

    mix = _matmul_out(o_ret, o_rw, p['w_out'])
    x1, h2 = _postmix(rows, x, mix, p['post_mix_g'], p['pre_ffn_g'], mod_ext, d)

    heads, _, nk, kd = p['peer_sub_keys'].shape
    q = _matmul(h2, p['peer_wq'], col_block0=0, n_out=heads * 2 * kd, tm=tm, tn=512, name="peer_q")
    s0, s1, e0, e1, thr = _peer_topk(q, p['peer_sub_keys'])
    eb = 512
    ni = eb // nk
    peer = _peer_dense(h2, p['peer_u'].T.astype(BF16), p['peer_v'].astype(BF16),
                       _by_expert_block(s0, heads, nk, ni), _by_expert_block(e0, heads, nk, ni),
                       s1, e1, thr[:, :, 0].T, heads, nk, eb=eb)
    y = _final(rows, x1, peer, p['post_ffn_g'], mod_ext, d)

    n_p = nb * t
    last_p = np.arange(nb) * t + t - 1
    last_s = n_p + np.arange(db) * dt + dt - 1
    pw_last = lambda idx: jnp.concatenate([proj_rkv[idx], proj_tail[idx, :tail_valid]], axis=1)
    return (y[:n_p].reshape(nb, t, d), y[n_p:].reshape(db, dt, d), sr_p, sr_s, sw_p, sw_s,
            pw_last(last_p), pw_last(last_s))


def kernel(x_prompt, x_sample, state_ret, state_rwkv, state_shift, c_prompt, c_sample, ada_w, ada_b, pre_mix_g, post_mix_g, pre_ffn_g, post_ffn_g, w_in, shift_mu, w0, w2, a0, a2, g2, k_k, k_a, r_k, lnx_g, lnx_b, w_out, peer_wq, peer_sub_keys, peer_u, peer_v):
    depth = ada_w.shape[0]
    assert depth == 1, "prompt and sample tokens are stacked per layer; deeper stacks need per-layer restacking"
    prm = dict(ada_w=ada_w[0], ada_b=ada_b[0], pre_mix_g=pre_mix_g[0], post_mix_g=post_mix_g[0],
               pre_ffn_g=pre_ffn_g[0], post_ffn_g=post_ffn_g[0], w_in=w_in[0], shift_mu=shift_mu[0],
               w0=w0[0], w2=w2[0], a0=a0[0], a2=a2[0], g2=g2[0], k_k=k_k[0], k_a=k_a[0], r_k=r_k[0],
               lnx_g=lnx_g[0], lnx_b=lnx_b[0], w_out=w_out[0], peer_wq=peer_wq[0],
               peer_sub_keys=peer_sub_keys[0], peer_u=peer_u[0], peer_v=peer_v[0])
    yp, ys, sr_p, sr_s, sw_p, sw_s, ss_p, ss_s = _layer(
        x_prompt, x_sample, state_ret[0], state_rwkv[0], state_shift[0], c_prompt, c_sample, prm)
    sd, wd, hd = state_ret.dtype, state_rwkv.dtype, state_shift.dtype
    return (yp, ys, sr_p[None].astype(sd), sr_s[None].astype(sd), sw_p[None].astype(wd), sw_s[None].astype(wd),
            ss_p[None].astype(hd), ss_s[None].astype(hd))
```

```python
import functools
import math

import numpy as np
import jax
import jax.numpy as jnp
from jax import lax
from jax.experimental import pallas as pl
from jax.experimental.pallas import tpu as pltpu

PAST_LEN = 16384
RET_CHUNK = 128
ROPE_BASE = 10000.0
LNX_EPS = 64e-5
NORM_EPS = 1e-6
PEER_TOPK = 16

LANES = 128
ROW_BLOCK = 128
VMEM_LIMIT = 56 * 1024 * 1024

HI = lax.Precision.HIGHEST
F32 = jnp.float32
BF16 = jnp.bfloat16


def _pick(n, target, mult=LANES):
    best = None
    for c in range(mult, min(n, target) + 1, mult):
        if n % c == 0:
            best = c
    assert best is not None, (n, target, mult)
    return best


def _params(sem, vmem=VMEM_LIMIT):
    return pltpu.CompilerParams(dimension_semantics=sem, vmem_limit_bytes=vmem)


def _dot(a, b, precision=None):
    return jnp.dot(a, b, preferred_element_type=F32, precision=precision)


def _dot_nt(a, b, precision=None):
    return lax.dot_general(a, b, (((1,), (1,)), ((), ())), preferred_element_type=F32, precision=precision)


def _dot_tn(a, b, precision=None):
    return lax.dot_general(a, b, (((0,), (0,)), ((), ())), preferred_element_type=F32, precision=precision)


def _block_ones(n, seg, dtype):
    r = lax.broadcasted_iota(jnp.int32, (n, n), 0) // seg
    c = lax.broadcasted_iota(jnp.int32, (n, n), 1) // seg
    return (r == c).astype(dtype)


def _ada_kernel(c_ref, w_ref, b_ref, o_ref):
    c = c_ref[...]
    sc = (c * jax.nn.sigmoid(c)).astype(BF16)
    o_ref[...] = _dot(sc, w_ref[...].astype(BF16)) + b_ref[...]


def _ada(c_all, ada_w, ada_b, tn=512):
    m, d = c_all.shape
    n = ada_w.shape[1]
    return pl.pallas_call(
        _ada_kernel,
        grid=(n // tn,),
        in_specs=[pl.BlockSpec((m, d), lambda j: (0, 0)),
                  pl.BlockSpec((d, tn), lambda j: (0, j)),
                  pl.BlockSpec((1, tn), lambda j: (0, j))],
        out_specs=pl.BlockSpec((m, tn), lambda j: (0, j)),
        out_shape=jax.ShapeDtypeStruct((m, n), F32),
        compiler_params=_params(("arbitrary",)),
        name="ada",
    )(c_all, ada_w, ada_b.reshape(1, n))


def _expand_slots(m, rows):
    slots, _, d = m.shape
    return jnp.broadcast_to(m, (slots, rows // slots, d)).reshape(rows, d)


def _rms(x, g):
    return x * lax.rsqrt(jnp.mean(x * x, axis=-1, keepdims=True) + NORM_EPS) * g


def _prenorm_kernel(x_ref, g_ref, sh_ref, sc_ref, h_ref):
    x = x_ref[...]
    rows = x.shape[0]
    y = _rms(x, g_ref[...])
    h_ref[...] = (y * (1.0 + _expand_slots(sc_ref[...], rows)) + _expand_slots(sh_ref[...], rows)).astype(h_ref.dtype)


def _postmix_kernel(x_ref, m_ref, gpost_ref, gpre_ref, gate_ref, sh_ref, sc_ref, x1_ref, h_ref):
    x = x_ref[...]
    rows = x.shape[0]
    x1 = x + _expand_slots(gate_ref[...], rows) * _rms(m_ref[...], gpost_ref[...])
    x1_ref[...] = x1
    y = _rms(x1, gpre_ref[...])
    h_ref[...] = (y * (1.0 + _expand_slots(sc_ref[...], rows)) + _expand_slots(sh_ref[...], rows)).astype(h_ref.dtype)


def _final_kernel(x_ref, m_ref, gpost_ref, gate_ref, y_ref):
    x = x_ref[...]
    rows = x.shape[0]
    y_ref[...] = x + _expand_slots(gate_ref[...], rows) * _rms(m_ref[...], gpost_ref[...])


class _Rows:
    def __init__(self, nb, t, db, dt):
        self.n_prompt = nb * t
        self.n_sample = db * dt
        self.n = self.n_prompt + self.n_sample
        self.rb = ROW_BLOCK
        assert t % self.rb == 0 and self.rb % dt == 0 and self.n_sample % self.rb == 0
        self.slots = self.rb // dt
        self.prompt_blocks = self.n_prompt // self.rb
        self.blocks_per_seq = t // self.rb
        self.n_blocks = self.n // self.rb
        self.nb, self.t, self.db, self.dt = nb, t, db, dt

    def slot_block(self, i):
        return jnp.where(i < self.prompt_blocks, i // self.blocks_per_seq, i - self.prompt_blocks + self.nb)

    def extend(self, m):
        mp = jnp.repeat(m[: self.nb], self.slots, axis=0)
        return jnp.concatenate([mp, m[self.nb:]], axis=0)[:, None, :]


def _mod_spec(rows, d, chunk):
    return pl.BlockSpec((rows.slots, 1, d), lambda i: (rows.slot_block(i), 0, chunk))


def _prenorm(rows, x, g, mod_ext, d):
    row = pl.BlockSpec((rows.rb, d), lambda i: (i, 0))
    vec = pl.BlockSpec((1, d), lambda i: (0, 0))
    return pl.pallas_call(
        _prenorm_kernel,
        grid=(rows.n_blocks,),
        in_specs=[row, vec, _mod_spec(rows, d, 0), _mod_spec(rows, d, 1)],
        out_specs=row,
        out_shape=jax.ShapeDtypeStruct((rows.n, d), BF16),
        compiler_params=_params(("parallel",)),
        name="prenorm",
    )(x, g.reshape(1, d), mod_ext, mod_ext)


def _postmix(rows, x, mix, gpost, gpre, mod_ext, d):
    row = pl.BlockSpec((rows.rb, d), lambda i: (i, 0))
    vec = pl.BlockSpec((1, d), lambda i: (0, 0))
    return pl.pallas_call(
        _postmix_kernel,
        grid=(rows.n_blocks,),
        in_specs=[row, row, vec, vec, _mod_spec(rows, d, 2), _mod_spec(rows, d, 3), _mod_spec(rows, d, 4)],
        out_specs=[row, row],
        out_shape=[jax.ShapeDtypeStruct((rows.n, d), F32), jax.ShapeDtypeStruct((rows.n, d), BF16)],
        compiler_params=_params(("parallel",)),
        name="postmix",
    )(x, mix, gpost.reshape(1, d), gpre.reshape(1, d), mod_ext, mod_ext, mod_ext)


def _final(rows, x1, peer, gpost, mod_ext, d):
    row = pl.BlockSpec((rows.rb, d), lambda i: (i, 0))
    vec = pl.BlockSpec((1, d), lambda i: (0, 0))
    return pl.pallas_call(
        _final_kernel,
        grid=(rows.n_blocks,),
        in_specs=[row, row, vec, _mod_spec(rows, d, 5)],
        out_specs=row,
        out_shape=jax.ShapeDtypeStruct((rows.n, d), F32),
        compiler_params=_params(("parallel",)),
        name="final",
    )(x1, peer, gpost.reshape(1, d), mod_ext)


def _mm_kernel(a_ref, w_ref, o_ref):
    o_ref[...] = _dot(a_ref[...], w_ref[...].astype(BF16)).astype(o_ref.dtype)


def _matmul(a, w, *, col_block0, n_out, tm, tn, name):
    m, k = a.shape
    return pl.pallas_call(
        _mm_kernel,
        grid=(pl.cdiv(n_out, tn), m // tm),
        in_specs=[pl.BlockSpec((tm, k), lambda j, i: (i, 0)),
                  pl.BlockSpec((k, tn), lambda j, i: (0, j + col_block0))],
        out_specs=pl.BlockSpec((tm, tn), lambda j, i: (i, j)),
        out_shape=jax.ShapeDtypeStruct((m, n_out), F32),
        compiler_params=_params(("parallel", "parallel")),
        name=name,
    )(a, w)


def _mm2_kernel(a_ref, b_ref, wa_ref, wb_ref, o_ref):
    o_ref[...] = _dot(a_ref[...], wa_ref[...].astype(BF16)) + _dot(b_ref[...], wb_ref[...].astype(BF16))


def _matmul_out(o_ret, o_rw, w_out, tn=512):
    m, ka = o_ret.shape
    tm = _pick(m, 1024)
    kb = o_rw.shape[1]
    n = w_out.shape[1]
    tn = _pick(n, tn)
    assert ka == kb
    return pl.pallas_call(
        _mm2_kernel,
        grid=(n // tn, m // tm),
        in_specs=[pl.BlockSpec((tm, ka), lambda j, i: (i, 0)),
                  pl.BlockSpec((tm, kb), lambda j, i: (i, 0)),
                  pl.BlockSpec((ka, tn), lambda j, i: (0, j)),
                  pl.BlockSpec((kb, tn), lambda j, i: (1, j))],
        out_specs=pl.BlockSpec((tm, tn), lambda j, i: (i, j)),
        out_shape=jax.ShapeDtypeStruct((m, n), F32),
        compiler_params=_params(("parallel", "parallel")),
        name="out_proj",
    )(o_ret, o_rw, w_out, w_out)


def _rope_tables(pos, hd):
    half = hd // 2
    inv = ROPE_BASE ** (-jnp.arange(half, dtype=F32) / half)
    ang = pos.astype(F32)[:, None] * inv[None, :]
    cos, sin = jnp.cos(ang), jnp.sin(ang)
    return jnp.concatenate([cos, cos], axis=-1), jnp.concatenate([-sin, sin], axis=-1)


def _rope(x, cos, sin_signed):
    half = x.shape[-1] // 2
    return x * cos + pltpu.roll(x, half, 1) * sin_signed


def _ret_finish(o, gate):
    o = o * lax.rsqrt(jnp.mean(o * o, axis=-1, keepdims=True) + NORM_EPS)
    return o * (gate * jax.nn.sigmoid(gate))


def _ret_prompt_kernel(q_ref, k_ref, v_ref, g_ref, cos_ref, sin_ref, lg_ref, o_ref, s_ref, *, chunk):
    t, hd = q_ref.shape
    lg = lg_ref[0, 0:1, :]
    ri = lax.broadcasted_iota(jnp.int32, (chunk, chunk), 0)
    ci = lax.broadcasted_iota(jnp.int32, (chunk, chunk), 1)
    causal = ri >= ci
    diff = jnp.where(causal, ri - ci, 0).astype(F32)
    mask = jnp.where(causal, jnp.exp(diff * lg), 0.0)
    rowf = lax.broadcasted_iota(jnp.int32, (chunk, hd), 0).astype(F32)
    q_dec = jnp.exp((rowf + 1.0) * lg)
    k_dec = jnp.exp((chunk - 1.0 - rowf) * lg)
    c_dec = jnp.exp(float(chunk) * lg)

    def step(c, s):
        r0 = pl.multiple_of(c * chunk, chunk)
        cos = cos_ref[pl.ds(r0, chunk), :]
        sin = sin_ref[pl.ds(r0, chunk), :]
        qc = _rope(q_ref[pl.ds(r0, chunk), :], cos, sin)
        kc = _rope(k_ref[pl.ds(r0, chunk), :], cos, sin) * (hd ** -0.5)
        vc = v_ref[pl.ds(r0, chunk), :]
        att = _dot_nt(qc, kc, HI) * mask
        o = _dot(att, vc, HI) + _dot(qc * q_dec, s, HI)
        s = s * c_dec + _dot_tn(kc * k_dec, vc, HI)
        o_ref[pl.ds(r0, chunk), :] = _ret_finish(o, g_ref[pl.ds(r0, chunk), :]).astype(o_ref.dtype)
        return s

    s_ref[0, 0] = lax.fori_loop(0, t // chunk, step, jnp.zeros((hd, hd), F32))


def _ret_sample_kernel(q_ref, k_ref, v_ref, g_ref, cos_ref, sin_ref, lg_ref, s0_ref, o_ref, s_ref, *, dt):
    rows, hd = q_ref.shape
    nseq = rows // dt
    lg = lg_ref[0, 0:1, :]
    ri = lax.broadcasted_iota(jnp.int32, (rows, rows), 0)
    ci = lax.broadcasted_iota(jnp.int32, (rows, rows), 1)
    ok = (ri >= ci) & ((ri // dt) == (ci // dt))
    diff = jnp.where(ok, ri - ci, 0).astype(F32)
    mask = jnp.where(ok, jnp.exp(diff * lg), 0.0)
    rown = lax.broadcasted_iota(jnp.int32, (rows, hd), 0)
    posf = (rown % dt).astype(F32)
    q_dec = jnp.exp((posf + 1.0) * lg)
    k_dec = jnp.exp((dt - 1.0 - posf) * lg)
    c_dec = jnp.exp(float(dt) * lg)
    cos, sin = cos_ref[...], sin_ref[...]
    qc = _rope(q_ref[...], cos, sin)
    kc = _rope(k_ref[...], cos, sin) * (hd ** -0.5)
    vc = v_ref[...]
    inner = _dot(_dot_nt(qc, kc, HI) * mask, vc, HI)
    qd = qc * q_dec
    kd = kc * k_dec
    seq = rown // dt
    cross = jnp.zeros((rows, hd), F32)
    for s in range(nseq):
        s0 = s0_ref[s, 0]
        mine = seq == s
        cross = cross + _dot(jnp.where(mine, qd, 0.0), s0, HI)
        s_ref[s, 0] = s0 * c_dec + _dot_tn(jnp.where(mine, kd, 0.0), vc, HI)
    o_ref[...] = _ret_finish(inner + cross, g_ref[...]).astype(o_ref.dtype)


def _retention(rows, proj, state_ret, ret_heads, hd):
    nb, t, db, dt = rows.nb, rows.t, rows.db, rows.dt
    lg = jnp.log1p(-jnp.exp2(-5.0 - jnp.arange(ret_heads, dtype=F32)))
    lg_tab = jnp.broadcast_to(lg[:, None, None], (ret_heads, 8, hd))
    lg_spec = pl.BlockSpec((1, 8, hd), lambda b, h: (h, 0, 0))
    assert hd == RET_CHUNK and t % RET_CHUNK == 0 and dt % RET_CHUNK != 0

    cos_p, sin_p = _rope_tables(jnp.arange(t, dtype=jnp.int32), hd)
    col = lambda part: pl.BlockSpec((t, hd), lambda b, h: (b, h + part * ret_heads))
    tab = pl.BlockSpec((t, hd), lambda b, h: (0, 0))
    o_p, s_p = pl.pallas_call(
        functools.partial(_ret_prompt_kernel, chunk=RET_CHUNK),
        grid=(nb, ret_heads),
        in_specs=[col(0), col(1), col(2), col(3), tab, tab, lg_spec],
        out_specs=[pl.BlockSpec((t, hd), lambda b, h: (b, h)),
                   pl.BlockSpec((1, 1, hd, hd), lambda b, h: (b, h, 0, 0))],
        out_shape=[jax.ShapeDtypeStruct((nb * t, ret_heads * hd), BF16),
                   jax.ShapeDtypeStruct((nb, ret_heads, hd, hd), F32)],
        compiler_params=_params(("parallel", "parallel")),
        name="ret_prompt",
    )(proj, proj, proj, proj, cos_p, sin_p, lg_tab)

    rb = RET_CHUNK
    nseq = rb // dt
    cos_s, sin_s = _rope_tables(PAST_LEN + jnp.arange(dt, dtype=jnp.int32), hd)
    cos_s, sin_s = jnp.tile(cos_s, (nseq, 1)), jnp.tile(sin_s, (nseq, 1))
    off = (nb * t) // rb
    col = lambda part: pl.BlockSpec((rb, hd), lambda b, h: (b + off, h + part * ret_heads))
    tab = pl.BlockSpec((rb, hd), lambda b, h: (0, 0))
    st = pl.BlockSpec((nseq, 1, hd, hd), lambda b, h: (b, h, 0, 0))
    o_s, s_s = pl.pallas_call(
        functools.partial(_ret_sample_kernel, dt=dt),
        grid=(db // nseq, ret_heads),
        in_specs=[col(0), col(1), col(2), col(3), tab, tab, lg_spec, st],
        out_specs=[pl.BlockSpec((rb, hd), lambda b, h: (b, h)), st],
        out_shape=[jax.ShapeDtypeStruct((db * dt, ret_heads * hd), BF16),
                   jax.ShapeDtypeStruct((db, ret_heads, hd, hd), F32)],
        compiler_params=_params(("parallel", "parallel")),
        name="ret_sample",
    )(proj, proj, proj, proj, cos_s, sin_s, lg_tab, state_ret)
    return jnp.concatenate([o_p, o_s], axis=0), s_p, s_s


def _shifted(x, first, tseq_mask):
    prev = pltpu.roll(x, 1, 0)
    return jnp.where(tseq_mask, first, prev)


def _rwkv_prep_kernel(pr_ref, pk_ref, pv_ref, pt_ref, fr_ref, fk_ref, fv_ref, ft_ref,
                      mur_ref, muk_ref, muv_ref, mut_ref, w0_ref, w2_ref, a0_ref, a2_ref, g2_ref,
                      kk_ref, ka_ref, rk_ref,
                      r_out, w_out, k_out, v_out, a_out, b_out, g_out, bonus_out,
                      *, dt, prompt_blocks, w_lora, a_lora, g_lora, hd):
    i = pl.program_id(0)
    rows, cn = pr_ref.shape
    rown = lax.broadcasted_iota(jnp.int32, (rows, 1), 0)
    sample_block = jnp.zeros((rows, 1), jnp.int32) + (i >= prompt_blocks).astype(jnp.int32)
    is_first = (rown == 0) | ((rown % dt == 0) & (sample_block == 1))

    def lerp(p_ref, f_ref, mu_ref):
        p = p_ref[...]
        prev = _shifted(p, _expand_slots(f_ref[...], rows), is_first)
        return p + mu_ref[...] * (prev - p)

    r = lerp(pr_ref, fr_ref, mur_ref)
    kw = lerp(pk_ref, fk_ref, muk_ref)
    vw = lerp(pv_ref, fv_ref, muv_ref)
    tail = lerp(pt_ref, ft_ref, mut_ref)
    tw = tail.shape[1]
    wl = tail[:, :w_lora]
    al = tail[:, w_lora:w_lora + a_lora]
    glane = lax.broadcasted_iota(jnp.int32, (rows, tw - w_lora - a_lora), 1)
    gl = jnp.where(glane < g_lora, jax.nn.sigmoid(tail[:, w_lora + a_lora:]), 0.0)

    w_log = -jax.nn.softplus(-(w0_ref[...] + _dot(jnp.tanh(wl), w2_ref[...], HI))) - 0.5
    decay = jnp.exp(-jnp.exp(w_log))
    a = jax.nn.sigmoid(a0_ref[...] + _dot(al, a2_ref[...], HI))
    g = _dot(gl, g2_ref[...], HI)

    seg = _block_ones(LANES, hd, F32)

    def segsum(x):
        return jnp.concatenate([_dot(x[:, c:c + LANES], seg, HI) for c in range(0, cn, LANES)], axis=1)

    kk = kw * kk_ref[...]
    kk = kk / jnp.maximum(jnp.sqrt(segsum(kk * kk)), 1e-12)
    kn = kw * (1.0 + (a - 1.0) * ka_ref[...])
    r_out[...] = r
    w_out[...] = decay
    k_out[...] = kn
    v_out[...] = vw
    a_out[...] = -kk
    b_out[...] = kk * a
    g_out[...] = g
    bonus_out[...] = segsum(r * kn * rk_ref[...]) * vw


def _rwkv_prep(rows, proj_rkv, proj_tail, first_rkv, first_tail, mu_rkv, mu_tail, w0, w2, a0, a2, g2p,
               k_k, k_a, r_k_flat, rw, hd, g_lora, cn=512):
    n = rows.n
    cn = _pick(rw, cn)
    tw = proj_tail.shape[1]
    nblk = (rows.n_blocks, rw // cn)
    part = lambda p: pl.BlockSpec((rows.rb, cn), lambda i, j: (i, j + p * (rw // cn)))
    fpart = lambda p: pl.BlockSpec((rows.slots, 1, cn), lambda i, j: (i, 0, j + p * (rw // cn)))
    mupart = lambda p: pl.BlockSpec((1, cn), lambda i, j: (0, j + p * (rw // cn)))
    vec = pl.BlockSpec((1, cn), lambda i, j: (0, j))
    lora = lambda k: pl.BlockSpec((k, cn), lambda i, j: (0, j))
    out = pl.BlockSpec((rows.rb, cn), lambda i, j: (i, j))
    kern = functools.partial(_rwkv_prep_kernel, dt=rows.dt, prompt_blocks=rows.prompt_blocks,
                             w_lora=w2.shape[0], a_lora=a2.shape[0], g_lora=g_lora, hd=hd)
    return pl.pallas_call(
        kern,
        grid=nblk,
        in_specs=[part(0), part(1), part(2), pl.BlockSpec((rows.rb, tw), lambda i, j: (i, 0)),
                  fpart(0), fpart(1), fpart(2), pl.BlockSpec((rows.slots, 1, tw), lambda i, j: (i, 0, 0)),
                  mupart(0), mupart(1), mupart(2), pl.BlockSpec((1, tw), lambda i, j: (0, 0)),
                  vec, lora(w2.shape[0]), vec, lora(a2.shape[0]), lora(g2p.shape[0]),
                  vec, vec, vec],
        out_specs=[out] * 8,
        out_shape=[jax.ShapeDtypeStruct((n, rw), F32)] * 8,
        compiler_params=_params(("parallel", "arbitrary")),
        name="rwkv_prep",
    )(proj_rkv, proj_rkv, proj_rkv, proj_tail, first_rkv, first_rkv, first_rkv, first_tail,
      mu_rkv, mu_rkv, mu_rkv, mu_tail, w0, w2, a0, a2, g2p, k_k, k_a, r_k_flat)


def _rwkv_scan_kernel(r_ref, w_ref, k_ref, v_ref, a_ref, b_ref, *rest, hd, has_state, n_chains):
    if has_state:
        s0_ref, o_ref, sout_ref, st_ref, ob_ref, vs_ref = rest
    else:
        o_ref, sout_ref, st_ref, ob_ref, vs_ref = rest
        s0_ref = None
    tb = pl.program_id(1)
    tc, width = r_ref.shape
    pw = 2 * hd
    npairs = width // pw

    @pl.when(tb == 0)
    def _():
        for p in range(npairs):
            if has_state:
                st_ref[p] = jnp.concatenate([s0_ref[0, 2 * p], s0_ref[0, 2 * p + 1]], axis=1)
            else:
                st_ref[p] = jnp.zeros((hd, pw), F32)

    ones_bf = _block_ones(pw, hd, BF16)
    sub = lax.broadcasted_iota(jnp.int32, (hd, pw), 0)
    lane = lax.broadcasted_iota(jnp.int32, (hd, pw), 1)
    diags = [(((lane % hd) - sub + hd) % hd == i).astype(F32).astype(BF16) for i in range(3)]
    lane8 = lax.broadcasted_iota(jnp.int32, (8, pw), 1)
    sel = lax.broadcasted_iota(jnp.int32, (8, pw), 0) == (lane8 // hd)
    chains = [range(c * npairs // n_chains, (c + 1) * npairs // n_chains) for c in range(n_chains)]

    def shift_in_head(x, i):
        return jnp.where(lane8 % hd >= i, pltpu.roll(x, i, 1), pltpu.roll(x, pw - hd + i, 1))

    def group(tg, carry):
        r0 = pl.multiple_of(tg * 8, 8)
        rows8 = pl.ds(r0, 8)
        for p in range(npairs):
            cols = pl.ds(p * pw, pw)
            v8 = v_ref[rows8, cols]
            v1 = v8.astype(BF16).astype(F32)
            v2 = (v8 - v1).astype(BF16).astype(F32)
            vs_ref[0, :, cols] = v1
            vs_ref[1, :, cols] = shift_in_head(v2, 1)
            vs_ref[2, :, cols] = shift_in_head((v8 - v1) - v2, 2)

        def tile_row(ref, p, j):
            return ref[rows8, pl.ds(p * pw, pw)][j:j + 1]

        for j in range(8):
            for chain in chains:
                old = [st_ref[p] for p in chain]
                pa = jnp.concatenate([s * tile_row(a_ref, p, j) for s, p in zip(old, chain)], axis=0)
                sa = _dot(pa.astype(BF16), ones_bf)
                xs = []
                for p in chain:
                    x = None
                    for i in range(3):
                        term = diags[i] * vs_ref[i, :, pl.ds(p * pw, pw)][j:j + 1].astype(BF16)
                        x = term if x is None else x + term
                    xs.append(x)
                vc = _dot(jnp.concatenate(xs, axis=0), ones_bf)
                for i, p in enumerate(chain):
                    blk = slice(i * hd, (i + 1) * hd)
                    s = (old[i] * tile_row(w_ref, p, j) + sa[blk] * tile_row(b_ref, p, j)
                         + vc[blk] * tile_row(k_ref, p, j))
                    st_ref[p] = s
                    rsel = jnp.where(sel, tile_row(r_ref, p, j), 0.0).astype(BF16)
                    o8 = _dot_nt(rsel, s.astype(BF16))
                    ob_ref[p, j:j + 1, 0:hd] = o8[0:1, :]
                    ob_ref[p, j:j + 1, hd:pw] = o8[1:2, :]
        for p in range(npairs):
            o_ref[rows8, pl.ds(p * pw, pw)] = ob_ref[p]
        return carry

    lax.fori_loop(0, tc // 8, group, 0)

    @pl.when(tb == pl.num_programs(1) - 1)
    def _():
        for p in range(npairs):
            s = st_ref[p]
            sout_ref[0, 2 * p] = s[:, :hd]
            sout_ref[0, 2 * p + 1] = s[:, hd:]


def _rwkv_scan(ops, nseq, t, row0, tc, heads, hd, state):
    width = heads * hd
    assert t % tc == 0 and row0 % tc == 0
    nt = t // tc
    blk = pl.BlockSpec((tc, width), lambda s, j: (row0 // tc + s * nt + j, 0))
    st = pl.BlockSpec((1, heads, hd, hd), lambda s, j: (s, 0, 0, 0))
    in_specs = [blk] * 6
    args = list(ops)
    if state is not None:
        in_specs.append(st)
        args.append(state)
    return pl.pallas_call(
        functools.partial(_rwkv_scan_kernel, hd=hd, has_state=state is not None, n_chains=1),
        grid=(nseq, nt),
        in_specs=in_specs,
        out_specs=[pl.BlockSpec((tc, width), lambda s, j: (s * nt + j, 0)), st],
        out_shape=[jax.ShapeDtypeStruct((nseq * t, width), F32),
                   jax.ShapeDtypeStruct((nseq, heads, hd, hd), F32)],
        scratch_shapes=[pltpu.VMEM((heads // 2, hd, 2 * hd), F32), pltpu.VMEM((heads // 2, 8, 2 * hd), F32),
                        pltpu.VMEM((3, 8, width), F32)],
        compiler_params=_params(("parallel", "arbitrary")),
        name="rwkv_scan_state" if state is not None else "rwkv_scan",
    )(*args)


def _rwkv_post_kernel(o_ref, bonus_ref, g_ref, lg_ref, lb_ref, out_ref, *, hd):
    o = o_ref[...]
    cn = o.shape[1]
    seg = _block_ones(LANES, hd, F32)

    def segmean(x):
        return jnp.concatenate([_dot(x[:, c:c + LANES], seg, HI) for c in range(0, cn, LANES)], axis=1) * (1.0 / hd)

    mean = segmean(o)
    cen = o - mean
    var = segmean(cen * cen)
    y = cen * lax.rsqrt(var + LNX_EPS) * lg_ref[...] + lb_ref[...]
    out_ref[...] = ((y + bonus_ref[...]) * g_ref[...]).astype(out_ref.dtype)


def _rwkv_post(o_raw, bonus, g, lnx_g, lnx_b, hd, rb=256, cn=512):
    n, rw = o_raw.shape
    rb, cn = _pick(n, rb), _pick(rw, cn)
    blk = pl.BlockSpec((rb, cn), lambda i, j: (i, j))
    vec = pl.BlockSpec((1, cn), lambda i, j: (0, j))
    return pl.pallas_call(
        functools.partial(_rwkv_post_kernel, hd=hd),
        grid=(n // rb, rw // cn),
        in_specs=[blk, blk, blk, vec, vec],
        out_specs=blk,
        out_shape=jax.ShapeDtypeStruct((n, rw), BF16),
        compiler_params=_params(("parallel", "parallel")),
        name="rwkv_post",
    )(o_raw, bonus, g, lnx_g, lnx_b)


def _take_top(s, k):
    n = s.shape[1]
    lane = lax.broadcasted_iota(jnp.int32, s.shape, 1).astype(F32)
    out = []
    for _ in range(k):
        m = jnp.max(s, axis=1, keepdims=True)
        first = jnp.min(jnp.where(s == m, lane, float(n)), axis=1, keepdims=True)
        s = jnp.where(lane == first, -jnp.inf, s)
        out.append(m)
    return out


def _peer_topk_kernel(q_ref, keys_ref, s0_ref, s1_ref, e0_ref, e1_ref, thr_ref, *, topk):
    rows = q_ref.shape[0]
    nk = keys_ref.shape[2]
    kd = keys_ref.shape[3]
    s0 = _dot_nt(q_ref[:, :kd], keys_ref[0, 0], HI)
    s1 = _dot_nt(q_ref[:, kd:], keys_ref[0, 1], HI)
    top0 = _take_top(s0, topk)
    top1 = _take_top(s1, topk)
    lane = lax.broadcasted_iota(jnp.int32, (rows, topk * topk), 1)
    c0 = jnp.zeros((rows, topk * topk), F32)
    c1 = jnp.zeros((rows, topk * topk), F32)
    for a in range(topk):
        c0 = jnp.where(lane // topk == a, top0[a], c0)
        c1 = jnp.where(lane % topk == a, top1[a], c1)
    best = _take_top(c0 + c1, topk)
    z = jnp.zeros((rows, 1), F32)
    for b in best:
        z = z + jnp.exp(b - best[0])
    s0_ref[...] = s0
    s1_ref[...] = s1
    e0_ref[...] = jnp.exp(s0 - top0[0]) / z
    e1_ref[...] = jnp.exp(s1 - top1[0])
    thr_ref[0] = jnp.broadcast_to(best[topk - 1], (rows, thr_ref.shape[2]))


def _peer_topk(q, sub_keys, rb=256):
    n = q.shape[0]
    rb = _pick(n, rb)
    heads, _, nk, kd = sub_keys.shape
    blk = pl.BlockSpec((rb, nk), lambda i, h: (i, h))
    shp = jax.ShapeDtypeStruct((n, heads * nk), F32)
    return pl.pallas_call(
        functools.partial(_peer_topk_kernel, topk=PEER_TOPK),
        grid=(n // rb, heads),
        in_specs=[pl.BlockSpec((rb, 2 * kd), lambda i, h: (i, h)),
                  pl.BlockSpec((1, 2, nk, kd), lambda i, h: (h, 0, 0, 0))],
        out_specs=[blk] * 4 + [pl.BlockSpec((1, rb, 8), lambda i, h: (h, i, 0))],
        out_shape=[shp] * 4 + [jax.ShapeDtypeStruct((heads, n, 8), F32)],
        compiler_params=_params(("parallel", "parallel")),
        name="peer_topk",
    )(q, sub_keys)


def _gelu(x):
    return 0.5 * x * (1.0 + lax.erf(x * (2.0 ** -0.5)))


def _peer_dense_kernel(h_ref, ut_ref, v_ref, s0_ref, e0_ref, s1_ref, e1_ref, thr_ref, o_ref, *, heads, nk):
    e = pl.program_id(1)
    eb = ut_ref.shape[1]
    ni = eb // nk

    @pl.when(e == 0)
    def _():
        o_ref[...] = jnp.zeros_like(o_ref)

    act = _gelu(_dot(h_ref[...], ut_ref[...]))
    s0 = s0_ref[0]
    e0 = e0_ref[0]
    thr = thr_ref[...]
    parts = []
    for ii in range(ni):
        g = None
        for h in range(heads):
            c = ii * heads + h
            s1 = s1_ref[:, h * nk:(h + 1) * nk]
            hit = (s0[:, c:c + 1] + s1) >= thr[:, h:h + 1]
            term = jnp.where(hit, e0[:, c:c + 1] * e1_ref[:, h * nk:(h + 1) * nk], 0.0)
            g = term if g is None else g + term
        parts.append(g * act[:, ii * nk:(ii + 1) * nk])
    coef = jnp.concatenate(parts, axis=1).astype(BF16)
    o_ref[...] += _dot(coef, v_ref[...])


def _peer_dense(h2, ut, v, s0r, e0r, s1, e1, thr, heads, nk, tp=512, eb=512):
    n, d = h2.shape
    tp = _pick(n, tp)
    ne = ut.shape[1]
    ni = eb // nk
    once = pl.Buffered(1)
    tok = pl.BlockSpec((tp, heads * nk), lambda i, e: (i, 0), pipeline_mode=once)
    sl = pl.BlockSpec((1, tp, ni * heads), lambda i, e: (e, i, 0))
    return pl.pallas_call(
        functools.partial(_peer_dense_kernel, heads=heads, nk=nk),
        grid=(n // tp, ne // eb),
        in_specs=[pl.BlockSpec((tp, d), lambda i, e: (i, 0), pipeline_mode=once),
                  pl.BlockSpec((d, eb), lambda i, e: (0, e)),
                  pl.BlockSpec((eb, d), lambda i, e: (e, 0)),
                  sl, sl, tok, tok, pl.BlockSpec((tp, heads), lambda i, e: (i, 0))],
        out_specs=pl.BlockSpec((tp, d), lambda i, e: (i, 0)),
        out_shape=jax.ShapeDtypeStruct((n, d), F32),
        compiler_params=_params(("parallel", "arbitrary")),
        name="peer_dense",
    )(h2, ut, v, s0r, e0r, s1, e1, thr)


def _by_expert_block(x, heads, nk, ni):
    n = x.shape[0]
    return x.reshape(n, heads, nk // ni, ni).transpose(2, 0, 3, 1).reshape(nk // ni, n, ni * heads)


def _layer(xp, xs, s_ret, s_rwkv, s_shift, cp, cs, p):
    nb, t, d = xp.shape
    db, dt, _ = xs.shape
    rows = _Rows(nb, t, db, dt)
    ret_heads, ret_hd = s_ret.shape[1], s_ret.shape[2]
    rw_heads, rw_hd = s_rwkv.shape[1], s_rwkv.shape[2]
    ret_w, rw = ret_heads * ret_hd, rw_heads * rw_hd
    w_lora, a_lora, g_lora = p['w2'].shape[0], p['a2'].shape[0], p['g2'].shape[0]
    rwkv_in = s_shift.shape[-1]
    ret_in = 4 * ret_w
    assert p['w_in'].shape[1] == ret_in + rwkv_in and rwkv_in == 3 * rw + w_lora + a_lora + g_lora

    x = jnp.concatenate([xp.reshape(nb * t, d), xs.reshape(db * dt, d)], axis=0)
    mod = _ada(jnp.concatenate([cp, cs], axis=0), p['ada_w'], p['ada_b'])
    mod_ext = rows.extend(mod)

    h = _prenorm(rows, x, p['pre_mix_g'], mod_ext, d)

    tm = _pick(rows.n, 1024)
    proj_ret = _matmul(h, p['w_in'], col_block0=0, n_out=ret_in, tm=tm, tn=512, name="in_proj_ret")
    proj_rkv = _matmul(h, p['w_in'], col_block0=ret_in // 512, n_out=3 * rw, tm=tm, tn=512, name="in_proj_rkv")
    tail0 = ret_in + 3 * rw
    tail_w = -(-(w_lora + a_lora + g_lora) // LANES) * LANES
    proj_tail = _matmul(h, p['w_in'], col_block0=tail0 // LANES, n_out=tail_w, tm=tm, tn=LANES, name="in_proj_tail")
    tail_valid = w_lora + a_lora + g_lora

    o_ret, sr_p, sr_s = _retention(rows, proj_ret, s_ret, ret_heads, ret_hd)

    def firsts(proj, shift_cols):
        w = proj.shape[1]
        starts = np.arange(rows.prompt_blocks) * rows.rb
        last = proj[rows.rb - 1:rows.n_prompt:rows.rb]
        prev = jnp.concatenate([jnp.zeros((1, w), F32), last[:-1]], axis=0)
        prev = jnp.where((starts % t == 0)[:, None], 0.0, prev)
        fp = jnp.zeros((rows.prompt_blocks, rows.slots, w), F32).at[:, 0].set(prev)
        return jnp.concatenate([fp.reshape(-1, w), shift_cols], axis=0)[:, None, :]

    shift_tail = jnp.pad(s_shift[:, 3 * rw:], ((0, 0), (0, tail_w - tail_valid)))
    first_rkv = firsts(proj_rkv, s_shift[:, :3 * rw])
    first_tail = firsts(proj_tail, shift_tail)
    mu = p['shift_mu']
    mu_tail = jnp.pad(mu[3 * rw:], (0, tail_w - tail_valid)).reshape(1, tail_w)
    g2p = jnp.pad(p['g2'], ((0, tail_w - w_lora - a_lora - g_lora), (0, 0)))
    vec = lambda a: a.reshape(1, rw)
    r_, w_, k_, v_, a_, b_, g_, bonus = _rwkv_prep(
        rows, proj_rkv, proj_tail, first_rkv, first_tail, mu[:3 * rw].reshape(1, 3 * rw), mu_tail,
        vec(p['w0']), p['w2'], vec(p['a0']), p['a2'], g2p, vec(p['k_k']), vec(p['k_a']), vec(p['r_k']),
        rw, rw_hd, g_lora)
    ops = (r_, w_, k_, v_, a_, b_)
    o_rw_p, sw_p = _rwkv_scan(ops, nb, t, 0, 256, rw_heads, rw_hd, None)
    o_rw_s, sw_s = _rwkv_scan(ops, db, dt, nb * t, dt, rw_heads, rw_hd, s_rwkv)
    o_rw = _rwkv_post(jnp.concatenate([o_rw_p, o_rw_s], axis=0), bonus, g_, vec(p['lnx_g']), vec(p['lnx_b']), rw_hd)

    mix = _matmul_out(o_ret, o_rw, p['w_out'])
    x1, h2 = _postmix(rows, x, mix, p['post_mix_g'], p['pre_ffn_g'], mod_ext, d)

    heads, _, nk, kd = p['peer_sub_keys'].shape
    q = _matmul(h2, p['peer_wq'], col_block0=0, n_out=heads * 2 * kd, tm=tm, tn=512, name="peer_q")
    s0, s1, e0, e1, thr = _peer_topk(q, p['peer_sub_keys'])
    eb = 512
    ni = eb // nk
    peer = _peer_dense(h2, p['peer_u'].T.astype(BF16), p['peer_v'].astype(BF16),
                       _by_expert_block(s0, heads, nk, ni), _by_expert_block(e0, heads, nk, ni),
                       s1, e1, thr[:, :, 0].T, heads, nk, eb=eb)
    y = _final(rows, x1, peer, p['post_ffn_g'], mod_ext, d)

    n_p = nb * t
    last_p = slice(t - 1, n_p, t)
    last_s = slice(n_p + dt - 1, None, dt)
    pw_last = lambda idx: jnp.concatenate([proj_rkv[idx], proj_tail[idx, :tail_valid]], axis=1)
    return (y[:n_p].reshape(nb, t, d), y[n_p:].reshape(db, dt, d), sr_p, sr_s, sw_p, sw_s,
            pw_last(last_p), pw_last(last_s))


def kernel(x_prompt, x_sample, state_ret, state_rwkv, state_shift, c_prompt, c_sample, ada_w, ada_b, pre_mix_g, post_mix_g, pre_ffn_g, post_ffn_g, w_in, shift_mu, w0, w2, a0, a2, g2, k_k, k_a, r_k, lnx_g, lnx_b, w_out, peer_wq, peer_sub_keys, peer_u, peer_v):
    depth = ada_w.shape[0]
    assert depth == 1, "prompt and sample tokens are stacked per layer; deeper stacks need per-layer restacking"
    prm = dict(ada_w=ada_w[0], ada_b=ada_b[0], pre_mix_g=pre_mix_g[0], post_mix_g=post_mix_g[0],
               pre_ffn_g=pre_ffn_g[0], post_ffn_g=post_ffn_g[0], w_in=w_in[0], shift_mu=shift_mu[0],
               w0=w0[0], w2=w2[0], a0=a0[0], a2=a2[0], g2=g2[0], k_k=k_k[0], k_a=k_a[0], r_k=r_k[0],
               lnx_g=lnx_g[0], lnx_b=lnx_b[0], w_out=w_out[0], peer_wq=peer_wq[0],
               peer_sub_keys=peer_sub_keys[0], peer_u=peer_u[0], peer_v=peer_v[0])
    yp, ys, sr_p, sr_s, sw_p, sw_s, ss_p, ss_s = _layer(
        x_prompt, x_sample, state_ret[0], state_rwkv[0], state_shift[0], c_prompt, c_sample, prm)
    sd, wd, hd = state_ret.dtype, state_rwkv.dtype, state_shift.dtype
    return (yp, ys, sr_p[None].astype(sd), sr_s[None].astype(sd), sw_p[None].astype(wd), sw_s[None].astype(wd),
            ss_p[None].astype(hd), ss_s[None].astype(hd))
```

```python
import functools
import math

import numpy as np
import jax
import jax.numpy as jnp
from jax import lax
from jax.experimental import pallas as pl
from jax.experimental.pallas import tpu as pltpu

PAST_LEN = 16384
RET_CHUNK = 128
ROPE_BASE = 10000.0
LNX_EPS = 64e-5
NORM_EPS = 1e-6
PEER_TOPK = 16

LANES = 128
ROW_BLOCK = 128
VMEM_LIMIT = 56 * 1024 * 1024

HI = lax.Precision.HIGHEST
F32 = jnp.float32
BF16 = jnp.bfloat16


def _pick(n, target, mult=LANES):
    best = None
    for c in range(mult, min(n, target) + 1, mult):
        if n % c == 0:
            best = c
    assert best is not None, (n, target, mult)
    return best


def _params(sem, vmem=VMEM_LIMIT):
    return pltpu.CompilerParams(dimension_semantics=sem, vmem_limit_bytes=vmem)


def _dot(a, b, precision=None):
    return jnp.dot(a, b, preferred_element_type=F32, precision=precision)


def _dot_nt(a, b, precision=None):
    return lax.dot_general(a, b, (((1,), (1,)), ((), ())), preferred_element_type=F32, precision=precision)


def _dot_tn(a, b, precision=None):
    return lax.dot_general(a, b, (((0,), (0,)), ((), ())), preferred_element_type=F32, precision=precision)


def _block_ones(n, seg, dtype):
    r = lax.broadcasted_iota(jnp.int32, (n, n), 0) // seg
    c = lax.broadcasted_iota(jnp.int32, (n, n), 1) // seg
    return (r == c).astype(dtype)


def _ada_kernel(c_ref, w_ref, b_ref, o_ref):
    c = c_ref[...]
    sc = (c * jax.nn.sigmoid(c)).astype(BF16)
    o_ref[...] = _dot(sc, w_ref[...].astype(BF16)) + b_ref[...]


def _ada(c_all, ada_w, ada_b, tn=512):
    m, d = c_all.shape
    n = ada_w.shape[1]
    return pl.pallas_call(
        _ada_kernel,
        grid=(n // tn,),
        in_specs=[pl.BlockSpec((m, d), lambda j: (0, 0)),
                  pl.BlockSpec((d, tn), lambda j: (0, j)),
                  pl.BlockSpec((1, tn), lambda j: (0, j))],
        out_specs=pl.BlockSpec((m, tn), lambda j: (0, j)),
        out_shape=jax.ShapeDtypeStruct((m, n), F32),
        compiler_params=_params(("arbitrary",)),
        name="ada",
    )(c_all, ada_w, ada_b.reshape(1, n))


def _expand_slots(m, rows):
    slots, _, d = m.shape
    return jnp.broadcast_to(m, (slots, rows // slots, d)).reshape(rows, d)


def _rms(x, g):
    return x * lax.rsqrt(jnp.mean(x * x, axis=-1, keepdims=True) + NORM_EPS) * g


def _prenorm_kernel(x_ref, g_ref, sh_ref, sc_ref, h_ref):
    x = x_ref[...]
    rows = x.shape[0]
    y = _rms(x, g_ref[...])
    h_ref[...] = (y * (1.0 + _expand_slots(sc_ref[...], rows)) + _expand_slots(sh_ref[...], rows)).astype(h_ref.dtype)


def _postmix_kernel(x_ref, m_ref, gpost_ref, gpre_ref, gate_ref, sh_ref, sc_ref, x1_ref, h_ref):
    x = x_ref[...]
    rows = x.shape[0]
    x1 = x + _expand_slots(gate_ref[...], rows) * _rms(m_ref[...], gpost_ref[...])
    x1_ref[...] = x1
    y = _rms(x1, gpre_ref[...])
    h_ref[...] = (y * (1.0 + _expand_slots(sc_ref[...], rows)) + _expand_slots(sh_ref[...], rows)).astype(h_ref.dtype)


def _final_kernel(x_ref, m_ref, gpost_ref, gate_ref, y_ref):
    x = x_ref[...]
    rows = x.shape[0]
    y_ref[...] = x + _expand_slots(gate_ref[...], rows) * _rms(m_ref[...], gpost_ref[...])


class _Rows:
    def __init__(self, nb, t, db, dt):
        self.n_prompt = nb * t
        self.n_sample = db * dt
        self.n = self.n_prompt + self.n_sample
        self.rb = ROW_BLOCK
        assert t % self.rb == 0 and self.rb % dt == 0 and self.n_sample % self.rb == 0
        self.slots = self.rb // dt
        self.prompt_blocks = self.n_prompt // self.rb
        self.blocks_per_seq = t // self.rb
        self.n_blocks = self.n // self.rb
        self.nb, self.t, self.db, self.dt = nb, t, db, dt

    def slot_block(self, i):
        return jnp.where(i < self.prompt_blocks, i // self.blocks_per_seq, i - self.prompt_blocks + self.nb)

    def extend(self, m):
        mp = jnp.repeat(m[: self.nb], self.slots, axis=0)
        return jnp.concatenate([mp, m[self.nb:]], axis=0)[:, None, :]


def _mod_spec(rows, d, chunk):
    return pl.BlockSpec((rows.slots, 1, d), lambda i: (rows.slot_block(i), 0, chunk))


def _prenorm(rows, x, g, mod_ext, d):
    row = pl.BlockSpec((rows.rb, d), lambda i: (i, 0))
    vec = pl.BlockSpec((1, d), lambda i: (0, 0))
    return pl.pallas_call(
        _prenorm_kernel,
        grid=(rows.n_blocks,),
        in_specs=[row, vec, _mod_spec(rows, d, 0), _mod_spec(rows, d, 1)],
        out_specs=row,
        out_shape=jax.ShapeDtypeStruct((rows.n, d), BF16),
        compiler_params=_params(("parallel",)),
        name="prenorm",
    )(x, g.reshape(1, d), mod_ext, mod_ext)


def _postmix(rows, x, mix, gpost, gpre, mod_ext, d):
    row = pl.BlockSpec((rows.rb, d), lambda i: (i, 0))
    vec = pl.BlockSpec((1, d), lambda i: (0, 0))
    return pl.pallas_call(
        _postmix_kernel,
        grid=(rows.n_blocks,),
        in_specs=[row, row, vec, vec, _mod_spec(rows, d, 2), _mod_spec(rows, d, 3), _mod_spec(rows, d, 4)],
        out_specs=[row, row],
        out_shape=[jax.ShapeDtypeStruct((rows.n, d), F32), jax.ShapeDtypeStruct((rows.n, d), BF16)],
        compiler_params=_params(("parallel",)),
        name="postmix",
    )(x, mix, gpost.reshape(1, d), gpre.reshape(1, d), mod_ext, mod_ext, mod_ext)


def _final(rows, x1, peer, gpost, mod_ext, d):
    row = pl.BlockSpec((rows.rb, d), lambda i: (i, 0))
    vec = pl.BlockSpec((1, d), lambda i: (0, 0))
    return pl.pallas_call(
        _final_kernel,
        grid=(rows.n_blocks,),
        in_specs=[row, row, vec, _mod_spec(rows, d, 5)],
        out_specs=row,
        out_shape=jax.ShapeDtypeStruct((rows.n, d), F32),
        compiler_params=_params(("parallel",)),
        name="final",
    )(x1, peer, gpost.reshape(1, d), mod_ext)


def _mm_kernel(a_ref, w_ref, o_ref):
    o_ref[...] = _dot(a_ref[...], w_ref[...].astype(BF16)).astype(o_ref.dtype)


def _matmul(a, w, *, col_block0, n_out, tm, tn, name):
    m, k = a.shape
    return pl.pallas_call(
        _mm_kernel,
        grid=(pl.cdiv(n_out, tn), m // tm),
        in_specs=[pl.BlockSpec((tm, k), lambda j, i: (i, 0)),
                  pl.BlockSpec((k, tn), lambda j, i: (0, j + col_block0))],
        out_specs=pl.BlockSpec((tm, tn), lambda j, i: (i, j)),
        out_shape=jax.ShapeDtypeStruct((m, n_out), F32),
        compiler_params=_params(("parallel", "parallel")),
        name=name,
    )(a, w)


def _mm2_kernel(a_ref, b_ref, wa_ref, wb_ref, o_ref):
    o_ref[...] = _dot(a_ref[...], wa_ref[...].astype(BF16)) + _dot(b_ref[...], wb_ref[...].astype(BF16))


def _matmul_out(o_ret, o_rw, w_out, tn=512):
    m, ka = o_ret.shape
    tm = _pick(m, 1024)
    kb = o_rw.shape[1]
    n = w_out.shape[1]
    tn = _pick(n, tn)
    assert ka == kb
    return pl.pallas_call(
        _mm2_kernel,
        grid=(n // tn, m // tm),
        in_specs=[pl.BlockSpec((tm, ka), lambda j, i: (i, 0)),
                  pl.BlockSpec((tm, kb), lambda j, i: (i, 0)),
                  pl.BlockSpec((ka, tn), lambda j, i: (0, j)),
                  pl.BlockSpec((kb, tn), lambda j, i: (1, j))],
        out_specs=pl.BlockSpec((tm, tn), lambda j, i: (i, j)),
        out_shape=jax.ShapeDtypeStruct((m, n), F32),
        compiler_params=_params(("parallel", "parallel")),
        name="out_proj",
    )(o_ret, o_rw, w_out, w_out)


def _rope_tables(pos, hd):
    half = hd // 2
    inv = ROPE_BASE ** (-jnp.arange(half, dtype=F32) / half)
    ang = pos.astype(F32)[:, None] * inv[None, :]
    cos, sin = jnp.cos(ang), jnp.sin(ang)
    return jnp.concatenate([cos, cos], axis=-1), jnp.concatenate([-sin, sin], axis=-1)


def _rope(x, cos, sin_signed):
    half = x.shape[-1] // 2
    return x * cos + pltpu.roll(x, half, 1) * sin_signed


def _ret_finish(o, gate):
    o = o * lax.rsqrt(jnp.mean(o * o, axis=-1, keepdims=True) + NORM_EPS)
    return o * (gate * jax.nn.sigmoid(gate))


def _ret_prompt_kernel(q_ref, k_ref, v_ref, g_ref, cos_ref, sin_ref, lg_ref, o_ref, s_ref, *, chunk):
    t, hd = q_ref.shape
    lg = lg_ref[0, 0:1, :]
    ri = lax.broadcasted_iota(jnp.int32, (chunk, chunk), 0)
    ci = lax.broadcasted_iota(jnp.int32, (chunk, chunk), 1)
    causal = ri >= ci
    diff = jnp.where(causal, ri - ci, 0).astype(F32)
    mask = jnp.where(causal, jnp.exp(diff * lg), 0.0)
    rowf = lax.broadcasted_iota(jnp.int32, (chunk, hd), 0).astype(F32)
    q_dec = jnp.exp((rowf + 1.0) * lg)
    k_dec = jnp.exp((chunk - 1.0 - rowf) * lg)
    c_dec = jnp.exp(float(chunk) * lg)

    def step(c, s):
        r0 = pl.multiple_of(c * chunk, chunk)
        cos = cos_ref[pl.ds(r0, chunk), :]
        sin = sin_ref[pl.ds(r0, chunk), :]
        qc = _rope(q_ref[pl.ds(r0, chunk), :], cos, sin)
        kc = _rope(k_ref[pl.ds(r0, chunk), :], cos, sin) * (hd ** -0.5)
        vc = v_ref[pl.ds(r0, chunk), :]
        att = _dot_nt(qc, kc, HI) * mask
        o = _dot(att, vc, HI) + _dot(qc * q_dec, s, HI)
        s = s * c_dec + _dot_tn(kc * k_dec, vc, HI)
        o_ref[pl.ds(r0, chunk), :] = _ret_finish(o, g_ref[pl.ds(r0, chunk), :]).astype(o_ref.dtype)
        return s

    s_ref[0, 0] = lax.fori_loop(0, t // chunk, step, jnp.zeros((hd, hd), F32))


def _ret_sample_kernel(q_ref, k_ref, v_ref, g_ref, cos_ref, sin_ref, lg_ref, s0_ref, o_ref, s_ref, *, dt):
    rows, hd = q_ref.shape
    nseq = rows // dt
    lg = lg_ref[0, 0:1, :]
    ri = lax.broadcasted_iota(jnp.int32, (rows, rows), 0)
    ci = lax.broadcasted_iota(jnp.int32, (rows, rows), 1)
    ok = (ri >= ci) & ((ri // dt) == (ci // dt))
    diff = jnp.where(ok, ri - ci, 0).astype(F32)
    mask = jnp.where(ok, jnp.exp(diff * lg), 0.0)
    rown = lax.broadcasted_iota(jnp.int32, (rows, hd), 0)
    posf = (rown % dt).astype(F32)
    q_dec = jnp.exp((posf + 1.0) * lg)
    k_dec = jnp.exp((dt - 1.0 - posf) * lg)
    c_dec = jnp.exp(float(dt) * lg)
    cos, sin = cos_ref[...], sin_ref[...]
    qc = _rope(q_ref[...], cos, sin)
    kc = _rope(k_ref[...], cos, sin) * (hd ** -0.5)
    vc = v_ref[...]
    inner = _dot(_dot_nt(qc, kc, HI) * mask, vc, HI)
    qd = qc * q_dec
    kd = kc * k_dec
    seq = rown // dt
    cross = jnp.zeros((rows, hd), F32)
    for s in range(nseq):
        s0 = s0_ref[s, 0]
        mine = seq == s
        cross = cross + _dot(jnp.where(mine, qd, 0.0), s0, HI)
        s_ref[s, 0] = s0 * c_dec + _dot_tn(jnp.where(mine, kd, 0.0), vc, HI)
    o_ref[...] = _ret_finish(inner + cross, g_ref[...]).astype(o_ref.dtype)


def _retention(rows, proj, state_ret, ret_heads, hd):
    nb, t, db, dt = rows.nb, rows.t, rows.db, rows.dt
    lg = jnp.log1p(-jnp.exp2(-5.0 - jnp.arange(ret_heads, dtype=F32)))
    lg_tab = jnp.broadcast_to(lg[:, None, None], (ret_heads, 8, hd))
    lg_spec = pl.BlockSpec((1, 8, hd), lambda b, h: (h, 0, 0))
    assert hd == RET_CHUNK and t % RET_CHUNK == 0 and dt % RET_CHUNK != 0

    cos_p, sin_p = _rope_tables(jnp.arange(t, dtype=jnp.int32), hd)
    col = lambda part: pl.BlockSpec((t, hd), lambda b, h: (b, h + part * ret_heads))
    tab = pl.BlockSpec((t, hd), lambda b, h: (0, 0))
    o_p, s_p = pl.pallas_call(
        functools.partial(_ret_prompt_kernel, chunk=RET_CHUNK),
        grid=(nb, ret_heads),
        in_specs=[col(0), col(1), col(2), col(3), tab, tab, lg_spec],
        out_specs=[pl.BlockSpec((t, hd), lambda b, h: (b, h)),
                   pl.BlockSpec((1, 1, hd, hd), lambda b, h: (b, h, 0, 0))],
        out_shape=[jax.ShapeDtypeStruct((nb * t, ret_heads * hd), BF16),
                   jax.ShapeDtypeStruct((nb, ret_heads, hd, hd), F32)],
        compiler_params=_params(("parallel", "parallel")),
        name="ret_prompt",
    )(proj, proj, proj, proj, cos_p, sin_p, lg_tab)

    rb = RET_CHUNK
    nseq = rb // dt
    cos_s, sin_s = _rope_tables(PAST_LEN + jnp.arange(dt, dtype=jnp.int32), hd)
    cos_s, sin_s = jnp.tile(cos_s, (nseq, 1)), jnp.tile(sin_s, (nseq, 1))
    off = (nb * t) // rb
    col = lambda part: pl.BlockSpec((rb, hd), lambda b, h: (b + off, h + part * ret_heads))
    tab = pl.BlockSpec((rb, hd), lambda b, h: (0, 0))
    st = pl.BlockSpec((nseq, 1, hd, hd), lambda b, h: (b, h, 0, 0))
    o_s, s_s = pl.pallas_call(
        functools.partial(_ret_sample_kernel, dt=dt),
        grid=(db // nseq, ret_heads),
        in_specs=[col(0), col(1), col(2), col(3), tab, tab, lg_spec, st],
        out_specs=[pl.BlockSpec((rb, hd), lambda b, h: (b, h)), st],
        out_shape=[jax.ShapeDtypeStruct((db * dt, ret_heads * hd), BF16),
                   jax.ShapeDtypeStruct((db, ret_heads, hd, hd), F32)],
        compiler_params=_params(("parallel", "parallel")),
        name="ret_sample",
    )(proj, proj, proj, proj, cos_s, sin_s, lg_tab, state_ret)
    return jnp.concatenate([o_p, o_s], axis=0), s_p, s_s


def _shifted(x, first, tseq_mask):
    prev = pltpu.roll(x, 1, 0)
    return jnp.where(tseq_mask, first, prev)


def _rwkv_prep_kernel(pr_ref, pk_ref, pv_ref, pt_ref, fr_ref, fk_ref, fv_ref, ft_ref,
                      mur_ref, muk_ref, muv_ref, mut_ref, w0_ref, w2_ref, a0_ref, a2_ref, g2_ref,
                      kk_ref, ka_ref, rk_ref,
                      r_out, w_out, k_out, v_out, a_out, b_out, g_out, bonus_out,
                      *, dt, prompt_blocks, w_lora, a_lora, g_lora, hd):
    i = pl.program_id(0)
    rows, cn = pr_ref.shape
    rown = lax.broadcasted_iota(jnp.int32, (rows, 1), 0)
    sample_block = jnp.zeros((rows, 1), jnp.int32) + (i >= prompt_blocks).astype(jnp.int32)
    is_first = (rown == 0) | ((rown % dt == 0) & (sample_block == 1))

    def lerp(p_ref, f_ref, mu_ref):
        p = p_ref[...]
        prev = _shifted(p, _expand_slots(f_ref[...], rows), is_first)
        return p + mu_ref[...] * (prev - p)

    r = lerp(pr_ref, fr_ref, mur_ref)
    kw = lerp(pk_ref, fk_ref, muk_ref)
    vw = lerp(pv_ref, fv_ref, muv_ref)
    tail = lerp(pt_ref, ft_ref, mut_ref)
    tw = tail.shape[1]
    wl = tail[:, :w_lora]
    al = tail[:, w_lora:w_lora + a_lora]
    glane = lax.broadcasted_iota(jnp.int32, (rows, tw - w_lora - a_lora), 1)
    gl = jnp.where(glane < g_lora, jax.nn.sigmoid(tail[:, w_lora + a_lora:]), 0.0)

    w_log = -jax.nn.softplus(-(w0_ref[...] + _dot(jnp.tanh(wl), w2_ref[...], HI))) - 0.5
    decay = jnp.exp(-jnp.exp(w_log))
    a = jax.nn.sigmoid(a0_ref[...] + _dot(al, a2_ref[...], HI))
    g = _dot(gl, g2_ref[...], HI)

    seg = _block_ones(LANES, hd, F32)

    def segsum(x):
        return jnp.concatenate([_dot(x[:, c:c + LANES], seg, HI) for c in range(0, cn, LANES)], axis=1)

    kk = kw * kk_ref[...]
    kk = kk / jnp.maximum(jnp.sqrt(segsum(kk * kk)), 1e-12)
    kn = kw * (1.0 + (a - 1.0) * ka_ref[...])
    r_out[...] = r
    w_out[...] = decay
    k_out[...] = kn
    v_out[...] = vw
    a_out[...] = -kk
    b_out[...] = kk * a
    g_out[...] = g
    bonus_out[...] = segsum(r * kn * rk_ref[...]) * vw


def _rwkv_prep(rows, proj_rkv, proj_tail, first_rkv, first_tail, mu_rkv, mu_tail, w0, w2, a0, a2, g2p,
               k_k, k_a, r_k_flat, rw, hd, g_lora, cn=512):
    n = rows.n
    cn = _pick(rw, cn)
    tw = proj_tail.shape[1]
    nblk = (rows.n_blocks, rw // cn)
    part = lambda p: pl.BlockSpec((rows.rb, cn), lambda i, j: (i, j + p * (rw // cn)))
    fpart = lambda p: pl.BlockSpec((rows.slots, 1, cn), lambda i, j: (i, 0, j + p * (rw // cn)))
    mupart = lambda p: pl.BlockSpec((1, cn), lambda i, j: (0, j + p * (rw // cn)))
    vec = pl.BlockSpec((1, cn), lambda i, j: (0, j))
    lora = lambda k: pl.BlockSpec((k, cn), lambda i, j: (0, j))
    out = pl.BlockSpec((rows.rb, cn), lambda i, j: (i, j))
    kern = functools.partial(_rwkv_prep_kernel, dt=rows.dt, prompt_blocks=rows.prompt_blocks,
                             w_lora=w2.shape[0], a_lora=a2.shape[0], g_lora=g_lora, hd=hd)
    return pl.pallas_call(
        kern,
        grid=nblk,
        in_specs=[part(0), part(1), part(2), pl.BlockSpec((rows.rb, tw), lambda i, j: (i, 0)),
                  fpart(0), fpart(1), fpart(2), pl.BlockSpec((rows.slots, 1, tw), lambda i, j: (i, 0, 0)),
                  mupart(0), mupart(1), mupart(2), pl.BlockSpec((1, tw), lambda i, j: (0, 0)),
                  vec, lora(w2.shape[0]), vec, lora(a2.shape[0]), lora(g2p.shape[0]),
                  vec, vec, vec],
        out_specs=[out] * 8,
        out_shape=[jax.ShapeDtypeStruct((n, rw), F32)] * 8,
        compiler_params=_params(("parallel", "arbitrary")),
        name="rwkv_prep",
    )(proj_rkv, proj_rkv, proj_rkv, proj_tail, first_rkv, first_rkv, first_rkv, first_tail,
      mu_rkv, mu_rkv, mu_rkv, mu_tail, w0, w2, a0, a2, g2p, k_k, k_a, r_k_flat)


def _rwkv_scan_kernel(r_ref, w_ref, k_ref, v_ref, a_ref, b_ref, *rest, hd, has_state, n_chains):
    if has_state:
        s0_ref, o_ref, sout_ref, st_ref, ob_ref, vs_ref = rest
    else:
        o_ref, sout_ref, st_ref, ob_ref, vs_ref = rest
        s0_ref = None
    tb = pl.program_id(1)
    tc, width = r_ref.shape
    pw = 2 * hd
    npairs = width // pw

    @pl.when(tb == 0)
    def _():
        for p in range(npairs):
            if has_state:
                st_ref[p] = jnp.concatenate([s0_ref[0, 2 * p], s0_ref[0, 2 * p + 1]], axis=1)
            else:
                st_ref[p] = jnp.zeros((hd, pw), F32)

    ones_bf = _block_ones(pw, hd, BF16)
    sub = lax.broadcasted_iota(jnp.int32, (hd, pw), 0)
    lane = lax.broadcasted_iota(jnp.int32, (hd, pw), 1)
    diags = [(((lane % hd) - sub + hd) % hd == i).astype(F32).astype(BF16) for i in range(3)]
    lane8 = lax.broadcasted_iota(jnp.int32, (8, pw), 1)
    sel = lax.broadcasted_iota(jnp.int32, (8, pw), 0) == (lane8 // hd)
    chains = [range(c * npairs // n_chains, (c + 1) * npairs // n_chains) for c in range(n_chains)]

    def shift_in_head(x, i):
        return jnp.where(lane8 % hd >= i, pltpu.roll(x, i, 1), pltpu.roll(x, pw - hd + i, 1))

    def group(tg, carry):
        r0 = pl.multiple_of(tg * 8, 8)
        rows8 = pl.ds(r0, 8)
        for p in range(npairs):
            cols = pl.ds(p * pw, pw)
            v8 = v_ref[rows8, cols]
            v1 = v8.astype(BF16).astype(F32)
            v2 = (v8 - v1).astype(BF16).astype(F32)
            vs_ref[0, :, cols] = v1
            vs_ref[1, :, cols] = shift_in_head(v2, 1)
            vs_ref[2, :, cols] = shift_in_head((v8 - v1) - v2, 2)

        def tile_row(ref, p, j):
            return ref[rows8, pl.ds(p * pw, pw)][j:j + 1]

        for j in range(8):
            for chain in chains:
                old = [st_ref[p] for p in chain]
                pa = jnp.concatenate([s * tile_row(a_ref, p, j) for s, p in zip(old, chain)], axis=0)
                sa = _dot(pa.astype(BF16), ones_bf)
                xs = []
                for p in chain:
                    x = None
                    for i in range(3):
                        term = diags[i] * vs_ref[i, :, pl.ds(p * pw, pw)][j:j + 1].astype(BF16)
                        x = term if x is None else x + term
                    xs.append(x)
                vc = _dot(jnp.concatenate(xs, axis=0), ones_bf)
                for i, p in enumerate(chain):
                    blk = slice(i * hd, (i + 1) * hd)
                    s = (old[i] * tile_row(w_ref, p, j) + sa[blk] * tile_row(b_ref, p, j)
                         + vc[blk] * tile_row(k_ref, p, j))
                    st_ref[p] = s
                    rsel = jnp.where(sel, tile_row(r_ref, p, j), 0.0).astype(BF16)
                    o8 = _dot_nt(rsel, s.astype(BF16))
                    ob_ref[p, j:j + 1, 0:hd] = o8[0:1, :]
                    ob_ref[p, j:j + 1, hd:pw] = o8[1:2, :]
        for p in range(npairs):
            o_ref[rows8, pl.ds(p * pw, pw)] = ob_ref[p]
        return carry

    lax.fori_loop(0, tc // 8, group, 0)

    @pl.when(tb == pl.num_programs(1) - 1)
    def _():
        for p in range(npairs):
            s = st_ref[p]
            sout_ref[0, 2 * p] = s[:, :hd]
            sout_ref[0, 2 * p + 1] = s[:, hd:]


def _rwkv_scan(ops, nseq, t, row0, tc, heads, hd, state):
    width = heads * hd
    assert t % tc == 0 and row0 % tc == 0
    nt = t // tc
    blk = pl.BlockSpec((tc, width), lambda s, j: (row0 // tc + s * nt + j, 0))
    st = pl.BlockSpec((1, heads, hd, hd), lambda s, j: (s, 0, 0, 0))
    in_specs = [blk] * 6
    args = list(ops)
    if state is not None:
        in_specs.append(st)
        args.append(state)
    return pl.pallas_call(
        functools.partial(_rwkv_scan_kernel, hd=hd, has_state=state is not None, n_chains=1),
        grid=(nseq, nt),
        in_specs=in_specs,
        out_specs=[pl.BlockSpec((tc, width), lambda s, j: (s * nt + j, 0)), st],
        out_shape=[jax.ShapeDtypeStruct((nseq * t, width), F32),
                   jax.ShapeDtypeStruct((nseq, heads, hd, hd), F32)],
        scratch_shapes=[pltpu.VMEM((heads // 2, hd, 2 * hd), F32), pltpu.VMEM((heads // 2, 8, 2 * hd), F32),
                        pltpu.VMEM((3, 8, width), F32)],
        compiler_params=_params(("parallel", "arbitrary")),
        name="rwkv_scan_state" if state is not None else "rwkv_scan",
    )(*args)


def _rwkv_post_kernel(o_ref, bonus_ref, g_ref, lg_ref, lb_ref, out_ref, *, hd):
    o = o_ref[...]
    cn = o.shape[1]
    seg = _block_ones(LANES, hd, F32)

    def segmean(x):
        return jnp.concatenate([_dot(x[:, c:c + LANES], seg, HI) for c in range(0, cn, LANES)], axis=1) * (1.0 / hd)

    mean = segmean(o)
    cen = o - mean
    var = segmean(cen * cen)
    y = cen * lax.rsqrt(var + LNX_EPS) * lg_ref[...] + lb_ref[...]
    out_ref[...] = ((y + bonus_ref[...]) * g_ref[...]).astype(out_ref.dtype)


def _rwkv_post(o_raw, bonus, g, lnx_g, lnx_b, hd, rb=256, cn=512):
    n, rw = o_raw.shape
    rb, cn = _pick(n, rb), _pick(rw, cn)
    blk = pl.BlockSpec((rb, cn), lambda i, j: (i, j))
    vec = pl.BlockSpec((1, cn), lambda i, j: (0, j))
    return pl.pallas_call(
        functools.partial(_rwkv_post_kernel, hd=hd),
        grid=(n // rb, rw // cn),
        in_specs=[blk, blk, blk, vec, vec],
        out_specs=blk,
        out_shape=jax.ShapeDtypeStruct((n, rw), BF16),
        compiler_params=_params(("parallel", "parallel")),
        name="rwkv_post",
    )(o_raw, bonus, g, lnx_g, lnx_b)


def _mmt_kernel(wt_ref, x_ref, o_ref):
    o_ref[...] = _dot(wt_ref[...].astype(BF16), x_ref[...])


def _matmul_t(wt, xt, tm=512, tn=1024, name="matmul_t"):
    m, k = wt.shape
    n = xt.shape[1]
    tm, tn = _pick(m, tm), _pick(n, tn)
    return pl.pallas_call(
        _mmt_kernel,
        grid=(n // tn, m // tm),
        in_specs=[pl.BlockSpec((tm, k), lambda j, i: (i, 0)),
                  pl.BlockSpec((k, tn), lambda j, i: (0, j))],
        out_specs=pl.BlockSpec((tm, tn), lambda j, i: (i, j)),
        out_shape=jax.ShapeDtypeStruct((m, n), F32),
        compiler_params=_params(("parallel", "parallel")),
        name=name,
    )(wt, xt)


def _take_top(s, k):
    n = s.shape[0]
    row = lax.broadcasted_iota(jnp.int32, s.shape, 0).astype(F32)
    out = []
    for _ in range(k):
        m = jnp.max(s, axis=0, keepdims=True)
        first = jnp.min(jnp.where(s == m, row, float(n)), axis=0, keepdims=True)
        s = jnp.where(row == first, -jnp.inf, s)
        out.append(m)
    return out


def _peer_topk_kernel(q_ref, keys_ref, s0_ref, s1_ref, e0_ref, e1_ref, thr_ref, *, topk):
    kd = keys_ref.shape[3]
    cols = q_ref.shape[1]
    s0 = _dot(keys_ref[0, 0], q_ref[:kd, :], HI)
    s1 = _dot(keys_ref[0, 1], q_ref[kd:, :], HI)
    top0 = _take_top(s0, topk)
    top1 = _take_top(s1, topk)
    rowk = lax.broadcasted_iota(jnp.int32, (topk, cols), 0)
    t1 = jnp.zeros((topk, cols), F32)
    for b in range(topk):
        t1 = jnp.where(rowk == b, top1[b], t1)
    cand = jnp.concatenate([top0[a] + t1 for a in range(topk)], axis=0)
    best = _take_top(cand, topk)
    z = jnp.zeros((1, cols), F32)
    for b in best:
        z = z + jnp.exp(b - best[0])
    s0_ref[...] = s0
    s1_ref[...] = s1
    e0_ref[...] = jnp.exp(s0 - top0[0]) / z
    e1_ref[...] = jnp.exp(s1 - top1[0])
    thr_ref[0] = jnp.broadcast_to(best[topk - 1], thr_ref.shape[1:])


def _peer_topk(qt, sub_keys, tc=256):
    n = qt.shape[1]
    tc = _pick(n, tc)
    heads, _, nk, kd = sub_keys.shape
    blk = pl.BlockSpec((nk, tc), lambda i, h: (h, i))
    shp = jax.ShapeDtypeStruct((heads * nk, n), F32)
    return pl.pallas_call(
        functools.partial(_peer_topk_kernel, topk=PEER_TOPK),
        grid=(n // tc, heads),
        in_specs=[pl.BlockSpec((2 * kd, tc), lambda i, h: (h, i)),
                  pl.BlockSpec((1, 2, nk, kd), lambda i, h: (h, 0, 0, 0))],
        out_specs=[blk] * 4 + [pl.BlockSpec((1, 8, tc), lambda i, h: (h, 0, i))],
        out_shape=[shp] * 4 + [jax.ShapeDtypeStruct((heads, 8, n), F32)],
        compiler_params=_params(("parallel", "parallel")),
        name="peer_topk",
    )(qt, sub_keys)


def _gelu(x):
    return 0.5 * x * (1.0 + lax.erf(x * (2.0 ** -0.5)))


def _peer_dense_kernel(u_ref, ht_ref, vt_ref, s0_ref, e0_ref, s1_ref, e1_ref, thr_ref, o_ref, act_ref, coef_ref,
                       *, heads, nk, jc):
    e = pl.program_id(1)
    eb = u_ref.shape[0]
    d = vt_ref.shape[0]
    ni = eb // nk
    dq = d // ni

    @pl.when(e == 0)
    def _():
        o_ref[...] = jnp.zeros_like(o_ref)
        coef_ref[1] = jnp.zeros(coef_ref.shape[1:], coef_ref.dtype)

    act_ref[...] = _dot(u_ref[...], ht_ref[...])
    cur = e % 2
    prev = (e + 1) % 2
    thr = thr_ref[...]

    def one_i(ii, carry):
        orow = pl.ds(pl.multiple_of(ii * dq, dq), dq)
        o_ref[orow, :] += _dot(vt_ref[orow, :], coef_ref[prev])
        s0 = s0_ref[0, ii]
        e0 = e0_ref[0, ii]
        for j0 in range(0, nk, jc):
            g = None
            for h in range(heads):
                jrows = slice(h * nk + j0, h * nk + j0 + jc)
                hit = (s0[h:h + 1] + s1_ref[jrows, :]) >= thr[h:h + 1]
                term = jnp.where(hit, e0[h:h + 1] * e1_ref[jrows, :], 0.0)
                g = term if g is None else g + term
            arow = pl.ds(pl.multiple_of(ii * nk + j0, jc), jc)
            coef_ref[cur, arow, :] = (g * _gelu(act_ref[arow, :])).astype(coef_ref.dtype)
        return carry

    lax.fori_loop(0, ni, one_i, 0)


def _peer_dense(ht, u, vt, s0r, e0r, s1, e1, thr, heads, nk, tp=512, eb=512):
    d, n = ht.shape
    tp = _pick(n, tp)
    nblk = u.shape[0] // eb
    ni = eb // nk
    once = pl.Buffered(1)
    tok = pl.BlockSpec((heads * nk, tp), lambda i, e: (0, i), pipeline_mode=once)
    sl = pl.BlockSpec((1, ni, heads, tp), lambda i, e: (jnp.minimum(e, nblk - 1), 0, 0, i))
    return pl.pallas_call(
        functools.partial(_peer_dense_kernel, heads=heads, nk=nk, jc=nk // 2),
        grid=(n // tp, nblk + 1),
        in_specs=[pl.BlockSpec((eb, d), lambda i, e: (jnp.minimum(e, nblk - 1), 0)),
                  pl.BlockSpec((d, tp), lambda i, e: (0, i), pipeline_mode=once),
                  pl.BlockSpec((d, eb), lambda i, e: (0, jnp.maximum(e - 1, 0))),
                  sl, sl, tok, tok, pl.BlockSpec((heads, tp), lambda i, e: (0, i))],
        out_specs=pl.BlockSpec((d, tp), lambda i, e: (0, i)),
        out_shape=jax.ShapeDtypeStruct((d, n), F32),
        scratch_shapes=[pltpu.VMEM((eb, tp), F32), pltpu.VMEM((2, eb, tp), BF16)],
        compiler_params=_params(("parallel", "arbitrary")),
        name="peer_dense",
    )(u, ht, vt, s0r, e0r, s1, e1, thr)


def _by_expert_block(x, heads, nk, ni):
    n = x.shape[1]
    return x.reshape(heads, nk // ni, ni, n).transpose(1, 2, 0, 3)


def _layer(xp, xs, s_ret, s_rwkv, s_shift, cp, cs, p):
    nb, t, d = xp.shape
    db, dt, _ = xs.shape
    rows = _Rows(nb, t, db, dt)
    ret_heads, ret_hd = s_ret.shape[1], s_ret.shape[2]
    rw_heads, rw_hd = s_rwkv.shape[1], s_rwkv.shape[2]
    ret_w, rw = ret_heads * ret_hd, rw_heads * rw_hd
    w_lora, a_lora, g_lora = p['w2'].shape[0], p['a2'].shape[0], p['g2'].shape[0]
    rwkv_in = s_shift.shape[-1]
    ret_in = 4 * ret_w
    assert p['w_in'].shape[1] == ret_in + rwkv_in and rwkv_in == 3 * rw + w_lora + a_lora + g_lora

    x = jnp.concatenate([xp.reshape(nb * t, d), xs.reshape(db * dt, d)], axis=0)
    mod = _ada(jnp.concatenate([cp, cs], axis=0), p['ada_w'], p['ada_b'])
    mod_ext = rows.extend(mod)

    h = _prenorm(rows, x, p['pre_mix_g'], mod_ext, d)

    tm = _pick(rows.n, 1024)
    proj_ret = _matmul(h, p['w_in'], col_block0=0, n_out=ret_in, tm=tm, tn=512, name="in_proj_ret")
    proj_rkv = _matmul(h, p['w_in'], col_block0=ret_in // 512, n_out=3 * rw, tm=tm, tn=512, name="in_proj_rkv")
    tail0 = ret_in + 3 * rw
    tail_w = -(-(w_lora + a_lora + g_lora) // LANES) * LANES
    proj_tail = _matmul(h, p['w_in'], col_block0=tail0 // LANES, n_out=tail_w, tm=tm, tn=LANES, name="in_proj_tail")
    tail_valid = w_lora + a_lora + g_lora

    o_ret, sr_p, sr_s = _retention(rows, proj_ret, s_ret, ret_heads, ret_hd)

    def firsts(proj, shift_cols):
        w = proj.shape[1]
        starts = np.arange(rows.prompt_blocks) * rows.rb
        last = proj[rows.rb - 1:rows.n_prompt:rows.rb]
        prev = jnp.concatenate([jnp.zeros((1, w), F32), last[:-1]], axis=0)
        prev = jnp.where((starts % t == 0)[:, None], 0.0, prev)
        fp = jnp.zeros((rows.prompt_blocks, rows.slots, w), F32).at[:, 0].set(prev)
        return jnp.concatenate([fp.reshape(-1, w), shift_cols], axis=0)[:, None, :]

    shift_tail = jnp.pad(s_shift[:, 3 * rw:], ((0, 0), (0, tail_w - tail_valid)))
    first_rkv = firsts(proj_rkv, s_shift[:, :3 * rw])
    first_tail = firsts(proj_tail, shift_tail)
    mu = p['shift_mu']
    mu_tail = jnp.pad(mu[3 * rw:], (0, tail_w - tail_valid)).reshape(1, tail_w)
    g2p = jnp.pad(p['g2'], ((0, tail_w - w_lora - a_lora - g_lora), (0, 0)))
    vec = lambda a: a.reshape(1, rw)
    r_, w_, k_, v_, a_, b_, g_, bonus = _rwkv_prep(
        rows, proj_rkv, proj_tail, first_rkv, first_tail, mu[:3 * rw].reshape(1, 3 * rw), mu_tail,
        vec(p['w0']), p['w2'], vec(p['a0']), p['a2'], g2p, vec(p['k_k']), vec(p['k_a']), vec(p['r_k']),
        rw, rw_hd, g_lora)
    ops = (r_, w_, k_, v_, a_, b_)
    o_rw_p, sw_p = _rwkv_scan(ops, nb, t, 0, 256, rw_heads, rw_hd, None)
    o_rw_s, sw_s = _rwkv_scan(ops, db, dt, nb * t, dt, rw_heads, rw_hd, s_rwkv)
    o_rw = _rwkv_post(jnp.concatenate([o_rw_p, o_rw_s], axis=0), bonus, g_, vec(p['lnx_g']), vec(p['lnx_b']), rw_hd)

    mix = _matmul_out(o_ret, o_rw, p['w_out'])
    x1, h2 = _postmix(rows, x, mix, p['post_mix_g'], p['pre_ffn_g'], mod_ext, d)

    heads, _, nk, kd = p['peer_sub_keys'].shape
    h2t = h2.T
    qt = _matmul_t(p['peer_wq'].T, h2t, name="peer_q")
    s0, s1, e0, e1, thr = _peer_topk(qt, p['peer_sub_keys'])
    eb = 512
    ni = eb // nk
    peer_t = _peer_dense(h2t, p['peer_u'].astype(BF16), p['peer_v'].T.astype(BF16),
                         _by_expert_block(s0, heads, nk, ni), _by_expert_block(e0, heads, nk, ni),
                         s1, e1, thr[:, 0, :], heads, nk, eb=eb)
    peer = peer_t.T
    y = _final(rows, x1, peer, p['post_ffn_g'], mod_ext, d)

    n_p = nb * t
    last_p = slice(t - 1, n_p, t)
    last_s = slice(n_p + dt - 1, None, dt)
    pw_last = lambda idx: jnp.concatenate([proj_rkv[idx], proj_tail[idx, :tail_valid]], axis=1)
    return (y[:n_p].reshape(nb, t, d), y[n_p:].reshape(db, dt, d), sr_p, sr_s, sw_p, sw_s,
            pw_last(last_p), pw_last(last_s))


def kernel(x_prompt, x_sample, state_ret, state_rwkv, state_shift, c_prompt, c_sample, ada_w, ada_b, pre_mix_g, post_mix_g, pre_ffn_g, post_ffn_g, w_in, shift_mu, w0, w2, a0, a2, g2, k_k, k_a, r_k, lnx_g, lnx_b, w_out, peer_wq, peer_sub_keys, peer_u, peer_v):
    depth = ada_w.shape[0]
    assert depth == 1, "prompt and sample tokens are stacked per layer; deeper stacks need per-layer restacking"
    prm = dict(ada_w=ada_w[0], ada_b=ada_b[0], pre_mix_g=pre_mix_g[0], post_mix_g=post_mix_g[0],
               pre_ffn_g=pre_ffn_g[0], post_ffn_g=post_ffn_g[0], w_in=w_in[0], shift_mu=shift_mu[0],
               w0=w0[0], w2=w2[0], a0=a0[0], a2=a2[0], g2=g2[0], k_k=k_k[0], k_a=k_a[0], r_k=r_k[0],
               lnx_g=lnx_g[0], lnx_b=lnx_b[0], w_out=w_out[0], peer_wq=peer_wq[0],
               peer_sub_keys=peer_sub_keys[0], peer_u=peer_u[0], peer_v=peer_v[0])
    yp, ys, sr_p, sr_s, sw_p, sw_s, ss_p, ss_s = _layer(
        x_prompt, x_sample, state_ret[0], state_rwkv[0], state_shift[0], c_prompt, c_sample, prm)
    sd, wd, hd = state_ret.dtype, state_rwkv.dtype, state_shift.dtype
    return (yp, ys, sr_p[None].astype(sd), sr_s[None].astype(sd), sw_p[None].astype(wd), sw_s[None].astype(wd),
            ss_p[None].astype(hd), ss_s[None].astype(hd))
```

```python
import functools
import math

import numpy as np
import jax
import jax.numpy as jnp
from jax import lax
from jax.experimental import pallas as pl
from jax.experimental.pallas import tpu as pltpu

PAST_LEN = 16384
RET_CHUNK = 128
ROPE_BASE = 10000.0
LNX_EPS = 64e-5
NORM_EPS = 1e-6
PEER_TOPK = 16

LANES = 128
ROW_BLOCK = 128
VMEM_LIMIT = 56 * 1024 * 1024

HI = lax.Precision.HIGHEST
F32 = jnp.float32
BF16 = jnp.bfloat16


def _pick(n, target, mult=LANES):
    best = None
    for c in range(mult, min(n, target) + 1, mult):
        if n % c == 0:
            best = c
    assert best is not None, (n, target, mult)
    return best


def _params(sem, vmem=VMEM_LIMIT):
    return pltpu.CompilerParams(dimension_semantics=sem, vmem_limit_bytes=vmem)


def _dot(a, b, precision=None):
    return jnp.dot(a, b, preferred_element_type=F32, precision=precision)


def _dot_nt(a, b, precision=None):
    return lax.dot_general(a, b, (((1,), (1,)), ((), ())), preferred_element_type=F32, precision=precision)


def _dot_tn(a, b, precision=None):
    return lax.dot_general(a, b, (((0,), (0,)), ((), ())), preferred_element_type=F32, precision=precision)


def _block_ones(n, seg, dtype):
    r = lax.broadcasted_iota(jnp.int32, (n, n), 0) // seg
    c = lax.broadcasted_iota(jnp.int32, (n, n), 1) // seg
    return (r == c).astype(dtype)


def _ada_kernel(c_ref, w_ref, b_ref, o_ref):
    c = c_ref[...]
    sc = (c * jax.nn.sigmoid(c)).astype(BF16)
    o_ref[...] = _dot(sc, w_ref[...].astype(BF16)) + b_ref[...]


def _ada(c_all, ada_w, ada_b, tn=512):
    m, d = c_all.shape
    n = ada_w.shape[1]
    return pl.pallas_call(
        _ada_kernel,
        grid=(n // tn,),
        in_specs=[pl.BlockSpec((m, d), lambda j: (0, 0)),
                  pl.BlockSpec((d, tn), lambda j: (0, j)),
                  pl.BlockSpec((1, tn), lambda j: (0, j))],
        out_specs=pl.BlockSpec((m, tn), lambda j: (0, j)),
        out_shape=jax.ShapeDtypeStruct((m, n), F32),
        compiler_params=_params(("arbitrary",)),
        name="ada",
    )(c_all, ada_w, ada_b.reshape(1, n))


def _expand_slots(m, rows):
    slots, _, d = m.shape
    return jnp.broadcast_to(m, (slots, rows // slots, d)).reshape(rows, d)


def _rms(x, g):
    return x * lax.rsqrt(jnp.mean(x * x, axis=-1, keepdims=True) + NORM_EPS) * g


def _from_either(i, first_blocks, a_ref, b_ref, body):
    @pl.when(i < first_blocks)
    def _():
        body(a_ref[...])

    @pl.when(i >= first_blocks)
    def _():
        body(b_ref[...])


def _two_specs(block, first_blocks):
    return [pl.BlockSpec(block, lambda i, *_: (jnp.minimum(i, first_blocks - 1), 0)),
            pl.BlockSpec(block, lambda i, *_: (jnp.maximum(i - first_blocks, 0), 0))]


def _prenorm_kernel(xp_ref, xs_ref, g_ref, sh_ref, sc_ref, h_ref, *, prompt_blocks):
    def body(x):
        rows = x.shape[0]
        y = _rms(x, g_ref[...])
        h = y * (1.0 + _expand_slots(sc_ref[...], rows)) + _expand_slots(sh_ref[...], rows)
        h_ref[...] = h.astype(h_ref.dtype)

    _from_either(pl.program_id(0), prompt_blocks, xp_ref, xs_ref, body)


def _postmix_kernel(xp_ref, xs_ref, m_ref, gpost_ref, gpre_ref, gate_ref, sh_ref, sc_ref, x1_ref, ht_ref,
                    *, prompt_blocks):
    def body(x):
        rows = x.shape[0]
        x1 = x + _expand_slots(gate_ref[...], rows) * _rms(m_ref[...], gpost_ref[...])
        x1_ref[...] = x1
        y = _rms(x1, gpre_ref[...])
        h = y * (1.0 + _expand_slots(sc_ref[...], rows)) + _expand_slots(sh_ref[...], rows)
        ht_ref[...] = h.T.astype(ht_ref.dtype)

    _from_either(pl.program_id(0), prompt_blocks, xp_ref, xs_ref, body)


def _final_kernel(x_ref, mt_ref, gpost_ref, gate_ref, y_ref):
    x = x_ref[...]
    rows = x.shape[0]
    y_ref[...] = x + _expand_slots(gate_ref[...], rows) * _rms(mt_ref[...].T, gpost_ref[...])


class _Rows:
    def __init__(self, nb, t, db, dt):
        self.n_prompt = nb * t
        self.n_sample = db * dt
        self.n = self.n_prompt + self.n_sample
        self.rb = ROW_BLOCK
        assert t % self.rb == 0 and self.rb % dt == 0 and self.n_sample % self.rb == 0
        self.slots = self.rb // dt
        self.prompt_blocks = self.n_prompt // self.rb
        self.blocks_per_seq = t // self.rb
        self.n_blocks = self.n // self.rb
        self.nb, self.t, self.db, self.dt = nb, t, db, dt

    def slot_block(self, i):
        return jnp.where(i < self.prompt_blocks, i // self.blocks_per_seq, i - self.prompt_blocks + self.nb)

    def extend(self, m):
        mp = jnp.repeat(m[: self.nb], self.slots, axis=0)
        return jnp.concatenate([mp, m[self.nb:]], axis=0)[:, None, :]


def _mod_spec(rows, d, chunk, block0=0):
    return pl.BlockSpec((rows.slots, 1, d), lambda i: (rows.slot_block(i + block0), 0, chunk))


def _prenorm(rows, xp, xs, g, mod_ext, d):
    row = pl.BlockSpec((rows.rb, d), lambda i: (i, 0))
    vec = pl.BlockSpec((1, d), lambda i: (0, 0))
    return pl.pallas_call(
        functools.partial(_prenorm_kernel, prompt_blocks=rows.prompt_blocks),
        grid=(rows.n_blocks,),
        in_specs=_two_specs((rows.rb, d), rows.prompt_blocks) + [vec, _mod_spec(rows, d, 0), _mod_spec(rows, d, 1)],
        out_specs=row,
        out_shape=jax.ShapeDtypeStruct((rows.n, d), BF16),
        compiler_params=_params(("parallel",)),
        name="prenorm",
    )(xp, xs, g.reshape(1, d), mod_ext, mod_ext)


def _postmix(rows, xp, xs, mix, gpost, gpre, mod_ext, d):
    row = pl.BlockSpec((rows.rb, d), lambda i: (i, 0))
    vec = pl.BlockSpec((1, d), lambda i: (0, 0))
    return pl.pallas_call(
        functools.partial(_postmix_kernel, prompt_blocks=rows.prompt_blocks),
        grid=(rows.n_blocks,),
        in_specs=_two_specs((rows.rb, d), rows.prompt_blocks)
        + [row, vec, vec, _mod_spec(rows, d, 2), _mod_spec(rows, d, 3), _mod_spec(rows, d, 4)],
        out_specs=[row, pl.BlockSpec((d, rows.rb), lambda i: (0, i))],
        out_shape=[jax.ShapeDtypeStruct((rows.n, d), F32), jax.ShapeDtypeStruct((d, rows.n), BF16)],
        compiler_params=_params(("parallel",)),
        name="postmix",
    )(xp, xs, mix, gpost.reshape(1, d), gpre.reshape(1, d), mod_ext, mod_ext, mod_ext)


def _final(rows, x1, peer_t, gpost, mod_ext, d, block0, n_blocks):
    vec = pl.BlockSpec((1, d), lambda i: (0, 0))
    return pl.pallas_call(
        _final_kernel,
        grid=(n_blocks,),
        in_specs=[pl.BlockSpec((rows.rb, d), lambda i: (i + block0, 0)),
                  pl.BlockSpec((d, rows.rb), lambda i: (0, i + block0)),
                  vec, _mod_spec(rows, d, 5, block0)],
        out_specs=pl.BlockSpec((rows.rb, d), lambda i: (i, 0)),
        out_shape=jax.ShapeDtypeStruct((n_blocks * rows.rb, d), F32),
        compiler_params=_params(("parallel",)),
        name="final",
    )(x1, peer_t, gpost.reshape(1, d), mod_ext)


def _mm_kernel(a_ref, w_ref, o_ref):
    o_ref[...] = _dot(a_ref[...], w_ref[...].astype(BF16)).astype(o_ref.dtype)


def _matmul(a, w, *, col_block0, n_out, tm, tn, name):
    m, k = a.shape
    return pl.pallas_call(
        _mm_kernel,
        grid=(pl.cdiv(n_out, tn), m // tm),
        in_specs=[pl.BlockSpec((tm, k), lambda j, i: (i, 0)),
                  pl.BlockSpec((k, tn), lambda j, i: (0, j + col_block0))],
        out_specs=pl.BlockSpec((tm, tn), lambda j, i: (i, j)),
        out_shape=jax.ShapeDtypeStruct((m, n_out), F32),
        compiler_params=_params(("parallel", "parallel")),
        name=name,
    )(a, w)


def _mm2_kernel(ap_ref, as_ref, b_ref, wa_ref, wb_ref, o_ref, *, first_blocks):
    def body(a):
        o_ref[...] = _dot(a, wa_ref[...].astype(BF16)) + _dot(b_ref[...], wb_ref[...].astype(BF16))

    _from_either(pl.program_id(1), first_blocks, ap_ref, as_ref, body)


def _matmul_out(o_ret_p, o_ret_s, o_rw, w_out, tn=512):
    m, kb = o_rw.shape
    ka = o_ret_p.shape[1]
    tm = _pick(math.gcd(o_ret_p.shape[0], o_ret_s.shape[0]), 1024)
    first_blocks = o_ret_p.shape[0] // tm
    n = w_out.shape[1]
    tn = _pick(n, tn)
    assert ka == kb
    pair = [pl.BlockSpec((tm, ka), lambda j, i: (jnp.minimum(i, first_blocks - 1), 0)),
            pl.BlockSpec((tm, ka), lambda j, i: (jnp.maximum(i - first_blocks, 0), 0))]
    return pl.pallas_call(
        functools.partial(_mm2_kernel, first_blocks=first_blocks),
        grid=(n // tn, m // tm),
        in_specs=pair + [pl.BlockSpec((tm, kb), lambda j, i: (i, 0)),
                         pl.BlockSpec((ka, tn), lambda j, i: (0, j)),
                         pl.BlockSpec((kb, tn), lambda j, i: (1, j))],
        out_specs=pl.BlockSpec((tm, tn), lambda j, i: (i, j)),
        out_shape=jax.ShapeDtypeStruct((m, n), F32),
        compiler_params=_params(("parallel", "parallel")),
        name="out_proj",
    )(o_ret_p, o_ret_s, o_rw, w_out, w_out)


def _rope_tables(pos, hd):
    half = hd // 2
    inv = ROPE_BASE ** (-jnp.arange(half, dtype=F32) / half)
    ang = pos.astype(F32)[:, None] * inv[None, :]
    cos, sin = jnp.cos(ang), jnp.sin(ang)
    return jnp.concatenate([cos, cos], axis=-1), jnp.concatenate([-sin, sin], axis=-1)


def _rope(x, cos, sin_signed):
    half = x.shape[-1] // 2
    return x * cos + pltpu.roll(x, half, 1) * sin_signed


def _ret_finish(o, gate):
    o = o * lax.rsqrt(jnp.mean(o * o, axis=-1, keepdims=True) + NORM_EPS)
    return o * (gate * jax.nn.sigmoid(gate))


def _ret_prompt_kernel(q_ref, k_ref, v_ref, g_ref, cos_ref, sin_ref, lg_ref, o_ref, s_ref, *, chunk):
    t, hd = q_ref.shape
    lg = lg_ref[0, 0:1, :]
    ri = lax.broadcasted_iota(jnp.int32, (chunk, chunk), 0)
    ci = lax.broadcasted_iota(jnp.int32, (chunk, chunk), 1)
    causal = ri >= ci
    diff = jnp.where(causal, ri - ci, 0).astype(F32)
    mask = jnp.where(causal, jnp.exp(diff * lg), 0.0)
    rowf = lax.broadcasted_iota(jnp.int32, (chunk, hd), 0).astype(F32)
    q_dec = jnp.exp((rowf + 1.0) * lg)
    k_dec = jnp.exp((chunk - 1.0 - rowf) * lg)
    c_dec = jnp.exp(float(chunk) * lg)

    def step(c, s):
        r0 = pl.multiple_of(c * chunk, chunk)
        cos = cos_ref[pl.ds(r0, chunk), :]
        sin = sin_ref[pl.ds(r0, chunk), :]
        qc = _rope(q_ref[pl.ds(r0, chunk), :], cos, sin)
        kc = _rope(k_ref[pl.ds(r0, chunk), :], cos, sin) * (hd ** -0.5)
        vc = v_ref[pl.ds(r0, chunk), :]
        att = _dot_nt(qc.astype(BF16), kc.astype(BF16)) * mask
        o = _dot(att.astype(BF16), vc.astype(BF16)) + _dot((qc * q_dec).astype(BF16), s.astype(BF16))
        s = s * c_dec + _dot_tn(kc * k_dec, vc, HI)
        o_ref[pl.ds(r0, chunk), :] = _ret_finish(o, g_ref[pl.ds(r0, chunk), :]).astype(o_ref.dtype)
        return s

    s_ref[0, 0] = lax.fori_loop(0, t // chunk, step, jnp.zeros((hd, hd), F32))


def _ret_sample_kernel(q_ref, k_ref, v_ref, g_ref, cos_ref, sin_ref, lg_ref, s0_ref, o_ref, s_ref, *, dt):
    rows, hd = q_ref.shape
    nseq = rows // dt
    lg = lg_ref[0, 0:1, :]
    ri = lax.broadcasted_iota(jnp.int32, (rows, rows), 0)
    ci = lax.broadcasted_iota(jnp.int32, (rows, rows), 1)
    ok = (ri >= ci) & ((ri // dt) == (ci // dt))
    diff = jnp.where(ok, ri - ci, 0).astype(F32)
    mask = jnp.where(ok, jnp.exp(diff * lg), 0.0)
    rown = lax.broadcasted_iota(jnp.int32, (rows, hd), 0)
    posf = (rown % dt).astype(F32)
    q_dec = jnp.exp((posf + 1.0) * lg)
    k_dec = jnp.exp((dt - 1.0 - posf) * lg)
    c_dec = jnp.exp(float(dt) * lg)
    cos, sin = cos_ref[...], sin_ref[...]
    qc = _rope(q_ref[...], cos, sin)
    kc = _rope(k_ref[...], cos, sin) * (hd ** -0.5)
    vc = v_ref[...]
    att = _dot_nt(qc.astype(BF16), kc.astype(BF16)) * mask
    inner = _dot(att.astype(BF16), vc.astype(BF16))
    qd = qc * q_dec
    kd = kc * k_dec
    seq = rown // dt
    cross = jnp.zeros((rows, hd), F32)
    for s in range(nseq):
        s0 = s0_ref[s, 0]
        mine = seq == s
        cross = cross + _dot(jnp.where(mine, qd, 0.0).astype(BF16), s0.astype(BF16))
        s_ref[s, 0] = s0 * c_dec + _dot_tn(jnp.where(mine, kd, 0.0), vc, HI)
    o_ref[...] = _ret_finish(inner + cross, g_ref[...]).astype(o_ref.dtype)


def _retention(rows, proj, state_ret, ret_heads, hd):
    nb, t, db, dt = rows.nb, rows.t, rows.db, rows.dt
    lg = jnp.log1p(-jnp.exp2(-5.0 - jnp.arange(ret_heads, dtype=F32)))
    lg_tab = jnp.broadcast_to(lg[:, None, None], (ret_heads, 8, hd))
    lg_spec = pl.BlockSpec((1, 8, hd), lambda b, h: (h, 0, 0))
    assert hd == RET_CHUNK and t % RET_CHUNK == 0 and dt % RET_CHUNK != 0

    cos_p, sin_p = _rope_tables(jnp.arange(t, dtype=jnp.int32), hd)
    col = lambda part: pl.BlockSpec((t, hd), lambda b, h: (b, h + part * ret_heads))
    tab = pl.BlockSpec((t, hd), lambda b, h: (0, 0))
    o_p, s_p = pl.pallas_call(
        functools.partial(_ret_prompt_kernel, chunk=RET_CHUNK),
        grid=(nb, ret_heads),
        in_specs=[col(0), col(1), col(2), col(3), tab, tab, lg_spec],
        out_specs=[pl.BlockSpec((t, hd), lambda b, h: (b, h)),
                   pl.BlockSpec((1, 1, hd, hd), lambda b, h: (b, h, 0, 0))],
        out_shape=[jax.ShapeDtypeStruct((nb * t, ret_heads * hd), BF16),
                   jax.ShapeDtypeStruct((nb, ret_heads, hd, hd), F32)],
        compiler_params=_params(("parallel", "parallel")),
        name="ret_prompt",
    )(proj, proj, proj, proj, cos_p, sin_p, lg_tab)

    rb = RET_CHUNK
    nseq = rb // dt
    cos_s, sin_s = _rope_tables(PAST_LEN + jnp.arange(dt, dtype=jnp.int32), hd)
    cos_s, sin_s = jnp.tile(cos_s, (nseq, 1)), jnp.tile(sin_s, (nseq, 1))
    off = (nb * t) // rb
    col = lambda part: pl.BlockSpec((rb, hd), lambda b, h: (b + off, h + part * ret_heads))
    tab = pl.BlockSpec((rb, hd), lambda b, h: (0, 0))
    st = pl.BlockSpec((nseq, 1, hd, hd), lambda b, h: (b, h, 0, 0))
    o_s, s_s = pl.pallas_call(
        functools.partial(_ret_sample_kernel, dt=dt),
        grid=(db // nseq, ret_heads),
        in_specs=[col(0), col(1), col(2), col(3), tab, tab, lg_spec, st],
        out_specs=[pl.BlockSpec((rb, hd), lambda b, h: (b, h)), st],
        out_shape=[jax.ShapeDtypeStruct((db * dt, ret_heads * hd), BF16),
                   jax.ShapeDtypeStruct((db, ret_heads, hd, hd), F32)],
        compiler_params=_params(("parallel", "parallel")),
        name="ret_sample",
    )(proj, proj, proj, proj, cos_s, sin_s, lg_tab, state_ret)
    return o_p, o_s, s_p, s_s


def _shifted(x, first, tseq_mask):
    prev = pltpu.roll(x, 1, 0)
    return jnp.where(tseq_mask, first, prev)


def _rwkv_prep_kernel(pr_ref, pk_ref, pv_ref, pt_ref, fr_ref, fk_ref, fv_ref, ft_ref,
                      mur_ref, muk_ref, muv_ref, mut_ref, w0_ref, w2_ref, a0_ref, a2_ref, g2_ref,
                      kk_ref, ka_ref, rk_ref,
                      r_out, w_out, k_out, v_out, a_out, b_out, g_out, bonus_out,
                      *, dt, prompt_blocks, w_lora, a_lora, g_lora, hd):
    i = pl.program_id(0)
    rows, cn = pr_ref.shape
    rown = lax.broadcasted_iota(jnp.int32, (rows, 1), 0)
    sample_block = jnp.zeros((rows, 1), jnp.int32) + (i >= prompt_blocks).astype(jnp.int32)
    is_first = (rown == 0) | ((rown % dt == 0) & (sample_block == 1))

    def lerp(p_ref, f_ref, mu_ref):
        p = p_ref[...]
        prev = _shifted(p, _expand_slots(f_ref[...], rows), is_first)
        return p + mu_ref[...] * (prev - p)

    r = lerp(pr_ref, fr_ref, mur_ref)
    kw = lerp(pk_ref, fk_ref, muk_ref)
    vw = lerp(pv_ref, fv_ref, muv_ref)
    tail = lerp(pt_ref, ft_ref, mut_ref)
    tw = tail.shape[1]
    wl = tail[:, :w_lora]
    al = tail[:, w_lora:w_lora + a_lora]
    glane = lax.broadcasted_iota(jnp.int32, (rows, tw - w_lora - a_lora), 1)
    gl = jnp.where(glane < g_lora, jax.nn.sigmoid(tail[:, w_lora + a_lora:]), 0.0)

    w_log = -jax.nn.softplus(-(w0_ref[...] + _dot(jnp.tanh(wl), w2_ref[...], HI))) - 0.5
    decay = jnp.exp(-jnp.exp(w_log))
    a = jax.nn.sigmoid(a0_ref[...] + _dot(al, a2_ref[...], HI))
    g = _dot(gl, g2_ref[...], HI)

    seg = _block_ones(LANES, hd, F32)

    def segsum(x):
        return jnp.concatenate([_dot(x[:, c:c + LANES], seg, HI) for c in range(0, cn, LANES)], axis=1)

    kk = kw * kk_ref[...]
    kk = kk / jnp.maximum(jnp.sqrt(segsum(kk * kk)), 1e-12)
    kn = kw * (1.0 + (a - 1.0) * ka_ref[...])
    r_out[...] = r
    w_out[...] = decay
    k_out[...] = kn
    v_out[...] = vw
    a_out[...] = -kk
    b_out[...] = kk * a
    g_out[...] = g
    bonus_out[...] = segsum(r * kn * rk_ref[...]) * vw


def _rwkv_prep(rows, proj_rkv, proj_tail, first_rkv, first_tail, mu_rkv, mu_tail, w0, w2, a0, a2, g2p,
               k_k, k_a, r_k_flat, rw, hd, g_lora, cn=512):
    n = rows.n
    cn = _pick(rw, cn)
    tw = proj_tail.shape[1]
    nblk = (rows.n_blocks, rw // cn)
    part = lambda p: pl.BlockSpec((rows.rb, cn), lambda i, j: (i, j + p * (rw // cn)))
    fpart = lambda p: pl.BlockSpec((rows.slots, 1, cn), lambda i, j: (i, 0, j + p * (rw // cn)))
    mupart = lambda p: pl.BlockSpec((1, cn), lambda i, j: (0, j + p * (rw // cn)))
    vec = pl.BlockSpec((1, cn), lambda i, j: (0, j))
    lora = lambda k: pl.BlockSpec((k, cn), lambda i, j: (0, j))
    out = pl.BlockSpec((rows.rb, cn), lambda i, j: (i, j))
    kern = functools.partial(_rwkv_prep_kernel, dt=rows.dt, prompt_blocks=rows.prompt_blocks,
                             w_lora=w2.shape[0], a_lora=a2.shape[0], g_lora=g_lora, hd=hd)
    return pl.pallas_call(
        kern,
        grid=nblk,
        in_specs=[part(0), part(1), part(2), pl.BlockSpec((rows.rb, tw), lambda i, j: (i, 0)),
                  fpart(0), fpart(1), fpart(2), pl.BlockSpec((rows.slots, 1, tw), lambda i, j: (i, 0, 0)),
                  mupart(0), mupart(1), mupart(2), pl.BlockSpec((1, tw), lambda i, j: (0, 0)),
                  vec, lora(w2.shape[0]), vec, lora(a2.shape[0]), lora(g2p.shape[0]),
                  vec, vec, vec],
        out_specs=[out] * 8,
        out_shape=[jax.ShapeDtypeStruct((n, rw), F32)] * 8,
        compiler_params=_params(("parallel", "arbitrary")),
        name="rwkv_prep",
    )(proj_rkv, proj_rkv, proj_rkv, proj_tail, first_rkv, first_rkv, first_rkv, first_tail,
      mu_rkv, mu_rkv, mu_rkv, mu_tail, w0, w2, a0, a2, g2p, k_k, k_a, r_k_flat)


def _rwkv_scan_kernel(r_ref, w_ref, k_ref, v_ref, a_ref, b_ref, *rest, hd, has_state, n_chains):
    if has_state:
        s0_ref, o_ref, sout_ref, st_ref, ob_ref, vs_ref = rest
    else:
        o_ref, sout_ref, st_ref, ob_ref, vs_ref = rest
        s0_ref = None
    tb = pl.program_id(1)
    tc, width = r_ref.shape
    pw = 2 * hd
    npairs = width // pw

    @pl.when(tb == 0)
    def _():
        for p in range(npairs):
            if has_state:
                st_ref[p] = jnp.concatenate([s0_ref[0, 2 * p], s0_ref[0, 2 * p + 1]], axis=1)
            else:
                st_ref[p] = jnp.zeros((hd, pw), F32)

    ones_bf = _block_ones(pw, hd, BF16)
    sub = lax.broadcasted_iota(jnp.int32, (hd, pw), 0)
    lane = lax.broadcasted_iota(jnp.int32, (hd, pw), 1)
    diags = [(((lane % hd) - sub + hd) % hd == i).astype(F32).astype(BF16) for i in range(3)]
    lane8 = lax.broadcasted_iota(jnp.int32, (8, pw), 1)
    sel = lax.broadcasted_iota(jnp.int32, (8, pw), 0) == (lane8 // hd)
    chains = [range(c * npairs // n_chains, (c + 1) * npairs // n_chains) for c in range(n_chains)]

    def shift_in_head(x, i):
        return jnp.where(lane8 % hd >= i, pltpu.roll(x, i, 1), pltpu.roll(x, pw - hd + i, 1))

    def group(tg, carry):
        r0 = pl.multiple_of(tg * 8, 8)
        rows8 = pl.ds(r0, 8)
        for p in range(npairs):
            cols = pl.ds(p * pw, pw)
            v8 = v_ref[rows8, cols]
            v1 = v8.astype(BF16).astype(F32)
            v2 = (v8 - v1).astype(BF16).astype(F32)
            vs_ref[0, :, cols] = v1
            vs_ref[1, :, cols] = shift_in_head(v2, 1)
            vs_ref[2, :, cols] = shift_in_head((v8 - v1) - v2, 2)

        def tile_row(ref, p, j):
            return ref[rows8, pl.ds(p * pw, pw)][j:j + 1]

        for j in range(8):
            for chain in chains:
                old = [st_ref[p] for p in chain]
                pa = jnp.concatenate([s * tile_row(a_ref, p, j) for s, p in zip(old, chain)], axis=0)
                sa = _dot(pa.astype(BF16), ones_bf)
                xs = []
                for p in chain:
                    x = None
                    for i in range(3):
                        term = diags[i] * vs_ref[i, :, pl.ds(p * pw, pw)][j:j + 1].astype(BF16)
                        x = term if x is None else x + term
                    xs.append(x)
                vc = _dot(jnp.concatenate(xs, axis=0), ones_bf)
                for i, p in enumerate(chain):
                    blk = slice(i * hd, (i + 1) * hd)
                    s = (old[i] * tile_row(w_ref, p, j) + sa[blk] * tile_row(b_ref, p, j)
                         + vc[blk] * tile_row(k_ref, p, j))
                    st_ref[p] = s
                    rsel = jnp.where(sel, tile_row(r_ref, p, j), 0.0).astype(BF16)
                    o8 = _dot_nt(rsel, s.astype(BF16))
                    ob_ref[p, j:j + 1, 0:hd] = o8[0:1, :]
                    ob_ref[p, j:j + 1, hd:pw] = o8[1:2, :]
        for p in range(npairs):
            o_ref[rows8, pl.ds(p * pw, pw)] = ob_ref[p]
        return carry

    lax.fori_loop(0, tc // 8, group, 0)

    @pl.when(tb == pl.num_programs(1) - 1)
    def _():
        for p in range(npairs):
            s = st_ref[p]
            sout_ref[0, 2 * p] = s[:, :hd]
            sout_ref[0, 2 * p + 1] = s[:, hd:]


def _rwkv_scan(ops, nseq, t, row0, tc, heads, hd, state):
    width = heads * hd
    assert t % tc == 0 and row0 % tc == 0
    nt = t // tc
    blk = pl.BlockSpec((tc, width), lambda s, j: (row0 // tc + s * nt + j, 0))
    st = pl.BlockSpec((1, heads, hd, hd), lambda s, j: (s, 0, 0, 0))
    in_specs = [blk] * 6
    args = list(ops)
    if state is not None:
        in_specs.append(st)
        args.append(state)
    return pl.pallas_call(
        functools.partial(_rwkv_scan_kernel, hd=hd, has_state=state is not None, n_chains=1),
        grid=(nseq, nt),
        in_specs=in_specs,
        out_specs=[pl.BlockSpec((tc, width), lambda s, j: (s * nt + j, 0)), st],
        out_shape=[jax.ShapeDtypeStruct((nseq * t, width), F32),
                   jax.ShapeDtypeStruct((nseq, heads, hd, hd), F32)],
        scratch_shapes=[pltpu.VMEM((heads // 2, hd, 2 * hd), F32), pltpu.VMEM((heads // 2, 8, 2 * hd), F32),
                        pltpu.VMEM((3, 8, width), F32)],
        compiler_params=_params(("parallel", "arbitrary")),
        name="rwkv_scan_state" if state is not None else "rwkv_scan",
    )(*args)


def _rwkv_post_kernel(op_ref, os_ref, bonus_ref, g_ref, lg_ref, lb_ref, out_ref, *, hd, first_blocks):
    cn = out_ref.shape[1]
    seg = _block_ones(LANES, hd, F32)

    def segmean(x):
        return jnp.concatenate([_dot(x[:, c:c + LANES], seg, HI) for c in range(0, cn, LANES)], axis=1) * (1.0 / hd)

    def body(o):
        mean = segmean(o)
        cen = o - mean
        var = segmean(cen * cen)
        y = cen * lax.rsqrt(var + LNX_EPS) * lg_ref[...] + lb_ref[...]
        out_ref[...] = ((y + bonus_ref[...]) * g_ref[...]).astype(out_ref.dtype)

    _from_either(pl.program_id(0), first_blocks, op_ref, os_ref, body)


def _rwkv_post(o_raw_p, o_raw_s, bonus, g, lnx_g, lnx_b, hd, rb=256, cn=512):
    n, rw = bonus.shape
    rb, cn = _pick(math.gcd(o_raw_p.shape[0], o_raw_s.shape[0]), rb), _pick(rw, cn)
    first_blocks = o_raw_p.shape[0] // rb
    blk = pl.BlockSpec((rb, cn), lambda i, j: (i, j))
    vec = pl.BlockSpec((1, cn), lambda i, j: (0, j))
    pair = [pl.BlockSpec((rb, cn), lambda i, j: (jnp.minimum(i, first_blocks - 1), j)),
            pl.BlockSpec((rb, cn), lambda i, j: (jnp.maximum(i - first_blocks, 0), j))]
    return pl.pallas_call(
        functools.partial(_rwkv_post_kernel, hd=hd, first_blocks=first_blocks),
        grid=(n // rb, rw // cn),
        in_specs=pair + [blk, blk, vec, vec],
        out_specs=blk,
        out_shape=jax.ShapeDtypeStruct((n, rw), BF16),
        compiler_params=_params(("parallel", "parallel")),
        name="rwkv_post",
    )(o_raw_p, o_raw_s, bonus, g, lnx_g, lnx_b)


def _mmt_kernel(wt_ref, x_ref, o_ref):
    o_ref[...] = _dot(wt_ref[...].astype(BF16), x_ref[...])


def _matmul_t(wt, xt, tm=512, tn=1024, name="matmul_t"):
    m, k = wt.shape
    n = xt.shape[1]
    tm, tn = _pick(m, tm), _pick(n, tn)
    return pl.pallas_call(
        _mmt_kernel,
        grid=(n // tn, m // tm),
        in_specs=[pl.BlockSpec((tm, k), lambda j, i: (i, 0)),
                  pl.BlockSpec((k, tn), lambda j, i: (0, j))],
        out_specs=pl.BlockSpec((tm, tn), lambda j, i: (i, j)),
        out_shape=jax.ShapeDtypeStruct((m, n), F32),
        compiler_params=_params(("parallel", "parallel")),
        name=name,
    )(wt, xt)


def _take_top(s, k):
    n = s.shape[0]
    row = lax.broadcasted_iota(jnp.int32, s.shape, 0).astype(F32)
    out = []
    for _ in range(k):
        m = jnp.max(s, axis=0, keepdims=True)
        first = jnp.min(jnp.where(s == m, row, float(n)), axis=0, keepdims=True)
        s = jnp.where(row == first, -jnp.inf, s)
        out.append(m)
    return out


def _peer_topk_kernel(q_ref, keys_ref, s0_ref, s1_ref, e0_ref, e1_ref, thr_ref, *, topk):
    kd = keys_ref.shape[3]
    cols = q_ref.shape[1]
    s0 = _dot(keys_ref[0, 0], q_ref[:kd, :], HI)
    s1 = _dot(keys_ref[0, 1], q_ref[kd:, :], HI)
    top0 = _take_top(s0, topk)
    top1 = _take_top(s1, topk)
    rowk = lax.broadcasted_iota(jnp.int32, (topk, cols), 0)
    t1 = jnp.zeros((topk, cols), F32)
    for b in range(topk):
        t1 = jnp.where(rowk == b, top1[b], t1)
    cand = jnp.concatenate([top0[a] + t1 for a in range(topk)], axis=0)
    best = _take_top(cand, topk)
    z = jnp.zeros((1, cols), F32)
    for b in best:
        z = z + jnp.exp(b - best[0])
    s0_ref[...] = s0
    s1_ref[...] = s1
    e0_ref[...] = jnp.exp(s0 - top0[0]) / z
    e1_ref[...] = jnp.exp(s1 - top1[0])
    thr_ref[0] = jnp.broadcast_to(best[topk - 1], thr_ref.shape[1:])


def _peer_topk(qt, sub_keys, tc=256):
    n = qt.shape[1]
    tc = _pick(n, tc)
    heads, _, nk, kd = sub_keys.shape
    blk = pl.BlockSpec((nk, tc), lambda i, h: (h, i))
    shp = jax.ShapeDtypeStruct((heads * nk, n), F32)
    return pl.pallas_call(
        functools.partial(_peer_topk_kernel, topk=PEER_TOPK),
        grid=(n // tc, heads),
        in_specs=[pl.BlockSpec((2 * kd, tc), lambda i, h: (h, i)),
                  pl.BlockSpec((1, 2, nk, kd), lambda i, h: (h, 0, 0, 0))],
        out_specs=[blk] * 4 + [pl.BlockSpec((1, 8, tc), lambda i, h: (h, 0, i))],
        out_shape=[shp] * 4 + [jax.ShapeDtypeStruct((heads, 8, n), F32)],
        compiler_params=_params(("parallel", "parallel")),
        name="peer_topk",
    )(qt, sub_keys)


def _gelu(x):
    return 0.5 * x * (1.0 + lax.erf(x * (2.0 ** -0.5)))


def _peer_dense_kernel(u_ref, ht_ref, vt_ref, s0_ref, e0_ref, s1_ref, e1_ref, thr_ref, o_ref, act_ref, coef_ref,
                       *, heads, nk, jc):
    e = pl.program_id(1)
    eb = u_ref.shape[0]
    d = vt_ref.shape[0]
    ni = eb // nk
    dq = d // ni

    @pl.when(e == 0)
    def _():
        o_ref[...] = jnp.zeros_like(o_ref)
        coef_ref[1] = jnp.zeros(coef_ref.shape[1:], coef_ref.dtype)

    act_ref[...] = _dot(u_ref[...], ht_ref[...])
    cur = e % 2
    prev = (e + 1) % 2
    thr = thr_ref[...]

    def one_i(ii, carry):
        orow = pl.ds(pl.multiple_of(ii * dq, dq), dq)
        o_ref[orow, :] += _dot(vt_ref[orow, :], coef_ref[prev])
        s0 = s0_ref[0, ii]
        e0 = e0_ref[0, ii]
        for j0 in range(0, nk, jc):
            g = None
            for h in range(heads):
                jrows = slice(h * nk + j0, h * nk + j0 + jc)
                hit = (s0[h:h + 1] + s1_ref[jrows, :]) >= thr[h:h + 1]
                term = jnp.where(hit, e0[h:h + 1] * e1_ref[jrows, :], 0.0)
                g = term if g is None else g + term
            arow = pl.ds(pl.multiple_of(ii * nk + j0, jc), jc)
            coef_ref[cur, arow, :] = (g * _gelu(act_ref[arow, :])).astype(coef_ref.dtype)
        return carry

    lax.fori_loop(0, ni, one_i, 0)


def _peer_dense(ht, u, vt, s0r, e0r, s1, e1, thr, heads, nk, tp=512, eb=512):
    d, n = ht.shape
    tp = _pick(n, tp)
    nblk = u.shape[0] // eb
    ni = eb // nk
    once = pl.Buffered(1)
    tok = pl.BlockSpec((heads * nk, tp), lambda i, e: (0, i), pipeline_mode=once)
    sl = pl.BlockSpec((1, ni, heads, tp), lambda i, e: (jnp.minimum(e, nblk - 1), 0, 0, i))
    return pl.pallas_call(
        functools.partial(_peer_dense_kernel, heads=heads, nk=nk, jc=nk // 2),
        grid=(n // tp, nblk + 1),
        in_specs=[pl.BlockSpec((eb, d), lambda i, e: (jnp.minimum(e, nblk - 1), 0)),
                  pl.BlockSpec((d, tp), lambda i, e: (0, i), pipeline_mode=once),
                  pl.BlockSpec((d, eb), lambda i, e: (0, jnp.maximum(e - 1, 0))),
                  sl, sl, tok, tok, pl.BlockSpec((heads, tp), lambda i, e: (0, i))],
        out_specs=pl.BlockSpec((d, tp), lambda i, e: (0, i)),
        out_shape=jax.ShapeDtypeStruct((d, n), F32),
        scratch_shapes=[pltpu.VMEM((eb, tp), F32), pltpu.VMEM((2, eb, tp), BF16)],
        compiler_params=_params(("parallel", "arbitrary")),
        name="peer_dense",
    )(u, ht, vt, s0r, e0r, s1, e1, thr)


def _by_expert_block(x, heads, nk, ni):
    n = x.shape[1]
    return x.reshape(heads, nk // ni, ni, n).transpose(1, 2, 0, 3)


def _layer(xp, xs, s_ret, s_rwkv, s_shift, cp, cs, p):
    nb, t, d = xp.shape
    db, dt, _ = xs.shape
    rows = _Rows(nb, t, db, dt)
    ret_heads, ret_hd = s_ret.shape[1], s_ret.shape[2]
    rw_heads, rw_hd = s_rwkv.shape[1], s_rwkv.shape[2]
    ret_w, rw = ret_heads * ret_hd, rw_heads * rw_hd
    w_lora, a_lora, g_lora = p['w2'].shape[0], p['a2'].shape[0], p['g2'].shape[0]
    rwkv_in = s_shift.shape[-1]
    ret_in = 4 * ret_w
    assert p['w_in'].shape[1] == ret_in + rwkv_in and rwkv_in == 3 * rw + w_lora + a_lora + g_lora

    xp, xs = xp.reshape(nb * t, d), xs.reshape(db * dt, d)
    mod = _ada(jnp.concatenate([cp, cs], axis=0), p['ada_w'], p['ada_b'])
    mod_ext = rows.extend(mod)

    h = _prenorm(rows, xp, xs, p['pre_mix_g'], mod_ext, d)

    tm = _pick(rows.n, 1024)
    proj_ret = _matmul(h, p['w_in'], col_block0=0, n_out=ret_in, tm=tm, tn=512, name="in_proj_ret")
    proj_rkv = _matmul(h, p['w_in'], col_block0=ret_in // 512, n_out=3 * rw, tm=tm, tn=512, name="in_proj_rkv")
    tail0 = ret_in + 3 * rw
    tail_w = -(-(w_lora + a_lora + g_lora) // LANES) * LANES
    proj_tail = _matmul(h, p['w_in'], col_block0=tail0 // LANES, n_out=tail_w, tm=tm, tn=LANES, name="in_proj_tail")
    tail_valid = w_lora + a_lora + g_lora

    o_ret_p, o_ret_s, sr_p, sr_s = _retention(rows, proj_ret, s_ret, ret_heads, ret_hd)

    def firsts(proj, shift_cols):
        w = proj.shape[1]
        starts = np.arange(rows.prompt_blocks) * rows.rb
        last = proj[rows.rb - 1:rows.n_prompt:rows.rb]
        prev = jnp.concatenate([jnp.zeros((1, w), F32), last[:-1]], axis=0)
        prev = jnp.where((starts % t == 0)[:, None], 0.0, prev)
        fp = jnp.zeros((rows.prompt_blocks, rows.slots, w), F32).at[:, 0].set(prev)
        return jnp.concatenate([fp.reshape(-1, w), shift_cols], axis=0)[:, None, :]

    shift_tail = jnp.pad(s_shift[:, 3 * rw:], ((0, 0), (0, tail_w - tail_valid)))
    first_rkv = firsts(proj_rkv, s_shift[:, :3 * rw])
    first_tail = firsts(proj_tail, shift_tail)
    mu = p['shift_mu']
    mu_tail = jnp.pad(mu[3 * rw:], (0, tail_w - tail_valid)).reshape(1, tail_w)
    g2p = jnp.pad(p['g2'], ((0, tail_w - w_lora - a_lora - g_lora), (0, 0)))
    vec = lambda a: a.reshape(1, rw)
    r_, w_, k_, v_, a_, b_, g_, bonus = _rwkv_prep(
        rows, proj_rkv, proj_tail, first_rkv, first_tail, mu[:3 * rw].reshape(1, 3 * rw), mu_tail,
        vec(p['w0']), p['w2'], vec(p['a0']), p['a2'], g2p, vec(p['k_k']), vec(p['k_a']), vec(p['r_k']),
        rw, rw_hd, g_lora)
    ops = (r_, w_, k_, v_, a_, b_)
    o_rw_p, sw_p = _rwkv_scan(ops, nb, t, 0, 256, rw_heads, rw_hd, None)
    o_rw_s, sw_s = _rwkv_scan(ops, db, dt, nb * t, dt, rw_heads, rw_hd, s_rwkv)
    o_rw = _rwkv_post(o_rw_p, o_rw_s, bonus, g_, vec(p['lnx_g']), vec(p['lnx_b']), rw_hd)

    mix = _matmul_out(o_ret_p, o_ret_s, o_rw, p['w_out'])
    x1, h2t = _postmix(rows, xp, xs, mix, p['post_mix_g'], p['pre_ffn_g'], mod_ext, d)

    heads, _, nk, kd = p['peer_sub_keys'].shape
    qt = _matmul_t(p['peer_wq'].T, h2t, name="peer_q")
    s0, s1, e0, e1, thr = _peer_topk(qt, p['peer_sub_keys'])
    eb = 512
    ni = eb // nk
    peer_t = _peer_dense(h2t, p['peer_u'].astype(BF16), p['peer_v'].T.astype(BF16),
                         _by_expert_block(s0, heads, nk, ni), _by_expert_block(e0, heads, nk, ni),
                         s1, e1, thr[:, 0, :], heads, nk, eb=eb)
    yp = _final(rows, x1, peer_t, p['post_ffn_g'], mod_ext, d, 0, rows.prompt_blocks)
    ys = _final(rows, x1, peer_t, p['post_ffn_g'], mod_ext, d, rows.prompt_blocks, rows.n_blocks - rows.prompt_blocks)

    n_p = nb * t
    last_p = slice(t - 1, n_p, t)
    last_s = slice(n_p + dt - 1, None, dt)
    pw_last = lambda idx: jnp.concatenate([proj_rkv[idx], proj_tail[idx, :tail_valid]], axis=1)
    return (yp.reshape(nb, t, d), ys.reshape(db, dt, d), sr_p, sr_s, sw_p, sw_s,
            pw_last(last_p), pw_last(last_s))


def kernel(x_prompt, x_sample, state_ret, state_rwkv, state_shift, c_prompt, c_sample, ada_w, ada_b, pre_mix_g, post_mix_g, pre_ffn_g, post_ffn_g, w_in, shift_mu, w0, w2, a0, a2, g2, k_k, k_a, r_k, lnx_g, lnx_b, w_out, peer_wq, peer_sub_keys, peer_u, peer_v):
    depth = ada_w.shape[0]
    assert depth == 1, "prompt and sample tokens are stacked per layer; deeper stacks need per-layer restacking"
    prm = dict(ada_w=ada_w[0], ada_b=ada_b[0], pre_mix_g=pre_mix_g[0], post_mix_g=post_mix_g[0],
               pre_ffn_g=pre_ffn_g[0], post_ffn_g=post_ffn_g[0], w_in=w_in[0], shift_mu=shift_mu[0],
               w0=w0[0], w2=w2[0], a0=a0[0], a2=a2[0], g2=g2[0], k_k=k_k[0], k_a=k_a[0], r_k=r_k[0],
               lnx_g=lnx_g[0], lnx_b=lnx_b[0], w_out=w_out[0], peer_wq=peer_wq[0],
               peer_sub_keys=peer_sub_keys[0], peer_u=peer_u[0], peer_v=peer_v[0])
    yp, ys, sr_p, sr_s, sw_p, sw_s, ss_p, ss_s = _layer(
        x_prompt, x_sample, state_ret[0], state_rwkv[0], state_shift[0], c_prompt, c_sample, prm)
    sd, wd, hd = state_ret.dtype, state_rwkv.dtype, state_shift.dtype
    return (yp, ys, sr_p[None].astype(sd), sr_s[None].astype(sd), sw_p[None].astype(wd), sw_s[None].astype(wd),
            ss_p[None].astype(hd), ss_s[None].astype(hd))
```

```python
import functools
import math

import numpy as np
import jax
import jax.numpy as jnp
from jax import lax
from jax.experimental import pallas as pl
from jax.experimental.pallas import tpu as pltpu

PAST_LEN = 16384
RET_CHUNK = 128
ROPE_BASE = 10000.0
LNX_EPS = 64e-5
NORM_EPS = 1e-6
PEER_TOPK = 16

LANES = 128
ROW_BLOCK = 128
VMEM_LIMIT = 56 * 1024 * 1024

HI = lax.Precision.HIGHEST
F32 = jnp.float32
BF16 = jnp.bfloat16


def _pick(n, target, mult=LANES):
    best = None
    for c in range(mult, min(n, target) + 1, mult):
        if n % c == 0:
            best = c
    assert best is not None, (n, target, mult)
    return best


def _params(sem, vmem=VMEM_LIMIT):
    return pltpu.CompilerParams(dimension_semantics=sem, vmem_limit_bytes=vmem)


def _dot(a, b, precision=None):
    return jnp.dot(a, b, preferred_element_type=F32, precision=precision)


def _dot_nt(a, b, precision=None):
    return lax.dot_general(a, b, (((1,), (1,)), ((), ())), preferred_element_type=F32, precision=precision)


def _dot_tn(a, b, precision=None):
    return lax.dot_general(a, b, (((0,), (0,)), ((), ())), preferred_element_type=F32, precision=precision)


def _block_ones(n, seg, dtype):
    r = lax.broadcasted_iota(jnp.int32, (n, n), 0) // seg
    c = lax.broadcasted_iota(jnp.int32, (n, n), 1) // seg
    return (r == c).astype(dtype)


def _ada_kernel(c_ref, w_ref, b_ref, o_ref):
    c = c_ref[...]
    sc = (c * jax.nn.sigmoid(c)).astype(BF16)
    o_ref[...] = _dot(sc, w_ref[...].astype(BF16)) + b_ref[...]


def _ada(c_all, ada_w, ada_b, tn=512):
    m, d = c_all.shape
    n = ada_w.shape[1]
    return pl.pallas_call(
        _ada_kernel,
        grid=(n // tn,),
        in_specs=[pl.BlockSpec((m, d), lambda j: (0, 0)),
                  pl.BlockSpec((d, tn), lambda j: (0, j)),
                  pl.BlockSpec((1, tn), lambda j: (0, j))],
        out_specs=pl.BlockSpec((m, tn), lambda j: (0, j)),
        out_shape=jax.ShapeDtypeStruct((m, n), F32),
        compiler_params=_params(("arbitrary",)),
        name="ada",
    )(c_all, ada_w, ada_b.reshape(1, n))


def _expand_slots(m, rows):
    slots, _, d = m.shape
    return jnp.broadcast_to(m, (slots, rows // slots, d)).reshape(rows, d)


def _rms(x, g):
    return x * lax.rsqrt(jnp.mean(x * x, axis=-1, keepdims=True) + NORM_EPS) * g


def _from_either(i, first_blocks, a_ref, b_ref, body):
    @pl.when(i < first_blocks)
    def _():
        body(a_ref[...])

    @pl.when(i >= first_blocks)
    def _():
        body(b_ref[...])


def _two_specs(block, first_blocks):
    return [pl.BlockSpec(block, lambda i, *_: (jnp.minimum(i, first_blocks - 1), 0)),
            pl.BlockSpec(block, lambda i, *_: (jnp.maximum(i - first_blocks, 0), 0))]


def _prenorm_kernel(xp_ref, xs_ref, g_ref, sh_ref, sc_ref, h_ref, *, prompt_blocks):
    def body(x):
        rows = x.shape[0]
        y = _rms(x, g_ref[...])
        h = y * (1.0 + _expand_slots(sc_ref[...], rows)) + _expand_slots(sh_ref[...], rows)
        h_ref[...] = h.astype(h_ref.dtype)

    _from_either(pl.program_id(0), prompt_blocks, xp_ref, xs_ref, body)


def _postmix_kernel(xp_ref, xs_ref, m_ref, gpost_ref, gpre_ref, gate_ref, sh_ref, sc_ref, x1_ref, ht_ref,
                    *, prompt_blocks):
    def body(x):
        rows = x.shape[0]
        x1 = x + _expand_slots(gate_ref[...], rows) * _rms(m_ref[...], gpost_ref[...])
        x1_ref[...] = x1
        y = _rms(x1, gpre_ref[...])
        h = y * (1.0 + _expand_slots(sc_ref[...], rows)) + _expand_slots(sh_ref[...], rows)
        ht_ref[...] = h.T.astype(ht_ref.dtype)

    _from_either(pl.program_id(0), prompt_blocks, xp_ref, xs_ref, body)


def _final_kernel(x_ref, mt_ref, gpost_ref, gate_ref, y_ref):
    x = x_ref[...]
    rows = x.shape[0]
    y_ref[...] = x + _expand_slots(gate_ref[...], rows) * _rms(mt_ref[...].T, gpost_ref[...])


class _Rows:
    def __init__(self, nb, t, db, dt):
        self.n_prompt = nb * t
        self.n_sample = db * dt
        self.n = self.n_prompt + self.n_sample
        self.rb = ROW_BLOCK
        assert t % self.rb == 0 and self.rb % dt == 0 and self.n_sample % self.rb == 0
        self.slots = self.rb // dt
        self.prompt_blocks = self.n_prompt // self.rb
        self.blocks_per_seq = t // self.rb
        self.n_blocks = self.n // self.rb
        self.nb, self.t, self.db, self.dt = nb, t, db, dt

    def slot_block(self, i):
        return jnp.where(i < self.prompt_blocks, i // self.blocks_per_seq, i - self.prompt_blocks + self.nb)

    def extend(self, m):
        mp = jnp.repeat(m[: self.nb], self.slots, axis=0)
        return jnp.concatenate([mp, m[self.nb:]], axis=0)[:, None, :]


def _mod_spec(rows, d, chunk, block0=0):
    return pl.BlockSpec((rows.slots, 1, d), lambda i: (rows.slot_block(i + block0), 0, chunk))


def _prenorm(rows, xp, xs, g, mod_ext, d):
    row = pl.BlockSpec((rows.rb, d), lambda i: (i, 0))
    vec = pl.BlockSpec((1, d), lambda i: (0, 0))
    return pl.pallas_call(
        functools.partial(_prenorm_kernel, prompt_blocks=rows.prompt_blocks),
        grid=(rows.n_blocks,),
        in_specs=_two_specs((rows.rb, d), rows.prompt_blocks) + [vec, _mod_spec(rows, d, 0), _mod_spec(rows, d, 1)],
        out_specs=row,
        out_shape=jax.ShapeDtypeStruct((rows.n, d), BF16),
        compiler_params=_params(("parallel",)),
        name="prenorm",
    )(xp, xs, g.reshape(1, d), mod_ext, mod_ext)


def _postmix(rows, xp, xs, mix, gpost, gpre, mod_ext, d):
    row = pl.BlockSpec((rows.rb, d), lambda i: (i, 0))
    vec = pl.BlockSpec((1, d), lambda i: (0, 0))
    return pl.pallas_call(
        functools.partial(_postmix_kernel, prompt_blocks=rows.prompt_blocks),
        grid=(rows.n_blocks,),
        in_specs=_two_specs((rows.rb, d), rows.prompt_blocks)
        + [row, vec, vec, _mod_spec(rows, d, 2), _mod_spec(rows, d, 3), _mod_spec(rows, d, 4)],
        out_specs=[row, pl.BlockSpec((d, rows.rb), lambda i: (0, i))],
        out_shape=[jax.ShapeDtypeStruct((rows.n, d), F32), jax.ShapeDtypeStruct((d, rows.n), BF16)],
        compiler_params=_params(("parallel",)),
        name="postmix",
    )(xp, xs, mix, gpost.reshape(1, d), gpre.reshape(1, d), mod_ext, mod_ext, mod_ext)


def _final(rows, x1, peer_t, gpost, mod_ext, d, block0, n_blocks):
    vec = pl.BlockSpec((1, d), lambda i: (0, 0))
    return pl.pallas_call(
        _final_kernel,
        grid=(n_blocks,),
        in_specs=[pl.BlockSpec((rows.rb, d), lambda i: (i + block0, 0)),
                  pl.BlockSpec((d, rows.rb), lambda i: (0, i + block0)),
                  vec, _mod_spec(rows, d, 5, block0)],
        out_specs=pl.BlockSpec((rows.rb, d), lambda i: (i, 0)),
        out_shape=jax.ShapeDtypeStruct((n_blocks * rows.rb, d), F32),
        compiler_params=_params(("parallel",)),
        name="final",
    )(x1, peer_t, gpost.reshape(1, d), mod_ext)


def _mm_kernel(a_ref, wt_ref, o_ref, wbf_ref):
    @pl.when(pl.program_id(1) == 0)
    def _():
        wbf_ref[...] = wt_ref[...].astype(BF16)

    o_ref[...] = _dot_nt(a_ref[...], wbf_ref[...]).astype(o_ref.dtype)


def _matmul(a, wt, *, col_block0, n_out, tm, tn, name):
    m, k = a.shape
    return pl.pallas_call(
        _mm_kernel,
        grid=(pl.cdiv(n_out, tn), m // tm),
        in_specs=[pl.BlockSpec((tm, k), lambda j, i: (i, 0)),
                  pl.BlockSpec((tn, k), lambda j, i: (j + col_block0, 0))],
        out_specs=pl.BlockSpec((tm, tn), lambda j, i: (i, j)),
        out_shape=jax.ShapeDtypeStruct((m, n_out), F32),
        scratch_shapes=[pltpu.VMEM((tn, k), BF16)],
        compiler_params=_params(("parallel", "arbitrary")),
        name=name,
    )(a, wt)


def _mm2_kernel(ap_ref, as_ref, b_ref, wa_ref, wb_ref, o_ref, *, first_blocks):
    def body(a):
        o_ref[...] = _dot(a, wa_ref[...].astype(BF16)) + _dot(b_ref[...], wb_ref[...].astype(BF16))

    _from_either(pl.program_id(1), first_blocks, ap_ref, as_ref, body)


def _matmul_out(o_ret_p, o_ret_s, o_rw, w_out, tn=512):
    m, kb = o_rw.shape
    ka = o_ret_p.shape[1]
    tm = _pick(math.gcd(o_ret_p.shape[0], o_ret_s.shape[0]), 1024)
    first_blocks = o_ret_p.shape[0] // tm
    n = w_out.shape[1]
    tn = _pick(n, tn)
    assert ka == kb
    pair = [pl.BlockSpec((tm, ka), lambda j, i: (jnp.minimum(i, first_blocks - 1), 0)),
            pl.BlockSpec((tm, ka), lambda j, i: (jnp.maximum(i - first_blocks, 0), 0))]
    return pl.pallas_call(
        functools.partial(_mm2_kernel, first_blocks=first_blocks),
        grid=(n // tn, m // tm),
        in_specs=pair + [pl.BlockSpec((tm, kb), lambda j, i: (i, 0)),
                         pl.BlockSpec((ka, tn), lambda j, i: (0, j)),
                         pl.BlockSpec((kb, tn), lambda j, i: (1, j))],
        out_specs=pl.BlockSpec((tm, tn), lambda j, i: (i, j)),
        out_shape=jax.ShapeDtypeStruct((m, n), F32),
        compiler_params=_params(("parallel", "parallel")),
        name="out_proj",
    )(o_ret_p, o_ret_s, o_rw, w_out, w_out)


def _rope_tables(pos, hd):
    half = hd // 2
    inv = ROPE_BASE ** (-jnp.arange(half, dtype=F32) / half)
    ang = pos.astype(F32)[:, None] * inv[None, :]
    cos, sin = jnp.cos(ang), jnp.sin(ang)
    return jnp.concatenate([cos, cos], axis=-1), jnp.concatenate([-sin, sin], axis=-1)


def _rope(x, cos, sin_signed):
    half = x.shape[-1] // 2
    return x * cos + pltpu.roll(x, half, 1) * sin_signed


def _ret_finish(o, gate):
    o = o * lax.rsqrt(jnp.mean(o * o, axis=-1, keepdims=True) + NORM_EPS)
    return o * (gate * jax.nn.sigmoid(gate))


def _ret_prompt_kernel(q_ref, k_ref, v_ref, g_ref, cos_ref, sin_ref, lg_ref, o_ref, s_ref, *, chunk):
    t, hd = q_ref.shape
    lg = lg_ref[0, 0:1, :]
    ri = lax.broadcasted_iota(jnp.int32, (chunk, chunk), 0)
    ci = lax.broadcasted_iota(jnp.int32, (chunk, chunk), 1)
    causal = ri >= ci
    diff = jnp.where(causal, ri - ci, 0).astype(F32)
    mask = jnp.where(causal, jnp.exp(diff * lg), 0.0)
    rowf = lax.broadcasted_iota(jnp.int32, (chunk, hd), 0).astype(F32)
    q_dec = jnp.exp((rowf + 1.0) * lg)
    k_dec = jnp.exp((chunk - 1.0 - rowf) * lg)
    c_dec = jnp.exp(float(chunk) * lg)

    def step(c, s):
        r0 = pl.multiple_of(c * chunk, chunk)
        cos = cos_ref[pl.ds(r0, chunk), :]
        sin = sin_ref[pl.ds(r0, chunk), :]
        qc = _rope(q_ref[pl.ds(r0, chunk), :], cos, sin)
        kc = _rope(k_ref[pl.ds(r0, chunk), :], cos, sin) * (hd ** -0.5)
        vc = v_ref[pl.ds(r0, chunk), :]
        att = _dot_nt(qc.astype(BF16), kc.astype(BF16)) * mask
        o = _dot(att.astype(BF16), vc.astype(BF16)) + _dot((qc * q_dec).astype(BF16), s.astype(BF16))
        s = s * c_dec + _dot_tn(kc * k_dec, vc, HI)
        o_ref[pl.ds(r0, chunk), :] = _ret_finish(o, g_ref[pl.ds(r0, chunk), :]).astype(o_ref.dtype)
        return s

    s_ref[0, 0] = lax.fori_loop(0, t // chunk, step, jnp.zeros((hd, hd), F32))


def _ret_sample_kernel(q_ref, k_ref, v_ref, g_ref, cos_ref, sin_ref, lg_ref, s0_ref, o_ref, s_ref, *, dt):
    rows, hd = q_ref.shape
    nseq = rows // dt
    lg = lg_ref[0, 0:1, :]
    ri = lax.broadcasted_iota(jnp.int32, (rows, rows), 0)
    ci = lax.broadcasted_iota(jnp.int32, (rows, rows), 1)
    ok = (ri >= ci) & ((ri // dt) == (ci // dt))
    diff = jnp.where(ok, ri - ci, 0).astype(F32)
    mask = jnp.where(ok, jnp.exp(diff * lg), 0.0)
    rown = lax.broadcasted_iota(jnp.int32, (rows, hd), 0)
    posf = (rown % dt).astype(F32)
    q_dec = jnp.exp((posf + 1.0) * lg)
    k_dec = jnp.exp((dt - 1.0 - posf) * lg)
    c_dec = jnp.exp(float(dt) * lg)
    cos, sin = cos_ref[...], sin_ref[...]
    qc = _rope(q_ref[...], cos, sin)
    kc = _rope(k_ref[...], cos, sin) * (hd ** -0.5)
    vc = v_ref[...]
    att = _dot_nt(qc.astype(BF16), kc.astype(BF16)) * mask
    inner = _dot(att.astype(BF16), vc.astype(BF16))
    qd = qc * q_dec
    kd = kc * k_dec
    seq = rown // dt
    cross = jnp.zeros((rows, hd), F32)
    for s in range(nseq):
        s0 = s0_ref[s, 0]
        mine = seq == s
        cross = cross + _dot(jnp.where(mine, qd, 0.0).astype(BF16), s0.astype(BF16))
        s_ref[s, 0] = s0 * c_dec + _dot_tn(jnp.where(mine, kd, 0.0), vc, HI)
    o_ref[...] = _ret_finish(inner + cross, g_ref[...]).astype(o_ref.dtype)


def _retention(rows, proj, state_ret, ret_heads, hd):
    nb, t, db, dt = rows.nb, rows.t, rows.db, rows.dt
    lg = jnp.log1p(-jnp.exp2(-5.0 - jnp.arange(ret_heads, dtype=F32)))
    lg_tab = jnp.broadcast_to(lg[:, None, None], (ret_heads, 8, hd))
    lg_spec = pl.BlockSpec((1, 8, hd), lambda b, h: (h, 0, 0))
    assert hd == RET_CHUNK and t % RET_CHUNK == 0 and dt % RET_CHUNK != 0

    cos_p, sin_p = _rope_tables(jnp.arange(t, dtype=jnp.int32), hd)
    col = lambda part: pl.BlockSpec((t, hd), lambda b, h: (b, h + part * ret_heads))
    tab = pl.BlockSpec((t, hd), lambda b, h: (0, 0))
    o_p, s_p = pl.pallas_call(
        functools.partial(_ret_prompt_kernel, chunk=RET_CHUNK),
        grid=(nb, ret_heads),
        in_specs=[col(0), col(1), col(2), col(3), tab, tab, lg_spec],
        out_specs=[pl.BlockSpec((t, hd), lambda b, h: (b, h)),
                   pl.BlockSpec((1, 1, hd, hd), lambda b, h: (b, h, 0, 0))],
        out_shape=[jax.ShapeDtypeStruct((nb * t, ret_heads * hd), BF16),
                   jax.ShapeDtypeStruct((nb, ret_heads, hd, hd), F32)],
        compiler_params=_params(("parallel", "parallel")),
        name="ret_prompt",
    )(proj, proj, proj, proj, cos_p, sin_p, lg_tab)

    rb = RET_CHUNK
    nseq = rb // dt
    cos_s, sin_s = _rope_tables(PAST_LEN + jnp.arange(dt, dtype=jnp.int32), hd)
    cos_s, sin_s = jnp.tile(cos_s, (nseq, 1)), jnp.tile(sin_s, (nseq, 1))
    off = (nb * t) // rb
    col = lambda part: pl.BlockSpec((rb, hd), lambda b, h: (b + off, h + part * ret_heads))
    tab = pl.BlockSpec((rb, hd), lambda b, h: (0, 0))
    st = pl.BlockSpec((nseq, 1, hd, hd), lambda b, h: (b, h, 0, 0))
    o_s, s_s = pl.pallas_call(
        functools.partial(_ret_sample_kernel, dt=dt),
        grid=(db // nseq, ret_heads),
        in_specs=[col(0), col(1), col(2), col(3), tab, tab, lg_spec, st],
        out_specs=[pl.BlockSpec((rb, hd), lambda b, h: (b, h)), st],
        out_shape=[jax.ShapeDtypeStruct((db * dt, ret_heads * hd), BF16),
                   jax.ShapeDtypeStruct((db, ret_heads, hd, hd), F32)],
        compiler_params=_params(("parallel", "parallel")),
        name="ret_sample",
    )(proj, proj, proj, proj, cos_s, sin_s, lg_tab, state_ret)
    return o_p, o_s, s_p, s_s


def _shifted(x, first, tseq_mask):
    prev = pltpu.roll(x, 1, 0)
    return jnp.where(tseq_mask, first, prev)


def _rwkv_prep_kernel(pr_ref, pk_ref, pv_ref, pt_ref, fr_ref, fk_ref, fv_ref, ft_ref,
                      mur_ref, muk_ref, muv_ref, mut_ref, w0_ref, w2_ref, a0_ref, a2_ref, g2_ref,
                      kk_ref, ka_ref, rk_ref,
                      r_out, w_out, k_out, v_out, a_out, b_out, g_out, bonus_out,
                      *, dt, prompt_blocks, w_lora, a_lora, g_lora, hd):
    i = pl.program_id(0)
    rows, cn = pr_ref.shape
    rown = lax.broadcasted_iota(jnp.int32, (rows, 1), 0)
    sample_block = jnp.zeros((rows, 1), jnp.int32) + (i >= prompt_blocks).astype(jnp.int32)
    is_first = (rown == 0) | ((rown % dt == 0) & (sample_block == 1))

    def lerp(p_ref, f_ref, mu_ref):
        p = p_ref[...]
        prev = _shifted(p, _expand_slots(f_ref[...], rows), is_first)
        return p + mu_ref[...] * (prev - p)

    r = lerp(pr_ref, fr_ref, mur_ref)
    kw = lerp(pk_ref, fk_ref, muk_ref)
    vw = lerp(pv_ref, fv_ref, muv_ref)
    tail = lerp(pt_ref, ft_ref, mut_ref)
    tw = tail.shape[1]
    wl = tail[:, :w_lora]
    al = tail[:, w_lora:w_lora + a_lora]
    glane = lax.broadcasted_iota(jnp.int32, (rows, tw - w_lora - a_lora), 1)
    gl = jnp.where(glane < g_lora, jax.nn.sigmoid(tail[:, w_lora + a_lora:]), 0.0)

    w_log = -jax.nn.softplus(-(w0_ref[...] + _dot(jnp.tanh(wl), w2_ref[...], HI))) - 0.5
    decay = jnp.exp(-jnp.exp(w_log))
    a = jax.nn.sigmoid(a0_ref[...] + _dot(al, a2_ref[...], HI))
    g = _dot(gl, g2_ref[...], HI)

    seg = _block_ones(LANES, hd, F32)

    def segsum(x):
        return jnp.concatenate([_dot(x[:, c:c + LANES], seg, HI) for c in range(0, cn, LANES)], axis=1)

    kk = kw * kk_ref[...]
    kk = kk / jnp.maximum(jnp.sqrt(segsum(kk * kk)), 1e-12)
    kn = kw * (1.0 + (a - 1.0) * ka_ref[...])
    r_out[...] = r
    w_out[...] = decay
    k_out[...] = kn
    v_out[...] = vw
    a_out[...] = -kk
    b_out[...] = kk * a
    g_out[...] = g
    bonus_out[...] = segsum(r * kn * rk_ref[...]) * vw


def _rwkv_prep(rows, proj_rkv, proj_tail, first_rkv, first_tail, mu_rkv, mu_tail, w0, w2, a0, a2, g2p,
               k_k, k_a, r_k_flat, rw, hd, g_lora, cn=512):
    n = rows.n
    cn = _pick(rw, cn)
    tw = proj_tail.shape[1]
    nblk = (rows.n_blocks, rw // cn)
    part = lambda p: pl.BlockSpec((rows.rb, cn), lambda i, j: (i, j + p * (rw // cn)))
    fpart = lambda p: pl.BlockSpec((rows.slots, 1, cn), lambda i, j: (i, 0, j + p * (rw // cn)))
    mupart = lambda p: pl.BlockSpec((1, cn), lambda i, j: (0, j + p * (rw // cn)))
    vec = pl.BlockSpec((1, cn), lambda i, j: (0, j))
    lora = lambda k: pl.BlockSpec((k, cn), lambda i, j: (0, j))
    out = pl.BlockSpec((rows.rb, cn), lambda i, j: (i, j))
    kern = functools.partial(_rwkv_prep_kernel, dt=rows.dt, prompt_blocks=rows.prompt_blocks,
                             w_lora=w2.shape[0], a_lora=a2.shape[0], g_lora=g_lora, hd=hd)
    return pl.pallas_call(
        kern,
        grid=nblk,
        in_specs=[part(0), part(1), part(2), pl.BlockSpec((rows.rb, tw), lambda i, j: (i, 0)),
                  fpart(0), fpart(1), fpart(2), pl.BlockSpec((rows.slots, 1, tw), lambda i, j: (i, 0, 0)),
                  mupart(0), mupart(1), mupart(2), pl.BlockSpec((1, tw), lambda i, j: (0, 0)),
                  vec, lora(w2.shape[0]), vec, lora(a2.shape[0]), lora(g2p.shape[0]),
                  vec, vec, vec],
        out_specs=[out] * 8,
        out_shape=[jax.ShapeDtypeStruct((n, rw), F32)] * 8,
        compiler_params=_params(("parallel", "arbitrary")),
        name="rwkv_prep",
    )(proj_rkv, proj_rkv, proj_rkv, proj_tail, first_rkv, first_rkv, first_rkv, first_tail,
      mu_rkv, mu_rkv, mu_rkv, mu_tail, w0, w2, a0, a2, g2p, k_k, k_a, r_k_flat)


def _rwkv_scan_kernel(r_ref, w_ref, k_ref, v_ref, a_ref, b_ref, *rest, hd, has_state):
    if has_state:
        s0_ref, o_ref, sout_ref, st_ref, ob_ref, vs_ref, row_ref = rest
    else:
        o_ref, sout_ref, st_ref, ob_ref, vs_ref, row_ref = rest
        s0_ref = None
    tb = pl.program_id(1)
    tc, width = r_ref.shape
    pw = 2 * hd
    npairs = width // pw

    @pl.when(tb == 0)
    def _():
        for p in range(npairs):
            if has_state:
                st_ref[p] = jnp.concatenate([s0_ref[0, 2 * p], s0_ref[0, 2 * p + 1]], axis=1)
            else:
                st_ref[p] = jnp.zeros((hd, pw), F32)

    half = npairs // 2
    ones2 = _block_ones(2 * pw, hd, BF16)
    sub = lax.broadcasted_iota(jnp.int32, (hd, pw), 0)
    lane = lax.broadcasted_iota(jnp.int32, (hd, pw), 1)
    diags = [(((lane % hd) - sub + hd) % hd == i).astype(F32).astype(BF16) for i in range(3)]
    lane8 = lax.broadcasted_iota(jnp.int32, (8, pw), 1)
    row16 = lax.broadcasted_iota(jnp.int32, (8, 2 * pw), 0)
    lane16 = lax.broadcasted_iota(jnp.int32, (8, 2 * pw), 1)
    sel2 = row16 == (lane16 // hd)

    def shift_in_head(x, i):
        return jnp.where(lane8 % hd >= i, pltpu.roll(x, i, 1), pltpu.roll(x, pw - hd + i, 1))

    def group(tg, carry):
        rows8 = pl.ds(pl.multiple_of(tg * 8, 8), 8)
        for idx, ref in enumerate((a_ref, w_ref, k_ref, b_ref, r_ref)):
            row_ref[idx] = ref[rows8, :]
        for p in range(npairs):
            cols = pl.ds(p * pw, pw)
            v8 = v_ref[rows8, cols]
            v1 = v8.astype(BF16).astype(F32)
            v2 = (v8 - v1).astype(BF16).astype(F32)
            vs_ref[0, :, cols] = v1
            vs_ref[1, :, cols] = shift_in_head(v2, 1)
            vs_ref[2, :, cols] = shift_in_head((v8 - v1) - v2, 2)

        def row(idx, p, j):
            return row_ref[idx, j:j + 1, pl.ds(p * pw, pw)]

        def side_by_side(tiles):
            return jnp.concatenate([jnp.concatenate([tiles[p], tiles[p + half]], axis=1) for p in range(half)],
                                   axis=0)

        def split(x, p):
            q, c = p % half, p // half
            return x[q * hd:(q + 1) * hd, c * pw:(c + 1) * pw]

        def moved_v(j):
            xs = []
            for p in range(npairs):
                x = None
                for i in range(3):
                    term = diags[i] * vs_ref[i, j:j + 1, pl.ds(p * pw, pw)].astype(BF16)
                    x = term if x is None else x + term
                xs.append(x)
            return _dot(side_by_side(xs), ones2)

        def emit_out(j, states):
            for p in range(half):
                r2 = jnp.concatenate([row(4, p, j), row(4, p + half, j)], axis=1)
                rsel = jnp.where(sel2, r2, 0.0).astype(BF16)
                o8 = _dot_nt(rsel, jnp.concatenate([states[p], states[p + half]], axis=1))
                ob_ref[p, j:j + 1, 0:hd] = o8[0:1, :]
                ob_ref[p, j:j + 1, hd:pw] = o8[1:2, :]
                ob_ref[p + half, j:j + 1, 0:hd] = o8[2:3, :]
                ob_ref[p + half, j:j + 1, hd:pw] = o8[3:4, :]

        vc_next = moved_v(0)
        states = None
        for j in range(8):
            old = [st_ref[p] for p in range(npairs)]
            sa = _dot(side_by_side([old[p] * row(0, p, j) for p in range(npairs)]).astype(BF16), ones2)
            if states is not None:
                emit_out(j - 1, states)
            vc = vc_next
            if j < 7:
                vc_next = moved_v(j + 1)
            states = []
            for p in range(npairs):
                s = old[p] * row(1, p, j) + split(sa, p) * row(3, p, j) + split(vc, p) * row(2, p, j)
                st_ref[p] = s
                states.append(s.astype(BF16))
        emit_out(7, states)
        for p in range(npairs):
            o_ref[rows8, pl.ds(p * pw, pw)] = ob_ref[p]
        return carry

    lax.fori_loop(0, tc // 8, group, 0)

    @pl.when(tb == pl.num_programs(1) - 1)
    def _():
        for p in range(npairs):
            s = st_ref[p]
            sout_ref[0, 2 * p] = s[:, :hd]
            sout_ref[0, 2 * p + 1] = s[:, hd:]


def _rwkv_scan(ops, nseq, t, row0, tc, heads, hd, state):
    width = heads * hd
    assert t % tc == 0 and row0 % tc == 0
    nt = t // tc
    blk = pl.BlockSpec((tc, width), lambda s, j: (row0 // tc + s * nt + j, 0))
    st = pl.BlockSpec((1, heads, hd, hd), lambda s, j: (s, 0, 0, 0))
    in_specs = [blk] * 6
    args = list(ops)
    if state is not None:
        in_specs.append(st)
        args.append(state)
    return pl.pallas_call(
        functools.partial(_rwkv_scan_kernel, hd=hd, has_state=state is not None),
        grid=(nseq, nt),
        in_specs=in_specs,
        out_specs=[pl.BlockSpec((tc, width), lambda s, j: (s * nt + j, 0)), st],
        out_shape=[jax.ShapeDtypeStruct((nseq * t, width), F32),
                   jax.ShapeDtypeStruct((nseq, heads, hd, hd), F32)],
        scratch_shapes=[pltpu.VMEM((heads // 2, hd, 2 * hd), F32), pltpu.VMEM((heads // 2, 8, 2 * hd), F32),
                        pltpu.VMEM((3, 8, width), F32), pltpu.VMEM((5, 8, width), F32)],
        compiler_params=_params(("parallel", "arbitrary")),
        name="rwkv_scan_state" if state is not None else "rwkv_scan",
    )(*args)


def _rwkv_post_kernel(op_ref, os_ref, bonus_ref, g_ref, lg_ref, lb_ref, out_ref, *, hd, first_blocks):
    cn = out_ref.shape[1]
    seg = _block_ones(LANES, hd, F32)

    def segmean(x):
        return jnp.concatenate([_dot(x[:, c:c + LANES], seg, HI) for c in range(0, cn, LANES)], axis=1) * (1.0 / hd)

    def body(o):
        mean = segmean(o)
        cen = o - mean
        var = segmean(cen * cen)
        y = cen * lax.rsqrt(var + LNX_EPS) * lg_ref[...] + lb_ref[...]
        out_ref[...] = ((y + bonus_ref[...]) * g_ref[...]).astype(out_ref.dtype)

    _from_either(pl.program_id(0), first_blocks, op_ref, os_ref, body)


def _rwkv_post(o_raw_p, o_raw_s, bonus, g, lnx_g, lnx_b, hd, rb=256, cn=512):
    n, rw = bonus.shape
    rb, cn = _pick(math.gcd(o_raw_p.shape[0], o_raw_s.shape[0]), rb), _pick(rw, cn)
    first_blocks = o_raw_p.shape[0] // rb
    blk = pl.BlockSpec((rb, cn), lambda i, j: (i, j))
    vec = pl.BlockSpec((1, cn), lambda i, j: (0, j))
    pair = [pl.BlockSpec((rb, cn), lambda i, j: (jnp.minimum(i, first_blocks - 1), j)),
            pl.BlockSpec((rb, cn), lambda i, j: (jnp.maximum(i - first_blocks, 0), j))]
    return pl.pallas_call(
        functools.partial(_rwkv_post_kernel, hd=hd, first_blocks=first_blocks),
        grid=(n // rb, rw // cn),
        in_specs=pair + [blk, blk, vec, vec],
        out_specs=blk,
        out_shape=jax.ShapeDtypeStruct((n, rw), BF16),
        compiler_params=_params(("parallel", "parallel")),
        name="rwkv_post",
    )(o_raw_p, o_raw_s, bonus, g, lnx_g, lnx_b)


def _mmt_kernel(wt_ref, x_ref, o_ref):
    o_ref[...] = _dot(wt_ref[...].astype(BF16), x_ref[...])


def _matmul_t(wt, xt, tm=512, tn=1024, name="matmul_t"):
    m, k = wt.shape
    n = xt.shape[1]
    tm, tn = _pick(m, tm), _pick(n, tn)
    return pl.pallas_call(
        _mmt_kernel,
        grid=(n // tn, m // tm),
        in_specs=[pl.BlockSpec((tm, k), lambda j, i: (i, 0)),
                  pl.BlockSpec((k, tn), lambda j, i: (0, j))],
        out_specs=pl.BlockSpec((tm, tn), lambda j, i: (i, j)),
        out_shape=jax.ShapeDtypeStruct((m, n), F32),
        compiler_params=_params(("parallel", "parallel")),
        name=name,
    )(wt, xt)


def _take_top(s, k):
    n = s.shape[0]
    row = lax.broadcasted_iota(jnp.int32, s.shape, 0).astype(F32)
    out = []
    for _ in range(k):
        m = jnp.max(s, axis=0, keepdims=True)
        first = jnp.min(jnp.where(s == m, row, float(n)), axis=0, keepdims=True)
        s = jnp.where(row == first, -jnp.inf, s)
        out.append(m)
    return out


def _peer_topk_kernel(q_ref, keys_ref, s0_ref, s1_ref, e0_ref, e1_ref, thr_ref, *, topk):
    kd = keys_ref.shape[3]
    cols = q_ref.shape[1]
    s0 = _dot(keys_ref[0, 0], q_ref[:kd, :], HI)
    s1 = _dot(keys_ref[0, 1], q_ref[kd:, :], HI)
    top0 = _take_top(s0, topk)
    top1 = _take_top(s1, topk)
    rowk = lax.broadcasted_iota(jnp.int32, (topk, cols), 0)
    t1 = jnp.zeros((topk, cols), F32)
    for b in range(topk):
        t1 = jnp.where(rowk == b, top1[b], t1)
    cand = jnp.concatenate([top0[a] + t1 for a in range(topk)], axis=0)
    best = _take_top(cand, topk)
    z = jnp.zeros((1, cols), F32)
    for b in best:
        z = z + jnp.exp(b - best[0])
    s0_ref[...] = s0
    s1_ref[...] = s1
    e0_ref[...] = jnp.exp(s0 - top0[0]) / z
    e1_ref[...] = jnp.exp(s1 - top1[0])
    thr_ref[0] = jnp.broadcast_to(best[topk - 1], thr_ref.shape[1:])


def _peer_topk(qt, sub_keys, tc=256):
    n = qt.shape[1]
    tc = _pick(n, tc)
    heads, _, nk, kd = sub_keys.shape
    blk = pl.BlockSpec((nk, tc), lambda i, h: (h, i))
    shp = jax.ShapeDtypeStruct((heads * nk, n), F32)
    return pl.pallas_call(
        functools.partial(_peer_topk_kernel, topk=PEER_TOPK),
        grid=(n // tc, heads),
        in_specs=[pl.BlockSpec((2 * kd, tc), lambda i, h: (h, i)),
                  pl.BlockSpec((1, 2, nk, kd), lambda i, h: (h, 0, 0, 0))],
        out_specs=[blk] * 4 + [pl.BlockSpec((1, 8, tc), lambda i, h: (h, 0, i))],
        out_shape=[shp] * 4 + [jax.ShapeDtypeStruct((heads, 8, n), F32)],
        compiler_params=_params(("parallel", "parallel")),
        name="peer_topk",
    )(qt, sub_keys)


def _gelu(x):
    return 0.5 * x * (1.0 + lax.erf(x * (2.0 ** -0.5)))


def _peer_dense_kernel(u_ref, ht_ref, vt_ref, s0_ref, e0_ref, s1_ref, e1_ref, thr_ref, o_ref, act_ref, coef_ref,
                       *, heads, nk, jc):
    e = pl.program_id(1)
    eb = u_ref.shape[0]
    d = vt_ref.shape[0]
    ni = eb // nk
    dq = d // ni

    @pl.when(e == 0)
    def _():
        o_ref[...] = jnp.zeros_like(o_ref)
        coef_ref[1] = jnp.zeros(coef_ref.shape[1:], coef_ref.dtype)

    act_ref[...] = _dot(u_ref[...], ht_ref[...])
    cur = e % 2
    prev = (e + 1) % 2
    thr = thr_ref[...]

    def one_i(ii, carry):
        orow = pl.ds(pl.multiple_of(ii * dq, dq), dq)
        o_ref[orow, :] += _dot(vt_ref[orow, :], coef_ref[prev])
        s0 = s0_ref[0, ii]
        e0 = e0_ref[0, ii]
        for j0 in range(0, nk, jc):
            g = None
            for h in range(heads):
                jrows = slice(h * nk + j0, h * nk + j0 + jc)
                hit = (s0[h:h + 1] + s1_ref[jrows, :]) >= thr[h:h + 1]
                term = jnp.where(hit, e0[h:h + 1] * e1_ref[jrows, :], 0.0)
                g = term if g is None else g + term
            arow = pl.ds(pl.multiple_of(ii * nk + j0, jc), jc)
            coef_ref[cur, arow, :] = (g * _gelu(act_ref[arow, :])).astype(coef_ref.dtype)
        return carry

    lax.fori_loop(0, ni, one_i, 0)


def _peer_dense(ht, u, vt, s0r, e0r, s1, e1, thr, heads, nk, tp=512, eb=512):
    d, n = ht.shape
    tp = _pick(n, tp)
    nblk = u.shape[0] // eb
    ni = eb // nk
    once = pl.Buffered(1)
    tok = pl.BlockSpec((heads * nk, tp), lambda i, e: (0, i), pipeline_mode=once)
    sl = pl.BlockSpec((1, ni, heads, tp), lambda i, e: (jnp.minimum(e, nblk - 1), 0, 0, i))
    return pl.pallas_call(
        functools.partial(_peer_dense_kernel, heads=heads, nk=nk, jc=nk // 2),
        grid=(n // tp, nblk + 1),
        in_specs=[pl.BlockSpec((eb, d), lambda i, e: (jnp.minimum(e, nblk - 1), 0)),
                  pl.BlockSpec((d, tp), lambda i, e: (0, i), pipeline_mode=once),
                  pl.BlockSpec((d, eb), lambda i, e: (0, jnp.maximum(e - 1, 0))),
                  sl, sl, tok, tok, pl.BlockSpec((heads, tp), lambda i, e: (0, i))],
        out_specs=pl.BlockSpec((d, tp), lambda i, e: (0, i)),
        out_shape=jax.ShapeDtypeStruct((d, n), F32),
        scratch_shapes=[pltpu.VMEM((eb, tp), F32), pltpu.VMEM((2, eb, tp), BF16)],
        compiler_params=_params(("parallel", "arbitrary")),
        name="peer_dense",
    )(u, ht, vt, s0r, e0r, s1, e1, thr)


def _by_expert_block(x, heads, nk, ni):
    n = x.shape[1]
    return x.reshape(heads, nk // ni, ni, n).transpose(1, 2, 0, 3)


def _layer(xp, xs, s_ret, s_rwkv, s_shift, cp, cs, p):
    nb, t, d = xp.shape
    db, dt, _ = xs.shape
    rows = _Rows(nb, t, db, dt)
    ret_heads, ret_hd = s_ret.shape[1], s_ret.shape[2]
    rw_heads, rw_hd = s_rwkv.shape[1], s_rwkv.shape[2]
    ret_w, rw = ret_heads * ret_hd, rw_heads * rw_hd
    w_lora, a_lora, g_lora = p['w2'].shape[0], p['a2'].shape[0], p['g2'].shape[0]
    rwkv_in = s_shift.shape[-1]
    ret_in = 4 * ret_w
    assert p['w_in'].shape[1] == ret_in + rwkv_in and rwkv_in == 3 * rw + w_lora + a_lora + g_lora

    xp, xs = xp.reshape(nb * t, d), xs.reshape(db * dt, d)
    mod = _ada(jnp.concatenate([cp, cs], axis=0), p['ada_w'], p['ada_b'])
    mod_ext = rows.extend(mod)

    h = _prenorm(rows, xp, xs, p['pre_mix_g'], mod_ext, d)

    tm = _pick(rows.n, 1024)
    w_in_t = p['w_in'].T
    proj_ret = _matmul(h, w_in_t, col_block0=0, n_out=ret_in, tm=tm, tn=512, name="in_proj_ret")
    proj_rkv = _matmul(h, w_in_t, col_block0=ret_in // 512, n_out=3 * rw, tm=tm, tn=512, name="in_proj_rkv")
    tail0 = ret_in + 3 * rw
    tail_w = -(-(w_lora + a_lora + g_lora) // LANES) * LANES
    proj_tail = _matmul(h, w_in_t, col_block0=tail0 // LANES, n_out=tail_w, tm=tm, tn=LANES, name="in_proj_tail")
    tail_valid = w_lora + a_lora + g_lora

    o_ret_p, o_ret_s, sr_p, sr_s = _retention(rows, proj_ret, s_ret, ret_heads, ret_hd)

    def firsts(proj, shift_cols):
        w = proj.shape[1]
        starts = np.arange(rows.prompt_blocks) * rows.rb
        last = proj[rows.rb - 1:rows.n_prompt:rows.rb]
        prev = jnp.concatenate([jnp.zeros((1, w), F32), last[:-1]], axis=0)
        prev = jnp.where((starts % t == 0)[:, None], 0.0, prev)
        fp = jnp.zeros((rows.prompt_blocks, rows.slots, w), F32).at[:, 0].set(prev)
        return jnp.concatenate([fp.reshape(-1, w), shift_cols], axis=0)[:, None, :]

    shift_tail = jnp.pad(s_shift[:, 3 * rw:], ((0, 0), (0, tail_w - tail_valid)))
    first_rkv = firsts(proj_rkv, s_shift[:, :3 * rw])
    first_tail = firsts(proj_tail, shift_tail)
    mu = p['shift_mu']
    mu_tail = jnp.pad(mu[3 * rw:], (0, tail_w - tail_valid)).reshape(1, tail_w)
    g2p = jnp.pad(p['g2'], ((0, tail_w - w_lora - a_lora - g_lora), (0, 0)))
    vec = lambda a: a.reshape(1, rw)
    r_, w_, k_, v_, a_, b_, g_, bonus = _rwkv_prep(
        rows, proj_rkv, proj_tail, first_rkv, first_tail, mu[:3 * rw].reshape(1, 3 * rw), mu_tail,
        vec(p['w0']), p['w2'], vec(p['a0']), p['a2'], g2p, vec(p['k_k']), vec(p['k_a']), vec(p['r_k']),
        rw, rw_hd, g_lora)
    ops = (r_, w_, k_, v_, a_, b_)
    o_rw_p, sw_p = _rwkv_scan(ops, nb, t, 0, 256, rw_heads, rw_hd, None)
    o_rw_s, sw_s = _rwkv_scan(ops, db, dt, nb * t, dt, rw_heads, rw_hd, s_rwkv)
    o_rw = _rwkv_post(o_rw_p, o_rw_s, bonus, g_, vec(p['lnx_g']), vec(p['lnx_b']), rw_hd)

    mix = _matmul_out(o_ret_p, o_ret_s, o_rw, p['w_out'])
    x1, h2t = _postmix(rows, xp, xs, mix, p['post_mix_g'], p['pre_ffn_g'], mod_ext, d)

    heads, _, nk, kd = p['peer_sub_keys'].shape
    qt = _matmul_t(p['peer_wq'].T, h2t, name="peer_q")
    s0, s1, e0, e1, thr = _peer_topk(qt, p['peer_sub_keys'])
    eb = 512
    ni = eb // nk
    peer_t = _peer_dense(h2t, p['peer_u'].astype(BF16), p['peer_v'].T.astype(BF16),
                         _by_expert_block(s0, heads, nk, ni), _by_expert_block(e0, heads, nk, ni),
                         s1, e1, thr[:, 0, :], heads, nk, eb=eb)
    yp = _final(rows, x1, peer_t, p['post_ffn_g'], mod_ext, d, 0, rows.prompt_blocks)
    ys = _final(rows, x1, peer_t, p['post_ffn_g'], mod_ext, d, rows.prompt_blocks, rows.n_blocks - rows.prompt_blocks)

    n_p = nb * t
    last_p = slice(t - 1, n_p, t)
    last_s = slice(n_p + dt - 1, None, dt)
    pw_last = lambda idx: jnp.concatenate([proj_rkv[idx], proj_tail[idx, :tail_valid]], axis=1)
    return (yp.reshape(nb, t, d), ys.reshape(db, dt, d), sr_p, sr_s, sw_p, sw_s,
            pw_last(last_p), pw_last(last_s))


def kernel(x_prompt, x_sample, state_ret, state_rwkv, state_shift, c_prompt, c_sample, ada_w, ada_b, pre_mix_g, post_mix_g, pre_ffn_g, post_ffn_g, w_in, shift_mu, w0, w2, a0, a2, g2, k_k, k_a, r_k, lnx_g, lnx_b, w_out, peer_wq, peer_sub_keys, peer_u, peer_v):
    depth = ada_w.shape[0]
    assert depth == 1, "prompt and sample tokens are stacked per layer; deeper stacks need per-layer restacking"
    prm = dict(ada_w=ada_w[0], ada_b=ada_b[0], pre_mix_g=pre_mix_g[0], post_mix_g=post_mix_g[0],
               pre_ffn_g=pre_ffn_g[0], post_ffn_g=post_ffn_g[0], w_in=w_in[0], shift_mu=shift_mu[0],
               w0=w0[0], w2=w2[0], a0=a0[0], a2=a2[0], g2=g2[0], k_k=k_k[0], k_a=k_a[0], r_k=r_k[0],
               lnx_g=lnx_g[0], lnx_b=lnx_b[0], w_out=w_out[0], peer_wq=peer_wq[0],
               peer_sub_keys=peer_sub_keys[0], peer_u=peer_u[0], peer_v=peer_v[0])
    yp, ys, sr_p, sr_s, sw_p, sw_s, ss_p, ss_s = _layer(
        x_prompt, x_sample, state_ret[0], state_rwkv[0], state_shift[0], c_prompt, c_sample, prm)
    sd, wd, hd = state_ret.dtype, state_rwkv.dtype, state_shift.dtype
    return (yp, ys, sr_p[None].astype(sd), sr_s[None].astype(sd), sw_p[None].astype(wd), sw_s[None].astype(wd),
            ss_p[None].astype(hd), ss_s[None].astype(hd))
```

```python
import functools
import math

import numpy as np
import jax
import jax.numpy as jnp
from jax import lax
from jax.experimental import pallas as pl
from jax.experimental.pallas import tpu as pltpu

PAST_LEN = 16384
RET_CHUNK = 128
ROPE_BASE = 10000.0
LNX_EPS = 64e-5
NORM_EPS = 1e-6
PEER_TOPK = 16
RET_HEADS_PER_STEP = 4

LANES = 128
ROW_BLOCK = 128
VMEM_LIMIT = 56 * 1024 * 1024

HI = lax.Precision.HIGHEST
F32 = jnp.float32
BF16 = jnp.bfloat16


def _pick(n, target, mult=LANES):
    best = None
    for c in range(mult, min(n, target) + 1, mult):
        if n % c == 0:
            best = c
    assert best is not None, (n, target, mult)
    return best


def _params(sem, vmem=VMEM_LIMIT):
    return pltpu.CompilerParams(dimension_semantics=sem, vmem_limit_bytes=vmem)


def _dot(a, b, precision=None):
    return jnp.dot(a, b, preferred_element_type=F32, precision=precision)


def _dot_nt(a, b, precision=None):
    return lax.dot_general(a, b, (((1,), (1,)), ((), ())), preferred_element_type=F32, precision=precision)


def _dot_tn(a, b, precision=None):
    return lax.dot_general(a, b, (((0,), (0,)), ((), ())), preferred_element_type=F32, precision=precision)


def _split_bf16(x, terms):
    out = []
    for _ in range(terms - 1):
        t = x.astype(BF16)
        out.append(t)
        x = x - t.astype(F32)
    return out + [x.astype(BF16)]


def _dot_split(x, w_terms):
    xh, xl = _split_bf16(x, 2)
    wh, wl = w_terms
    return _dot(xh, wh) + (_dot(xl, wh) + _dot(xh, wl))


def _dot_exact01(x, ones):
    return sum(_dot(t, ones) for t in _split_bf16(x, 3))


def _block_ones(n, seg, dtype):
    r = lax.broadcasted_iota(jnp.int32, (n, n), 0) // seg
    c = lax.broadcasted_iota(jnp.int32, (n, n), 1) // seg
    return (r == c).astype(dtype)


def _ada_kernel(c_ref, w_ref, b_ref, o_ref):
    c = c_ref[...]
    sc = (c * jax.nn.sigmoid(c)).astype(BF16)
    o_ref[...] = _dot(sc, w_ref[...].astype(BF16)) + b_ref[...]


def _ada(c_all, ada_w, ada_b, tn=512):
    m, d = c_all.shape
    n = ada_w.shape[1]
    return pl.pallas_call(
        _ada_kernel,
        grid=(n // tn,),
        in_specs=[pl.BlockSpec((m, d), lambda j: (0, 0)),
                  pl.BlockSpec((d, tn), lambda j: (0, j)),
                  pl.BlockSpec((1, tn), lambda j: (0, j))],
        out_specs=pl.BlockSpec((m, tn), lambda j: (0, j)),
        out_shape=jax.ShapeDtypeStruct((m, n), F32),
        compiler_params=_params(("arbitrary",)),
        name="ada",
    )(c_all, ada_w, ada_b.reshape(1, n))


def _expand_slots(m, rows):
    slots, _, d = m.shape
    return jnp.broadcast_to(m, (slots, rows // slots, d)).reshape(rows, d)


def _rms(x, g):
    return x * lax.rsqrt(jnp.mean(x * x, axis=-1, keepdims=True) + NORM_EPS) * g


def _from_either(i, first_blocks, a_ref, b_ref, body):
    @pl.when(i < first_blocks)
    def _():
        body(a_ref[...])

    @pl.when(i >= first_blocks)
    def _():
        body(b_ref[...])


def _two_specs(block, first_blocks):
    return [pl.BlockSpec(block, lambda i, *_: (jnp.minimum(i, first_blocks - 1), 0)),
            pl.BlockSpec(block, lambda i, *_: (jnp.maximum(i - first_blocks, 0), 0))]


def _prenorm_kernel(xp_ref, xs_ref, g_ref, sh_ref, sc_ref, h_ref, *, prompt_blocks):
    def body(x):
        rows = x.shape[0]
        y = _rms(x, g_ref[...])
        h = y * (1.0 + _expand_slots(sc_ref[...], rows)) + _expand_slots(sh_ref[...], rows)
        h_ref[...] = h.astype(h_ref.dtype)

    _from_either(pl.program_id(0), prompt_blocks, xp_ref, xs_ref, body)


def _postmix_kernel(xp_ref, xs_ref, m_ref, gpost_ref, gpre_ref, gate_ref, sh_ref, sc_ref, x1_ref, ht_ref,
                    *, prompt_blocks):
    def body(x):
        rows = x.shape[0]
        x1 = x + _expand_slots(gate_ref[...], rows) * _rms(m_ref[...], gpost_ref[...])
        x1_ref[...] = x1
        y = _rms(x1, gpre_ref[...])
        h = y * (1.0 + _expand_slots(sc_ref[...], rows)) + _expand_slots(sh_ref[...], rows)
        ht_ref[...] = h.T.astype(ht_ref.dtype)

    _from_either(pl.program_id(0), prompt_blocks, xp_ref, xs_ref, body)


def _final_kernel(x_ref, mt_ref, gpost_ref, gate_ref, y_ref):
    x = x_ref[...]
    rows = x.shape[0]
    y_ref[...] = x + _expand_slots(gate_ref[...], rows) * _rms(mt_ref[...].T, gpost_ref[...])


class _Rows:
    def __init__(self, nb, t, db, dt):
        self.n_prompt = nb * t
        self.n_sample = db * dt
        self.n = self.n_prompt + self.n_sample
        self.rb = ROW_BLOCK
        assert t % self.rb == 0 and self.rb % dt == 0 and self.n_sample % self.rb == 0
        self.slots = self.rb // dt
        self.prompt_blocks = self.n_prompt // self.rb
        self.blocks_per_seq = t // self.rb
        self.n_blocks = self.n // self.rb
        self.nb, self.t, self.db, self.dt = nb, t, db, dt

    def slot_block(self, i):
        return jnp.where(i < self.prompt_blocks, i // self.blocks_per_seq, i - self.prompt_blocks + self.nb)

    def extend(self, m):
        mp = jnp.repeat(m[: self.nb], self.slots, axis=0)
        return jnp.concatenate([mp, m[self.nb:]], axis=0)[:, None, :]


def _mod_spec(rows, d, chunk, block0=0):
    return pl.BlockSpec((rows.slots, 1, d), lambda i: (rows.slot_block(i + block0), 0, chunk))


def _prenorm(rows, xp, xs, g, mod_ext, d):
    row = pl.BlockSpec((rows.rb, d), lambda i: (i, 0))
    vec = pl.BlockSpec((1, d), lambda i: (0, 0))
    return pl.pallas_call(
        functools.partial(_prenorm_kernel, prompt_blocks=rows.prompt_blocks),
        grid=(rows.n_blocks,),
        in_specs=_two_specs((rows.rb, d), rows.prompt_blocks) + [vec, _mod_spec(rows, d, 0), _mod_spec(rows, d, 1)],
        out_specs=row,
        out_shape=jax.ShapeDtypeStruct((rows.n, d), BF16),
        compiler_params=_params(("parallel",)),
        name="prenorm",
    )(xp, xs, g.reshape(1, d), mod_ext, mod_ext)


def _postmix(rows, xp, xs, mix, gpost, gpre, mod_ext, d):
    row = pl.BlockSpec((rows.rb, d), lambda i: (i, 0))
    vec = pl.BlockSpec((1, d), lambda i: (0, 0))
    return pl.pallas_call(
        functools.partial(_postmix_kernel, prompt_blocks=rows.prompt_blocks),
        grid=(rows.n_blocks,),
        in_specs=_two_specs((rows.rb, d), rows.prompt_blocks)
        + [row, vec, vec, _mod_spec(rows, d, 2), _mod_spec(rows, d, 3), _mod_spec(rows, d, 4)],
        out_specs=[row, pl.BlockSpec((d, rows.rb), lambda i: (0, i))],
        out_shape=[jax.ShapeDtypeStruct((rows.n, d), F32), jax.ShapeDtypeStruct((d, rows.n), BF16)],
        compiler_params=_params(("parallel",)),
        name="postmix",
    )(xp, xs, mix, gpost.reshape(1, d), gpre.reshape(1, d), mod_ext, mod_ext, mod_ext)


def _final(rows, x1, peer_t, gpost, mod_ext, d, block0, n_blocks):
    vec = pl.BlockSpec((1, d), lambda i: (0, 0))
    return pl.pallas_call(
        _final_kernel,
        grid=(n_blocks,),
        in_specs=[pl.BlockSpec((rows.rb, d), lambda i: (i + block0, 0)),
                  pl.BlockSpec((d, rows.rb), lambda i: (0, i + block0)),
                  vec, _mod_spec(rows, d, 5, block0)],
        out_specs=pl.BlockSpec((rows.rb, d), lambda i: (i, 0)),
        out_shape=jax.ShapeDtypeStruct((n_blocks * rows.rb, d), F32),
        compiler_params=_params(("parallel",)),
        name="final",
    )(x1, peer_t, gpost.reshape(1, d), mod_ext)


def _mm_kernel(a_ref, wt_ref, o_ref, wbf_ref):
    @pl.when(pl.program_id(1) == 0)
    def _():
        wbf_ref[...] = wt_ref[...].astype(BF16)

    o_ref[...] = _dot_nt(a_ref[...], wbf_ref[...]).astype(o_ref.dtype)


def _matmul(a, wt, *, col_block0, n_out, tm, tn, name):
    m, k = a.shape
    return pl.pallas_call(
        _mm_kernel,
        grid=(pl.cdiv(n_out, tn), m // tm),
        in_specs=[pl.BlockSpec((tm, k), lambda j, i: (i, 0)),
                  pl.BlockSpec((tn, k), lambda j, i: (j + col_block0, 0))],
        out_specs=pl.BlockSpec((tm, tn), lambda j, i: (i, j)),
        out_shape=jax.ShapeDtypeStruct((m, n_out), F32),
        scratch_shapes=[pltpu.VMEM((tn, k), BF16)],
        compiler_params=_params(("parallel", "arbitrary")),
        name=name,
    )(a, wt)


def _mm2_kernel(ap_ref, as_ref, b_ref, wa_ref, wb_ref, o_ref, *, first_blocks):
    def body(a):
        o_ref[...] = _dot(a, wa_ref[...].astype(BF16)) + _dot(b_ref[...], wb_ref[...].astype(BF16))

    _from_either(pl.program_id(1), first_blocks, ap_ref, as_ref, body)


def _matmul_out(o_ret_p, o_ret_s, o_rw, w_out, tn=512):
    m, kb = o_rw.shape
    ka = o_ret_p.shape[1]
    tm = _pick(math.gcd(o_ret_p.shape[0], o_ret_s.shape[0]), 1024)
    first_blocks = o_ret_p.shape[0] // tm
    n = w_out.shape[1]
    tn = _pick(n, tn)
    assert ka == kb
    pair = [pl.BlockSpec((tm, ka), lambda j, i: (jnp.minimum(i, first_blocks - 1), 0)),
            pl.BlockSpec((tm, ka), lambda j, i: (jnp.maximum(i - first_blocks, 0), 0))]
    return pl.pallas_call(
        functools.partial(_mm2_kernel, first_blocks=first_blocks),
        grid=(n // tn, m // tm),
        in_specs=pair + [pl.BlockSpec((tm, kb), lambda j, i: (i, 0)),
                         pl.BlockSpec((ka, tn), lambda j, i: (0, j)),
                         pl.BlockSpec((kb, tn), lambda j, i: (1, j))],
        out_specs=pl.BlockSpec((tm, tn), lambda j, i: (i, j)),
        out_shape=jax.ShapeDtypeStruct((m, n), F32),
        compiler_params=_params(("parallel", "parallel")),
        name="out_proj",
    )(o_ret_p, o_ret_s, o_rw, w_out, w_out)


def _rope_tables(pos, hd):
    half = hd // 2
    inv = ROPE_BASE ** (-jnp.arange(half, dtype=F32) / half)
    ang = pos.astype(F32)[:, None] * inv[None, :]
    cos, sin = jnp.cos(ang), jnp.sin(ang)
    return jnp.concatenate([cos, cos], axis=-1), jnp.concatenate([-sin, sin], axis=-1)


def _rope(x, cos, sin_signed):
    half = x.shape[-1] // 2
    return x * cos + pltpu.roll(x, half, 1) * sin_signed


def _ret_finish(o, gate):
    o = o * lax.rsqrt(jnp.mean(o * o, axis=-1, keepdims=True) + NORM_EPS)
    return o * (gate * jax.nn.sigmoid(gate))


def _ret_prompt_kernel(q_ref, k_ref, v_ref, g_ref, cos_ref, sin_ref, lg_ref, o_ref, s_ref, *, chunk, hd):
    t = q_ref.shape[0]
    nh = q_ref.shape[1] // hd
    ri = lax.broadcasted_iota(jnp.int32, (chunk, chunk), 0)
    ci = lax.broadcasted_iota(jnp.int32, (chunk, chunk), 1)
    causal = ri >= ci
    diff = jnp.where(causal, ri - ci, 0).astype(F32)
    rowf = lax.broadcasted_iota(jnp.int32, (chunk, hd), 0).astype(F32)
    lgs = [lg_ref[i, 0:1, :] for i in range(nh)]
    masks = [jnp.where(causal, jnp.exp(diff * lg), 0.0) for lg in lgs]
    q_decs = [jnp.exp((rowf + 1.0) * lg) for lg in lgs]
    k_decs = [jnp.exp((chunk - 1.0 - rowf) * lg) for lg in lgs]
    c_decs = [jnp.exp(float(chunk) * lg) for lg in lgs]

    def step(c, states):
        r0 = pl.multiple_of(c * chunk, chunk)
        rows = pl.ds(r0, chunk)
        cos, sin = cos_ref[rows, :], sin_ref[rows, :]
        cols = [slice(i * hd, (i + 1) * hd) for i in range(nh)]
        qs = [_rope(q_ref[rows, cl], cos, sin) for cl in cols]
        ks = [_rope(k_ref[rows, cl], cos, sin) * (hd ** -0.5) for cl in cols]
        vs = [v_ref[rows, cl] for cl in cols]
        atts = [_dot_nt(q.astype(BF16), k.astype(BF16)) * m for q, k, m in zip(qs, ks, masks)]
        cross = [_dot((q * qd).astype(BF16), s.astype(BF16)) for q, qd, s in zip(qs, q_decs, states)]
        inner = [_dot(a.astype(BF16), v.astype(BF16)) for a, v in zip(atts, vs)]
        new = [s * cd + _dot_tn(k * kd, v, HI) for s, cd, k, kd, v in zip(states, c_decs, ks, k_decs, vs)]
        for i, cl in enumerate(cols):
            o_ref[rows, cl] = _ret_finish(inner[i] + cross[i], g_ref[rows, cl]).astype(o_ref.dtype)
        return tuple(new)

    final = lax.fori_loop(0, t // chunk, step, tuple(jnp.zeros((hd, hd), F32) for _ in range(nh)))
    for i in range(nh):
        s_ref[0, i] = final[i]


def _ret_sample_kernel(q_ref, k_ref, v_ref, g_ref, cos_ref, sin_ref, lg_ref, s0_ref, o_ref, s_ref, *, dt):
    rows, hd = q_ref.shape
    nseq = rows // dt
    lg = lg_ref[0, 0:1, :]
    ri = lax.broadcasted_iota(jnp.int32, (rows, rows), 0)
    ci = lax.broadcasted_iota(jnp.int32, (rows, rows), 1)
    ok = (ri >= ci) & ((ri // dt) == (ci // dt))
    diff = jnp.where(ok, ri - ci, 0).astype(F32)
    mask = jnp.where(ok, jnp.exp(diff * lg), 0.0)
    rown = lax.broadcasted_iota(jnp.int32, (rows, hd), 0)
    posf = (rown % dt).astype(F32)
    q_dec = jnp.exp((posf + 1.0) * lg)
    k_dec = jnp.exp((dt - 1.0 - posf) * lg)
    c_dec = jnp.exp(float(dt) * lg)
    cos, sin = cos_ref[...], sin_ref[...]
    qc = _rope(q_ref[...], cos, sin)
    kc = _rope(k_ref[...], cos, sin) * (hd ** -0.5)
    vc = v_ref[...]
    att = _dot_nt(qc.astype(BF16), kc.astype(BF16)) * mask
    inner = _dot(att.astype(BF16), vc.astype(BF16))
    qd = qc * q_dec
    kd = kc * k_dec
    seq = rown // dt
    cross = jnp.zeros((rows, hd), F32)
    for s in range(nseq):
        s0 = s0_ref[s, 0]
        mine = seq == s
        cross = cross + _dot(jnp.where(mine, qd, 0.0).astype(BF16), s0.astype(BF16))
        s_ref[s, 0] = s0 * c_dec + _dot_tn(jnp.where(mine, kd, 0.0), vc, HI)
    o_ref[...] = _ret_finish(inner + cross, g_ref[...]).astype(o_ref.dtype)


def _retention(rows, proj, state_ret, ret_heads, hd):
    nb, t, db, dt = rows.nb, rows.t, rows.db, rows.dt
    lg = jnp.log1p(-jnp.exp2(-5.0 - jnp.arange(ret_heads, dtype=F32)))
    lg_tab = jnp.broadcast_to(lg[:, None, None], (ret_heads, 8, hd))
    lg_spec = pl.BlockSpec((1, 8, hd), lambda b, h: (h, 0, 0))
    assert hd == RET_CHUNK and t % RET_CHUNK == 0 and dt % RET_CHUNK != 0

    cos_p, sin_p = _rope_tables(jnp.arange(t, dtype=jnp.int32), hd)
    nh = RET_HEADS_PER_STEP if ret_heads % RET_HEADS_PER_STEP == 0 else 1
    hsteps = ret_heads // nh
    col = lambda part: pl.BlockSpec((t, nh * hd), lambda b, h: (b, h + part * hsteps))
    tab = pl.BlockSpec((t, hd), lambda b, h: (0, 0))
    o_p, s_p = pl.pallas_call(
        functools.partial(_ret_prompt_kernel, chunk=RET_CHUNK, hd=hd),
        grid=(nb, hsteps),
        in_specs=[col(0), col(1), col(2), col(3), tab, tab, pl.BlockSpec((nh, 8, hd), lambda b, h: (h, 0, 0))],
        out_specs=[pl.BlockSpec((t, nh * hd), lambda b, h: (b, h)),
                   pl.BlockSpec((1, nh, hd, hd), lambda b, h: (b, h, 0, 0))],
        out_shape=[jax.ShapeDtypeStruct((nb * t, ret_heads * hd), BF16),
                   jax.ShapeDtypeStruct((nb, ret_heads, hd, hd), F32)],
        compiler_params=_params(("parallel", "parallel")),
        name="ret_prompt",
    )(proj, proj, proj, proj, cos_p, sin_p, lg_tab)

    rb = RET_CHUNK
    nseq = rb // dt
    cos_s, sin_s = _rope_tables(PAST_LEN + jnp.arange(dt, dtype=jnp.int32), hd)
    cos_s, sin_s = jnp.tile(cos_s, (nseq, 1)), jnp.tile(sin_s, (nseq, 1))
    off = (nb * t) // rb
    col = lambda part: pl.BlockSpec((rb, hd), lambda b, h: (b + off, h + part * ret_heads))
    tab = pl.BlockSpec((rb, hd), lambda b, h: (0, 0))
    st = pl.BlockSpec((nseq, 1, hd, hd), lambda b, h: (b, h, 0, 0))
    o_s, s_s = pl.pallas_call(
        functools.partial(_ret_sample_kernel, dt=dt),
        grid=(db // nseq, ret_heads),
        in_specs=[col(0), col(1), col(2), col(3), tab, tab, lg_spec, st],
        out_specs=[pl.BlockSpec((rb, hd), lambda b, h: (b, h)), st],
        out_shape=[jax.ShapeDtypeStruct((db * dt, ret_heads * hd), BF16),
                   jax.ShapeDtypeStruct((db, ret_heads, hd, hd), F32)],
        compiler_params=_params(("parallel", "parallel")),
        name="ret_sample",
    )(proj, proj, proj, proj, cos_s, sin_s, lg_tab, state_ret)
    return o_p, o_s, s_p, s_s


def _shifted(x, first, tseq_mask):
    prev = pltpu.roll(x, 1, 0)
    return jnp.where(tseq_mask, first, prev)


def _rwkv_prep_kernel(pr_ref, pk_ref, pv_ref, pt_ref, fr_ref, fk_ref, fv_ref, ft_ref,
                      mur_ref, muk_ref, muv_ref, mut_ref, w0_ref, w2h_ref, w2l_ref, a0_ref, a2h_ref, a2l_ref,
                      g2h_ref, g2l_ref,
                      kk_ref, ka_ref, rk_ref,
                      r_out, w_out, k_out, v_out, a_out, b_out, g_out, bonus_out,
                      *, dt, prompt_blocks, w_lora, a_lora, g_lora, hd):
    i = pl.program_id(0)
    rows, cn = pr_ref.shape
    rown = lax.broadcasted_iota(jnp.int32, (rows, 1), 0)
    sample_block = jnp.zeros((rows, 1), jnp.int32) + (i >= prompt_blocks).astype(jnp.int32)
    is_first = (rown == 0) | ((rown % dt == 0) & (sample_block == 1))

    def lerp(p_ref, f_ref, mu_ref):
        p = p_ref[...]
        prev = _shifted(p, _expand_slots(f_ref[...], rows), is_first)
        return p + mu_ref[...] * (prev - p)

    r = lerp(pr_ref, fr_ref, mur_ref)
    kw = lerp(pk_ref, fk_ref, muk_ref)
    vw = lerp(pv_ref, fv_ref, muv_ref)
    tail = lerp(pt_ref, ft_ref, mut_ref)
    tw = tail.shape[1]
    wl = tail[:, :w_lora]
    al = tail[:, w_lora:w_lora + a_lora]
    glane = lax.broadcasted_iota(jnp.int32, (rows, tw - w_lora - a_lora), 1)
    gl = jnp.where(glane < g_lora, jax.nn.sigmoid(tail[:, w_lora + a_lora:]), 0.0)

    w_log = -jax.nn.softplus(-(w0_ref[...] + _dot_split(jnp.tanh(wl), (w2h_ref[...], w2l_ref[...])))) - 0.5
    decay = jnp.exp(-jnp.exp(w_log))
    a = jax.nn.sigmoid(a0_ref[...] + _dot_split(al, (a2h_ref[...], a2l_ref[...])))
    g = _dot_split(gl, (g2h_ref[...], g2l_ref[...]))

    seg = _block_ones(LANES, hd, BF16)

    def segsum(x):
        return jnp.concatenate([_dot_exact01(x[:, c:c + LANES], seg) for c in range(0, cn, LANES)], axis=1)

    kk = kw * kk_ref[...]
    kk = kk / jnp.maximum(jnp.sqrt(segsum(kk * kk)), 1e-12)
    kn = kw * (1.0 + (a - 1.0) * ka_ref[...])
    r_out[...] = r
    w_out[...] = decay
    k_out[...] = kn
    v_out[...] = vw
    a_out[...] = -kk
    b_out[...] = kk * a
    g_out[...] = g
    bonus_out[...] = segsum(r * kn * rk_ref[...]) * vw


def _hi_lo(w):
    hi = w.astype(BF16)
    return hi, (w - hi.astype(F32)).astype(BF16)


def _rwkv_prep(rows, proj_rkv, proj_tail, first_rkv, first_tail, mu_rkv, mu_tail, w0, w2, a0, a2, g2p,
               k_k, k_a, r_k_flat, rw, hd, g_lora, cn=512):
    n = rows.n
    cn = _pick(rw, cn)
    tw = proj_tail.shape[1]
    nblk = (rows.n_blocks, rw // cn)
    part = lambda p: pl.BlockSpec((rows.rb, cn), lambda i, j: (i, j + p * (rw // cn)))
    fpart = lambda p: pl.BlockSpec((rows.slots, 1, cn), lambda i, j: (i, 0, j + p * (rw // cn)))
    mupart = lambda p: pl.BlockSpec((1, cn), lambda i, j: (0, j + p * (rw // cn)))
    vec = pl.BlockSpec((1, cn), lambda i, j: (0, j))
    lora = lambda k: pl.BlockSpec((k, cn), lambda i, j: (0, j))
    out = pl.BlockSpec((rows.rb, cn), lambda i, j: (i, j))
    kern = functools.partial(_rwkv_prep_kernel, dt=rows.dt, prompt_blocks=rows.prompt_blocks,
                             w_lora=w2.shape[0], a_lora=a2.shape[0], g_lora=g_lora, hd=hd)
    return pl.pallas_call(
        kern,
        grid=nblk,
        in_specs=[part(0), part(1), part(2), pl.BlockSpec((rows.rb, tw), lambda i, j: (i, 0)),
                  fpart(0), fpart(1), fpart(2), pl.BlockSpec((rows.slots, 1, tw), lambda i, j: (i, 0, 0)),
                  mupart(0), mupart(1), mupart(2), pl.BlockSpec((1, tw), lambda i, j: (0, 0)),
                  vec, lora(w2.shape[0]), lora(w2.shape[0]), vec, lora(a2.shape[0]), lora(a2.shape[0]),
                  lora(g2p.shape[0]), lora(g2p.shape[0]), vec, vec, vec],
        out_specs=[out] * 8,
        out_shape=[jax.ShapeDtypeStruct((n, rw), F32)] * 8,
        compiler_params=_params(("parallel", "arbitrary")),
        name="rwkv_prep",
    )(proj_rkv, proj_rkv, proj_rkv, proj_tail, first_rkv, first_rkv, first_rkv, first_tail,
      mu_rkv, mu_rkv, mu_rkv, mu_tail, w0, *_hi_lo(w2), a0, *_hi_lo(a2), *_hi_lo(g2p), k_k, k_a, r_k_flat)


def _rwkv_scan_kernel(r_ref, w_ref, k_ref, v_ref, a_ref, b_ref, *rest, hd, has_state):
    if has_state:
        s0_ref, o_ref, sout_ref, st_ref, ob_ref, vs_ref, row_ref = rest
    else:
        o_ref, sout_ref, st_ref, ob_ref, vs_ref, row_ref = rest
        s0_ref = None
    tb = pl.program_id(1)
    tc, width = r_ref.shape
    pw = 2 * hd
    npairs = width // pw

    @pl.when(tb == 0)
    def _():
        for p in range(npairs):
            if has_state:
                st_ref[p] = jnp.concatenate([s0_ref[0, 2 * p], s0_ref[0, 2 * p + 1]], axis=1)
            else:
                st_ref[p] = jnp.zeros((hd, pw), F32)

    half = npairs // 2
    ones2 = _block_ones(2 * pw, hd, BF16)
    sub = lax.broadcasted_iota(jnp.int32, (hd, pw), 0)
    lane = lax.broadcasted_iota(jnp.int32, (hd, pw), 1)
    diags = [(((lane % hd) - sub + hd) % hd == i).astype(F32).astype(BF16) for i in range(3)]
    lane8 = lax.broadcasted_iota(jnp.int32, (8, pw), 1)
    row16 = lax.broadcasted_iota(jnp.int32, (8, 2 * pw), 0)
    lane16 = lax.broadcasted_iota(jnp.int32, (8, 2 * pw), 1)
    sel2 = row16 == (lane16 // hd)

    def shift_in_head(x, i):
        return jnp.where(lane8 % hd >= i, pltpu.roll(x, i, 1), pltpu.roll(x, pw - hd + i, 1))

    def group(tg, carry):
        rows8 = pl.ds(pl.multiple_of(tg * 8, 8), 8)
        for idx, ref in enumerate((a_ref, w_ref, k_ref, b_ref, r_ref)):
            row_ref[idx] = ref[rows8, :]
        for p in range(npairs):
            cols = pl.ds(p * pw, pw)
            v8 = v_ref[rows8, cols]
            v1 = v8.astype(BF16).astype(F32)
            v2 = (v8 - v1).astype(BF16).astype(F32)
            vs_ref[0, :, cols] = v1
            vs_ref[1, :, cols] = shift_in_head(v2, 1)
            vs_ref[2, :, cols] = shift_in_head((v8 - v1) - v2, 2)

        def row(idx, p, j):
            return row_ref[idx, j:j + 1, pl.ds(p * pw, pw)]

        def side_by_side(tiles):
            return jnp.concatenate([jnp.concatenate([tiles[p], tiles[p + half]], axis=1) for p in range(half)],
                                   axis=0)

        def split(x, p):
            q, c = p % half, p // half
            return x[q * hd:(q + 1) * hd, c * pw:(c + 1) * pw]

        def moved_v(j):
            xs = []
            for p in range(npairs):
                x = None
                for i in range(3):
                    term = diags[i] * vs_ref[i, j:j + 1, pl.ds(p * pw, pw)].astype(BF16)
                    x = term if x is None else x + term
                xs.append(x)
            return _dot(side_by_side(xs), ones2)

        def emit_out(j, states):
            for p in range(half):
                r2 = jnp.concatenate([row(4, p, j), row(4, p + half, j)], axis=1)
                rsel = jnp.where(sel2, r2, 0.0).astype(BF16)
                o8 = _dot_nt(rsel, jnp.concatenate([states[p], states[p + half]], axis=1))
                ob_ref[p, j:j + 1, 0:hd] = o8[0:1, :]
                ob_ref[p, j:j + 1, hd:pw] = o8[1:2, :]
                ob_ref[p + half, j:j + 1, 0:hd] = o8[2:3, :]
                ob_ref[p + half, j:j + 1, hd:pw] = o8[3:4, :]

        vc_next = moved_v(0)
        states = None
        for j in range(8):
            old = [st_ref[p] for p in range(npairs)]
            sa = _dot(side_by_side([old[p] * row(0, p, j) for p in range(npairs)]).astype(BF16), ones2)
            if states is not None:
                emit_out(j - 1, states)
            vc = vc_next
            if j < 7:
                vc_next = moved_v(j + 1)
            states = []
            for p in range(npairs):
                s = old[p] * row(1, p, j) + split(sa, p) * row(3, p, j) + split(vc, p) * row(2, p, j)
                st_ref[p] = s
                states.append(s.astype(BF16))
        emit_out(7, states)
        for p in range(npairs):
            o_ref[rows8, pl.ds(p * pw, pw)] = ob_ref[p]
        return carry

    lax.fori_loop(0, tc // 8, group, 0)

    @pl.when(tb == pl.num_programs(1) - 1)
    def _():
        for p in range(npairs):
            s = st_ref[p]
            sout_ref[0, 2 * p] = s[:, :hd]
            sout_ref[0, 2 * p + 1] = s[:, hd:]


def _rwkv_scan(ops, nseq, t, row0, tc, heads, hd, state):
    width = heads * hd
    assert t % tc == 0 and row0 % tc == 0
    nt = t // tc
    blk = pl.BlockSpec((tc, width), lambda s, j: (row0 // tc + s * nt + j, 0))
    st = pl.BlockSpec((1, heads, hd, hd), lambda s, j: (s, 0, 0, 0))
    in_specs = [blk] * 6
    args = list(ops)
    if state is not None:
        in_specs.append(st)
        args.append(state)
    return pl.pallas_call(
        functools.partial(_rwkv_scan_kernel, hd=hd, has_state=state is not None),
        grid=(nseq, nt),
        in_specs=in_specs,
        out_specs=[pl.BlockSpec((tc, width), lambda s, j: (s * nt + j, 0)), st],
        out_shape=[jax.ShapeDtypeStruct((nseq * t, width), F32),
                   jax.ShapeDtypeStruct((nseq, heads, hd, hd), F32)],
        scratch_shapes=[pltpu.VMEM((heads // 2, hd, 2 * hd), F32), pltpu.VMEM((heads // 2, 8, 2 * hd), F32),
                        pltpu.VMEM((3, 8, width), F32), pltpu.VMEM((5, 8, width), F32)],
        compiler_params=_params(("parallel", "arbitrary")),
        name="rwkv_scan_state" if state is not None else "rwkv_scan",
    )(*args)


def _rwkv_post_kernel(op_ref, os_ref, bonus_ref, g_ref, lg_ref, lb_ref, out_ref, *, hd, first_blocks):
    cn = out_ref.shape[1]
    seg = _block_ones(LANES, hd, BF16)

    def segmean(x):
        return jnp.concatenate([_dot_exact01(x[:, c:c + LANES], seg) for c in range(0, cn, LANES)],
                               axis=1) * (1.0 / hd)

    def body(o):
        mean = segmean(o)
        cen = o - mean
        var = segmean(cen * cen)
        y = cen * lax.rsqrt(var + LNX_EPS) * lg_ref[...] + lb_ref[...]
        out_ref[...] = ((y + bonus_ref[...]) * g_ref[...]).astype(out_ref.dtype)

    _from_either(pl.program_id(0), first_blocks, op_ref, os_ref, body)


def _rwkv_post(o_raw_p, o_raw_s, bonus, g, lnx_g, lnx_b, hd, rb=256, cn=512):
    n, rw = bonus.shape
    rb, cn = _pick(math.gcd(o_raw_p.shape[0], o_raw_s.shape[0]), rb), _pick(rw, cn)
    first_blocks = o_raw_p.shape[0] // rb
    blk = pl.BlockSpec((rb, cn), lambda i, j: (i, j))
    vec = pl.BlockSpec((1, cn), lambda i, j: (0, j))
    pair = [pl.BlockSpec((rb, cn), lambda i, j: (jnp.minimum(i, first_blocks - 1), j)),
            pl.BlockSpec((rb, cn), lambda i, j: (jnp.maximum(i - first_blocks, 0), j))]
    return pl.pallas_call(
        functools.partial(_rwkv_post_kernel, hd=hd, first_blocks=first_blocks),
        grid=(n // rb, rw // cn),
        in_specs=pair + [blk, blk, vec, vec],
        out_specs=blk,
        out_shape=jax.ShapeDtypeStruct((n, rw), BF16),
        compiler_params=_params(("parallel", "parallel")),
        name="rwkv_post",
    )(o_raw_p, o_raw_s, bonus, g, lnx_g, lnx_b)


def _mmt_kernel(wt_ref, x_ref, o_ref):
    o_ref[...] = _dot(wt_ref[...].astype(BF16), x_ref[...])


def _matmul_t(wt, xt, tm=512, tn=1024, name="matmul_t"):
    m, k = wt.shape
    n = xt.shape[1]
    tm, tn = _pick(m, tm), _pick(n, tn)
    return pl.pallas_call(
        _mmt_kernel,
        grid=(n // tn, m // tm),
        in_specs=[pl.BlockSpec((tm, k), lambda j, i: (i, 0)),
                  pl.BlockSpec((k, tn), lambda j, i: (0, j))],
        out_specs=pl.BlockSpec((tm, tn), lambda j, i: (i, j)),
        out_shape=jax.ShapeDtypeStruct((m, n), F32),
        compiler_params=_params(("parallel", "parallel")),
        name=name,
    )(wt, xt)


def _take_top(s, k):
    n = s.shape[0]
    row = lax.broadcasted_iota(jnp.int32, s.shape, 0).astype(F32)
    out = []
    for _ in range(k):
        m = jnp.max(s, axis=0, keepdims=True)
        first = jnp.min(jnp.where(s == m, row, float(n)), axis=0, keepdims=True)
        s = jnp.where(row == first, -jnp.inf, s)
        out.append(m)
    return out


def _peer_topk_kernel(q_ref, keys_ref, s0_ref, s1_ref, e0_ref, e1_ref, thr_ref, *, topk):
    kd = keys_ref.shape[3]
    cols = q_ref.shape[1]
    s0 = _dot(keys_ref[0, 0], q_ref[:kd, :], HI)
    s1 = _dot(keys_ref[0, 1], q_ref[kd:, :], HI)
    top0 = _take_top(s0, topk)
    top1 = _take_top(s1, topk)
    width = [topk // (a + 1) for a in range(topk)]
    start = [sum(width[:a]) for a in range(topk)]
    n_cand = sum(width)
    rows = -(-n_cand // 8) * 8
    r = lax.broadcasted_iota(jnp.int32, (rows, cols), 0)
    t0 = jnp.broadcast_to(top0[0], (rows, cols))
    first = jnp.zeros((rows, cols), jnp.int32)
    for a in range(1, topk):
        t0 = jnp.where(r >= start[a], top0[a], t0)
        first = jnp.where(r >= start[a], start[a], first)
    b_of_row = r - first
    t1 = jnp.broadcast_to(top1[0], (rows, cols))
    for b in range(1, topk):
        t1 = jnp.where(b_of_row == b, top1[b], t1)
    cand = jnp.where(r < n_cand, t0 + t1, -jnp.inf)
    best = _take_top(cand, topk)
    z = jnp.zeros((1, cols), F32)
    for b in best:
        z = z + jnp.exp(b - best[0])
    s0_ref[...] = s0
    s1_ref[...] = s1
    e0_ref[...] = jnp.exp(s0 - top0[0]) / z
    e1_ref[...] = jnp.exp(s1 - top1[0])
    thr_ref[0] = jnp.broadcast_to(best[topk - 1], thr_ref.shape[1:])


def _peer_topk(qt, sub_keys, tc=256):
    n = qt.shape[1]
    tc = _pick(n, tc)
    heads, _, nk, kd = sub_keys.shape
    blk = pl.BlockSpec((nk, tc), lambda i, h: (h, i))
    shp = jax.ShapeDtypeStruct((heads * nk, n), F32)
    return pl.pallas_call(
        functools.partial(_peer_topk_kernel, topk=PEER_TOPK),
        grid=(n // tc, heads),
        in_specs=[pl.BlockSpec((2 * kd, tc), lambda i, h: (h, i)),
                  pl.BlockSpec((1, 2, nk, kd), lambda i, h: (h, 0, 0, 0))],
        out_specs=[blk] * 4 + [pl.BlockSpec((1, 8, tc), lambda i, h: (h, 0, i))],
        out_shape=[shp] * 4 + [jax.ShapeDtypeStruct((heads, 8, n), F32)],
        compiler_params=_params(("parallel", "parallel")),
        name="peer_topk",
    )(qt, sub_keys)


def _gelu(x):
    return 0.5 * x * (1.0 + lax.erf(x * (2.0 ** -0.5)))


def _peer_dense_kernel(u_ref, ht_ref, vt_ref, s0_ref, e0_ref, s1_ref, e1_ref, thr_ref, o_ref, act_ref, coef_ref,
                       *, heads, nk, jc):
    e = pl.program_id(1)
    eb = u_ref.shape[0]
    d = vt_ref.shape[0]
    ni = eb // nk
    dq = d // ni

    @pl.when(e == 0)
    def _():
        o_ref[...] = jnp.zeros_like(o_ref)
        coef_ref[1] = jnp.zeros(coef_ref.shape[1:], coef_ref.dtype)

    act_ref[...] = _dot(u_ref[...], ht_ref[...])
    cur = e % 2
    prev = (e + 1) % 2
    thr = thr_ref[...]

    def one_i(ii, carry):
        orow = pl.ds(pl.multiple_of(ii * dq, dq), dq)
        o_ref[orow, :] += _dot(vt_ref[orow, :], coef_ref[prev])
        s0 = s0_ref[0, ii]
        e0 = e0_ref[0, ii]
        for j0 in range(0, nk, jc):
            g = None
            for h in range(heads):
                jrows = slice(h * nk + j0, h * nk + j0 + jc)
                hit = (s0[h:h + 1] + s1_ref[jrows, :]) >= thr[h:h + 1]
                term = jnp.where(hit, e0[h:h + 1] * e1_ref[jrows, :], 0.0)
                g = term if g is None else g + term
            arow = pl.ds(pl.multiple_of(ii * nk + j0, jc), jc)
            coef_ref[cur, arow, :] = (g * _gelu(act_ref[arow, :])).astype(coef_ref.dtype)
        return carry

    lax.fori_loop(0, ni, one_i, 0)


def _peer_dense(ht, u, vt, s0r, e0r, s1, e1, thr, heads, nk, tp=512, eb=512):
    d, n = ht.shape
    tp = _pick(n, tp)
    nblk = u.shape[0] // eb
    ni = eb // nk
    once = pl.Buffered(1)
    tok = pl.BlockSpec((heads * nk, tp), lambda i, e: (0, i), pipeline_mode=once)
    sl = pl.BlockSpec((1, ni, heads, tp), lambda i, e: (jnp.minimum(e, nblk - 1), 0, 0, i))
    return pl.pallas_call(
        functools.partial(_peer_dense_kernel, heads=heads, nk=nk, jc=nk // 2),
        grid=(n // tp, nblk + 1),
        in_specs=[pl.BlockSpec((eb, d), lambda i, e: (jnp.minimum(e, nblk - 1), 0)),
                  pl.BlockSpec((d, tp), lambda i, e: (0, i), pipeline_mode=once),
                  pl.BlockSpec((d, eb), lambda i, e: (0, jnp.maximum(e - 1, 0))),
                  sl, sl, tok, tok, pl.BlockSpec((heads, tp), lambda i, e: (0, i))],
        out_specs=pl.BlockSpec((d, tp), lambda i, e: (0, i)),
        out_shape=jax.ShapeDtypeStruct((d, n), F32),
        scratch_shapes=[pltpu.VMEM((eb, tp), F32), pltpu.VMEM((2, eb, tp), BF16)],
        compiler_params=_params(("parallel", "arbitrary")),
        name="peer_dense",
    )(u, ht, vt, s0r, e0r, s1, e1, thr)


def _by_expert_block(x, heads, nk, ni):
    n = x.shape[1]
    return x.reshape(heads, nk // ni, ni, n).transpose(1, 2, 0, 3)


def _layer(xp, xs, s_ret, s_rwkv, s_shift, cp, cs, p):
    nb, t, d = xp.shape
    db, dt, _ = xs.shape
    rows = _Rows(nb, t, db, dt)
    ret_heads, ret_hd = s_ret.shape[1], s_ret.shape[2]
    rw_heads, rw_hd = s_rwkv.shape[1], s_rwkv.shape[2]
    ret_w, rw = ret_heads * ret_hd, rw_heads * rw_hd
    w_lora, a_lora, g_lora = p['w2'].shape[0], p['a2'].shape[0], p['g2'].shape[0]
    rwkv_in = s_shift.shape[-1]
    ret_in = 4 * ret_w
    assert p['w_in'].shape[1] == ret_in + rwkv_in and rwkv_in == 3 * rw + w_lora + a_lora + g_lora

    xp, xs = xp.reshape(nb * t, d), xs.reshape(db * dt, d)
    mod = _ada(jnp.concatenate([cp, cs], axis=0), p['ada_w'], p['ada_b'])
    mod_ext = rows.extend(mod)

    h = _prenorm(rows, xp, xs, p['pre_mix_g'], mod_ext, d)

    tm = _pick(rows.n, 1024)
    w_in_t = p['w_in'].T
    proj_ret = _matmul(h, w_in_t, col_block0=0, n_out=ret_in, tm=tm, tn=512, name="in_proj_ret")
    proj_rkv = _matmul(h, w_in_t, col_block0=ret_in // 512, n_out=3 * rw, tm=tm, tn=512, name="in_proj_rkv")
    tail0 = ret_in + 3 * rw
    tail_w = -(-(w_lora + a_lora + g_lora) // LANES) * LANES
    proj_tail = _matmul(h, w_in_t, col_block0=tail0 // LANES, n_out=tail_w, tm=tm, tn=LANES, name="in_proj_tail")
    tail_valid = w_lora + a_lora + g_lora

    o_ret_p, o_ret_s, sr_p, sr_s = _retention(rows, proj_ret, s_ret, ret_heads, ret_hd)

    def firsts(proj, shift_cols):
        w = proj.shape[1]
        starts = np.arange(rows.prompt_blocks) * rows.rb
        last = proj[rows.rb - 1:rows.n_prompt:rows.rb]
        prev = jnp.concatenate([jnp.zeros((1, w), F32), last[:-1]], axis=0)
        prev = jnp.where((starts % t == 0)[:, None], 0.0, prev)
        fp = jnp.zeros((rows.prompt_blocks, rows.slots, w), F32).at[:, 0].set(prev)
        return jnp.concatenate([fp.reshape(-1, w), shift_cols], axis=0)[:, None, :]

    shift_tail = jnp.pad(s_shift[:, 3 * rw:], ((0, 0), (0, tail_w - tail_valid)))
    first_rkv = firsts(proj_rkv, s_shift[:, :3 * rw])
    first_tail = firsts(proj_tail, shift_tail)
    mu = p['shift_mu']
    mu_tail = jnp.pad(mu[3 * rw:], (0, tail_w - tail_valid)).reshape(1, tail_w)
    g2p = jnp.pad(p['g2'], ((0, tail_w - w_lora - a_lora - g_lora), (0, 0)))
    vec = lambda a: a.reshape(1, rw)
    r_, w_, k_, v_, a_, b_, g_, bonus = _rwkv_prep(
        rows, proj_rkv, proj_tail, first_rkv, first_tail, mu[:3 * rw].reshape(1, 3 * rw), mu_tail,
        vec(p['w0']), p['w2'], vec(p['a0']), p['a2'], g2p, vec(p['k_k']), vec(p['k_a']), vec(p['r_k']),
        rw, rw_hd, g_lora)
    ops = (r_, w_, k_, v_, a_, b_)
    o_rw_p, sw_p = _rwkv_scan(ops, nb, t, 0, 256, rw_heads, rw_hd, None)
    o_rw_s, sw_s = _rwkv_scan(ops, db, dt, nb * t, dt, rw_heads, rw_hd, s_rwkv)
    o_rw = _rwkv_post(o_rw_p, o_rw_s, bonus, g_, vec(p['lnx_g']), vec(p['lnx_b']), rw_hd)

    mix = _matmul_out(o_ret_p, o_ret_s, o_rw, p['w_out'])
    x1, h2t = _postmix(rows, xp, xs, mix, p['post_mix_g'], p['pre_ffn_g'], mod_ext, d)

    heads, _, nk, kd = p['peer_sub_keys'].shape
    qt = _matmul_t(p['peer_wq'].T, h2t, name="peer_q")
    s0, s1, e0, e1, thr = _peer_topk(qt, p['peer_sub_keys'])
    eb = 512
    ni = eb // nk
    peer_t = _peer_dense(h2t, p['peer_u'].astype(BF16), p['peer_v'].T.astype(BF16),
                         _by_expert_block(s0, heads, nk, ni), _by_expert_block(e0, heads, nk, ni),
                         s1, e1, thr[:, 0, :], heads, nk, eb=eb)
    yp = _final(rows, x1, peer_t, p['post_ffn_g'], mod_ext, d, 0, rows.prompt_blocks)
    ys = _final(rows, x1, peer_t, p['post_ffn_g'], mod_ext, d, rows.prompt_blocks, rows.n_blocks - rows.prompt_blocks)

    n_p = nb * t
    last_p = slice(t - 1, n_p, t)
    last_s = slice(n_p + dt - 1, None, dt)
    pw_last = lambda idx: jnp.concatenate([proj_rkv[idx], proj_tail[idx, :tail_valid]], axis=1)
    return (yp.reshape(nb, t, d), ys.reshape(db, dt, d), sr_p, sr_s, sw_p, sw_s,
            pw_last(last_p), pw_last(last_s))


def kernel(x_prompt, x_sample, state_ret, state_rwkv, state_shift, c_prompt, c_sample, ada_w, ada_b, pre_mix_g, post_mix_g, pre_ffn_g, post_ffn_g, w_in, shift_mu, w0, w2, a0, a2, g2, k_k, k_a, r_k, lnx_g, lnx_b, w_out, peer_wq, peer_sub_keys, peer_u, peer_v):
    depth = ada_w.shape[0]
    assert depth == 1, "prompt and sample tokens are stacked per layer; deeper stacks need per-layer restacking"
    prm = dict(ada_w=ada_w[0], ada_b=ada_b[0], pre_mix_g=pre_mix_g[0], post_mix_g=post_mix_g[0],
               pre_ffn_g=pre_ffn_g[0], post_ffn_g=post_ffn_g[0], w_in=w_in[0], shift_mu=shift_mu[0],
               w0=w0[0], w2=w2[0], a0=a0[0], a2=a2[0], g2=g2[0], k_k=k_k[0], k_a=k_a[0], r_k=r_k[0],
               lnx_g=lnx_g[0], lnx_b=lnx_b[0], w_out=w_out[0], peer_wq=peer_wq[0],
               peer_sub_keys=peer_sub_keys[0], peer_u=peer_u[0], peer_v=peer_v[0])
    yp, ys, sr_p, sr_s, sw_p, sw_s, ss_p, ss_s = _layer(
        x_prompt, x_sample, state_ret[0], state_rwkv[0], state_shift[0], c_prompt, c_sample, prm)
    sd, wd, hd = state_ret.dtype, state_rwkv.dtype, state_shift.dtype
    return (yp, ys, sr_p[None].astype(sd), sr_s[None].astype(sd), sw_p[None].astype(wd), sw_s[None].astype(wd),
            ss_p[None].astype(hd), ss_s[None].astype(hd))
```

```python
import functools
import math

import numpy as np
import jax
import jax.numpy as jnp
from jax import lax
from jax.experimental import pallas as pl
from jax.experimental.pallas import tpu as pltpu

PAST_LEN = 16384
RET_CHUNK = 128
ROPE_BASE = 10000.0
LNX_EPS = 64e-5
NORM_EPS = 1e-6
PEER_TOPK = 16
RET_HEADS_PER_STEP = 4

LANES = 128
ROW_BLOCK = 128
VMEM_LIMIT = 56 * 1024 * 1024

HI = lax.Precision.HIGHEST
F32 = jnp.float32
BF16 = jnp.bfloat16


def _pick(n, target, mult=LANES):
    best = None
    for c in range(mult, min(n, target) + 1, mult):
        if n % c == 0:
            best = c
    assert best is not None, (n, target, mult)
    return best


def _params(sem, vmem=VMEM_LIMIT):
    return pltpu.CompilerParams(dimension_semantics=sem, vmem_limit_bytes=vmem)


def _dot(a, b, precision=None):
    return jnp.dot(a, b, preferred_element_type=F32, precision=precision)


def _dot_nt(a, b, precision=None):
    return lax.dot_general(a, b, (((1,), (1,)), ((), ())), preferred_element_type=F32, precision=precision)


def _dot_tn(a, b, precision=None):
    return lax.dot_general(a, b, (((0,), (0,)), ((), ())), preferred_element_type=F32, precision=precision)


def _split_bf16(x, terms):
    out = []
    for _ in range(terms - 1):
        t = x.astype(BF16)
        out.append(t)
        x = x - t.astype(F32)
    return out + [x.astype(BF16)]


def _dot_split(x, w_terms):
    xh, xl = _split_bf16(x, 2)
    wh, wl = w_terms
    return _dot(xh, wh) + (_dot(xl, wh) + _dot(xh, wl))


def _dot_exact01(x, ones):
    return sum(_dot(t, ones) for t in _split_bf16(x, 3))


def _block_ones(n, seg, dtype):
    r = lax.broadcasted_iota(jnp.int32, (n, n), 0) // seg
    c = lax.broadcasted_iota(jnp.int32, (n, n), 1) // seg
    return (r == c).astype(dtype)


def _ada_kernel(c_ref, w_ref, b_ref, o_ref):
    c = c_ref[...]
    sc = (c * jax.nn.sigmoid(c)).astype(BF16)
    o_ref[...] = _dot(sc, w_ref[...].astype(BF16)) + b_ref[...]


def _ada(c_all, ada_w, ada_b, tn=512):
    m, d = c_all.shape
    n = ada_w.shape[1]
    return pl.pallas_call(
        _ada_kernel,
        grid=(n // tn,),
        in_specs=[pl.BlockSpec((m, d), lambda j: (0, 0)),
                  pl.BlockSpec((d, tn), lambda j: (0, j)),
                  pl.BlockSpec((1, tn), lambda j: (0, j))],
        out_specs=pl.BlockSpec((m, tn), lambda j: (0, j)),
        out_shape=jax.ShapeDtypeStruct((m, n), F32),
        compiler_params=_params(("arbitrary",)),
        name="ada",
    )(c_all, ada_w, ada_b.reshape(1, n))


def _expand_slots(m, rows):
    slots, _, d = m.shape
    return jnp.broadcast_to(m, (slots, rows // slots, d)).reshape(rows, d)


def _rms(x, g):
    return x * lax.rsqrt(jnp.mean(x * x, axis=-1, keepdims=True) + NORM_EPS) * g


def _from_either(i, first_blocks, a_ref, b_ref, body):
    @pl.when(i < first_blocks)
    def _():
        body(a_ref[...])

    @pl.when(i >= first_blocks)
    def _():
        body(b_ref[...])


def _two_specs(block, first_blocks):
    return [pl.BlockSpec(block, lambda i, *_: (jnp.minimum(i, first_blocks - 1), 0)),
            pl.BlockSpec(block, lambda i, *_: (jnp.maximum(i - first_blocks, 0), 0))]


def _prenorm_kernel(xp_ref, xs_ref, g_ref, sh_ref, sc_ref, h_ref, *, prompt_blocks):
    def body(x):
        rows = x.shape[0]
        y = _rms(x, g_ref[...])
        h = y * (1.0 + _expand_slots(sc_ref[...], rows)) + _expand_slots(sh_ref[...], rows)
        h_ref[...] = h.astype(h_ref.dtype)

    _from_either(pl.program_id(0), prompt_blocks, xp_ref, xs_ref, body)


def _postmix_kernel(xp_ref, xs_ref, m_ref, gpost_ref, gpre_ref, gate_ref, sh_ref, sc_ref, x1_ref, ht_ref,
                    *, prompt_blocks):
    def body(x):
        rows = x.shape[0]
        x1 = x + _expand_slots(gate_ref[...], rows) * _rms(m_ref[...], gpost_ref[...])
        x1_ref[...] = x1
        y = _rms(x1, gpre_ref[...])
        h = y * (1.0 + _expand_slots(sc_ref[...], rows)) + _expand_slots(sh_ref[...], rows)
        ht_ref[...] = h.T.astype(ht_ref.dtype)

    _from_either(pl.program_id(0), prompt_blocks, xp_ref, xs_ref, body)


def _final_kernel(x_ref, mt_ref, gpost_ref, gate_ref, y_ref):
    x = x_ref[...]
    rows = x.shape[0]
    y_ref[...] = x + _expand_slots(gate_ref[...], rows) * _rms(mt_ref[...].T, gpost_ref[...])


class _Rows:
    def __init__(self, nb, t, db, dt):
        self.n_prompt = nb * t
        self.n_sample = db * dt
        self.n = self.n_prompt + self.n_sample
        self.rb = ROW_BLOCK
        assert t % self.rb == 0 and self.rb % dt == 0 and self.n_sample % self.rb == 0
        self.slots = self.rb // dt
        self.prompt_blocks = self.n_prompt // self.rb
        self.blocks_per_seq = t // self.rb
        self.n_blocks = self.n // self.rb
        self.nb, self.t, self.db, self.dt = nb, t, db, dt

    def slot_block(self, i):
        return jnp.where(i < self.prompt_blocks, i // self.blocks_per_seq, i - self.prompt_blocks + self.nb)

    def extend(self, m):
        mp = jnp.repeat(m[: self.nb], self.slots, axis=0)
        return jnp.concatenate([mp, m[self.nb:]], axis=0)[:, None, :]


def _mod_spec(rows, d, chunk, block0=0):
    return pl.BlockSpec((rows.slots, 1, d), lambda i: (rows.slot_block(i + block0), 0, chunk))


def _prenorm(rows, xp, xs, g, mod_ext, d):
    row = pl.BlockSpec((rows.rb, d), lambda i: (i, 0))
    vec = pl.BlockSpec((1, d), lambda i: (0, 0))
    return pl.pallas_call(
        functools.partial(_prenorm_kernel, prompt_blocks=rows.prompt_blocks),
        grid=(rows.n_blocks,),
        in_specs=_two_specs((rows.rb, d), rows.prompt_blocks) + [vec, _mod_spec(rows, d, 0), _mod_spec(rows, d, 1)],
        out_specs=row,
        out_shape=jax.ShapeDtypeStruct((rows.n, d), BF16),
        compiler_params=_params(("parallel",)),
        name="prenorm",
    )(xp, xs, g.reshape(1, d), mod_ext, mod_ext)


def _postmix(rows, xp, xs, mix, gpost, gpre, mod_ext, d):
    row = pl.BlockSpec((rows.rb, d), lambda i: (i, 0))
    vec = pl.BlockSpec((1, d), lambda i: (0, 0))
    return pl.pallas_call(
        functools.partial(_postmix_kernel, prompt_blocks=rows.prompt_blocks),
        grid=(rows.n_blocks,),
        in_specs=_two_specs((rows.rb, d), rows.prompt_blocks)
        + [row, vec, vec, _mod_spec(rows, d, 2), _mod_spec(rows, d, 3), _mod_spec(rows, d, 4)],
        out_specs=[row, pl.BlockSpec((d, rows.rb), lambda i: (0, i))],
        out_shape=[jax.ShapeDtypeStruct((rows.n, d), F32), jax.ShapeDtypeStruct((d, rows.n), BF16)],
        compiler_params=_params(("parallel",)),
        name="postmix",
    )(xp, xs, mix, gpost.reshape(1, d), gpre.reshape(1, d), mod_ext, mod_ext, mod_ext)


def _final(rows, x1, peer_t, gpost, mod_ext, d, block0, n_blocks):
    vec = pl.BlockSpec((1, d), lambda i: (0, 0))
    return pl.pallas_call(
        _final_kernel,
        grid=(n_blocks,),
        in_specs=[pl.BlockSpec((rows.rb, d), lambda i: (i + block0, 0)),
                  pl.BlockSpec((d, rows.rb), lambda i: (0, i + block0)),
                  vec, _mod_spec(rows, d, 5, block0)],
        out_specs=pl.BlockSpec((rows.rb, d), lambda i: (i, 0)),
        out_shape=jax.ShapeDtypeStruct((n_blocks * rows.rb, d), F32),
        compiler_params=_params(("parallel",)),
        name="final",
    )(x1, peer_t, gpost.reshape(1, d), mod_ext)


def _mm_kernel(a_ref, wt_ref, o_ref, wbf_ref):
    @pl.when(pl.program_id(1) == 0)
    def _():
        wbf_ref[...] = wt_ref[...].astype(BF16)

    o_ref[...] = _dot_nt(a_ref[...], wbf_ref[...]).astype(o_ref.dtype)


def _matmul(a, wt, *, col_block0, n_out, tm, tn, name):
    m, k = a.shape
    return pl.pallas_call(
        _mm_kernel,
        grid=(pl.cdiv(n_out, tn), m // tm),
        in_specs=[pl.BlockSpec((tm, k), lambda j, i: (i, 0)),
                  pl.BlockSpec((tn, k), lambda j, i: (j + col_block0, 0))],
        out_specs=pl.BlockSpec((tm, tn), lambda j, i: (i, j)),
        out_shape=jax.ShapeDtypeStruct((m, n_out), F32),
        scratch_shapes=[pltpu.VMEM((tn, k), BF16)],
        compiler_params=_params(("parallel", "arbitrary")),
        name=name,
    )(a, wt)


def _mm2_kernel(ap_ref, as_ref, b_ref, wa_ref, wb_ref, o_ref, *, first_blocks):
    def body(a):
        o_ref[...] = _dot(a, wa_ref[...].astype(BF16)) + _dot(b_ref[...], wb_ref[...].astype(BF16))

    _from_either(pl.program_id(1), first_blocks, ap_ref, as_ref, body)


def _matmul_out(o_ret_p, o_ret_s, o_rw, w_out, tn=512):
    m, kb = o_rw.shape
    ka = o_ret_p.shape[1]
    tm = _pick(math.gcd(o_ret_p.shape[0], o_ret_s.shape[0]), 1024)
    first_blocks = o_ret_p.shape[0] // tm
    n = w_out.shape[1]
    tn = _pick(n, tn)
    assert ka == kb
    pair = [pl.BlockSpec((tm, ka), lambda j, i: (jnp.minimum(i, first_blocks - 1), 0)),
            pl.BlockSpec((tm, ka), lambda j, i: (jnp.maximum(i - first_blocks, 0), 0))]
    return pl.pallas_call(
        functools.partial(_mm2_kernel, first_blocks=first_blocks),
        grid=(n // tn, m // tm),
        in_specs=pair + [pl.BlockSpec((tm, kb), lambda j, i: (i, 0)),
                         pl.BlockSpec((ka, tn), lambda j, i: (0, j)),
                         pl.BlockSpec((kb, tn), lambda j, i: (1, j))],
        out_specs=pl.BlockSpec((tm, tn), lambda j, i: (i, j)),
        out_shape=jax.ShapeDtypeStruct((m, n), F32),
        compiler_params=_params(("parallel", "parallel")),
        name="out_proj",
    )(o_ret_p, o_ret_s, o_rw, w_out, w_out)


def _rope_tables(pos, hd):
    half = hd // 2
    inv = ROPE_BASE ** (-jnp.arange(half, dtype=F32) / half)
    ang = pos.astype(F32)[:, None] * inv[None, :]
    cos, sin = jnp.cos(ang), jnp.sin(ang)
    return jnp.concatenate([cos, cos], axis=-1), jnp.concatenate([-sin, sin], axis=-1)


def _rope(x, cos, sin_signed):
    half = x.shape[-1] // 2
    return x * cos + pltpu.roll(x, half, 1) * sin_signed


def _ret_finish(o, gate):
    o = o * lax.rsqrt(jnp.mean(o * o, axis=-1, keepdims=True) + NORM_EPS)
    return o * (gate * jax.nn.sigmoid(gate))


def _ret_prompt_kernel(q_ref, k_ref, v_ref, g_ref, cos_ref, sin_ref, lg_ref, o_ref, s_ref, *, chunk, hd):
    t = q_ref.shape[0]
    nh = q_ref.shape[1] // hd
    ri = lax.broadcasted_iota(jnp.int32, (chunk, chunk), 0)
    ci = lax.broadcasted_iota(jnp.int32, (chunk, chunk), 1)
    causal = ri >= ci
    diff = jnp.where(causal, ri - ci, 0).astype(F32)
    rowf = lax.broadcasted_iota(jnp.int32, (chunk, hd), 0).astype(F32)
    lgs = [lg_ref[i, 0:1, :] for i in range(nh)]
    masks = [jnp.where(causal, jnp.exp(diff * lg), 0.0) for lg in lgs]
    q_decs = [jnp.exp((rowf + 1.0) * lg) for lg in lgs]
    k_decs = [jnp.exp((chunk - 1.0 - rowf) * lg) for lg in lgs]
    c_decs = [jnp.exp(float(chunk) * lg) for lg in lgs]

    def step(c, states):
        r0 = pl.multiple_of(c * chunk, chunk)
        rows = pl.ds(r0, chunk)
        cos, sin = cos_ref[rows, :], sin_ref[rows, :]
        cols = [slice(i * hd, (i + 1) * hd) for i in range(nh)]
        qs = [_rope(q_ref[rows, cl], cos, sin) for cl in cols]
        ks = [_rope(k_ref[rows, cl], cos, sin) * (hd ** -0.5) for cl in cols]
        vs = [v_ref[rows, cl] for cl in cols]
        atts = [_dot_nt(q.astype(BF16), k.astype(BF16)) * m for q, k, m in zip(qs, ks, masks)]
        cross = [_dot((q * qd).astype(BF16), s.astype(BF16)) for q, qd, s in zip(qs, q_decs, states)]
        inner = [_dot(a.astype(BF16), v.astype(BF16)) for a, v in zip(atts, vs)]
        new = [s * cd + _dot_tn(k * kd, v, HI) for s, cd, k, kd, v in zip(states, c_decs, ks, k_decs, vs)]
        for i, cl in enumerate(cols):
            o_ref[rows, cl] = _ret_finish(inner[i] + cross[i], g_ref[rows, cl]).astype(o_ref.dtype)
        return tuple(new)

    final = lax.fori_loop(0, t // chunk, step, tuple(jnp.zeros((hd, hd), F32) for _ in range(nh)))
    for i in range(nh):
        s_ref[0, i] = final[i]


def _ret_sample_kernel(q_ref, k_ref, v_ref, g_ref, cos_ref, sin_ref, lg_ref, s0_ref, o_ref, s_ref, *, dt):
    rows, hd = q_ref.shape
    nseq = rows // dt
    lg = lg_ref[0, 0:1, :]
    ri = lax.broadcasted_iota(jnp.int32, (rows, rows), 0)
    ci = lax.broadcasted_iota(jnp.int32, (rows, rows), 1)
    ok = (ri >= ci) & ((ri // dt) == (ci // dt))
    diff = jnp.where(ok, ri - ci, 0).astype(F32)
    mask = jnp.where(ok, jnp.exp(diff * lg), 0.0)
    rown = lax.broadcasted_iota(jnp.int32, (rows, hd), 0)
    posf = (rown % dt).astype(F32)
    q_dec = jnp.exp((posf + 1.0) * lg)
    k_dec = jnp.exp((dt - 1.0 - posf) * lg)
    c_dec = jnp.exp(float(dt) * lg)
    cos, sin = cos_ref[...], sin_ref[...]
    qc = _rope(q_ref[...], cos, sin)
    kc = _rope(k_ref[...], cos, sin) * (hd ** -0.5)
    vc = v_ref[...]
    att = _dot_nt(qc.astype(BF16), kc.astype(BF16)) * mask
    inner = _dot(att.astype(BF16), vc.astype(BF16))
    qd = qc * q_dec
    kd = kc * k_dec
    seq = rown // dt
    cross = jnp.zeros((rows, hd), F32)
    for s in range(nseq):
        s0 = s0_ref[s, 0]
        mine = seq == s
        cross = cross + _dot(jnp.where(mine, qd, 0.0).astype(BF16), s0.astype(BF16))
        s_ref[s, 0] = s0 * c_dec + _dot_tn(jnp.where(mine, kd, 0.0), vc, HI)
    o_ref[...] = _ret_finish(inner + cross, g_ref[...]).astype(o_ref.dtype)


def _retention(rows, proj, state_ret, ret_heads, hd):
    nb, t, db, dt = rows.nb, rows.t, rows.db, rows.dt
    lg = jnp.log1p(-jnp.exp2(-5.0 - jnp.arange(ret_heads, dtype=F32)))
    lg_tab = jnp.broadcast_to(lg[:, None, None], (ret_heads, 8, hd))
    lg_spec = pl.BlockSpec((1, 8, hd), lambda b, h: (h, 0, 0))
    assert hd == RET_CHUNK and t % RET_CHUNK == 0 and dt % RET_CHUNK != 0

    cos_p, sin_p = _rope_tables(jnp.arange(t, dtype=jnp.int32), hd)
    nh = RET_HEADS_PER_STEP if ret_heads % RET_HEADS_PER_STEP == 0 else 1
    hsteps = ret_heads // nh
    col = lambda part: pl.BlockSpec((t, nh * hd), lambda b, h: (b, h + part * hsteps))
    tab = pl.BlockSpec((t, hd), lambda b, h: (0, 0))
    o_p, s_p = pl.pallas_call(
        functools.partial(_ret_prompt_kernel, chunk=RET_CHUNK, hd=hd),
        grid=(nb, hsteps),
        in_specs=[col(0), col(1), col(2), col(3), tab, tab, pl.BlockSpec((nh, 8, hd), lambda b, h: (h, 0, 0))],
        out_specs=[pl.BlockSpec((t, nh * hd), lambda b, h: (b, h)),
                   pl.BlockSpec((1, nh, hd, hd), lambda b, h: (b, h, 0, 0))],
        out_shape=[jax.ShapeDtypeStruct((nb * t, ret_heads * hd), BF16),
                   jax.ShapeDtypeStruct((nb, ret_heads, hd, hd), F32)],
        compiler_params=_params(("parallel", "parallel")),
        name="ret_prompt",
    )(proj, proj, proj, proj, cos_p, sin_p, lg_tab)

    rb = RET_CHUNK
    nseq = rb // dt
    cos_s, sin_s = _rope_tables(PAST_LEN + jnp.arange(dt, dtype=jnp.int32), hd)
    cos_s, sin_s = jnp.tile(cos_s, (nseq, 1)), jnp.tile(sin_s, (nseq, 1))
    off = (nb * t) // rb
    col = lambda part: pl.BlockSpec((rb, hd), lambda b, h: (b + off, h + part * ret_heads))
    tab = pl.BlockSpec((rb, hd), lambda b, h: (0, 0))
    st = pl.BlockSpec((nseq, 1, hd, hd), lambda b, h: (b, h, 0, 0))
    o_s, s_s = pl.pallas_call(
        functools.partial(_ret_sample_kernel, dt=dt),
        grid=(db // nseq, ret_heads),
        in_specs=[col(0), col(1), col(2), col(3), tab, tab, lg_spec, st],
        out_specs=[pl.BlockSpec((rb, hd), lambda b, h: (b, h)), st],
        out_shape=[jax.ShapeDtypeStruct((db * dt, ret_heads * hd), BF16),
                   jax.ShapeDtypeStruct((db, ret_heads, hd, hd), F32)],
        compiler_params=_params(("parallel", "parallel")),
        name="ret_sample",
    )(proj, proj, proj, proj, cos_s, sin_s, lg_tab, state_ret)
    return o_p, o_s, s_p, s_s


def _shifted(x, first, tseq_mask):
    prev = pltpu.roll(x, 1, 0)
    return jnp.where(tseq_mask, first, prev)


def _rwkv_prep_kernel(pr_ref, pk_ref, pv_ref, pt_ref, sr_ref, sk_ref, sv_ref, st_ref,
                      mur_ref, muk_ref, muv_ref, mut_ref, w0_ref, w2h_ref, w2l_ref, a0_ref, a2h_ref, a2l_ref,
                      g2h_ref, g2l_ref,
                      kk_ref, ka_ref, rk_ref,
                      r_out, w_out, k_out, v_out, a_out, b_out, g_out, bonus_out, carry_ref, tcarry_ref,
                      *, dt, prompt_blocks, blocks_per_seq, w_lora, a_lora, g_lora, hd):
    i, j = pl.program_id(0), pl.program_id(1)
    rows, cn = pr_ref.shape
    rown = lax.broadcasted_iota(jnp.int32, (rows, 1), 0)
    sample_block = jnp.zeros((rows, 1), jnp.int32) + (i >= prompt_blocks).astype(jnp.int32)
    is_first = (rown == 0) | ((rown % dt == 0) & (sample_block == 1))
    seq_start = i % blocks_per_seq == 0

    @pl.when(i == 0)
    def _():
        for part in range(3):
            carry_ref[part, j] = jnp.zeros(carry_ref.shape[2:], F32)
        tcarry_ref[...] = jnp.zeros_like(tcarry_ref)

    def lerp(p_ref, s_ref, mu_ref, carried):
        p = p_ref[...]
        from_prompt = jnp.broadcast_to(jnp.where(seq_start, 0.0, carried), p.shape)
        first = jnp.where(sample_block == 1, _expand_slots(s_ref[...], rows), from_prompt)
        prev = _shifted(p, first, is_first)
        return p + mu_ref[...] * (prev - p)

    r = lerp(pr_ref, sr_ref, mur_ref, carry_ref[0, j, 0:1, :])
    kw = lerp(pk_ref, sk_ref, muk_ref, carry_ref[1, j, 0:1, :])
    vw = lerp(pv_ref, sv_ref, muv_ref, carry_ref[2, j, 0:1, :])
    tail = lerp(pt_ref, st_ref, mut_ref, tcarry_ref[0:1, :])
    for part, p_ref in enumerate((pr_ref, pk_ref, pv_ref)):
        carry_ref[part, j, 0:1, :] = p_ref[rows - 1:rows, :]

    @pl.when(j == pl.num_programs(1) - 1)
    def _():
        tcarry_ref[0:1, :] = pt_ref[rows - 1:rows, :]

    tw = tail.shape[1]
    wl = tail[:, :w_lora]
    al = tail[:, w_lora:w_lora + a_lora]
    glane = lax.broadcasted_iota(jnp.int32, (rows, tw - w_lora - a_lora), 1)
    gl = jnp.where(glane < g_lora, jax.nn.sigmoid(tail[:, w_lora + a_lora:]), 0.0)

    w_log = -jax.nn.softplus(-(w0_ref[...] + _dot_split(jnp.tanh(wl), (w2h_ref[...], w2l_ref[...])))) - 0.5
    decay = jnp.exp(-jnp.exp(w_log))
    a = jax.nn.sigmoid(a0_ref[...] + _dot_split(al, (a2h_ref[...], a2l_ref[...])))
    g = _dot_split(gl, (g2h_ref[...], g2l_ref[...]))

    seg = _block_ones(LANES, hd, BF16)

    def segsum(x):
        return jnp.concatenate([_dot_exact01(x[:, c:c + LANES], seg) for c in range(0, cn, LANES)], axis=1)

    kk = kw * kk_ref[...]
    kk = kk / jnp.maximum(jnp.sqrt(segsum(kk * kk)), 1e-12)
    kn = kw * (1.0 + (a - 1.0) * ka_ref[...])
    r_out[...] = r
    w_out[...] = decay
    k_out[...] = kn
    v_out[...] = vw
    a_out[...] = -kk
    b_out[...] = kk * a
    g_out[...] = g
    bonus_out[...] = segsum(r * kn * rk_ref[...]) * vw


def _hi_lo(w):
    hi = w.astype(BF16)
    return hi, (w - hi.astype(F32)).astype(BF16)


def _rwkv_prep(rows, proj, s_shift, mu, w0, w2, a0, a2, g2p, k_k, k_a, r_k_flat, rw, hd, g_lora, cn=512):
    n = rows.n
    cn = _pick(rw, cn)
    ncb = rw // cn
    tw = g2p.shape[0] + w2.shape[0] + a2.shape[0]
    assert (3 * rw) % tw == 0
    tb = (3 * rw) // tw
    sblk = lambda i: jnp.maximum(i - rows.prompt_blocks, 0)
    part = lambda p: pl.BlockSpec((rows.rb, cn), lambda i, j: (i, j + p * ncb))
    spart = lambda p: pl.BlockSpec((rows.slots, 1, cn), lambda i, j: (sblk(i), 0, j + p * ncb))
    mupart = lambda p: pl.BlockSpec((1, cn), lambda i, j: (0, j + p * ncb))
    vec = pl.BlockSpec((1, cn), lambda i, j: (0, j))
    lora = lambda k: pl.BlockSpec((k, cn), lambda i, j: (0, j))
    out = pl.BlockSpec((rows.rb, cn), lambda i, j: (i, j))
    kern = functools.partial(_rwkv_prep_kernel, dt=rows.dt, prompt_blocks=rows.prompt_blocks,
                             blocks_per_seq=rows.blocks_per_seq,
                             w_lora=w2.shape[0], a_lora=a2.shape[0], g_lora=g_lora, hd=hd)
    shift3 = s_shift[:, None, :]
    mu2 = mu.reshape(1, -1)
    return pl.pallas_call(
        kern,
        grid=(rows.n_blocks, ncb),
        in_specs=[part(0), part(1), part(2), pl.BlockSpec((rows.rb, tw), lambda i, j: (i, tb)),
                  spart(0), spart(1), spart(2), pl.BlockSpec((rows.slots, 1, tw), lambda i, j: (sblk(i), 0, tb)),
                  mupart(0), mupart(1), mupart(2), pl.BlockSpec((1, tw), lambda i, j: (0, tb)),
                  vec, lora(w2.shape[0]), lora(w2.shape[0]), vec, lora(a2.shape[0]), lora(a2.shape[0]),
                  lora(g2p.shape[0]), lora(g2p.shape[0]), vec, vec, vec],
        out_specs=[out] * 8,
        out_shape=[jax.ShapeDtypeStruct((n, rw), F32)] * 8,
        scratch_shapes=[pltpu.VMEM((3, ncb, 8, cn), F32), pltpu.VMEM((8, tw), F32)],
        compiler_params=_params(("arbitrary", "arbitrary")),
        name="rwkv_prep",
    )(proj, proj, proj, proj, shift3, shift3, shift3, shift3, mu2, mu2, mu2, mu2,
      w0, *_hi_lo(w2), a0, *_hi_lo(a2), *_hi_lo(g2p), k_k, k_a, r_k_flat)


def _rwkv_scan_kernel(r_ref, w_ref, k_ref, v_ref, a_ref, b_ref, *rest, hd, has_state):
    if has_state:
        s0_ref, o_ref, sout_ref, st_ref, ob_ref, vs_ref, row_ref = rest
    else:
        o_ref, sout_ref, st_ref, ob_ref, vs_ref, row_ref = rest
        s0_ref = None
    tb = pl.program_id(1)
    tc, width = r_ref.shape
    pw = 2 * hd
    npairs = width // pw

    @pl.when(tb == 0)
    def _():
        for p in range(npairs):
            if has_state:
                st_ref[p] = jnp.concatenate([s0_ref[0, 2 * p], s0_ref[0, 2 * p + 1]], axis=1)
            else:
                st_ref[p] = jnp.zeros((hd, pw), F32)

    half = npairs // 2
    ones2 = _block_ones(2 * pw, hd, BF16)
    sub = lax.broadcasted_iota(jnp.int32, (hd, pw), 0)
    lane = lax.broadcasted_iota(jnp.int32, (hd, pw), 1)
    diags = [(((lane % hd) - sub + hd) % hd == i).astype(F32).astype(BF16) for i in range(3)]
    lane8 = lax.broadcasted_iota(jnp.int32, (8, pw), 1)
    row16 = lax.broadcasted_iota(jnp.int32, (8, 2 * pw), 0)
    lane16 = lax.broadcasted_iota(jnp.int32, (8, 2 * pw), 1)
    sel2 = row16 == (lane16 // hd)

    def shift_in_head(x, i):
        return jnp.where(lane8 % hd >= i, pltpu.roll(x, i, 1), pltpu.roll(x, pw - hd + i, 1))

    def group(tg, carry):
        rows8 = pl.ds(pl.multiple_of(tg * 8, 8), 8)
        for idx, ref in enumerate((a_ref, w_ref, k_ref, b_ref, r_ref)):
            row_ref[idx] = ref[rows8, :]
        for p in range(npairs):
            cols = pl.ds(p * pw, pw)
            v8 = v_ref[rows8, cols]
            v1 = v8.astype(BF16).astype(F32)
            v2 = (v8 - v1).astype(BF16).astype(F32)
            vs_ref[0, :, cols] = v1
            vs_ref[1, :, cols] = shift_in_head(v2, 1)
            vs_ref[2, :, cols] = shift_in_head((v8 - v1) - v2, 2)

        def row(idx, p, j):
            return row_ref[idx, j:j + 1, pl.ds(p * pw, pw)]

        def side_by_side(tiles):
            return jnp.concatenate([jnp.concatenate([tiles[p], tiles[p + half]], axis=1) for p in range(half)],
                                   axis=0)

        def split(x, p):
            q, c = p % half, p // half
            return x[q * hd:(q + 1) * hd, c * pw:(c + 1) * pw]

        def moved_v(j):
            xs = []
            for p in range(npairs):
                x = None
                for i in range(3):
                    term = diags[i] * vs_ref[i, j:j + 1, pl.ds(p * pw, pw)].astype(BF16)
                    x = term if x is None else x + term
                xs.append(x)
            return _dot(side_by_side(xs), ones2)

        def emit_out(j, states):
            for p in range(half):
                r2 = jnp.concatenate([row(4, p, j), row(4, p + half, j)], axis=1)
                rsel = jnp.where(sel2, r2, 0.0).astype(BF16)
                o8 = _dot_nt(rsel, jnp.concatenate([states[p], states[p + half]], axis=1))
                ob_ref[p, j:j + 1, 0:hd] = o8[0:1, :]
                ob_ref[p, j:j + 1, hd:pw] = o8[1:2, :]
                ob_ref[p + half, j:j + 1, 0:hd] = o8[2:3, :]
                ob_ref[p + half, j:j + 1, hd:pw] = o8[3:4, :]

        vc_next = moved_v(0)
        states = None
        for j in range(8):
            old = [st_ref[p] for p in range(npairs)]
            sa = _dot(side_by_side([old[p] * row(0, p, j) for p in range(npairs)]).astype(BF16), ones2)
            if states is not None:
                emit_out(j - 1, states)
            vc = vc_next
            if j < 7:
                vc_next = moved_v(j + 1)
            states = []
            for p in range(npairs):
                s = old[p] * row(1, p, j) + split(sa, p) * row(3, p, j) + split(vc, p) * row(2, p, j)
                st_ref[p] = s
                states.append(s.astype(BF16))
        emit_out(7, states)
        for p in range(npairs):
            o_ref[rows8, pl.ds(p * pw, pw)] = ob_ref[p]
        return carry

    lax.fori_loop(0, tc // 8, group, 0)

    @pl.when(tb == pl.num_programs(1) - 1)
    def _():
        for p in range(npairs):
            s = st_ref[p]
            sout_ref[0, 2 * p] = s[:, :hd]
            sout_ref[0, 2 * p + 1] = s[:, hd:]


def _rwkv_scan(ops, nseq, t, row0, tc, heads, hd, state):
    width = heads * hd
    assert t % tc == 0 and row0 % tc == 0
    nt = t // tc
    blk = pl.BlockSpec((tc, width), lambda s, j: (row0 // tc + s * nt + j, 0))
    st = pl.BlockSpec((1, heads, hd, hd), lambda s, j: (s, 0, 0, 0))
    in_specs = [blk] * 6
    args = list(ops)
    if state is not None:
        in_specs.append(st)
        args.append(state)
    return pl.pallas_call(
        functools.partial(_rwkv_scan_kernel, hd=hd, has_state=state is not None),
        grid=(nseq, nt),
        in_specs=in_specs,
        out_specs=[pl.BlockSpec((tc, width), lambda s, j: (s * nt + j, 0)), st],
        out_shape=[jax.ShapeDtypeStruct((nseq * t, width), F32),
                   jax.ShapeDtypeStruct((nseq, heads, hd, hd), F32)],
        scratch_shapes=[pltpu.VMEM((heads // 2, hd, 2 * hd), F32), pltpu.VMEM((heads // 2, 8, 2 * hd), F32),
                        pltpu.VMEM((3, 8, width), F32), pltpu.VMEM((5, 8, width), F32)],
        compiler_params=_params(("parallel", "arbitrary")),
        name="rwkv_scan_state" if state is not None else "rwkv_scan",
    )(*args)


def _rwkv_post_kernel(op_ref, os_ref, bonus_ref, g_ref, lg_ref, lb_ref, out_ref, *, hd, first_blocks):
    cn = out_ref.shape[1]
    seg = _block_ones(LANES, hd, BF16)

    def segmean(x):
        return jnp.concatenate([_dot_exact01(x[:, c:c + LANES], seg) for c in range(0, cn, LANES)],
                               axis=1) * (1.0 / hd)

    def body(o):
        mean = segmean(o)
        cen = o - mean
        var = segmean(cen * cen)
        y = cen * lax.rsqrt(var + LNX_EPS) * lg_ref[...] + lb_ref[...]
        out_ref[...] = ((y + bonus_ref[...]) * g_ref[...]).astype(out_ref.dtype)

    _from_either(pl.program_id(0), first_blocks, op_ref, os_ref, body)


def _rwkv_post(o_raw_p, o_raw_s, bonus, g, lnx_g, lnx_b, hd, rb=256, cn=512):
    n, rw = bonus.shape
    rb, cn = _pick(math.gcd(o_raw_p.shape[0], o_raw_s.shape[0]), rb), _pick(rw, cn)
    first_blocks = o_raw_p.shape[0] // rb
    blk = pl.BlockSpec((rb, cn), lambda i, j: (i, j))
    vec = pl.BlockSpec((1, cn), lambda i, j: (0, j))
    pair = [pl.BlockSpec((rb, cn), lambda i, j: (jnp.minimum(i, first_blocks - 1), j)),
            pl.BlockSpec((rb, cn), lambda i, j: (jnp.maximum(i - first_blocks, 0), j))]
    return pl.pallas_call(
        functools.partial(_rwkv_post_kernel, hd=hd, first_blocks=first_blocks),
        grid=(n // rb, rw // cn),
        in_specs=pair + [blk, blk, vec, vec],
        out_specs=blk,
        out_shape=jax.ShapeDtypeStruct((n, rw), BF16),
        compiler_params=_params(("parallel", "parallel")),
        name="rwkv_post",
    )(o_raw_p, o_raw_s, bonus, g, lnx_g, lnx_b)


def _mmt_kernel(wt_ref, x_ref, o_ref):
    o_ref[...] = _dot(wt_ref[...].astype(BF16), x_ref[...])


def _matmul_t(wt, xt, tm=512, tn=1024, name="matmul_t"):
    m, k = wt.shape
    n = xt.shape[1]
    tm, tn = _pick(m, tm), _pick(n, tn)
    return pl.pallas_call(
        _mmt_kernel,
        grid=(n // tn, m // tm),
        in_specs=[pl.BlockSpec((tm, k), lambda j, i: (i, 0)),
                  pl.BlockSpec((k, tn), lambda j, i: (0, j))],
        out_specs=pl.BlockSpec((tm, tn), lambda j, i: (i, j)),
        out_shape=jax.ShapeDtypeStruct((m, n), F32),
        compiler_params=_params(("parallel", "parallel")),
        name=name,
    )(wt, xt)


def _take_top(s, k):
    n = s.shape[0]
    row = lax.broadcasted_iota(jnp.int32, s.shape, 0).astype(F32)
    out = []
    for _ in range(k):
        m = jnp.max(s, axis=0, keepdims=True)
        first = jnp.min(jnp.where(s == m, row, float(n)), axis=0, keepdims=True)
        s = jnp.where(row == first, -jnp.inf, s)
        out.append(m)
    return out


def _peer_topk_kernel(q_ref, keys_ref, s0_ref, s1_ref, e0_ref, e1_ref, thr_ref, *, topk):
    kd = keys_ref.shape[3]
    cols = q_ref.shape[1]
    s0 = _dot(keys_ref[0, 0], q_ref[:kd, :], HI)
    s1 = _dot(keys_ref[0, 1], q_ref[kd:, :], HI)
    top0 = _take_top(s0, topk)
    top1 = _take_top(s1, topk)
    width = [topk // (a + 1) for a in range(topk)]
    start = [sum(width[:a]) for a in range(topk)]
    n_cand = sum(width)
    rows = -(-n_cand // 8) * 8
    r = lax.broadcasted_iota(jnp.int32, (rows, cols), 0)
    t0 = jnp.broadcast_to(top0[0], (rows, cols))
    first = jnp.zeros((rows, cols), jnp.int32)
    for a in range(1, topk):
        t0 = jnp.where(r >= start[a], top0[a], t0)
        first = jnp.where(r >= start[a], start[a], first)
    b_of_row = r - first
    t1 = jnp.broadcast_to(top1[0], (rows, cols))
    for b in range(1, topk):
        t1 = jnp.where(b_of_row == b, top1[b], t1)
    cand = jnp.where(r < n_cand, t0 + t1, -jnp.inf)
    best = _take_top(cand, topk)
    z = jnp.zeros((1, cols), F32)
    for b in best:
        z = z + jnp.exp(b - best[0])
    s0_ref[...] = s0
    s1_ref[...] = s1
    e0_ref[...] = jnp.exp(s0 - top0[0]) / z
    e1_ref[...] = jnp.exp(s1 - top1[0])
    thr_ref[0] = jnp.broadcast_to(best[topk - 1], thr_ref.shape[1:])


def _peer_topk(qt, sub_keys, tc=256):
    n = qt.shape[1]
    tc = _pick(n, tc)
    heads, _, nk, kd = sub_keys.shape
    blk = pl.BlockSpec((nk, tc), lambda i, h: (h, i))
    shp = jax.ShapeDtypeStruct((heads * nk, n), F32)
    return pl.pallas_call(
        functools.partial(_peer_topk_kernel, topk=PEER_TOPK),
        grid=(n // tc, heads),
        in_specs=[pl.BlockSpec((2 * kd, tc), lambda i, h: (h, i)),
                  pl.BlockSpec((1, 2, nk, kd), lambda i, h: (h, 0, 0, 0))],
        out_specs=[blk] * 4 + [pl.BlockSpec((1, 8, tc), lambda i, h: (h, 0, i))],
        out_shape=[shp] * 4 + [jax.ShapeDtypeStruct((heads, 8, n), F32)],
        compiler_params=_params(("parallel", "parallel")),
        name="peer_topk",
    )(qt, sub_keys)


def _gelu(x):
    return 0.5 * x * (1.0 + lax.erf(x * (2.0 ** -0.5)))


def _peer_dense_kernel(u_ref, ht_ref, vt_ref, s0_ref, e0_ref, s1_ref, e1_ref, thr_ref, o_ref, act_ref, coef_ref,
                       *, heads, nk, jc):
    e = pl.program_id(1)
    eb = u_ref.shape[0]
    d = vt_ref.shape[0]
    ni = eb // nk
    dq = d // ni

    @pl.when(e == 0)
    def _():
        o_ref[...] = jnp.zeros_like(o_ref)
        coef_ref[1] = jnp.zeros(coef_ref.shape[1:], coef_ref.dtype)

    act_ref[...] = _dot(u_ref[...], ht_ref[...])
    cur = e % 2
    prev = (e + 1) % 2
    thr = thr_ref[...]

    def one_i(ii, carry):
        orow = pl.ds(pl.multiple_of(ii * dq, dq), dq)
        o_ref[orow, :] += _dot(vt_ref[orow, :], coef_ref[prev])
        s0 = s0_ref[0, ii]
        e0 = e0_ref[0, ii]
        for j0 in range(0, nk, jc):
            g = None
            for h in range(heads):
                jrows = slice(h * nk + j0, h * nk + j0 + jc)
                hit = (s0[h:h + 1] + s1_ref[jrows, :]) >= thr[h:h + 1]
                term = jnp.where(hit, e0[h:h + 1] * e1_ref[jrows, :], 0.0)
                g = term if g is None else g + term
            arow = pl.ds(pl.multiple_of(ii * nk + j0, jc), jc)
            coef_ref[cur, arow, :] = (g * _gelu(act_ref[arow, :])).astype(coef_ref.dtype)
        return carry

    lax.fori_loop(0, ni, one_i, 0)


def _peer_dense(ht, u, vt, s0r, e0r, s1, e1, thr, heads, nk, tp=512, eb=512):
    d, n = ht.shape
    tp = _pick(n, tp)
    nblk = u.shape[0] // eb
    ni = eb // nk
    once = pl.Buffered(1)
    tok = pl.BlockSpec((heads * nk, tp), lambda i, e: (0, i), pipeline_mode=once)
    sl = pl.BlockSpec((1, ni, heads, tp), lambda i, e: (jnp.minimum(e, nblk - 1), 0, 0, i))
    return pl.pallas_call(
        functools.partial(_peer_dense_kernel, heads=heads, nk=nk, jc=nk // 2),
        grid=(n // tp, nblk + 1),
        in_specs=[pl.BlockSpec((eb, d), lambda i, e: (jnp.minimum(e, nblk - 1), 0)),
                  pl.BlockSpec((d, tp), lambda i, e: (0, i), pipeline_mode=once),
                  pl.BlockSpec((d, eb), lambda i, e: (0, jnp.maximum(e - 1, 0))),
                  sl, sl, tok, tok, pl.BlockSpec((heads, tp), lambda i, e: (0, i))],
        out_specs=pl.BlockSpec((d, tp), lambda i, e: (0, i)),
        out_shape=jax.ShapeDtypeStruct((d, n), F32),
        scratch_shapes=[pltpu.VMEM((eb, tp), F32), pltpu.VMEM((2, eb, tp), BF16)],
        compiler_params=_params(("parallel", "arbitrary")),
        name="peer_dense",
    )(u, ht, vt, s0r, e0r, s1, e1, thr)


def _by_expert_block(x, heads, nk, ni):
    n = x.shape[1]
    return x.reshape(heads, nk // ni, ni, n).transpose(1, 2, 0, 3)


def _layer(xp, xs, s_ret, s_rwkv, s_shift, cp, cs, p):
    nb, t, d = xp.shape
    db, dt, _ = xs.shape
    rows = _Rows(nb, t, db, dt)
    ret_heads, ret_hd = s_ret.shape[1], s_ret.shape[2]
    rw_heads, rw_hd = s_rwkv.shape[1], s_rwkv.shape[2]
    ret_w, rw = ret_heads * ret_hd, rw_heads * rw_hd
    w_lora, a_lora, g_lora = p['w2'].shape[0], p['a2'].shape[0], p['g2'].shape[0]
    rwkv_in = s_shift.shape[-1]
    ret_in = 4 * ret_w
    assert p['w_in'].shape[1] == ret_in + rwkv_in and rwkv_in == 3 * rw + w_lora + a_lora + g_lora

    xp, xs = xp.reshape(nb * t, d), xs.reshape(db * dt, d)
    mod = _ada(jnp.concatenate([cp, cs], axis=0), p['ada_w'], p['ada_b'])
    mod_ext = rows.extend(mod)

    h = _prenorm(rows, xp, xs, p['pre_mix_g'], mod_ext, d)

    tm = _pick(rows.n, 1024)
    w_in_t = p['w_in'].T
    proj_ret = _matmul(h, w_in_t, col_block0=0, n_out=ret_in, tm=tm, tn=512, name="in_proj_ret")
    proj_rw = _matmul(h, w_in_t, col_block0=ret_in // 512, n_out=rwkv_in, tm=tm, tn=512, name="in_proj_rwkv")

    o_ret_p, o_ret_s, sr_p, sr_s = _retention(rows, proj_ret, s_ret, ret_heads, ret_hd)

    tail_w = -(-(w_lora + a_lora + g_lora) // LANES) * LANES
    g2p = jnp.pad(p['g2'], ((0, tail_w - w_lora - a_lora - g_lora), (0, 0)))
    vec = lambda a: a.reshape(1, rw)
    r_, w_, k_, v_, a_, b_, g_, bonus = _rwkv_prep(
        rows, proj_rw, s_shift, p['shift_mu'], vec(p['w0']), p['w2'], vec(p['a0']), p['a2'], g2p,
        vec(p['k_k']), vec(p['k_a']), vec(p['r_k']), rw, rw_hd, g_lora)
    ops = (r_, w_, k_, v_, a_, b_)
    o_rw_p, sw_p = _rwkv_scan(ops, nb, t, 0, 256, rw_heads, rw_hd, None)
    o_rw_s, sw_s = _rwkv_scan(ops, db, dt, nb * t, dt, rw_heads, rw_hd, s_rwkv)
    o_rw = _rwkv_post(o_rw_p, o_rw_s, bonus, g_, vec(p['lnx_g']), vec(p['lnx_b']), rw_hd)

    mix = _matmul_out(o_ret_p, o_ret_s, o_rw, p['w_out'])
    x1, h2t = _postmix(rows, xp, xs, mix, p['post_mix_g'], p['pre_ffn_g'], mod_ext, d)

    heads, _, nk, kd = p['peer_sub_keys'].shape
    qt = _matmul_t(p['peer_wq'].T, h2t, name="peer_q")
    s0, s1, e0, e1, thr = _peer_topk(qt, p['peer_sub_keys'])
    eb = 512
    ni = eb // nk
    peer_t = _peer_dense(h2t, p['peer_u'].astype(BF16), p['peer_v'].T.astype(BF16),
                         _by_expert_block(s0, heads, nk, ni), _by_expert_block(e0, heads, nk, ni),
                         s1, e1, thr[:, 0, :], heads, nk, eb=eb)
    yp = _final(rows, x1, peer_t, p['post_ffn_g'], mod_ext, d, 0, rows.prompt_blocks)
    ys = _final(rows, x1, peer_t, p['post_ffn_g'], mod_ext, d, rows.prompt_blocks, rows.n_blocks - rows.prompt_blocks)

    n_p = nb * t
    last_p = slice(t - 1, n_p, t)
    last_s = slice(n_p + dt - 1, None, dt)
    return (yp.reshape(nb, t, d), ys.reshape(db, dt, d), sr_p, sr_s, sw_p, sw_s, proj_rw[last_p], proj_rw[last_s])


def kernel(x_prompt, x_sample, state_ret, state_rwkv, state_shift, c_prompt, c_sample, ada_w, ada_b, pre_mix_g, post_mix_g, pre_ffn_g, post_ffn_g, w_in, shift_mu, w0, w2, a0, a2, g2, k_k, k_a, r_k, lnx_g, lnx_b, w_out, peer_wq, peer_sub_keys, peer_u, peer_v):
    depth = ada_w.shape[0]
    assert depth == 1, "prompt and sample tokens are stacked per layer; deeper stacks need per-layer restacking"
    prm = dict(ada_w=ada_w[0], ada_b=ada_b[0], pre_mix_g=pre_mix_g[0], post_mix_g=post_mix_g[0],
               pre_ffn_g=pre_ffn_g[0], post_ffn_g=post_ffn_g[0], w_in=w_in[0], shift_mu=shift_mu[0],
               w0=w0[0], w2=w2[0], a0=a0[0], a2=a2[0], g2=g2[0], k_k=k_k[0], k_a=k_a[0], r_k=r_k[0],
               lnx_g=lnx_g[0], lnx_b=lnx_b[0], w_out=w_out[0], peer_wq=peer_wq[0],
               peer_sub_keys=peer_sub_keys[0], peer_u=peer_u[0], peer_v=peer_v[0])
    yp, ys, sr_p, sr_s, sw_p, sw_s, ss_p, ss_s = _layer(
        x_prompt, x_sample, state_ret[0], state_rwkv[0], state_shift[0], c_prompt, c_sample, prm)
    sd, wd, hd = state_ret.dtype, state_rwkv.dtype, state_shift.dtype
    return (yp, ys, sr_p[None].astype(sd), sr_s[None].astype(sd), sw_p[None].astype(wd), sw_s[None].astype(wd),
            ss_p[None].astype(hd), ss_s[None].astype(hd))
```

```python
import functools
import math

import numpy as np
import jax
import jax.numpy as jnp
from jax import lax
from jax.experimental import pallas as pl
from jax.experimental.pallas import tpu as pltpu

PAST_LEN = 16384
RET_CHUNK = 128
ROPE_BASE = 10000.0
LNX_EPS = 64e-5
NORM_EPS = 1e-6
PEER_TOPK = 16
RET_HEADS_PER_STEP = 4

LANES = 128
ROW_BLOCK = 128
VMEM_LIMIT = 56 * 1024 * 1024

HI = lax.Precision.HIGHEST
F32 = jnp.float32
BF16 = jnp.bfloat16


def _pick(n, target, mult=LANES):
    best = None
    for c in range(mult, min(n, target) + 1, mult):
        if n % c == 0:
            best = c
    assert best is not None, (n, target, mult)
    return best


def _params(sem, vmem=VMEM_LIMIT):
    return pltpu.CompilerParams(dimension_semantics=sem, vmem_limit_bytes=vmem)


def _dot(a, b, precision=None):
    return jnp.dot(a, b, preferred_element_type=F32, precision=precision)


def _dot_nt(a, b, precision=None):
    return lax.dot_general(a, b, (((1,), (1,)), ((), ())), preferred_element_type=F32, precision=precision)


def _dot_tn(a, b, precision=None):
    return lax.dot_general(a, b, (((0,), (0,)), ((), ())), preferred_element_type=F32, precision=precision)


def _split_bf16(x, terms):
    out = []
    for _ in range(terms - 1):
        t = x.astype(BF16)
        out.append(t)
        x = x - t.astype(F32)
    return out + [x.astype(BF16)]


def _dot_split(x, w_terms):
    xh, xl = _split_bf16(x, 2)
    wh, wl = w_terms
    return _dot(xh, wh) + (_dot(xl, wh) + _dot(xh, wl))


def _dot_exact01(x, ones):
    return sum(_dot(t, ones) for t in _split_bf16(x, 3))


def _block_ones(n, seg, dtype):
    r = lax.broadcasted_iota(jnp.int32, (n, n), 0) // seg
    c = lax.broadcasted_iota(jnp.int32, (n, n), 1) // seg
    return (r == c).astype(dtype)


def _ada_kernel(c_ref, w_ref, b_ref, o_ref):
    c = c_ref[...]
    sc = (c * jax.nn.sigmoid(c)).astype(BF16)
    o_ref[...] = _dot(sc, w_ref[...].astype(BF16)) + b_ref[...]


def _ada(c_all, ada_w, ada_b, tn=512):
    m, d = c_all.shape
    n = ada_w.shape[1]
    return pl.pallas_call(
        _ada_kernel,
        grid=(n // tn,),
        in_specs=[pl.BlockSpec((m, d), lambda j: (0, 0)),
                  pl.BlockSpec((d, tn), lambda j: (0, j)),
                  pl.BlockSpec((1, tn), lambda j: (0, j))],
        out_specs=pl.BlockSpec((m, tn), lambda j: (0, j)),
        out_shape=jax.ShapeDtypeStruct((m, n), F32),
        compiler_params=_params(("arbitrary",)),
        name="ada",
    )(c_all, ada_w, ada_b.reshape(1, n))


def _expand_slots(m, rows):
    slots, _, d = m.shape
    return jnp.broadcast_to(m, (slots, rows // slots, d)).reshape(rows, d)


def _rms(x, g):
    return x * lax.rsqrt(jnp.mean(x * x, axis=-1, keepdims=True) + NORM_EPS) * g


def _from_either(i, first_blocks, a_ref, b_ref, body):
    @pl.when(i < first_blocks)
    def _():
        body(a_ref[...])

    @pl.when(i >= first_blocks)
    def _():
        body(b_ref[...])


def _two_specs(block, first_blocks):
    return [pl.BlockSpec(block, lambda i, *_: (jnp.minimum(i, first_blocks - 1), 0)),
            pl.BlockSpec(block, lambda i, *_: (jnp.maximum(i - first_blocks, 0), 0))]


def _prenorm_kernel(xp_ref, xs_ref, g_ref, sh_ref, sc_ref, h_ref, *, prompt_blocks):
    def body(x):
        rows = x.shape[0]
        y = _rms(x, g_ref[...])
        h = y * (1.0 + _expand_slots(sc_ref[...], rows)) + _expand_slots(sh_ref[...], rows)
        h_ref[...] = h.astype(h_ref.dtype)

    _from_either(pl.program_id(0), prompt_blocks, xp_ref, xs_ref, body)


def _postmix_kernel(xp_ref, xs_ref, m_ref, gpost_ref, gpre_ref, gate_ref, sh_ref, sc_ref, x1_ref, ht_ref,
                    *, prompt_blocks):
    def body(x):
        rows = x.shape[0]
        x1 = x + _expand_slots(gate_ref[...], rows) * _rms(m_ref[...], gpost_ref[...])
        x1_ref[...] = x1
        y = _rms(x1, gpre_ref[...])
        h = y * (1.0 + _expand_slots(sc_ref[...], rows)) + _expand_slots(sh_ref[...], rows)
        ht_ref[...] = h.T.astype(ht_ref.dtype)

    _from_either(pl.program_id(0), prompt_blocks, xp_ref, xs_ref, body)


def _final_kernel(x_ref, mt_ref, gpost_ref, gate_ref, y_ref):
    x = x_ref[...]
    rows = x.shape[0]
    y_ref[...] = x + _expand_slots(gate_ref[...], rows) * _rms(mt_ref[...].T, gpost_ref[...])


class _Rows:
    def __init__(self, nb, t, db, dt):
        self.n_prompt = nb * t
        self.n_sample = db * dt
        self.n = self.n_prompt + self.n_sample
        self.rb = ROW_BLOCK
        assert t % self.rb == 0 and self.rb % dt == 0 and self.n_sample % self.rb == 0
        self.slots = self.rb // dt
        self.prompt_blocks = self.n_prompt // self.rb
        self.blocks_per_seq = t // self.rb
        self.n_blocks = self.n // self.rb
        self.nb, self.t, self.db, self.dt = nb, t, db, dt

    def slot_block(self, i):
        return jnp.where(i < self.prompt_blocks, i // self.blocks_per_seq, i - self.prompt_blocks + self.nb)

    def extend(self, m):
        mp = jnp.repeat(m[: self.nb], self.slots, axis=0)
        return jnp.concatenate([mp, m[self.nb:]], axis=0)[:, None, :]


def _mod_spec(rows, d, chunk, block0=0):
    return pl.BlockSpec((rows.slots, 1, d), lambda i: (rows.slot_block(i + block0), 0, chunk))


def _prenorm(rows, xp, xs, g, mod_ext, d):
    row = pl.BlockSpec((rows.rb, d), lambda i: (i, 0))
    vec = pl.BlockSpec((1, d), lambda i: (0, 0))
    return pl.pallas_call(
        functools.partial(_prenorm_kernel, prompt_blocks=rows.prompt_blocks),
        grid=(rows.n_blocks,),
        in_specs=_two_specs((rows.rb, d), rows.prompt_blocks) + [vec, _mod_spec(rows, d, 0), _mod_spec(rows, d, 1)],
        out_specs=row,
        out_shape=jax.ShapeDtypeStruct((rows.n, d), BF16),
        compiler_params=_params(("parallel",)),
        name="prenorm",
    )(xp, xs, g.reshape(1, d), mod_ext, mod_ext)


def _postmix(rows, xp, xs, mix, gpost, gpre, mod_ext, d):
    row = pl.BlockSpec((rows.rb, d), lambda i: (i, 0))
    vec = pl.BlockSpec((1, d), lambda i: (0, 0))
    return pl.pallas_call(
        functools.partial(_postmix_kernel, prompt_blocks=rows.prompt_blocks),
        grid=(rows.n_blocks,),
        in_specs=_two_specs((rows.rb, d), rows.prompt_blocks)
        + [row, vec, vec, _mod_spec(rows, d, 2), _mod_spec(rows, d, 3), _mod_spec(rows, d, 4)],
        out_specs=[row, pl.BlockSpec((d, rows.rb), lambda i: (0, i))],
        out_shape=[jax.ShapeDtypeStruct((rows.n, d), F32), jax.ShapeDtypeStruct((d, rows.n), BF16)],
        compiler_params=_params(("parallel",)),
        name="postmix",
    )(xp, xs, mix, gpost.reshape(1, d), gpre.reshape(1, d), mod_ext, mod_ext, mod_ext)


def _final(rows, x1, peer_t, gpost, mod_ext, d, block0, n_blocks):
    vec = pl.BlockSpec((1, d), lambda i: (0, 0))
    return pl.pallas_call(
        _final_kernel,
        grid=(n_blocks,),
        in_specs=[pl.BlockSpec((rows.rb, d), lambda i: (i + block0, 0)),
                  pl.BlockSpec((d, rows.rb), lambda i: (0, i + block0)),
                  vec, _mod_spec(rows, d, 5, block0)],
        out_specs=pl.BlockSpec((rows.rb, d), lambda i: (i, 0)),
        out_shape=jax.ShapeDtypeStruct((n_blocks * rows.rb, d), F32),
        compiler_params=_params(("parallel",)),
        name="final",
    )(x1, peer_t, gpost.reshape(1, d), mod_ext)


def _mm_kernel(a_ref, wt_ref, o_ref, wbf_ref):
    @pl.when(pl.program_id(1) == 0)
    def _():
        wbf_ref[...] = wt_ref[...].astype(BF16)

    o_ref[...] = _dot_nt(a_ref[...], wbf_ref[...]).astype(o_ref.dtype)


def _matmul(a, wt, *, col_block0, n_out, tm, tn, name):
    m, k = a.shape
    return pl.pallas_call(
        _mm_kernel,
        grid=(pl.cdiv(n_out, tn), m // tm),
        in_specs=[pl.BlockSpec((tm, k), lambda j, i: (i, 0)),
                  pl.BlockSpec((tn, k), lambda j, i: (j + col_block0, 0))],
        out_specs=pl.BlockSpec((tm, tn), lambda j, i: (i, j)),
        out_shape=jax.ShapeDtypeStruct((m, n_out), F32),
        scratch_shapes=[pltpu.VMEM((tn, k), BF16)],
        compiler_params=_params(("parallel", "arbitrary")),
        name=name,
    )(a, wt)


def _mm2_kernel(ap_ref, as_ref, b_ref, wa_ref, wb_ref, o_ref, *, first_blocks):
    def body(a):
        o_ref[...] = _dot(a, wa_ref[...].astype(BF16)) + _dot(b_ref[...], wb_ref[...].astype(BF16))

    _from_either(pl.program_id(1), first_blocks, ap_ref, as_ref, body)


def _matmul_out(o_ret_p, o_ret_s, o_rw, w_out, tn=512):
    m, kb = o_rw.shape
    ka = o_ret_p.shape[1]
    tm = _pick(math.gcd(o_ret_p.shape[0], o_ret_s.shape[0]), 1024)
    first_blocks = o_ret_p.shape[0] // tm
    n = w_out.shape[1]
    tn = _pick(n, tn)
    assert ka == kb
    pair = [pl.BlockSpec((tm, ka), lambda j, i: (jnp.minimum(i, first_blocks - 1), 0)),
            pl.BlockSpec((tm, ka), lambda j, i: (jnp.maximum(i - first_blocks, 0), 0))]
    return pl.pallas_call(
        functools.partial(_mm2_kernel, first_blocks=first_blocks),
        grid=(n // tn, m // tm),
        in_specs=pair + [pl.BlockSpec((tm, kb), lambda j, i: (i, 0)),
                         pl.BlockSpec((ka, tn), lambda j, i: (0, j)),
                         pl.BlockSpec((kb, tn), lambda j, i: (1, j))],
        out_specs=pl.BlockSpec((tm, tn), lambda j, i: (i, j)),
        out_shape=jax.ShapeDtypeStruct((m, n), F32),
        compiler_params=_params(("parallel", "parallel")),
        name="out_proj",
    )(o_ret_p, o_ret_s, o_rw, w_out, w_out)


def _rope_tables(pos, hd):
    half = hd // 2
    inv = ROPE_BASE ** (-jnp.arange(half, dtype=F32) / half)
    ang = pos.astype(F32)[:, None] * inv[None, :]
    cos, sin = jnp.cos(ang), jnp.sin(ang)
    return jnp.concatenate([cos, cos], axis=-1), jnp.concatenate([-sin, sin], axis=-1)


def _rope(x, cos, sin_signed):
    half = x.shape[-1] // 2
    return x * cos + pltpu.roll(x, half, 1) * sin_signed


def _ret_finish(o, gate):
    o = o * lax.rsqrt(jnp.mean(o * o, axis=-1, keepdims=True) + NORM_EPS)
    return o * (gate * jax.nn.sigmoid(gate))


def _ret_prompt_kernel(q_ref, k_ref, v_ref, g_ref, cos_ref, sin_ref, lg_ref, o_ref, s_ref, *, chunk, hd):
    t = q_ref.shape[0]
    nh = q_ref.shape[1] // hd
    ri = lax.broadcasted_iota(jnp.int32, (chunk, chunk), 0)
    ci = lax.broadcasted_iota(jnp.int32, (chunk, chunk), 1)
    causal = ri >= ci
    diff = jnp.where(causal, ri - ci, 0).astype(F32)
    rowf = lax.broadcasted_iota(jnp.int32, (chunk, hd), 0).astype(F32)
    lgs = [lg_ref[i, 0:1, :] for i in range(nh)]
    masks = [jnp.where(causal, jnp.exp(diff * lg), 0.0) for lg in lgs]
    q_decs = [jnp.exp((rowf + 1.0) * lg) for lg in lgs]
    k_decs = [jnp.exp((chunk - 1.0 - rowf) * lg) for lg in lgs]
    c_decs = [jnp.exp(float(chunk) * lg) for lg in lgs]

    def step(c, states):
        r0 = pl.multiple_of(c * chunk, chunk)
        rows = pl.ds(r0, chunk)
        cos, sin = cos_ref[rows, :], sin_ref[rows, :]
        cols = [slice(i * hd, (i + 1) * hd) for i in range(nh)]
        qs = [_rope(q_ref[rows, cl], cos, sin) for cl in cols]
        ks = [_rope(k_ref[rows, cl], cos, sin) * (hd ** -0.5) for cl in cols]
        vs = [v_ref[rows, cl] for cl in cols]
        atts = [_dot_nt(q.astype(BF16), k.astype(BF16)) * m for q, k, m in zip(qs, ks, masks)]
        cross = [_dot((q * qd).astype(BF16), s.astype(BF16)) for q, qd, s in zip(qs, q_decs, states)]
        inner = [_dot(a.astype(BF16), v.astype(BF16)) for a, v in zip(atts, vs)]
        new = [s * cd + _dot_tn(k * kd, v, HI) for s, cd, k, kd, v in zip(states, c_decs, ks, k_decs, vs)]
        for i, cl in enumerate(cols):
            o_ref[rows, cl] = _ret_finish(inner[i] + cross[i], g_ref[rows, cl]).astype(o_ref.dtype)
        return tuple(new)

    final = lax.fori_loop(0, t // chunk, step, tuple(jnp.zeros((hd, hd), F32) for _ in range(nh)))
    for i in range(nh):
        s_ref[0, i] = final[i]


def _ret_sample_kernel(q_ref, k_ref, v_ref, g_ref, cos_ref, sin_ref, lg_ref, s0_ref, o_ref, s_ref, *, dt):
    rows, hd = q_ref.shape
    nseq = rows // dt
    lg = lg_ref[0, 0:1, :]
    ri = lax.broadcasted_iota(jnp.int32, (rows, rows), 0)
    ci = lax.broadcasted_iota(jnp.int32, (rows, rows), 1)
    ok = (ri >= ci) & ((ri // dt) == (ci // dt))
    diff = jnp.where(ok, ri - ci, 0).astype(F32)
    mask = jnp.where(ok, jnp.exp(diff * lg), 0.0)
    rown = lax.broadcasted_iota(jnp.int32, (rows, hd), 0)
    posf = (rown % dt).astype(F32)
    q_dec = jnp.exp((posf + 1.0) * lg)
    k_dec = jnp.exp((dt - 1.0 - posf) * lg)
    c_dec = jnp.exp(float(dt) * lg)
    cos, sin = cos_ref[...], sin_ref[...]
    qc = _rope(q_ref[...], cos, sin)
    kc = _rope(k_ref[...], cos, sin) * (hd ** -0.5)
    vc = v_ref[...]
    att = _dot_nt(qc.astype(BF16), kc.astype(BF16)) * mask
    inner = _dot(att.astype(BF16), vc.astype(BF16))
    qd = qc * q_dec
    kd = kc * k_dec
    seq = rown // dt
    cross = jnp.zeros((rows, hd), F32)
    for s in range(nseq):
        s0 = s0_ref[s, 0]
        mine = seq == s
        cross = cross + _dot(jnp.where(mine, qd, 0.0).astype(BF16), s0.astype(BF16))
        s_ref[s, 0] = s0 * c_dec + _dot_tn(jnp.where(mine, kd, 0.0), vc, HI)
    o_ref[...] = _ret_finish(inner + cross, g_ref[...]).astype(o_ref.dtype)


def _retention(rows, proj, state_ret, ret_heads, hd):
    nb, t, db, dt = rows.nb, rows.t, rows.db, rows.dt
    lg = jnp.log1p(-jnp.exp2(-5.0 - jnp.arange(ret_heads, dtype=F32)))
    lg_tab = jnp.broadcast_to(lg[:, None, None], (ret_heads, 8, hd))
    lg_spec = pl.BlockSpec((1, 8, hd), lambda b, h: (h, 0, 0))
    assert hd == RET_CHUNK and t % RET_CHUNK == 0 and dt % RET_CHUNK != 0

    cos_p, sin_p = _rope_tables(jnp.arange(t, dtype=jnp.int32), hd)
    nh = RET_HEADS_PER_STEP if ret_heads % RET_HEADS_PER_STEP == 0 else 1
    hsteps = ret_heads // nh
    col = lambda part: pl.BlockSpec((t, nh * hd), lambda b, h: (b, h + part * hsteps))
    tab = pl.BlockSpec((t, hd), lambda b, h: (0, 0))
    o_p, s_p = pl.pallas_call(
        functools.partial(_ret_prompt_kernel, chunk=RET_CHUNK, hd=hd),
        grid=(nb, hsteps),
        in_specs=[col(0), col(1), col(2), col(3), tab, tab, pl.BlockSpec((nh, 8, hd), lambda b, h: (h, 0, 0))],
        out_specs=[pl.BlockSpec((t, nh * hd), lambda b, h: (b, h)),
                   pl.BlockSpec((1, nh, hd, hd), lambda b, h: (b, h, 0, 0))],
        out_shape=[jax.ShapeDtypeStruct((nb * t, ret_heads * hd), BF16),
                   jax.ShapeDtypeStruct((nb, ret_heads, hd, hd), F32)],
        compiler_params=_params(("parallel", "parallel")),
        name="ret_prompt",
    )(proj, proj, proj, proj, cos_p, sin_p, lg_tab)

    rb = RET_CHUNK
    nseq = rb // dt
    cos_s, sin_s = _rope_tables(PAST_LEN + jnp.arange(dt, dtype=jnp.int32), hd)
    cos_s, sin_s = jnp.tile(cos_s, (nseq, 1)), jnp.tile(sin_s, (nseq, 1))
    off = (nb * t) // rb
    col = lambda part: pl.BlockSpec((rb, hd), lambda b, h: (b + off, h + part * ret_heads))
    tab = pl.BlockSpec((rb, hd), lambda b, h: (0, 0))
    st = pl.BlockSpec((nseq, 1, hd, hd), lambda b, h: (b, h, 0, 0))
    o_s, s_s = pl.pallas_call(
        functools.partial(_ret_sample_kernel, dt=dt),
        grid=(db // nseq, ret_heads),
        in_specs=[col(0), col(1), col(2), col(3), tab, tab, lg_spec, st],
        out_specs=[pl.BlockSpec((rb, hd), lambda b, h: (b, h)), st],
        out_shape=[jax.ShapeDtypeStruct((db * dt, ret_heads * hd), BF16),
                   jax.ShapeDtypeStruct((db, ret_heads, hd, hd), F32)],
        compiler_params=_params(("parallel", "parallel")),
        name="ret_sample",
    )(proj, proj, proj, proj, cos_s, sin_s, lg_tab, state_ret)
    return o_p, o_s, s_p, s_s


def _shifted(x, first, tseq_mask):
    prev = pltpu.roll(x, 1, 0)
    return jnp.where(tseq_mask, first, prev)


def _rwkv_prep_kernel(pr_ref, pk_ref, pv_ref, pt_ref, sr_ref, sk_ref, sv_ref, st_ref,
                      mur_ref, muk_ref, muv_ref, mut_ref, w0_ref, w2h_ref, w2l_ref, a0_ref, a2h_ref, a2l_ref,
                      g2h_ref, g2l_ref,
                      kk_ref, ka_ref, rk_ref,
                      r_out, w_out, k_out, v_out, a_out, b_out, g_out, bonus_out, carry_ref, tcarry_ref,
                      *, dt, prompt_blocks, blocks_per_seq, w_lora, a_lora, g_lora, hd):
    i, j = pl.program_id(0), pl.program_id(1)
    rows, cn = pr_ref.shape
    rown = lax.broadcasted_iota(jnp.int32, (rows, 1), 0)
    sample_block = jnp.zeros((rows, 1), jnp.int32) + (i >= prompt_blocks).astype(jnp.int32)
    is_first = (rown == 0) | ((rown % dt == 0) & (sample_block == 1))
    seq_start = i % blocks_per_seq == 0

    @pl.when(i == 0)
    def _():
        for part in range(3):
            carry_ref[part, j] = jnp.zeros(carry_ref.shape[2:], F32)
        tcarry_ref[...] = jnp.zeros_like(tcarry_ref)

    def lerp(p_ref, s_ref, mu_ref, carried):
        p = p_ref[...]
        from_prompt = jnp.broadcast_to(jnp.where(seq_start, 0.0, carried), p.shape)
        first = jnp.where(sample_block == 1, _expand_slots(s_ref[...], rows), from_prompt)
        prev = _shifted(p, first, is_first)
        return p + mu_ref[...] * (prev - p)

    r = lerp(pr_ref, sr_ref, mur_ref, carry_ref[0, j, 0:1, :])
    kw = lerp(pk_ref, sk_ref, muk_ref, carry_ref[1, j, 0:1, :])
    vw = lerp(pv_ref, sv_ref, muv_ref, carry_ref[2, j, 0:1, :])
    tail = lerp(pt_ref, st_ref, mut_ref, tcarry_ref[0:1, :])
    for part, p_ref in enumerate((pr_ref, pk_ref, pv_ref)):
        carry_ref[part, j, 0:1, :] = p_ref[rows - 1:rows, :]

    @pl.when(j == pl.num_programs(1) - 1)
    def _():
        tcarry_ref[0:1, :] = pt_ref[rows - 1:rows, :]

    tw = tail.shape[1]
    wl = tail[:, :w_lora]
    al = tail[:, w_lora:w_lora + a_lora]
    glane = lax.broadcasted_iota(jnp.int32, (rows, tw - w_lora - a_lora), 1)
    gl = jnp.where(glane < g_lora, jax.nn.sigmoid(tail[:, w_lora + a_lora:]), 0.0)

    w_log = -jax.nn.softplus(-(w0_ref[...] + _dot_split(jnp.tanh(wl), (w2h_ref[...], w2l_ref[...])))) - 0.5
    decay = jnp.exp(-jnp.exp(w_log))
    a = jax.nn.sigmoid(a0_ref[...] + _dot_split(al, (a2h_ref[...], a2l_ref[...])))
    g = _dot_split(gl, (g2h_ref[...], g2l_ref[...]))

    seg = _block_ones(LANES, hd, BF16)

    def segsum(x):
        return jnp.concatenate([_dot_exact01(x[:, c:c + LANES], seg) for c in range(0, cn, LANES)], axis=1)

    kk = kw * kk_ref[...]
    kk = kk / jnp.maximum(jnp.sqrt(segsum(kk * kk)), 1e-12)
    kn = kw * (1.0 + (a - 1.0) * ka_ref[...])
    r_out[...] = r
    w_out[...] = decay
    k_out[...] = kn
    v_out[...] = vw
    a_out[...] = -kk
    b_out[...] = kk * a
    g_out[...] = g
    bonus_out[...] = segsum(r * kn * rk_ref[...]) * vw


def _hi_lo(w):
    hi = w.astype(BF16)
    return hi, (w - hi.astype(F32)).astype(BF16)


def _rwkv_prep(rows, proj, s_shift, mu, w0, w2, a0, a2, g2p, k_k, k_a, r_k_flat, rw, hd, g_lora, cn=512):
    n = rows.n
    cn = _pick(rw, cn)
    ncb = rw // cn
    tw = g2p.shape[0] + w2.shape[0] + a2.shape[0]
    assert (3 * rw) % tw == 0
    tb = (3 * rw) // tw
    sblk = lambda i: jnp.maximum(i - rows.prompt_blocks, 0)
    part = lambda p: pl.BlockSpec((rows.rb, cn), lambda i, j: (i, j + p * ncb))
    spart = lambda p: pl.BlockSpec((rows.slots, 1, cn), lambda i, j: (sblk(i), 0, j + p * ncb))
    mupart = lambda p: pl.BlockSpec((1, cn), lambda i, j: (0, j + p * ncb))
    vec = pl.BlockSpec((1, cn), lambda i, j: (0, j))
    lora = lambda k: pl.BlockSpec((k, cn), lambda i, j: (0, j))
    out = pl.BlockSpec((rows.rb, cn), lambda i, j: (i, j))
    kern = functools.partial(_rwkv_prep_kernel, dt=rows.dt, prompt_blocks=rows.prompt_blocks,
                             blocks_per_seq=rows.blocks_per_seq,
                             w_lora=w2.shape[0], a_lora=a2.shape[0], g_lora=g_lora, hd=hd)
    shift3 = s_shift[:, None, :]
    mu2 = mu.reshape(1, -1)
    return pl.pallas_call(
        kern,
        grid=(rows.n_blocks, ncb),
        in_specs=[part(0), part(1), part(2), pl.BlockSpec((rows.rb, tw), lambda i, j: (i, tb)),
                  spart(0), spart(1), spart(2), pl.BlockSpec((rows.slots, 1, tw), lambda i, j: (sblk(i), 0, tb)),
                  mupart(0), mupart(1), mupart(2), pl.BlockSpec((1, tw), lambda i, j: (0, tb)),
                  vec, lora(w2.shape[0]), lora(w2.shape[0]), vec, lora(a2.shape[0]), lora(a2.shape[0]),
                  lora(g2p.shape[0]), lora(g2p.shape[0]), vec, vec, vec],
        out_specs=[out] * 8,
        out_shape=[jax.ShapeDtypeStruct((n, rw), F32)] * 8,
        scratch_shapes=[pltpu.VMEM((3, ncb, 8, cn), F32), pltpu.VMEM((8, tw), F32)],
        compiler_params=_params(("arbitrary", "arbitrary")),
        name="rwkv_prep",
    )(proj, proj, proj, proj, shift3, shift3, shift3, shift3, mu2, mu2, mu2, mu2,
      w0, *_hi_lo(w2), a0, *_hi_lo(a2), *_hi_lo(g2p), k_k, k_a, r_k_flat)


def _rwkv_scan_kernel(r_ref, w_ref, k_ref, v_ref, a_ref, b_ref, *rest, hd, has_state):
    if has_state:
        s0_ref, o_ref, sout_ref, st_ref, ob_ref, vs_ref, row_ref = rest
    else:
        o_ref, sout_ref, st_ref, ob_ref, vs_ref, row_ref = rest
        s0_ref = None
    tb = pl.program_id(1)
    tc, width = r_ref.shape
    pw = 2 * hd
    npairs = width // pw

    @pl.when(tb == 0)
    def _():
        for p in range(npairs):
            if has_state:
                st_ref[p] = jnp.concatenate([s0_ref[0, 2 * p], s0_ref[0, 2 * p + 1]], axis=1)
            else:
                st_ref[p] = jnp.zeros((hd, pw), F32)

    half = npairs // 2
    ones2 = _block_ones(2 * pw, hd, BF16)
    sub = lax.broadcasted_iota(jnp.int32, (hd, pw), 0)
    lane = lax.broadcasted_iota(jnp.int32, (hd, pw), 1)
    diags = [(((lane % hd) - sub + hd) % hd == i).astype(F32).astype(BF16) for i in range(3)]
    lane8 = lax.broadcasted_iota(jnp.int32, (8, pw), 1)
    row16 = lax.broadcasted_iota(jnp.int32, (8, 2 * pw), 0)
    lane16 = lax.broadcasted_iota(jnp.int32, (8, 2 * pw), 1)
    sel2 = row16 == (lane16 // hd)

    def shift_in_head(x, i):
        return jnp.where(lane8 % hd >= i, pltpu.roll(x, i, 1), pltpu.roll(x, pw - hd + i, 1))

    def group(tg, carry):
        rows8 = pl.ds(pl.multiple_of(tg * 8, 8), 8)
        for idx, ref in enumerate((a_ref, w_ref, k_ref, b_ref, r_ref)):
            row_ref[idx] = ref[rows8, :]
        for p in range(npairs):
            cols = pl.ds(p * pw, pw)
            v8 = v_ref[rows8, cols]
            v1 = v8.astype(BF16).astype(F32)
            v2 = (v8 - v1).astype(BF16).astype(F32)
            vs_ref[0, :, cols] = v1
            vs_ref[1, :, cols] = shift_in_head(v2, 1)
            vs_ref[2, :, cols] = shift_in_head((v8 - v1) - v2, 2)

        def row(idx, p, j):
            return row_ref[idx, j:j + 1, pl.ds(p * pw, pw)]

        def side_by_side(tiles):
            return jnp.concatenate([jnp.concatenate([tiles[p], tiles[p + half]], axis=1) for p in range(half)],
                                   axis=0)

        def split(x, p):
            q, c = p % half, p // half
            return x[q * hd:(q + 1) * hd, c * pw:(c + 1) * pw]

        def moved_v(j):
            xs = []
            for p in range(npairs):
                x = None
                for i in range(3):
                    term = diags[i] * vs_ref[i, j:j + 1, pl.ds(p * pw, pw)].astype(BF16)
                    x = term if x is None else x + term
                xs.append(x)
            return _dot(side_by_side(xs), ones2)

        def emit_out(j, states):
            for p in range(half):
                r2 = jnp.concatenate([row(4, p, j), row(4, p + half, j)], axis=1)
                rsel = jnp.where(sel2, r2, 0.0).astype(BF16)
                o8 = _dot_nt(rsel, jnp.concatenate([states[p], states[p + half]], axis=1))
                ob_ref[p, j:j + 1, 0:hd] = o8[0:1, :]
                ob_ref[p, j:j + 1, hd:pw] = o8[1:2, :]
                ob_ref[p + half, j:j + 1, 0:hd] = o8[2:3, :]
                ob_ref[p + half, j:j + 1, hd:pw] = o8[3:4, :]

        vc_next = moved_v(0)
        states = None
        for j in range(8):
            old = [st_ref[p] for p in range(npairs)]
            sa = _dot(side_by_side([old[p] * row(0, p, j) for p in range(npairs)]).astype(BF16), ones2)
            if states is not None:
                emit_out(j - 1, states)
            vc = vc_next
            if j < 7:
                vc_next = moved_v(j + 1)
            states = []
            for p in range(npairs):
                s = old[p] * row(1, p, j) + split(sa, p) * row(3, p, j) + split(vc, p) * row(2, p, j)
                st_ref[p] = s
                states.append(s.astype(BF16))
        emit_out(7, states)
        for p in range(npairs):
            o_ref[rows8, pl.ds(p * pw, pw)] = ob_ref[p]
        return carry

    lax.fori_loop(0, tc // 8, group, 0)

    @pl.when(tb == pl.num_programs(1) - 1)
    def _():
        for p in range(npairs):
            s = st_ref[p]
            sout_ref[0, 2 * p] = s[:, :hd]
            sout_ref[0, 2 * p + 1] = s[:, hd:]


def _rwkv_scan(ops, nseq, t, row0, tc, heads, hd, state):
    width = heads * hd
    assert t % tc == 0 and row0 % tc == 0
    nt = t // tc
    blk = pl.BlockSpec((tc, width), lambda s, j: (row0 // tc + s * nt + j, 0))
    st = pl.BlockSpec((1, heads, hd, hd), lambda s, j: (s, 0, 0, 0))
    in_specs = [blk] * 6
    args = list(ops)
    if state is not None:
        in_specs.append(st)
        args.append(state)
    return pl.pallas_call(
        functools.partial(_rwkv_scan_kernel, hd=hd, has_state=state is not None),
        grid=(nseq, nt),
        in_specs=in_specs,
        out_specs=[pl.BlockSpec((tc, width), lambda s, j: (s * nt + j, 0)), st],
        out_shape=[jax.ShapeDtypeStruct((nseq * t, width), F32),
                   jax.ShapeDtypeStruct((nseq, heads, hd, hd), F32)],
        scratch_shapes=[pltpu.VMEM((heads // 2, hd, 2 * hd), F32), pltpu.VMEM((heads // 2, 8, 2 * hd), F32),
                        pltpu.VMEM((3, 8, width), F32), pltpu.VMEM((5, 8, width), F32)],
        compiler_params=_params(("parallel", "arbitrary")),
        name="rwkv_scan_state" if state is not None else "rwkv_scan",
    )(*args)


def _rwkv_lanes_kernel(r_ref, w_ref, k_ref, v_ref, a_ref, b_ref, s0_ref, o_ref, sout_ref, x_ref, ot_ref, *, dt, hd):
    nseq = s0_ref.shape[-1]
    nh = s0_ref.shape[0]
    for idx, ref in enumerate((a_ref, w_ref, k_ref, b_ref, r_ref, v_ref)):
        for t in range(dt):
            x_ref[idx, t] = ref[pl.ds(t, nseq, stride=dt), :].T

    for hh in range(nh):
        ch = slice(hh * hd, (hh + 1) * hd)

        def vgroup(vg, carry):
            v0 = pl.multiple_of(hh * hd + vg * 8, 8)
            v8 = [x_ref[5, t, pl.ds(v0, 8), :] for t in range(dt)]
            outs = [[None] * 8 for _ in range(dt)]
            for j in range(8):
                s = s0_ref[hh, vg * 8 + j]
                for t in range(dt):
                    sa = jnp.sum(s * x_ref[0, t, ch, :], axis=0, keepdims=True)
                    s = s * x_ref[1, t, ch, :] + sa * x_ref[3, t, ch, :] + v8[t][j:j + 1] * x_ref[2, t, ch, :]
                    outs[t][j] = jnp.sum(s * x_ref[4, t, ch, :], axis=0, keepdims=True)
                sout_ref[hh, vg * 8 + j] = s
            for t in range(dt):
                ot_ref[t, pl.ds(v0, 8), :] = jnp.concatenate(outs[t], axis=0)
            return carry

        lax.fori_loop(0, hd // 8, vgroup, 0)

    for t in range(dt):
        o_ref[pl.ds(t, nseq, stride=dt), :] = ot_ref[t].T


def _rwkv_scan_lanes(ops, nseq, dt, row0, heads, hd, state_lanes):
    width = heads * hd
    rows = nseq * dt
    nh = LANES // hd
    assert row0 % rows == 0 and heads % nh == 0
    blk = pl.BlockSpec((rows, nh * hd), lambda g: (row0 // rows, g))
    st = pl.BlockSpec((nh, hd, hd, nseq), lambda g: (g, 0, 0, 0))
    return pl.pallas_call(
        functools.partial(_rwkv_lanes_kernel, dt=dt, hd=hd),
        grid=(heads // nh,),
        in_specs=[blk] * 6 + [st],
        out_specs=[pl.BlockSpec((rows, nh * hd), lambda g: (0, g)), st],
        out_shape=[jax.ShapeDtypeStruct((rows, width), F32),
                   jax.ShapeDtypeStruct((heads, hd, hd, nseq), F32)],
        scratch_shapes=[pltpu.VMEM((6, dt, nh * hd, nseq), F32), pltpu.VMEM((dt, nh * hd, nseq), F32)],
        compiler_params=_params(("parallel",)),
        name="rwkv_scan_lanes",
    )(*ops, state_lanes)


def _rwkv_post_kernel(op_ref, os_ref, bonus_ref, g_ref, lg_ref, lb_ref, out_ref, *, hd, first_blocks):
    cn = out_ref.shape[1]
    seg = _block_ones(LANES, hd, BF16)

    def segmean(x):
        return jnp.concatenate([_dot_exact01(x[:, c:c + LANES], seg) for c in range(0, cn, LANES)],
                               axis=1) * (1.0 / hd)

    def body(o):
        mean = segmean(o)
        cen = o - mean
        var = segmean(cen * cen)
        y = cen * lax.rsqrt(var + LNX_EPS) * lg_ref[...] + lb_ref[...]
        out_ref[...] = ((y + bonus_ref[...]) * g_ref[...]).astype(out_ref.dtype)

    _from_either(pl.program_id(0), first_blocks, op_ref, os_ref, body)


def _rwkv_post(o_raw_p, o_raw_s, bonus, g, lnx_g, lnx_b, hd, rb=256, cn=512):
    n, rw = bonus.shape
    rb, cn = _pick(math.gcd(o_raw_p.shape[0], o_raw_s.shape[0]), rb), _pick(rw, cn)
    first_blocks = o_raw_p.shape[0] // rb
    blk = pl.BlockSpec((rb, cn), lambda i, j: (i, j))
    vec = pl.BlockSpec((1, cn), lambda i, j: (0, j))
    pair = [pl.BlockSpec((rb, cn), lambda i, j: (jnp.minimum(i, first_blocks - 1), j)),
            pl.BlockSpec((rb, cn), lambda i, j: (jnp.maximum(i - first_blocks, 0), j))]
    return pl.pallas_call(
        functools.partial(_rwkv_post_kernel, hd=hd, first_blocks=first_blocks),
        grid=(n // rb, rw // cn),
        in_specs=pair + [blk, blk, vec, vec],
        out_specs=blk,
        out_shape=jax.ShapeDtypeStruct((n, rw), BF16),
        compiler_params=_params(("parallel", "parallel")),
        name="rwkv_post",
    )(o_raw_p, o_raw_s, bonus, g, lnx_g, lnx_b)


def _mmt_kernel(wt_ref, x_ref, o_ref):
    o_ref[...] = _dot(wt_ref[...].astype(BF16), x_ref[...])


def _matmul_t(wt, xt, tm=512, tn=1024, name="matmul_t"):
    m, k = wt.shape
    n = xt.shape[1]
    tm, tn = _pick(m, tm), _pick(n, tn)
    return pl.pallas_call(
        _mmt_kernel,
        grid=(n // tn, m // tm),
        in_specs=[pl.BlockSpec((tm, k), lambda j, i: (i, 0)),
                  pl.BlockSpec((k, tn), lambda j, i: (0, j))],
        out_specs=pl.BlockSpec((tm, tn), lambda j, i: (i, j)),
        out_shape=jax.ShapeDtypeStruct((m, n), F32),
        compiler_params=_params(("parallel", "parallel")),
        name=name,
    )(wt, xt)


def _take_top(s, k):
    n = s.shape[0]
    row = lax.broadcasted_iota(jnp.int32, s.shape, 0).astype(F32)
    out = []
    for _ in range(k):
        m = jnp.max(s, axis=0, keepdims=True)
        first = jnp.min(jnp.where(s == m, row, float(n)), axis=0, keepdims=True)
        s = jnp.where(row == first, -jnp.inf, s)
        out.append(m)
    return out


def _peer_topk_kernel(q_ref, keys_ref, s0_ref, s1_ref, e0_ref, e1_ref, thr_ref, *, topk):
    kd = keys_ref.shape[3]
    cols = q_ref.shape[1]
    s0 = _dot(keys_ref[0, 0], q_ref[:kd, :], HI)
    s1 = _dot(keys_ref[0, 1], q_ref[kd:, :], HI)
    top0 = _take_top(s0, topk)
    top1 = _take_top(s1, topk)
    width = [topk // (a + 1) for a in range(topk)]
    start = [sum(width[:a]) for a in range(topk)]
    n_cand = sum(width)
    rows = -(-n_cand // 8) * 8
    r = lax.broadcasted_iota(jnp.int32, (rows, cols), 0)
    t0 = jnp.broadcast_to(top0[0], (rows, cols))
    first = jnp.zeros((rows, cols), jnp.int32)
    for a in range(1, topk):
        t0 = jnp.where(r >= start[a], top0[a], t0)
        first = jnp.where(r >= start[a], start[a], first)
    b_of_row = r - first
    t1 = jnp.broadcast_to(top1[0], (rows, cols))
    for b in range(1, topk):
        t1 = jnp.where(b_of_row == b, top1[b], t1)
    cand = jnp.where(r < n_cand, t0 + t1, -jnp.inf)
    best = _take_top(cand, topk)
    z = jnp.zeros((1, cols), F32)
    for b in best:
        z = z + jnp.exp(b - best[0])
    s0_ref[...] = s0
    s1_ref[...] = s1
    e0_ref[...] = jnp.exp(s0 - top0[0]) / z
    e1_ref[...] = jnp.exp(s1 - top1[0])
    thr_ref[0] = jnp.broadcast_to(best[topk - 1], thr_ref.shape[1:])


def _peer_topk(qt, sub_keys, tc=256):
    n = qt.shape[1]
    tc = _pick(n, tc)
    heads, _, nk, kd = sub_keys.shape
    blk = pl.BlockSpec((nk, tc), lambda i, h: (h, i))
    shp = jax.ShapeDtypeStruct((heads * nk, n), F32)
    return pl.pallas_call(
        functools.partial(_peer_topk_kernel, topk=PEER_TOPK),
        grid=(n // tc, heads),
        in_specs=[pl.BlockSpec((2 * kd, tc), lambda i, h: (h, i)),
                  pl.BlockSpec((1, 2, nk, kd), lambda i, h: (h, 0, 0, 0))],
        out_specs=[blk] * 4 + [pl.BlockSpec((1, 8, tc), lambda i, h: (h, 0, i))],
        out_shape=[shp] * 4 + [jax.ShapeDtypeStruct((heads, 8, n), F32)],
        compiler_params=_params(("parallel", "parallel")),
        name="peer_topk",
    )(qt, sub_keys)


def _gelu(x):
    return 0.5 * x * (1.0 + lax.erf(x * (2.0 ** -0.5)))


def _peer_dense_kernel(u_ref, ht_ref, vt_ref, s0_ref, e0_ref, s1_ref, e1_ref, thr_ref, o_ref, act_ref, coef_ref,
                       *, heads, nk, jc):
    e = pl.program_id(1)
    eb = u_ref.shape[0]
    d = vt_ref.shape[0]
    ni = eb // nk
    dq = d // ni

    @pl.when(e == 0)
    def _():
        o_ref[...] = jnp.zeros_like(o_ref)
        coef_ref[1] = jnp.zeros(coef_ref.shape[1:], coef_ref.dtype)

    act_ref[...] = _dot(u_ref[...], ht_ref[...])
    cur = e % 2
    prev = (e + 1) % 2
    thr = thr_ref[...]

    def one_i(ii, carry):
        orow = pl.ds(pl.multiple_of(ii * dq, dq), dq)
        o_ref[orow, :] += _dot(vt_ref[orow, :], coef_ref[prev])
        s0 = s0_ref[0, ii]
        e0 = e0_ref[0, ii]
        for j0 in range(0, nk, jc):
            g = None
            for h in range(heads):
                jrows = slice(h * nk + j0, h * nk + j0 + jc)
                hit = (s0[h:h + 1] + s1_ref[jrows, :]) >= thr[h:h + 1]
                term = jnp.where(hit, e0[h:h + 1] * e1_ref[jrows, :], 0.0)
                g = term if g is None else g + term
            arow = pl.ds(pl.multiple_of(ii * nk + j0, jc), jc)
            coef_ref[cur, arow, :] = (g * _gelu(act_ref[arow, :])).astype(coef_ref.dtype)
        return carry

    lax.fori_loop(0, ni, one_i, 0)


def _peer_dense(ht, u, vt, s0r, e0r, s1, e1, thr, heads, nk, tp=512, eb=512):
    d, n = ht.shape
    tp = _pick(n, tp)
    nblk = u.shape[0] // eb
    ni = eb // nk
    once = pl.Buffered(1)
    tok = pl.BlockSpec((heads * nk, tp), lambda i, e: (0, i), pipeline_mode=once)
    sl = pl.BlockSpec((1, ni, heads, tp), lambda i, e: (jnp.minimum(e, nblk - 1), 0, 0, i))
    return pl.pallas_call(
        functools.partial(_peer_dense_kernel, heads=heads, nk=nk, jc=nk // 2),
        grid=(n // tp, nblk + 1),
        in_specs=[pl.BlockSpec((eb, d), lambda i, e: (jnp.minimum(e, nblk - 1), 0)),
                  pl.BlockSpec((d, tp), lambda i, e: (0, i), pipeline_mode=once),
                  pl.BlockSpec((d, eb), lambda i, e: (0, jnp.maximum(e - 1, 0))),
                  sl, sl, tok, tok, pl.BlockSpec((heads, tp), lambda i, e: (0, i))],
        out_specs=pl.BlockSpec((d, tp), lambda i, e: (0, i)),
        out_shape=jax.ShapeDtypeStruct((d, n), F32),
        scratch_shapes=[pltpu.VMEM((eb, tp), F32), pltpu.VMEM((2, eb, tp), BF16)],
        compiler_params=_params(("parallel", "arbitrary")),
        name="peer_dense",
    )(u, ht, vt, s0r, e0r, s1, e1, thr)


def _by_expert_block(x, heads, nk, ni):
    n = x.shape[1]
    return x.reshape(heads, nk // ni, ni, n).transpose(1, 2, 0, 3)


def _layer(xp, xs, s_ret, s_rwkv, s_shift, cp, cs, p):
    nb, t, d = xp.shape
    db, dt, _ = xs.shape
    rows = _Rows(nb, t, db, dt)
    ret_heads, ret_hd = s_ret.shape[1], s_ret.shape[2]
    rw_heads, rw_hd = s_rwkv.shape[1], s_rwkv.shape[2]
    ret_w, rw = ret_heads * ret_hd, rw_heads * rw_hd
    w_lora, a_lora, g_lora = p['w2'].shape[0], p['a2'].shape[0], p['g2'].shape[0]
    rwkv_in = s_shift.shape[-1]
    ret_in = 4 * ret_w
    assert p['w_in'].shape[1] == ret_in + rwkv_in and rwkv_in == 3 * rw + w_lora + a_lora + g_lora

    xp, xs = xp.reshape(nb * t, d), xs.reshape(db * dt, d)
    mod = _ada(jnp.concatenate([cp, cs], axis=0), p['ada_w'], p['ada_b'])
    mod_ext = rows.extend(mod)

    h = _prenorm(rows, xp, xs, p['pre_mix_g'], mod_ext, d)

    tm = _pick(rows.n, 1024)
    w_in_t = p['w_in'].T
    proj_ret = _matmul(h, w_in_t, col_block0=0, n_out=ret_in, tm=tm, tn=512, name="in_proj_ret")
    proj_rw = _matmul(h, w_in_t, col_block0=ret_in // 512, n_out=rwkv_in, tm=tm, tn=512, name="in_proj_rwkv")

    o_ret_p, o_ret_s, sr_p, sr_s = _retention(rows, proj_ret, s_ret, ret_heads, ret_hd)

    tail_w = -(-(w_lora + a_lora + g_lora) // LANES) * LANES
    g2p = jnp.pad(p['g2'], ((0, tail_w - w_lora - a_lora - g_lora), (0, 0)))
    vec = lambda a: a.reshape(1, rw)
    r_, w_, k_, v_, a_, b_, g_, bonus = _rwkv_prep(
        rows, proj_rw, s_shift, p['shift_mu'], vec(p['w0']), p['w2'], vec(p['a0']), p['a2'], g2p,
        vec(p['k_k']), vec(p['k_a']), vec(p['r_k']), rw, rw_hd, g_lora)
    ops = (r_, w_, k_, v_, a_, b_)
    o_rw_p, sw_p = _rwkv_scan(ops, nb, t, 0, 256, rw_heads, rw_hd, None)
    o_rw_s, sw_lanes = _rwkv_scan_lanes(ops, db, dt, nb * t, rw_heads, rw_hd, jnp.transpose(s_rwkv, (1, 2, 3, 0)))
    sw_s = jnp.transpose(sw_lanes, (3, 0, 1, 2))
    o_rw = _rwkv_post(o_rw_p, o_rw_s, bonus, g_, vec(p['lnx_g']), vec(p['lnx_b']), rw_hd)

    mix = _matmul_out(o_ret_p, o_ret_s, o_rw, p['w_out'])
    x1, h2t = _postmix(rows, xp, xs, mix, p['post_mix_g'], p['pre_ffn_g'], mod_ext, d)

    heads, _, nk, kd = p['peer_sub_keys'].shape
    qt = _matmul_t(p['peer_wq'].T, h2t, name="peer_q")
    s0, s1, e0, e1, thr = _peer_topk(qt, p['peer_sub_keys'])
    eb = 512
    ni = eb // nk
    peer_t = _peer_dense(h2t, p['peer_u'].astype(BF16), p['peer_v'].T.astype(BF16),
                         _by_expert_block(s0, heads, nk, ni), _by_expert_block(e0, heads, nk, ni),
                         s1, e1, thr[:, 0, :], heads, nk, eb=eb)
    yp = _final(rows, x1, peer_t, p['post_ffn_g'], mod_ext, d, 0, rows.prompt_blocks)
    ys = _final(rows, x1, peer_t, p['post_ffn_g'], mod_ext, d, rows.prompt_blocks, rows.n_blocks - rows.prompt_blocks)

    n_p = nb * t
    last_p = slice(t - 1, n_p, t)
    last_s = slice(n_p + dt - 1, None, dt)
    return (yp.reshape(nb, t, d), ys.reshape(db, dt, d), sr_p, sr_s, sw_p, sw_s, proj_rw[last_p], proj_rw[last_s])


def kernel(x_prompt, x_sample, state_ret, state_rwkv, state_shift, c_prompt, c_sample, ada_w, ada_b, pre_mix_g, post_mix_g, pre_ffn_g, post_ffn_g, w_in, shift_mu, w0, w2, a0, a2, g2, k_k, k_a, r_k, lnx_g, lnx_b, w_out, peer_wq, peer_sub_keys, peer_u, peer_v):
    depth = ada_w.shape[0]
    assert depth == 1, "prompt and sample tokens are stacked per layer; deeper stacks need per-layer restacking"
    prm = dict(ada_w=ada_w[0], ada_b=ada_b[0], pre_mix_g=pre_mix_g[0], post_mix_g=post_mix_g[0],
               pre_ffn_g=pre_ffn_g[0], post_ffn_g=post_ffn_g[0], w_in=w_in[0], shift_mu=shift_mu[0],
               w0=w0[0], w2=w2[0], a0=a0[0], a2=a2[0], g2=g2[0], k_k=k_k[0], k_a=k_a[0], r_k=r_k[0],
               lnx_g=lnx_g[0], lnx_b=lnx_b[0], w_out=w_out[0], peer_wq=peer_wq[0],
               peer_sub_keys=peer_sub_keys[0], peer_u=peer_u[0], peer_v=peer_v[0])
    yp, ys, sr_p, sr_s, sw_p, sw_s, ss_p, ss_s = _layer(
        x_prompt, x_sample, state_ret[0], state_rwkv[0], state_shift[0], c_prompt, c_sample, prm)
    sd, wd, hd = state_ret.dtype, state_rwkv.dtype, state_shift.dtype
    return (yp, ys, sr_p[None].astype(sd), sr_s[None].astype(sd), sw_p[None].astype(wd), sw_s[None].astype(wd),
            ss_p[None].astype(hd), ss_s[None].astype(hd))
```

```python
import functools
import math

import numpy as np
import jax
import jax.numpy as jnp
from jax import lax
from jax.experimental import pallas as pl
from jax.experimental.pallas import tpu as pltpu

PAST_LEN = 16384
RET_CHUNK = 128
ROPE_BASE = 10000.0
LNX_EPS = 64e-5
NORM_EPS = 1e-6
PEER_TOPK = 16
PEER_MASK_ROWS = 16
RET_HEADS_PER_STEP = 4

LANES = 128
ROW_BLOCK = 128
VMEM_LIMIT = 56 * 1024 * 1024

HI = lax.Precision.HIGHEST
F32 = jnp.float32
BF16 = jnp.bfloat16


def _pick(n, target, mult=LANES):
    best = None
    for c in range(mult, min(n, target) + 1, mult):
        if n % c == 0:
            best = c
    assert best is not None, (n, target, mult)
    return best


def _params(sem, vmem=VMEM_LIMIT):
    return pltpu.CompilerParams(dimension_semantics=sem, vmem_limit_bytes=vmem)


def _dot(a, b, precision=None):
    return jnp.dot(a, b, preferred_element_type=F32, precision=precision)


def _dot_nt(a, b, precision=None):
    return lax.dot_general(a, b, (((1,), (1,)), ((), ())), preferred_element_type=F32, precision=precision)


def _dot_tn(a, b, precision=None):
    return lax.dot_general(a, b, (((0,), (0,)), ((), ())), preferred_element_type=F32, precision=precision)


def _split_bf16(x, terms):
    out = []
    for _ in range(terms - 1):
        t = x.astype(BF16)
        out.append(t)
        x = x - t.astype(F32)
    return out + [x.astype(BF16)]


def _dot_split(x, w_terms):
    xh, xl = _split_bf16(x, 2)
    wh, wl = w_terms
    return _dot(xh, wh) + (_dot(xl, wh) + _dot(xh, wl))


def _dot_exact01(x, ones):
    return sum(_dot(t, ones) for t in _split_bf16(x, 3))


def _block_ones(n, seg, dtype):
    r = lax.broadcasted_iota(jnp.int32, (n, n), 0) // seg
    c = lax.broadcasted_iota(jnp.int32, (n, n), 1) // seg
    return (r == c).astype(dtype)


def _ada_kernel(c_ref, w_ref, b_ref, o_ref):
    c = c_ref[...]
    sc = (c * jax.nn.sigmoid(c)).astype(BF16)
    o_ref[...] = _dot(sc, w_ref[...].astype(BF16)) + b_ref[...]


def _ada(c_all, ada_w, ada_b, tn=512):
    m, d = c_all.shape
    n = ada_w.shape[1]
    return pl.pallas_call(
        _ada_kernel,
        grid=(n // tn,),
        in_specs=[pl.BlockSpec((m, d), lambda j: (0, 0)),
                  pl.BlockSpec((d, tn), lambda j: (0, j)),
                  pl.BlockSpec((1, tn), lambda j: (0, j))],
        out_specs=pl.BlockSpec((m, tn), lambda j: (0, j)),
        out_shape=jax.ShapeDtypeStruct((m, n), F32),
        compiler_params=_params(("arbitrary",)),
        name="ada",
    )(c_all, ada_w, ada_b.reshape(1, n))


def _expand_slots(m, rows):
    slots, _, d = m.shape
    return jnp.broadcast_to(m, (slots, rows // slots, d)).reshape(rows, d)


def _rms(x, g):
    return x * lax.rsqrt(jnp.mean(x * x, axis=-1, keepdims=True) + NORM_EPS) * g


def _from_either(i, first_blocks, a_ref, b_ref, body):
    @pl.when(i < first_blocks)
    def _():
        body(a_ref[...])

    @pl.when(i >= first_blocks)
    def _():
        body(b_ref[...])


def _two_specs(block, first_blocks):
    return [pl.BlockSpec(block, lambda i, *_: (jnp.minimum(i, first_blocks - 1), 0)),
            pl.BlockSpec(block, lambda i, *_: (jnp.maximum(i - first_blocks, 0), 0))]


def _prenorm_kernel(xp_ref, xs_ref, g_ref, sh_ref, sc_ref, h_ref, *, prompt_blocks):
    def body(x):
        rows = x.shape[0]
        y = _rms(x, g_ref[...])
        h = y * (1.0 + _expand_slots(sc_ref[...], rows)) + _expand_slots(sh_ref[...], rows)
        h_ref[...] = h.astype(h_ref.dtype)

    _from_either(pl.program_id(0), prompt_blocks, xp_ref, xs_ref, body)


def _postmix_kernel(xp_ref, xs_ref, m_ref, gpost_ref, gpre_ref, gate_ref, sh_ref, sc_ref, x1_ref, ht_ref,
                    *, prompt_blocks):
    def body(x):
        rows = x.shape[0]
        x1 = x + _expand_slots(gate_ref[...], rows) * _rms(m_ref[...], gpost_ref[...])
        x1_ref[...] = x1
        y = _rms(x1, gpre_ref[...])
        h = y * (1.0 + _expand_slots(sc_ref[...], rows)) + _expand_slots(sh_ref[...], rows)
        ht_ref[...] = h.T.astype(ht_ref.dtype)

    _from_either(pl.program_id(0), prompt_blocks, xp_ref, xs_ref, body)


def _final_kernel(x_ref, mt_ref, gpost_ref, gate_ref, y_ref):
    x = x_ref[...]
    rows = x.shape[0]
    y_ref[...] = x + _expand_slots(gate_ref[...], rows) * _rms(mt_ref[...].T, gpost_ref[...])


class _Rows:
    def __init__(self, nb, t, db, dt):
        self.n_prompt = nb * t
        self.n_sample = db * dt
        self.n = self.n_prompt + self.n_sample
        self.rb = ROW_BLOCK
        assert t % self.rb == 0 and self.rb % dt == 0 and self.n_sample % self.rb == 0
        self.slots = self.rb // dt
        self.prompt_blocks = self.n_prompt // self.rb
        self.blocks_per_seq = t // self.rb
        self.n_blocks = self.n // self.rb
        self.nb, self.t, self.db, self.dt = nb, t, db, dt

    def slot_block(self, i):
        return jnp.where(i < self.prompt_blocks, i // self.blocks_per_seq, i - self.prompt_blocks + self.nb)

    def extend(self, m):
        mp = jnp.repeat(m[: self.nb], self.slots, axis=0)
        return jnp.concatenate([mp, m[self.nb:]], axis=0)[:, None, :]


def _mod_spec(rows, d, chunk, block0=0):
    return pl.BlockSpec((rows.slots, 1, d), lambda i: (rows.slot_block(i + block0), 0, chunk))


def _prenorm(rows, xp, xs, g, mod_ext, d):
    row = pl.BlockSpec((rows.rb, d), lambda i: (i, 0))
    vec = pl.BlockSpec((1, d), lambda i: (0, 0))
    return pl.pallas_call(
        functools.partial(_prenorm_kernel, prompt_blocks=rows.prompt_blocks),
        grid=(rows.n_blocks,),
        in_specs=_two_specs((rows.rb, d), rows.prompt_blocks) + [vec, _mod_spec(rows, d, 0), _mod_spec(rows, d, 1)],
        out_specs=row,
        out_shape=jax.ShapeDtypeStruct((rows.n, d), BF16),
        compiler_params=_params(("parallel",)),
        name="prenorm",
    )(xp, xs, g.reshape(1, d), mod_ext, mod_ext)


def _postmix(rows, xp, xs, mix, gpost, gpre, mod_ext, d):
    row = pl.BlockSpec((rows.rb, d), lambda i: (i, 0))
    vec = pl.BlockSpec((1, d), lambda i: (0, 0))
    return pl.pallas_call(
        functools.partial(_postmix_kernel, prompt_blocks=rows.prompt_blocks),
        grid=(rows.n_blocks,),
        in_specs=_two_specs((rows.rb, d), rows.prompt_blocks)
        + [row, vec, vec, _mod_spec(rows, d, 2), _mod_spec(rows, d, 3), _mod_spec(rows, d, 4)],
        out_specs=[row, pl.BlockSpec((d, rows.rb), lambda i: (0, i))],
        out_shape=[jax.ShapeDtypeStruct((rows.n, d), F32), jax.ShapeDtypeStruct((d, rows.n), BF16)],
        compiler_params=_params(("parallel",)),
        name="postmix",
    )(xp, xs, mix, gpost.reshape(1, d), gpre.reshape(1, d), mod_ext, mod_ext, mod_ext)


def _final(rows, x1, peer_t, gpost, mod_ext, d, block0, n_blocks):
    vec = pl.BlockSpec((1, d), lambda i: (0, 0))
    return pl.pallas_call(
        _final_kernel,
        grid=(n_blocks,),
        in_specs=[pl.BlockSpec((rows.rb, d), lambda i: (i + block0, 0)),
                  pl.BlockSpec((d, rows.rb), lambda i: (0, i + block0)),
                  vec, _mod_spec(rows, d, 5, block0)],
        out_specs=pl.BlockSpec((rows.rb, d), lambda i: (i, 0)),
        out_shape=jax.ShapeDtypeStruct((n_blocks * rows.rb, d), F32),
        compiler_params=_params(("parallel",)),
        name="final",
    )(x1, peer_t, gpost.reshape(1, d), mod_ext)


def _mm_kernel(a_ref, wt_ref, o_ref, wbf_ref):
    @pl.when(pl.program_id(1) == 0)
    def _():
        wbf_ref[...] = wt_ref[...].astype(BF16)

    o_ref[...] = _dot_nt(a_ref[...], wbf_ref[...]).astype(o_ref.dtype)


def _matmul(a, wt, *, col_block0, n_out, tm, tn, name):
    m, k = a.shape
    return pl.pallas_call(
        _mm_kernel,
        grid=(pl.cdiv(n_out, tn), m // tm),
        in_specs=[pl.BlockSpec((tm, k), lambda j, i: (i, 0)),
                  pl.BlockSpec((tn, k), lambda j, i: (j + col_block0, 0))],
        out_specs=pl.BlockSpec((tm, tn), lambda j, i: (i, j)),
        out_shape=jax.ShapeDtypeStruct((m, n_out), F32),
        scratch_shapes=[pltpu.VMEM((tn, k), BF16)],
        compiler_params=_params(("parallel", "arbitrary")),
        name=name,
    )(a, wt)


def _mm2_kernel(ap_ref, as_ref, b_ref, wa_ref, wb_ref, o_ref, *, first_blocks):
    def body(a):
        o_ref[...] = _dot(a, wa_ref[...].astype(BF16)) + _dot(b_ref[...], wb_ref[...].astype(BF16))

    _from_either(pl.program_id(1), first_blocks, ap_ref, as_ref, body)


def _matmul_out(o_ret_p, o_ret_s, o_rw, w_out, tn=512):
    m, kb = o_rw.shape
    ka = o_ret_p.shape[1]
    tm = _pick(math.gcd(o_ret_p.shape[0], o_ret_s.shape[0]), 1024)
    first_blocks = o_ret_p.shape[0] // tm
    n = w_out.shape[1]
    tn = _pick(n, tn)
    assert ka == kb
    pair = [pl.BlockSpec((tm, ka), lambda j, i: (jnp.minimum(i, first_blocks - 1), 0)),
            pl.BlockSpec((tm, ka), lambda j, i: (jnp.maximum(i - first_blocks, 0), 0))]
    return pl.pallas_call(
        functools.partial(_mm2_kernel, first_blocks=first_blocks),
        grid=(n // tn, m // tm),
        in_specs=pair + [pl.BlockSpec((tm, kb), lambda j, i: (i, 0)),
                         pl.BlockSpec((ka, tn), lambda j, i: (0, j)),
                         pl.BlockSpec((kb, tn), lambda j, i: (1, j))],
        out_specs=pl.BlockSpec((tm, tn), lambda j, i: (i, j)),
        out_shape=jax.ShapeDtypeStruct((m, n), F32),
        compiler_params=_params(("parallel", "parallel")),
        name="out_proj",
    )(o_ret_p, o_ret_s, o_rw, w_out, w_out)


def _rope_tables(pos, hd):
    half = hd // 2
    inv = ROPE_BASE ** (-jnp.arange(half, dtype=F32) / half)
    ang = pos.astype(F32)[:, None] * inv[None, :]
    cos, sin = jnp.cos(ang), jnp.sin(ang)
    return jnp.concatenate([cos, cos], axis=-1), jnp.concatenate([-sin, sin], axis=-1)


def _rope(x, cos, sin_signed):
    half = x.shape[-1] // 2
    return x * cos + pltpu.roll(x, half, 1) * sin_signed


def _ret_finish(o, gate):
    o = o * lax.rsqrt(jnp.mean(o * o, axis=-1, keepdims=True) + NORM_EPS)
    return o * (gate * jax.nn.sigmoid(gate))


def _ret_prompt_kernel(q_ref, k_ref, v_ref, g_ref, cos_ref, sin_ref, lg_ref, o_ref, s_ref, *, chunk, hd):
    t = q_ref.shape[0]
    nh = q_ref.shape[1] // hd
    ri = lax.broadcasted_iota(jnp.int32, (chunk, chunk), 0)
    ci = lax.broadcasted_iota(jnp.int32, (chunk, chunk), 1)
    causal = ri >= ci
    diff = jnp.where(causal, ri - ci, 0).astype(F32)
    rowf = lax.broadcasted_iota(jnp.int32, (chunk, hd), 0).astype(F32)
    lgs = [lg_ref[i, 0:1, :] for i in range(nh)]
    masks = [jnp.where(causal, jnp.exp(diff * lg), 0.0) for lg in lgs]
    q_decs = [jnp.exp((rowf + 1.0) * lg) for lg in lgs]
    k_decs = [jnp.exp((chunk - 1.0 - rowf) * lg) for lg in lgs]
    c_decs = [jnp.exp(float(chunk) * lg) for lg in lgs]

    def step(c, states):
        r0 = pl.multiple_of(c * chunk, chunk)
        rows = pl.ds(r0, chunk)
        cos, sin = cos_ref[rows, :], sin_ref[rows, :]
        cols = [slice(i * hd, (i + 1) * hd) for i in range(nh)]
        qs = [_rope(q_ref[rows, cl], cos, sin) for cl in cols]
        ks = [_rope(k_ref[rows, cl], cos, sin) * (hd ** -0.5) for cl in cols]
        vs = [v_ref[rows, cl] for cl in cols]
        atts = [_dot_nt(q.astype(BF16), k.astype(BF16)) * m for q, k, m in zip(qs, ks, masks)]
        cross = [_dot((q * qd).astype(BF16), s.astype(BF16)) for q, qd, s in zip(qs, q_decs, states)]
        inner = [_dot(a.astype(BF16), v.astype(BF16)) for a, v in zip(atts, vs)]
        new = [s * cd + _dot_tn(k * kd, v, HI) for s, cd, k, kd, v in zip(states, c_decs, ks, k_decs, vs)]
        for i, cl in enumerate(cols):
            o_ref[rows, cl] = _ret_finish(inner[i] + cross[i], g_ref[rows, cl]).astype(o_ref.dtype)
        return tuple(new)

    final = lax.fori_loop(0, t // chunk, step, tuple(jnp.zeros((hd, hd), F32) for _ in range(nh)))
    for i in range(nh):
        s_ref[0, i] = final[i]


def _ret_sample_kernel(q_ref, k_ref, v_ref, g_ref, cos_ref, sin_ref, lg_ref, s0_ref, o_ref, s_ref, *, dt):
    rows, hd = q_ref.shape
    nseq = rows // dt
    lg = lg_ref[0, 0:1, :]
    ri = lax.broadcasted_iota(jnp.int32, (rows, rows), 0)
    ci = lax.broadcasted_iota(jnp.int32, (rows, rows), 1)
    ok = (ri >= ci) & ((ri // dt) == (ci // dt))
    diff = jnp.where(ok, ri - ci, 0).astype(F32)
    mask = jnp.where(ok, jnp.exp(diff * lg), 0.0)
    rown = lax.broadcasted_iota(jnp.int32, (rows, hd), 0)
    posf = (rown % dt).astype(F32)
    q_dec = jnp.exp((posf + 1.0) * lg)
    k_dec = jnp.exp((dt - 1.0 - posf) * lg)
    c_dec = jnp.exp(float(dt) * lg)
    cos, sin = cos_ref[...], sin_ref[...]
    qc = _rope(q_ref[...], cos, sin)
    kc = _rope(k_ref[...], cos, sin) * (hd ** -0.5)
    vc = v_ref[...]
    att = _dot_nt(qc.astype(BF16), kc.astype(BF16)) * mask
    inner = _dot(att.astype(BF16), vc.astype(BF16))
    qd = qc * q_dec
    kd = kc * k_dec
    seq = rown // dt
    cross = jnp.zeros((rows, hd), F32)
    for s in range(nseq):
        s0 = s0_ref[s, 0]
        mine = seq == s
        cross = cross + _dot(jnp.where(mine, qd, 0.0).astype(BF16), s0.astype(BF16))
        s_ref[s, 0] = s0 * c_dec + _dot_tn(jnp.where(mine, kd, 0.0), vc, HI)
    o_ref[...] = _ret_finish(inner + cross, g_ref[...]).astype(o_ref.dtype)


def _retention(rows, proj, state_ret, ret_heads, hd):
    nb, t, db, dt = rows.nb, rows.t, rows.db, rows.dt
    lg = jnp.log1p(-jnp.exp2(-5.0 - jnp.arange(ret_heads, dtype=F32)))
    lg_tab = jnp.broadcast_to(lg[:, None, None], (ret_heads, 8, hd))
    lg_spec = pl.BlockSpec((1, 8, hd), lambda b, h: (h, 0, 0))
    assert hd == RET_CHUNK and t % RET_CHUNK == 0 and dt % RET_CHUNK != 0

    cos_p, sin_p = _rope_tables(jnp.arange(t, dtype=jnp.int32), hd)
    nh = RET_HEADS_PER_STEP if ret_heads % RET_HEADS_PER_STEP == 0 else 1
    hsteps = ret_heads // nh
    col = lambda part: pl.BlockSpec((t, nh * hd), lambda b, h: (b, h + part * hsteps))
    tab = pl.BlockSpec((t, hd), lambda b, h: (0, 0))
    o_p, s_p = pl.pallas_call(
        functools.partial(_ret_prompt_kernel, chunk=RET_CHUNK, hd=hd),
        grid=(nb, hsteps),
        in_specs=[col(0), col(1), col(2), col(3), tab, tab, pl.BlockSpec((nh, 8, hd), lambda b, h: (h, 0, 0))],
        out_specs=[pl.BlockSpec((t, nh * hd), lambda b, h: (b, h)),
                   pl.BlockSpec((1, nh, hd, hd), lambda b, h: (b, h, 0, 0))],
        out_shape=[jax.ShapeDtypeStruct((nb * t, ret_heads * hd), BF16),
                   jax.ShapeDtypeStruct((nb, ret_heads, hd, hd), F32)],
        compiler_params=_params(("parallel", "parallel")),
        name="ret_prompt",
    )(proj, proj, proj, proj, cos_p, sin_p, lg_tab)

    rb = RET_CHUNK
    nseq = rb // dt
    cos_s, sin_s = _rope_tables(PAST_LEN + jnp.arange(dt, dtype=jnp.int32), hd)
    cos_s, sin_s = jnp.tile(cos_s, (nseq, 1)), jnp.tile(sin_s, (nseq, 1))
    off = (nb * t) // rb
    col = lambda part: pl.BlockSpec((rb, hd), lambda b, h: (b + off, h + part * ret_heads))
    tab = pl.BlockSpec((rb, hd), lambda b, h: (0, 0))
    st = pl.BlockSpec((nseq, 1, hd, hd), lambda b, h: (b, h, 0, 0))
    o_s, s_s = pl.pallas_call(
        functools.partial(_ret_sample_kernel, dt=dt),
        grid=(db // nseq, ret_heads),
        in_specs=[col(0), col(1), col(2), col(3), tab, tab, lg_spec, st],
        out_specs=[pl.BlockSpec((rb, hd), lambda b, h: (b, h)), st],
        out_shape=[jax.ShapeDtypeStruct((db * dt, ret_heads * hd), BF16),
                   jax.ShapeDtypeStruct((db, ret_heads, hd, hd), F32)],
        compiler_params=_params(("parallel", "parallel")),
        name="ret_sample",
    )(proj, proj, proj, proj, cos_s, sin_s, lg_tab, state_ret)
    return o_p, o_s, s_p, s_s


def _shifted(x, first, tseq_mask):
    prev = pltpu.roll(x, 1, 0)
    return jnp.where(tseq_mask, first, prev)


def _rwkv_prep_kernel(pr_ref, pk_ref, pv_ref, pt_ref, sr_ref, sk_ref, sv_ref, st_ref,
                      mur_ref, muk_ref, muv_ref, mut_ref, w0_ref, w2h_ref, w2l_ref, a0_ref, a2h_ref, a2l_ref,
                      g2h_ref, g2l_ref,
                      kk_ref, ka_ref, rk_ref,
                      r_out, w_out, k_out, v_out, a_out, b_out, g_out, bonus_out, carry_ref, tcarry_ref,
                      *, dt, prompt_blocks, blocks_per_seq, w_lora, a_lora, g_lora, hd):
    i, j = pl.program_id(0), pl.program_id(1)
    rows, cn = pr_ref.shape
    rown = lax.broadcasted_iota(jnp.int32, (rows, 1), 0)
    sample_block = jnp.zeros((rows, 1), jnp.int32) + (i >= prompt_blocks).astype(jnp.int32)
    is_first = (rown == 0) | ((rown % dt == 0) & (sample_block == 1))
    seq_start = i % blocks_per_seq == 0

    @pl.when(i == 0)
    def _():
        for part in range(3):
            carry_ref[part, j] = jnp.zeros(carry_ref.shape[2:], F32)
        tcarry_ref[...] = jnp.zeros_like(tcarry_ref)

    def lerp(p_ref, s_ref, mu_ref, carried):
        p = p_ref[...]
        from_prompt = jnp.broadcast_to(jnp.where(seq_start, 0.0, carried), p.shape)
        first = jnp.where(sample_block == 1, _expand_slots(s_ref[...], rows), from_prompt)
        prev = _shifted(p, first, is_first)
        return p + mu_ref[...] * (prev - p)

    r = lerp(pr_ref, sr_ref, mur_ref, carry_ref[0, j, 0:1, :])
    kw = lerp(pk_ref, sk_ref, muk_ref, carry_ref[1, j, 0:1, :])
    vw = lerp(pv_ref, sv_ref, muv_ref, carry_ref[2, j, 0:1, :])
    tail = lerp(pt_ref, st_ref, mut_ref, tcarry_ref[0:1, :])
    for part, p_ref in enumerate((pr_ref, pk_ref, pv_ref)):
        carry_ref[part, j, 0:1, :] = p_ref[rows - 1:rows, :]

    @pl.when(j == pl.num_programs(1) - 1)
    def _():
        tcarry_ref[0:1, :] = pt_ref[rows - 1:rows, :]

    tw = tail.shape[1]
    wl = tail[:, :w_lora]
    al = tail[:, w_lora:w_lora + a_lora]
    glane = lax.broadcasted_iota(jnp.int32, (rows, tw - w_lora - a_lora), 1)
    gl = jnp.where(glane < g_lora, jax.nn.sigmoid(tail[:, w_lora + a_lora:]), 0.0)

    w_log = -jax.nn.softplus(-(w0_ref[...] + _dot_split(jnp.tanh(wl), (w2h_ref[...], w2l_ref[...])))) - 0.5
    decay = jnp.exp(-jnp.exp(w_log))
    a = jax.nn.sigmoid(a0_ref[...] + _dot_split(al, (a2h_ref[...], a2l_ref[...])))
    g = _dot_split(gl, (g2h_ref[...], g2l_ref[...]))

    seg = _block_ones(LANES, hd, BF16)

    def segsum(x):
        return jnp.concatenate([_dot_exact01(x[:, c:c + LANES], seg) for c in range(0, cn, LANES)], axis=1)

    kk = kw * kk_ref[...]
    kk = kk / jnp.maximum(jnp.sqrt(segsum(kk * kk)), 1e-12)
    kn = kw * (1.0 + (a - 1.0) * ka_ref[...])
    r_out[...] = r
    w_out[...] = decay
    k_out[...] = kn
    v_out[...] = vw
    a_out[...] = -kk
    b_out[...] = kk * a
    g_out[...] = g
    bonus_out[...] = segsum(r * kn * rk_ref[...]) * vw


def _hi_lo(w):
    hi = w.astype(BF16)
    return hi, (w - hi.astype(F32)).astype(BF16)


def _rwkv_prep(rows, proj, s_shift, mu, w0, w2, a0, a2, g2p, k_k, k_a, r_k_flat, rw, hd, g_lora, cn=512):
    n = rows.n
    cn = _pick(rw, cn)
    ncb = rw // cn
    tw = g2p.shape[0] + w2.shape[0] + a2.shape[0]
    assert (3 * rw) % tw == 0
    tb = (3 * rw) // tw
    sblk = lambda i: jnp.maximum(i - rows.prompt_blocks, 0)
    part = lambda p: pl.BlockSpec((rows.rb, cn), lambda i, j: (i, j + p * ncb))
    spart = lambda p: pl.BlockSpec((rows.slots, 1, cn), lambda i, j: (sblk(i), 0, j + p * ncb))
    mupart = lambda p: pl.BlockSpec((1, cn), lambda i, j: (0, j + p * ncb))
    vec = pl.BlockSpec((1, cn), lambda i, j: (0, j))
    lora = lambda k: pl.BlockSpec((k, cn), lambda i, j: (0, j))
    out = pl.BlockSpec((rows.rb, cn), lambda i, j: (i, j))
    kern = functools.partial(_rwkv_prep_kernel, dt=rows.dt, prompt_blocks=rows.prompt_blocks,
                             blocks_per_seq=rows.blocks_per_seq,
                             w_lora=w2.shape[0], a_lora=a2.shape[0], g_lora=g_lora, hd=hd)
    shift3 = s_shift[:, None, :]
    mu2 = mu.reshape(1, -1)
    return pl.pallas_call(
        kern,
        grid=(rows.n_blocks, ncb),
        in_specs=[part(0), part(1), part(2), pl.BlockSpec((rows.rb, tw), lambda i, j: (i, tb)),
                  spart(0), spart(1), spart(2), pl.BlockSpec((rows.slots, 1, tw), lambda i, j: (sblk(i), 0, tb)),
                  mupart(0), mupart(1), mupart(2), pl.BlockSpec((1, tw), lambda i, j: (0, tb)),
                  vec, lora(w2.shape[0]), lora(w2.shape[0]), vec, lora(a2.shape[0]), lora(a2.shape[0]),
                  lora(g2p.shape[0]), lora(g2p.shape[0]), vec, vec, vec],
        out_specs=[out] * 8,
        out_shape=[jax.ShapeDtypeStruct((n, rw), F32)] * 8,
        scratch_shapes=[pltpu.VMEM((3, ncb, 8, cn), F32), pltpu.VMEM((8, tw), F32)],
        compiler_params=_params(("arbitrary", "arbitrary")),
        name="rwkv_prep",
    )(proj, proj, proj, proj, shift3, shift3, shift3, shift3, mu2, mu2, mu2, mu2,
      w0, *_hi_lo(w2), a0, *_hi_lo(a2), *_hi_lo(g2p), k_k, k_a, r_k_flat)


def _rwkv_scan_kernel(r_ref, w_ref, k_ref, v_ref, a_ref, b_ref, *rest, hd, has_state):
    if has_state:
        s0_ref, o_ref, sout_ref, st_ref, ob_ref, vs_ref, row_ref = rest
    else:
        o_ref, sout_ref, st_ref, ob_ref, vs_ref, row_ref = rest
        s0_ref = None
    tb = pl.program_id(1)
    tc, width = r_ref.shape
    pw = 2 * hd
    npairs = width // pw

    @pl.when(tb == 0)
    def _():
        for p in range(npairs):
            if has_state:
                st_ref[p] = jnp.concatenate([s0_ref[0, 2 * p], s0_ref[0, 2 * p + 1]], axis=1)
            else:
                st_ref[p] = jnp.zeros((hd, pw), F32)

    half = npairs // 2
    ones2 = _block_ones(2 * pw, hd, BF16)
    sub = lax.broadcasted_iota(jnp.int32, (hd, pw), 0)
    lane = lax.broadcasted_iota(jnp.int32, (hd, pw), 1)
    diags = [(((lane % hd) - sub + hd) % hd == i).astype(F32).astype(BF16) for i in range(3)]
    lane8 = lax.broadcasted_iota(jnp.int32, (8, pw), 1)
    row16 = lax.broadcasted_iota(jnp.int32, (8, 2 * pw), 0)
    lane16 = lax.broadcasted_iota(jnp.int32, (8, 2 * pw), 1)
    sel2 = row16 == (lane16 // hd)

    def shift_in_head(x, i):
        return jnp.where(lane8 % hd >= i, pltpu.roll(x, i, 1), pltpu.roll(x, pw - hd + i, 1))

    def group(tg, carry):
        rows8 = pl.ds(pl.multiple_of(tg * 8, 8), 8)
        for idx, ref in enumerate((a_ref, w_ref, k_ref, b_ref, r_ref)):
            row_ref[idx] = ref[rows8, :]
        for p in range(npairs):
            cols = pl.ds(p * pw, pw)
            v8 = v_ref[rows8, cols]
            v1 = v8.astype(BF16).astype(F32)
            v2 = (v8 - v1).astype(BF16).astype(F32)
            vs_ref[0, :, cols] = v1
            vs_ref[1, :, cols] = shift_in_head(v2, 1)
            vs_ref[2, :, cols] = shift_in_head((v8 - v1) - v2, 2)

        def row(idx, p, j):
            return row_ref[idx, j:j + 1, pl.ds(p * pw, pw)]

        def side_by_side(tiles):
            return jnp.concatenate([jnp.concatenate([tiles[p], tiles[p + half]], axis=1) for p in range(half)],
                                   axis=0)

        def split(x, p):
            q, c = p % half, p // half
            return x[q * hd:(q + 1) * hd, c * pw:(c + 1) * pw]

        def moved_v(j):
            xs = []
            for p in range(npairs):
                x = None
                for i in range(3):
                    term = diags[i] * vs_ref[i, j:j + 1, pl.ds(p * pw, pw)].astype(BF16)
                    x = term if x is None else x + term
                xs.append(x)
            return _dot(side_by_side(xs), ones2)

        def emit_out(j, states):
            for p in range(half):
                r2 = jnp.concatenate([row(4, p, j), row(4, p + half, j)], axis=1)
                rsel = jnp.where(sel2, r2, 0.0).astype(BF16)
                o8 = _dot_nt(rsel, jnp.concatenate([states[p], states[p + half]], axis=1))
                ob_ref[p, j:j + 1, 0:hd] = o8[0:1, :]
                ob_ref[p, j:j + 1, hd:pw] = o8[1:2, :]
                ob_ref[p + half, j:j + 1, 0:hd] = o8[2:3, :]
                ob_ref[p + half, j:j + 1, hd:pw] = o8[3:4, :]

        vc_next = moved_v(0)
        states = None
        for j in range(8):
            old = [st_ref[p] for p in range(npairs)]
            sa = _dot(side_by_side([old[p] * row(0, p, j) for p in range(npairs)]).astype(BF16), ones2)
            if states is not None:
                emit_out(j - 1, states)
            vc = vc_next
            if j < 7:
                vc_next = moved_v(j + 1)
            states = []
            for p in range(npairs):
                s = old[p] * row(1, p, j) + split(sa, p) * row(3, p, j) + split(vc, p) * row(2, p, j)
                st_ref[p] = s
                states.append(s.astype(BF16))
        emit_out(7, states)
        for p in range(npairs):
            o_ref[rows8, pl.ds(p * pw, pw)] = ob_ref[p]
        return carry

    lax.fori_loop(0, tc // 8, group, 0)

    @pl.when(tb == pl.num_programs(1) - 1)
    def _():
        for p in range(npairs):
            s = st_ref[p]
            sout_ref[0, 2 * p] = s[:, :hd]
            sout_ref[0, 2 * p + 1] = s[:, hd:]


def _rwkv_scan(ops, nseq, t, row0, tc, heads, hd, state):
    width = heads * hd
    assert t % tc == 0 and row0 % tc == 0
    nt = t // tc
    blk = pl.BlockSpec((tc, width), lambda s, j: (row0 // tc + s * nt + j, 0))
    st = pl.BlockSpec((1, heads, hd, hd), lambda s, j: (s, 0, 0, 0))
    in_specs = [blk] * 6
    args = list(ops)
    if state is not None:
        in_specs.append(st)
        args.append(state)
    return pl.pallas_call(
        functools.partial(_rwkv_scan_kernel, hd=hd, has_state=state is not None),
        grid=(nseq, nt),
        in_specs=in_specs,
        out_specs=[pl.BlockSpec((tc, width), lambda s, j: (s * nt + j, 0)), st],
        out_shape=[jax.ShapeDtypeStruct((nseq * t, width), F32),
                   jax.ShapeDtypeStruct((nseq, heads, hd, hd), F32)],
        scratch_shapes=[pltpu.VMEM((heads // 2, hd, 2 * hd), F32), pltpu.VMEM((heads // 2, 8, 2 * hd), F32),
                        pltpu.VMEM((3, 8, width), F32), pltpu.VMEM((5, 8, width), F32)],
        compiler_params=_params(("parallel", "arbitrary")),
        name="rwkv_scan_state" if state is not None else "rwkv_scan",
    )(*args)


def _rwkv_lanes_kernel(r_ref, w_ref, k_ref, v_ref, a_ref, b_ref, s0_ref, o_ref, sout_ref, x_ref, ot_ref, *, dt, hd):
    nseq = s0_ref.shape[-1]
    nh = s0_ref.shape[0]
    for idx, ref in enumerate((a_ref, w_ref, k_ref, b_ref, r_ref, v_ref)):
        for t in range(dt):
            x_ref[idx, t] = ref[pl.ds(t, nseq, stride=dt), :].T

    for hh in range(nh):
        ch = slice(hh * hd, (hh + 1) * hd)

        def vgroup(vg, carry):
            v0 = pl.multiple_of(hh * hd + vg * 8, 8)
            v8 = [x_ref[5, t, pl.ds(v0, 8), :] for t in range(dt)]
            outs = [[None] * 8 for _ in range(dt)]
            for j in range(8):
                s = s0_ref[hh, vg * 8 + j]
                for t in range(dt):
                    sa = jnp.sum(s * x_ref[0, t, ch, :], axis=0, keepdims=True)
                    s = s * x_ref[1, t, ch, :] + sa * x_ref[3, t, ch, :] + v8[t][j:j + 1] * x_ref[2, t, ch, :]
                    outs[t][j] = jnp.sum(s * x_ref[4, t, ch, :], axis=0, keepdims=True)
                sout_ref[hh, vg * 8 + j] = s
            for t in range(dt):
                ot_ref[t, pl.ds(v0, 8), :] = jnp.concatenate(outs[t], axis=0)
            return carry

        lax.fori_loop(0, hd // 8, vgroup, 0)

    for t in range(dt):
        o_ref[pl.ds(t, nseq, stride=dt), :] = ot_ref[t].T


def _rwkv_scan_lanes(ops, nseq, dt, row0, heads, hd, state_lanes):
    width = heads * hd
    rows = nseq * dt
    nh = LANES // hd
    assert row0 % rows == 0 and heads % nh == 0
    blk = pl.BlockSpec((rows, nh * hd), lambda g: (row0 // rows, g))
    st = pl.BlockSpec((nh, hd, hd, nseq), lambda g: (g, 0, 0, 0))
    return pl.pallas_call(
        functools.partial(_rwkv_lanes_kernel, dt=dt, hd=hd),
        grid=(heads // nh,),
        in_specs=[blk] * 6 + [st],
        out_specs=[pl.BlockSpec((rows, nh * hd), lambda g: (0, g)), st],
        out_shape=[jax.ShapeDtypeStruct((rows, width), F32),
                   jax.ShapeDtypeStruct((heads, hd, hd, nseq), F32)],
        scratch_shapes=[pltpu.VMEM((6, dt, nh * hd, nseq), F32), pltpu.VMEM((dt, nh * hd, nseq), F32)],
        compiler_params=_params(("parallel",)),
        name="rwkv_scan_lanes",
    )(*ops, state_lanes)


def _rwkv_post_kernel(op_ref, os_ref, bonus_ref, g_ref, lg_ref, lb_ref, out_ref, *, hd, first_blocks):
    cn = out_ref.shape[1]
    seg = _block_ones(LANES, hd, BF16)

    def segmean(x):
        return jnp.concatenate([_dot_exact01(x[:, c:c + LANES], seg) for c in range(0, cn, LANES)],
                               axis=1) * (1.0 / hd)

    def body(o):
        mean = segmean(o)
        cen = o - mean
        var = segmean(cen * cen)
        y = cen * lax.rsqrt(var + LNX_EPS) * lg_ref[...] + lb_ref[...]
        out_ref[...] = ((y + bonus_ref[...]) * g_ref[...]).astype(out_ref.dtype)

    _from_either(pl.program_id(0), first_blocks, op_ref, os_ref, body)


def _rwkv_post(o_raw_p, o_raw_s, bonus, g, lnx_g, lnx_b, hd, rb=256, cn=512):
    n, rw = bonus.shape
    rb, cn = _pick(math.gcd(o_raw_p.shape[0], o_raw_s.shape[0]), rb), _pick(rw, cn)
    first_blocks = o_raw_p.shape[0] // rb
    blk = pl.BlockSpec((rb, cn), lambda i, j: (i, j))
    vec = pl.BlockSpec((1, cn), lambda i, j: (0, j))
    pair = [pl.BlockSpec((rb, cn), lambda i, j: (jnp.minimum(i, first_blocks - 1), j)),
            pl.BlockSpec((rb, cn), lambda i, j: (jnp.maximum(i - first_blocks, 0), j))]
    return pl.pallas_call(
        functools.partial(_rwkv_post_kernel, hd=hd, first_blocks=first_blocks),
        grid=(n // rb, rw // cn),
        in_specs=pair + [blk, blk, vec, vec],
        out_specs=blk,
        out_shape=jax.ShapeDtypeStruct((n, rw), BF16),
        compiler_params=_params(("parallel", "parallel")),
        name="rwkv_post",
    )(o_raw_p, o_raw_s, bonus, g, lnx_g, lnx_b)


def _mmt_kernel(wt_ref, x_ref, o_ref):
    o_ref[...] = _dot(wt_ref[...].astype(BF16), x_ref[...])


def _matmul_t(wt, xt, tm=512, tn=1024, name="matmul_t"):
    m, k = wt.shape
    n = xt.shape[1]
    tm, tn = _pick(m, tm), _pick(n, tn)
    return pl.pallas_call(
        _mmt_kernel,
        grid=(n // tn, m // tm),
        in_specs=[pl.BlockSpec((tm, k), lambda j, i: (i, 0)),
                  pl.BlockSpec((k, tn), lambda j, i: (0, j))],
        out_specs=pl.BlockSpec((tm, tn), lambda j, i: (i, j)),
        out_shape=jax.ShapeDtypeStruct((m, n), F32),
        compiler_params=_params(("parallel", "parallel")),
        name=name,
    )(wt, xt)


def _take_top(s, k):
    n = s.shape[0]
    row = lax.broadcasted_iota(jnp.int32, s.shape, 0).astype(F32)
    out = []
    for _ in range(k):
        m = jnp.max(s, axis=0, keepdims=True)
        first = jnp.min(jnp.where(s == m, row, float(n)), axis=0, keepdims=True)
        s = jnp.where(row == first, -jnp.inf, s)
        out.append(m)
    return out


def _peer_topk_kernel(q_ref, keys_ref, s0_ref, s1_ref, e0_ref, e1_ref, thr_ref, *, topk):
    kd = keys_ref.shape[3]
    cols = q_ref.shape[1]
    s0 = _dot(keys_ref[0, 0], q_ref[:kd, :], HI)
    s1 = _dot(keys_ref[0, 1], q_ref[kd:, :], HI)
    top0 = _take_top(s0, topk)
    top1 = _take_top(s1, topk)
    width = [topk // (a + 1) for a in range(topk)]
    start = [sum(width[:a]) for a in range(topk)]
    n_cand = sum(width)
    rows = -(-n_cand // 8) * 8
    r = lax.broadcasted_iota(jnp.int32, (rows, cols), 0)
    t0 = jnp.broadcast_to(top0[0], (rows, cols))
    first = jnp.zeros((rows, cols), jnp.int32)
    for a in range(1, topk):
        t0 = jnp.where(r >= start[a], top0[a], t0)
        first = jnp.where(r >= start[a], start[a], first)
    b_of_row = r - first
    t1 = jnp.broadcast_to(top1[0], (rows, cols))
    for b in range(1, topk):
        t1 = jnp.where(b_of_row == b, top1[b], t1)
    cand = jnp.where(r < n_cand, t0 + t1, -jnp.inf)
    best = _take_top(cand, topk)
    z = jnp.zeros((1, cols), F32)
    for b in best:
        z = z + jnp.exp(b - best[0])
    s0_ref[...] = s0
    s1_ref[...] = s1
    e0_ref[...] = jnp.exp(s0 - top0[0]) / z
    e1_ref[...] = jnp.exp(s1 - top1[0])
    thr_ref[0] = jnp.broadcast_to(best[topk - 1], thr_ref.shape[1:])


def _peer_topk(qt, sub_keys, tc=256):
    n = qt.shape[1]
    tc = _pick(n, tc)
    heads, _, nk, kd = sub_keys.shape
    blk = pl.BlockSpec((nk, tc), lambda i, h: (h, i))
    shp = jax.ShapeDtypeStruct((heads * nk, n), F32)
    return pl.pallas_call(
        functools.partial(_peer_topk_kernel, topk=PEER_TOPK),
        grid=(n // tc, heads),
        in_specs=[pl.BlockSpec((2 * kd, tc), lambda i, h: (h, i)),
                  pl.BlockSpec((1, 2, nk, kd), lambda i, h: (h, 0, 0, 0))],
        out_specs=[blk] * 4 + [pl.BlockSpec((1, 8, tc), lambda i, h: (h, 0, i))],
        out_shape=[shp] * 4 + [jax.ShapeDtypeStruct((heads, 8, n), F32)],
        compiler_params=_params(("parallel", "parallel")),
        name="peer_topk",
    )(qt, sub_keys)


def _gelu(x):
    return 0.5 * x * (1.0 + lax.erf(x * (2.0 ** -0.5)))


def _peer_dense_kernel(u_ref, ht_ref, vt_ref, s0_ref, e0_ref, s1_ref, e1_ref, thr_ref, o_ref, act_ref, coef_ref,
                       *, heads, nk, jc):
    e = pl.program_id(1)
    eb = u_ref.shape[0]
    d = vt_ref.shape[0]
    ni = eb // nk
    dq = d // ni

    @pl.when(e == 0)
    def _():
        o_ref[...] = jnp.zeros_like(o_ref)
        coef_ref[1] = jnp.zeros(coef_ref.shape[1:], coef_ref.dtype)

    act_ref[...] = _dot(u_ref[...], ht_ref[...])
    cur = e % 2
    prev = (e + 1) % 2
    thr = thr_ref[...]

    def one_i(ii, carry):
        orow = pl.ds(pl.multiple_of(ii * dq, dq), dq)
        o_ref[orow, :] += _dot(vt_ref[orow, :], coef_ref[prev])
        s0 = s0_ref[0, ii]
        e0 = e0_ref[0, ii]
        for j0 in range(0, nk, jc):
            g = None
            for h in range(heads):
                jrows = slice(h * nk + j0, h * nk + j0 + jc)
                hit = (s0[h:h + 1] + s1_ref[jrows, :]) >= thr[h:h + 1]
                term = jnp.where(hit, e0[h:h + 1] * e1_ref[jrows, :], 0.0)
                g = term if g is None else g + term
            arow = pl.ds(pl.multiple_of(ii * nk + j0, jc), jc)
            coef_ref[cur, arow, :] = (g * _gelu(act_ref[arow, :])).astype(coef_ref.dtype)
        return carry

    lax.fori_loop(0, ni, one_i, 0)


def _peer_dense(ht, u, vt, s0r, e0r, s1, e1, thr, heads, nk, tp=512, eb=512):
    d, n = ht.shape
    tp = _pick(n, tp)
    nblk = u.shape[0] // eb
    ni = eb // nk
    once = pl.Buffered(1)
    tok = pl.BlockSpec((heads * nk, tp), lambda i, e: (0, i), pipeline_mode=once)
    sl = pl.BlockSpec((1, ni, heads, tp), lambda i, e: (jnp.minimum(e, nblk - 1), 0, 0, i))
    return pl.pallas_call(
        functools.partial(_peer_dense_kernel, heads=heads, nk=nk, jc=PEER_MASK_ROWS),
        grid=(n // tp, nblk + 1),
        in_specs=[pl.BlockSpec((eb, d), lambda i, e: (jnp.minimum(e, nblk - 1), 0)),
                  pl.BlockSpec((d, tp), lambda i, e: (0, i), pipeline_mode=once),
                  pl.BlockSpec((d, eb), lambda i, e: (0, jnp.maximum(e - 1, 0))),
                  sl, sl, tok, tok, pl.BlockSpec((heads, tp), lambda i, e: (0, i))],
        out_specs=pl.BlockSpec((d, tp), lambda i, e: (0, i)),
        out_shape=jax.ShapeDtypeStruct((d, n), F32),
        scratch_shapes=[pltpu.VMEM((eb, tp), F32), pltpu.VMEM((2, eb, tp), BF16)],
        compiler_params=_params(("parallel", "arbitrary")),
        name="peer_dense",
    )(u, ht, vt, s0r, e0r, s1, e1, thr)


def _by_expert_block(x, heads, nk, ni):
    n = x.shape[1]
    return x.reshape(heads, nk // ni, ni, n).transpose(1, 2, 0, 3)


def _layer(xp, xs, s_ret, s_rwkv, s_shift, cp, cs, p):
    nb, t, d = xp.shape
    db, dt, _ = xs.shape
    rows = _Rows(nb, t, db, dt)
    ret_heads, ret_hd = s_ret.shape[1], s_ret.shape[2]
    rw_heads, rw_hd = s_rwkv.shape[1], s_rwkv.shape[2]
    ret_w, rw = ret_heads * ret_hd, rw_heads * rw_hd
    w_lora, a_lora, g_lora = p['w2'].shape[0], p['a2'].shape[0], p['g2'].shape[0]
    rwkv_in = s_shift.shape[-1]
    ret_in = 4 * ret_w
    assert p['w_in'].shape[1] == ret_in + rwkv_in and rwkv_in == 3 * rw + w_lora + a_lora + g_lora

    xp, xs = xp.reshape(nb * t, d), xs.reshape(db * dt, d)
    mod = _ada(jnp.concatenate([cp, cs], axis=0), p['ada_w'], p['ada_b'])
    mod_ext = rows.extend(mod)

    h = _prenorm(rows, xp, xs, p['pre_mix_g'], mod_ext, d)

    tm = _pick(rows.n, 1024)
    w_in_t = p['w_in'].T
    proj_ret = _matmul(h, w_in_t, col_block0=0, n_out=ret_in, tm=tm, tn=512, name="in_proj_ret")
    proj_rw = _matmul(h, w_in_t, col_block0=ret_in // 512, n_out=rwkv_in, tm=tm, tn=512, name="in_proj_rwkv")

    o_ret_p, o_ret_s, sr_p, sr_s = _retention(rows, proj_ret, s_ret, ret_heads, ret_hd)

    tail_w = -(-(w_lora + a_lora + g_lora) // LANES) * LANES
    g2p = jnp.pad(p['g2'], ((0, tail_w - w_lora - a_lora - g_lora), (0, 0)))
    vec = lambda a: a.reshape(1, rw)
    r_, w_, k_, v_, a_, b_, g_, bonus = _rwkv_prep(
        rows, proj_rw, s_shift, p['shift_mu'], vec(p['w0']), p['w2'], vec(p['a0']), p['a2'], g2p,
        vec(p['k_k']), vec(p['k_a']), vec(p['r_k']), rw, rw_hd, g_lora)
    ops = (r_, w_, k_, v_, a_, b_)
    o_rw_p, sw_p = _rwkv_scan(ops, nb, t, 0, 256, rw_heads, rw_hd, None)
    o_rw_s, sw_lanes = _rwkv_scan_lanes(ops, db, dt, nb * t, rw_heads, rw_hd, jnp.transpose(s_rwkv, (1, 2, 3, 0)))
    sw_s = jnp.transpose(sw_lanes, (3, 0, 1, 2))
    o_rw = _rwkv_post(o_rw_p, o_rw_s, bonus, g_, vec(p['lnx_g']), vec(p['lnx_b']), rw_hd)

    mix = _matmul_out(o_ret_p, o_ret_s, o_rw, p['w_out'])
    x1, h2t = _postmix(rows, xp, xs, mix, p['post_mix_g'], p['pre_ffn_g'], mod_ext, d)

    heads, _, nk, kd = p['peer_sub_keys'].shape
    qt = _matmul_t(p['peer_wq'].T, h2t, name="peer_q")
    s0, s1, e0, e1, thr = _peer_topk(qt, p['peer_sub_keys'])
    eb = 512
    ni = eb // nk
    peer_t = _peer_dense(h2t, p['peer_u'].astype(BF16), p['peer_v'].T.astype(BF16),
                         _by_expert_block(s0, heads, nk, ni), _by_expert_block(e0, heads, nk, ni),
                         s1, e1, thr[:, 0, :], heads, nk, eb=eb)
    yp = _final(rows, x1, peer_t, p['post_ffn_g'], mod_ext, d, 0, rows.prompt_blocks)
    ys = _final(rows, x1, peer_t, p['post_ffn_g'], mod_ext, d, rows.prompt_blocks, rows.n_blocks - rows.prompt_blocks)

    n_p = nb * t
    last_p = slice(t - 1, n_p, t)
    last_s = slice(n_p + dt - 1, None, dt)
    return (yp.reshape(nb, t, d), ys.reshape(db, dt, d), sr_p, sr_s, sw_p, sw_s, proj_rw[last_p], proj_rw[last_s])


def kernel(x_prompt, x_sample, state_ret, state_rwkv, state_shift, c_prompt, c_sample, ada_w, ada_b, pre_mix_g, post_mix_g, pre_ffn_g, post_ffn_g, w_in, shift_mu, w0, w2, a0, a2, g2, k_k, k_a, r_k, lnx_g, lnx_b, w_out, peer_wq, peer_sub_keys, peer_u, peer_v):
    depth = ada_w.shape[0]
    assert depth == 1, "prompt and sample tokens are stacked per layer; deeper stacks need per-layer restacking"
    prm = dict(ada_w=ada_w[0], ada_b=ada_b[0], pre_mix_g=pre_mix_g[0], post_mix_g=post_mix_g[0],
               pre_ffn_g=pre_ffn_g[0], post_ffn_g=post_ffn_g[0], w_in=w_in[0], shift_mu=shift_mu[0],
               w0=w0[0], w2=w2[0], a0=a0[0], a2=a2[0], g2=g2[0], k_k=k_k[0], k_a=k_a[0], r_k=r_k[0],
               lnx_g=lnx_g[0], lnx_b=lnx_b[0], w_out=w_out[0], peer_wq=peer_wq[0],
               peer_sub_keys=peer_sub_keys[0], peer_u=peer_u[0], peer_v=peer_v[0])
    yp, ys, sr_p, sr_s, sw_p, sw_s, ss_p, ss_s = _layer(
        x_prompt, x_sample, state_ret[0], state_rwkv[0], state_shift[0], c_prompt, c_sample, prm)
    sd, wd, hd = state_ret.dtype, state_rwkv.dtype, state_shift.dtype
    return (yp, ys, sr_p[None].astype(sd), sr_s[None].astype(sd), sw_p[None].astype(wd), sw_s[None].astype(wd),
            ss_p[None].astype(hd), ss_s[None].astype(hd))
```

```python
import functools
import math

import jax
import jax.numpy as jnp
from jax import lax
from jax.experimental import pallas as pl
from jax.experimental.pallas import tpu as pltpu

PAST_LEN = 16384
RET_CHUNK = 128
ROPE_BASE = 10000.0
LNX_EPS = 64e-5
NORM_EPS = 1e-6
PEER_TOPK = 16
PEER_MASK_ROWS = 16
RET_HEADS_PER_STEP = 4

LANES = 128
ROW_BLOCK = 128
VMEM_LIMIT = 56 * 1024 * 1024

HI = lax.Precision.HIGHEST
F32 = jnp.float32
BF16 = jnp.bfloat16


def _pick(n, target, mult=LANES):
    best = None
    for c in range(mult, min(n, target) + 1, mult):
        if n % c == 0:
            best = c
    assert best is not None, (n, target, mult)
    return best


def _params(sem, vmem=VMEM_LIMIT):
    return pltpu.CompilerParams(dimension_semantics=sem, vmem_limit_bytes=vmem)


def _dot(a, b, precision=None):
    return jnp.dot(a, b, preferred_element_type=F32, precision=precision)


def _dot_nt(a, b, precision=None):
    return lax.dot_general(a, b, (((1,), (1,)), ((), ())), preferred_element_type=F32, precision=precision)


def _dot_tn(a, b, precision=None):
    return lax.dot_general(a, b, (((0,), (0,)), ((), ())), preferred_element_type=F32, precision=precision)


def _split_bf16(x, terms):
    out = []
    for _ in range(terms - 1):
        t = x.astype(BF16)
        out.append(t)
        x = x - t.astype(F32)
    return out + [x.astype(BF16)]


def _dot_split(x, w_terms):
    xh, xl = _split_bf16(x, 2)
    wh, wl = w_terms
    return _dot(xh, wh) + (_dot(xl, wh) + _dot(xh, wl))


def _dot_exact01(x, ones):
    return sum(_dot(t, ones) for t in _split_bf16(x, 3))


def _block_ones(n, seg, dtype):
    r = lax.broadcasted_iota(jnp.int32, (n, n), 0) // seg
    c = lax.broadcasted_iota(jnp.int32, (n, n), 1) // seg
    return (r == c).astype(dtype)


def _ada_kernel(c_ref, w_ref, b_ref, o_ref):
    c = c_ref[...]
    sc = (c * jax.nn.sigmoid(c)).astype(BF16)
    o_ref[...] = _dot(sc, w_ref[...].astype(BF16)) + b_ref[...]


def _ada(c_all, ada_w, ada_b, tn=512):
    m, d = c_all.shape
    n = ada_w.shape[1]
    return pl.pallas_call(
        _ada_kernel,
        grid=(n // tn,),
        in_specs=[pl.BlockSpec((m, d), lambda j: (0, 0)),
                  pl.BlockSpec((d, tn), lambda j: (0, j)),
                  pl.BlockSpec((1, tn), lambda j: (0, j))],
        out_specs=pl.BlockSpec((m, tn), lambda j: (0, j)),
        out_shape=jax.ShapeDtypeStruct((m, n), F32),
        compiler_params=_params(("arbitrary",)),
        name="ada",
    )(c_all, ada_w, ada_b.reshape(1, n))


def _expand_slots(m, rows):
    slots, _, d = m.shape
    return jnp.broadcast_to(m, (slots, rows // slots, d)).reshape(rows, d)


def _rms(x, g):
    return x * lax.rsqrt(jnp.mean(x * x, axis=-1, keepdims=True) + NORM_EPS) * g


def _from_either(i, first_blocks, a_ref, b_ref, body):
    @pl.when(i < first_blocks)
    def _():
        body(a_ref[...])

    @pl.when(i >= first_blocks)
    def _():
        body(b_ref[...])


def _two_specs(block, first_blocks):
    return [pl.BlockSpec(block, lambda i, *_: (jnp.minimum(i, first_blocks - 1), 0)),
            pl.BlockSpec(block, lambda i, *_: (jnp.maximum(i - first_blocks, 0), 0))]


def _prenorm_kernel(xp_ref, xs_ref, g_ref, sh_ref, sc_ref, h_ref, *, prompt_blocks):
    def body(x):
        rows = x.shape[0]
        y = _rms(x, g_ref[...])
        h = y * (1.0 + _expand_slots(sc_ref[...], rows)) + _expand_slots(sh_ref[...], rows)
        h_ref[...] = h.astype(h_ref.dtype)

    _from_either(pl.program_id(0), prompt_blocks, xp_ref, xs_ref, body)


def _postmix_kernel(xp_ref, xs_ref, m_ref, gpost_ref, gpre_ref, gate_ref, sh_ref, sc_ref, x1_ref, ht_ref,
                    *, prompt_blocks):
    def body(x):
        rows = x.shape[0]
        x1 = x + _expand_slots(gate_ref[...], rows) * _rms(m_ref[...], gpost_ref[...])
        x1_ref[...] = x1
        y = _rms(x1, gpre_ref[...])
        h = y * (1.0 + _expand_slots(sc_ref[...], rows)) + _expand_slots(sh_ref[...], rows)
        ht_ref[...] = h.T.astype(ht_ref.dtype)

    _from_either(pl.program_id(0), prompt_blocks, xp_ref, xs_ref, body)


def _final_kernel(x_ref, mt_ref, gpost_ref, gate_ref, y_ref):
    x = x_ref[...]
    rows = x.shape[0]
    y_ref[...] = x + _expand_slots(gate_ref[...], rows) * _rms(mt_ref[...].T, gpost_ref[...])


class _Rows:
    def __init__(self, nb, t, db, dt):
        self.n_prompt = nb * t
        self.n_sample = db * dt
        self.n = self.n_prompt + self.n_sample
        self.rb = ROW_BLOCK
        assert t % self.rb == 0 and self.rb % dt == 0 and self.n_sample % self.rb == 0
        self.slots = self.rb // dt
        self.prompt_blocks = self.n_prompt // self.rb
        self.blocks_per_seq = t // self.rb
        self.n_blocks = self.n // self.rb
        self.nb, self.t, self.db, self.dt = nb, t, db, dt

    def slot_block(self, i):
        return jnp.where(i < self.prompt_blocks, i // self.blocks_per_seq, i - self.prompt_blocks + self.nb)

    def extend(self, m):
        mp = jnp.repeat(m[: self.nb], self.slots, axis=0)
        return jnp.concatenate([mp, m[self.nb:]], axis=0)[:, None, :]


def _mod_spec(rows, d, chunk, block0=0):
    return pl.BlockSpec((rows.slots, 1, d), lambda i: (rows.slot_block(i + block0), 0, chunk))


def _prenorm(rows, xp, xs, g, mod_ext, d):
    row = pl.BlockSpec((rows.rb, d), lambda i: (i, 0))
    vec = pl.BlockSpec((1, d), lambda i: (0, 0))
    return pl.pallas_call(
        functools.partial(_prenorm_kernel, prompt_blocks=rows.prompt_blocks),
        grid=(rows.n_blocks,),
        in_specs=_two_specs((rows.rb, d), rows.prompt_blocks) + [vec, _mod_spec(rows, d, 0), _mod_spec(rows, d, 1)],
        out_specs=row,
        out_shape=jax.ShapeDtypeStruct((rows.n, d), BF16),
        compiler_params=_params(("parallel",)),
        name="prenorm",
    )(xp, xs, g.reshape(1, d), mod_ext, mod_ext)


def _postmix(rows, xp, xs, mix, gpost, gpre, mod_ext, d):
    row = pl.BlockSpec((rows.rb, d), lambda i: (i, 0))
    vec = pl.BlockSpec((1, d), lambda i: (0, 0))
    return pl.pallas_call(
        functools.partial(_postmix_kernel, prompt_blocks=rows.prompt_blocks),
        grid=(rows.n_blocks,),
        in_specs=_two_specs((rows.rb, d), rows.prompt_blocks)
        + [row, vec, vec, _mod_spec(rows, d, 2), _mod_spec(rows, d, 3), _mod_spec(rows, d, 4)],
        out_specs=[row, pl.BlockSpec((d, rows.rb), lambda i: (0, i))],
        out_shape=[jax.ShapeDtypeStruct((rows.n, d), F32), jax.ShapeDtypeStruct((d, rows.n), BF16)],
        compiler_params=_params(("parallel",)),
        name="postmix",
    )(xp, xs, mix, gpost.reshape(1, d), gpre.reshape(1, d), mod_ext, mod_ext, mod_ext)


def _final(rows, x1, peer_t, gpost, mod_ext, d, block0, n_blocks):
    vec = pl.BlockSpec((1, d), lambda i: (0, 0))
    return pl.pallas_call(
        _final_kernel,
        grid=(n_blocks,),
        in_specs=[pl.BlockSpec((rows.rb, d), lambda i: (i + block0, 0)),
                  pl.BlockSpec((d, rows.rb), lambda i: (0, i + block0)),
                  vec, _mod_spec(rows, d, 5, block0)],
        out_specs=pl.BlockSpec((rows.rb, d), lambda i: (i, 0)),
        out_shape=jax.ShapeDtypeStruct((n_blocks * rows.rb, d), F32),
        compiler_params=_params(("parallel",)),
        name="final",
    )(x1, peer_t, gpost.reshape(1, d), mod_ext)


def _mm_kernel(a_ref, wt_ref, o_ref, wbf_ref):
    @pl.when(pl.program_id(1) == 0)
    def _():
        wbf_ref[...] = wt_ref[...].astype(BF16)

    o_ref[...] = _dot_nt(a_ref[...], wbf_ref[...]).astype(o_ref.dtype)


def _matmul(a, wt, *, col_block0, n_out, tm, tn, name):
    m, k = a.shape
    return pl.pallas_call(
        _mm_kernel,
        grid=(pl.cdiv(n_out, tn), m // tm),
        in_specs=[pl.BlockSpec((tm, k), lambda j, i: (i, 0)),
                  pl.BlockSpec((tn, k), lambda j, i: (j + col_block0, 0))],
        out_specs=pl.BlockSpec((tm, tn), lambda j, i: (i, j)),
        out_shape=jax.ShapeDtypeStruct((m, n_out), F32),
        scratch_shapes=[pltpu.VMEM((tn, k), BF16)],
        compiler_params=_params(("parallel", "arbitrary")),
        name=name,
    )(a, wt)


def _mm2_kernel(ap_ref, as_ref, b_ref, wa_ref, wb_ref, o_ref, *, first_blocks):
    def body(a):
        o_ref[...] = _dot(a, wa_ref[...].astype(BF16)) + _dot(b_ref[...], wb_ref[...].astype(BF16))

    _from_either(pl.program_id(1), first_blocks, ap_ref, as_ref, body)


def _matmul_out(o_ret_p, o_ret_s, o_rw, w_out, tn=512):
    m, kb = o_rw.shape
    ka = o_ret_p.shape[1]
    tm = _pick(math.gcd(o_ret_p.shape[0], o_ret_s.shape[0]), 1024)
    first_blocks = o_ret_p.shape[0] // tm
    n = w_out.shape[1]
    tn = _pick(n, tn)
    assert ka == kb
    pair = [pl.BlockSpec((tm, ka), lambda j, i: (jnp.minimum(i, first_blocks - 1), 0)),
            pl.BlockSpec((tm, ka), lambda j, i: (jnp.maximum(i - first_blocks, 0), 0))]
    return pl.pallas_call(
        functools.partial(_mm2_kernel, first_blocks=first_blocks),
        grid=(n // tn, m // tm),
        in_specs=pair + [pl.BlockSpec((tm, kb), lambda j, i: (i, 0)),
                         pl.BlockSpec((ka, tn), lambda j, i: (0, j)),
                         pl.BlockSpec((kb, tn), lambda j, i: (1, j))],
        out_specs=pl.BlockSpec((tm, tn), lambda j, i: (i, j)),
        out_shape=jax.ShapeDtypeStruct((m, n), F32),
        compiler_params=_params(("parallel", "parallel")),
        name="out_proj",
    )(o_ret_p, o_ret_s, o_rw, w_out, w_out)


def _rope_tables(pos, hd):
    half = hd // 2
    inv = ROPE_BASE ** (-jnp.arange(half, dtype=F32) / half)
    ang = pos.astype(F32)[:, None] * inv[None, :]
    cos, sin = jnp.cos(ang), jnp.sin(ang)
    return jnp.concatenate([cos, cos], axis=-1), jnp.concatenate([-sin, sin], axis=-1)


def _rope(x, cos, sin_signed):
    half = x.shape[-1] // 2
    return x * cos + pltpu.roll(x, half, 1) * sin_signed


def _ret_finish(o, gate):
    o = o * lax.rsqrt(jnp.mean(o * o, axis=-1, keepdims=True) + NORM_EPS)
    return o * (gate * jax.nn.sigmoid(gate))


def _ret_prompt_kernel(q_ref, k_ref, v_ref, g_ref, cos_ref, sin_ref, lg_ref, o_ref, s_ref, *, chunk, hd):
    t = q_ref.shape[0]
    nh = q_ref.shape[1] // hd
    ri = lax.broadcasted_iota(jnp.int32, (chunk, chunk), 0)
    ci = lax.broadcasted_iota(jnp.int32, (chunk, chunk), 1)
    causal = ri >= ci
    diff = jnp.where(causal, ri - ci, 0).astype(F32)
    rowf = lax.broadcasted_iota(jnp.int32, (chunk, hd), 0).astype(F32)
    lgs = [lg_ref[i, 0:1, :] for i in range(nh)]
    masks = [jnp.where(causal, jnp.exp(diff * lg), 0.0) for lg in lgs]
    q_decs = [jnp.exp((rowf + 1.0) * lg) for lg in lgs]
    k_decs = [jnp.exp((chunk - 1.0 - rowf) * lg) for lg in lgs]
    c_decs = [jnp.exp(float(chunk) * lg) for lg in lgs]

    def step(c, states):
        r0 = pl.multiple_of(c * chunk, chunk)
        rows = pl.ds(r0, chunk)
        cos, sin = cos_ref[rows, :], sin_ref[rows, :]
        cols = [slice(i * hd, (i + 1) * hd) for i in range(nh)]
        qs = [_rope(q_ref[rows, cl], cos, sin) for cl in cols]
        ks = [_rope(k_ref[rows, cl], cos, sin) * (hd ** -0.5) for cl in cols]
        vs = [v_ref[rows, cl] for cl in cols]
        atts = [_dot_nt(q.astype(BF16), k.astype(BF16)) * m for q, k, m in zip(qs, ks, masks)]
        cross = [_dot((q * qd).astype(BF16), s.astype(BF16)) for q, qd, s in zip(qs, q_decs, states)]
        inner = [_dot(a.astype(BF16), v.astype(BF16)) for a, v in zip(atts, vs)]
        new = [s * cd + _dot_tn(k * kd, v, HI) for s, cd, k, kd, v in zip(states, c_decs, ks, k_decs, vs)]
        for i, cl in enumerate(cols):
            o_ref[rows, cl] = _ret_finish(inner[i] + cross[i], g_ref[rows, cl]).astype(o_ref.dtype)
        return tuple(new)

    final = lax.fori_loop(0, t // chunk, step, tuple(jnp.zeros((hd, hd), F32) for _ in range(nh)))
    for i in range(nh):
        s_ref[0, i] = final[i]


def _ret_sample_kernel(q_ref, k_ref, v_ref, g_ref, cos_ref, sin_ref, lg_ref, s0_ref, o_ref, s_ref, *, dt):
    rows, hd = q_ref.shape
    nseq = rows // dt
    lg = lg_ref[0, 0:1, :]
    ri = lax.broadcasted_iota(jnp.int32, (rows, rows), 0)
    ci = lax.broadcasted_iota(jnp.int32, (rows, rows), 1)
    ok = (ri >= ci) & ((ri // dt) == (ci // dt))
    diff = jnp.where(ok, ri - ci, 0).astype(F32)
    mask = jnp.where(ok, jnp.exp(diff * lg), 0.0)
    rown = lax.broadcasted_iota(jnp.int32, (rows, hd), 0)
    posf = (rown % dt).astype(F32)
    q_dec = jnp.exp((posf + 1.0) * lg)
    k_dec = jnp.exp((dt - 1.0 - posf) * lg)
    c_dec = jnp.exp(float(dt) * lg)
    cos, sin = cos_ref[...], sin_ref[...]
    qc = _rope(q_ref[...], cos, sin)
    kc = _rope(k_ref[...], cos, sin) * (hd ** -0.5)
    vc = v_ref[...]
    att = _dot_nt(qc.astype(BF16), kc.astype(BF16)) * mask
    inner = _dot(att.astype(BF16), vc.astype(BF16))
    qd = qc * q_dec
    kd = kc * k_dec
    seq = rown // dt
    cross = jnp.zeros((rows, hd), F32)
    for s in range(nseq):
        s0 = s0_ref[s, 0]
        mine = seq == s
        cross = cross + _dot(jnp.where(mine, qd, 0.0).astype(BF16), s0.astype(BF16))
        s_ref[s, 0] = s0 * c_dec + _dot_tn(jnp.where(mine, kd, 0.0), vc, HI)
    o_ref[...] = _ret_finish(inner + cross, g_ref[...]).astype(o_ref.dtype)


def _retention(rows, proj, state_ret, ret_heads, hd):
    nb, t, db, dt = rows.nb, rows.t, rows.db, rows.dt
    lg = jnp.log1p(-jnp.exp2(-5.0 - jnp.arange(ret_heads, dtype=F32)))
    lg_tab = jnp.broadcast_to(lg[:, None, None], (ret_heads, 8, hd))
    lg_spec = pl.BlockSpec((1, 8, hd), lambda b, h: (h, 0, 0))
    assert hd == RET_CHUNK and t % RET_CHUNK == 0 and dt % RET_CHUNK != 0

    cos_p, sin_p = _rope_tables(jnp.arange(t, dtype=jnp.int32), hd)
    nh = RET_HEADS_PER_STEP if ret_heads % RET_HEADS_PER_STEP == 0 else 1
    hsteps = ret_heads // nh
    col = lambda part: pl.BlockSpec((t, nh * hd), lambda b, h: (b, h + part * hsteps))
    tab = pl.BlockSpec((t, hd), lambda b, h: (0, 0))
    o_p, s_p = pl.pallas_call(
        functools.partial(_ret_prompt_kernel, chunk=RET_CHUNK, hd=hd),
        grid=(nb, hsteps),
        in_specs=[col(0), col(1), col(2), col(3), tab, tab, pl.BlockSpec((nh, 8, hd), lambda b, h: (h, 0, 0))],
        out_specs=[pl.BlockSpec((t, nh * hd), lambda b, h: (b, h)),
                   pl.BlockSpec((1, nh, hd, hd), lambda b, h: (b, h, 0, 0))],
        out_shape=[jax.ShapeDtypeStruct((nb * t, ret_heads * hd), BF16),
                   jax.ShapeDtypeStruct((nb, ret_heads, hd, hd), F32)],
        compiler_params=_params(("parallel", "parallel")),
        name="ret_prompt",
    )(proj, proj, proj, proj, cos_p, sin_p, lg_tab)

    rb = RET_CHUNK
    nseq = rb // dt
    cos_s, sin_s = _rope_tables(PAST_LEN + jnp.arange(dt, dtype=jnp.int32), hd)
    cos_s, sin_s = jnp.tile(cos_s, (nseq, 1)), jnp.tile(sin_s, (nseq, 1))
    off = (nb * t) // rb
    col = lambda part: pl.BlockSpec((rb, hd), lambda b, h: (b + off, h + part * ret_heads))
    tab = pl.BlockSpec((rb, hd), lambda b, h: (0, 0))
    st = pl.BlockSpec((nseq, 1, hd, hd), lambda b, h: (b, h, 0, 0))
    o_s, s_s = pl.pallas_call(
        functools.partial(_ret_sample_kernel, dt=dt),
        grid=(db // nseq, ret_heads),
        in_specs=[col(0), col(1), col(2), col(3), tab, tab, lg_spec, st],
        out_specs=[pl.BlockSpec((rb, hd), lambda b, h: (b, h)), st],
        out_shape=[jax.ShapeDtypeStruct((db * dt, ret_heads * hd), BF16),
                   jax.ShapeDtypeStruct((db, ret_heads, hd, hd), F32)],
        compiler_params=_params(("parallel", "parallel")),
        name="ret_sample",
    )(proj, proj, proj, proj, cos_s, sin_s, lg_tab, state_ret)
    return o_p, o_s, s_p, s_s


def _shifted(x, first, tseq_mask):
    prev = pltpu.roll(x, 1, 0)
    return jnp.where(tseq_mask, first, prev)


def _rwkv_prep_kernel(pr_ref, pk_ref, pv_ref, pt_ref, sr_ref, sk_ref, sv_ref, st_ref,
                      mur_ref, muk_ref, muv_ref, mut_ref, w0_ref, w2h_ref, w2l_ref, a0_ref, a2h_ref, a2l_ref,
                      g2h_ref, g2l_ref,
                      kk_ref, ka_ref, rk_ref,
                      r_out, w_out, k_out, v_out, a_out, b_out, g_out, bonus_out, carry_ref, tcarry_ref,
                      *, dt, prompt_blocks, blocks_per_seq, w_lora, a_lora, g_lora, hd):
    i, j = pl.program_id(0), pl.program_id(1)
    rows, cn = pr_ref.shape
    rown = lax.broadcasted_iota(jnp.int32, (rows, 1), 0)
    sample_block = jnp.zeros((rows, 1), jnp.int32) + (i >= prompt_blocks).astype(jnp.int32)
    is_first = (rown == 0) | ((rown % dt == 0) & (sample_block == 1))
    seq_start = i % blocks_per_seq == 0

    @pl.when(i == 0)
    def _():
        for part in range(3):
            carry_ref[part, j] = jnp.zeros(carry_ref.shape[2:], F32)
        tcarry_ref[...] = jnp.zeros_like(tcarry_ref)

    def lerp(p_ref, s_ref, mu_ref, carried):
        p = p_ref[...]
        from_prompt = jnp.broadcast_to(jnp.where(seq_start, 0.0, carried), p.shape)
        first = jnp.where(sample_block == 1, _expand_slots(s_ref[...], rows), from_prompt)
        prev = _shifted(p, first, is_first)
        return p + mu_ref[...] * (prev - p)

    r = lerp(pr_ref, sr_ref, mur_ref, carry_ref[0, j, 0:1, :])
    kw = lerp(pk_ref, sk_ref, muk_ref, carry_ref[1, j, 0:1, :])
    vw = lerp(pv_ref, sv_ref, muv_ref, carry_ref[2, j, 0:1, :])
    tail = lerp(pt_ref, st_ref, mut_ref, tcarry_ref[0:1, :])
    for part, p_ref in enumerate((pr_ref, pk_ref, pv_ref)):
        carry_ref[part, j, 0:1, :] = p_ref[rows - 1:rows, :]

    @pl.when(j == pl.num_programs(1) - 1)
    def _():
        tcarry_ref[0:1, :] = pt_ref[rows - 1:rows, :]

    tw = tail.shape[1]
    wl = tail[:, :w_lora]
    al = tail[:, w_lora:w_lora + a_lora]
    glane = lax.broadcasted_iota(jnp.int32, (rows, tw - w_lora - a_lora), 1)
    gl = jnp.where(glane < g_lora, jax.nn.sigmoid(tail[:, w_lora + a_lora:]), 0.0)

    w_log = -jax.nn.softplus(-(w0_ref[...] + _dot_split(jnp.tanh(wl), (w2h_ref[...], w2l_ref[...])))) - 0.5
    decay = jnp.exp(-jnp.exp(w_log))
    a = jax.nn.sigmoid(a0_ref[...] + _dot_split(al, (a2h_ref[...], a2l_ref[...])))
    g = _dot_split(gl, (g2h_ref[...], g2l_ref[...]))

    seg = _block_ones(LANES, hd, BF16)

    def segsum(x):
        return jnp.concatenate([_dot_exact01(x[:, c:c + LANES], seg) for c in range(0, cn, LANES)], axis=1)

    kk = kw * kk_ref[...]
    kk = kk / jnp.maximum(jnp.sqrt(segsum(kk * kk)), 1e-12)
    kn = kw * (1.0 + (a - 1.0) * ka_ref[...])
    r_out[...] = r
    w_out[...] = decay
    k_out[...] = kn
    v_out[...] = vw
    a_out[...] = -kk
    b_out[...] = kk * a
    g_out[...] = g
    bonus_out[...] = segsum(r * kn * rk_ref[...]) * vw


def _hi_lo(w):
    hi = w.astype(BF16)
    return hi, (w - hi.astype(F32)).astype(BF16)


def _rwkv_prep(rows, proj, s_shift, mu, w0, w2, a0, a2, g2p, k_k, k_a, r_k_flat, rw, hd, g_lora, cn=512):
    n = rows.n
    cn = _pick(rw, cn)
    ncb = rw // cn
    tw = g2p.shape[0] + w2.shape[0] + a2.shape[0]
    assert (3 * rw) % tw == 0
    tb = (3 * rw) // tw
    sblk = lambda i: jnp.maximum(i - rows.prompt_blocks, 0)
    part = lambda p: pl.BlockSpec((rows.rb, cn), lambda i, j: (i, j + p * ncb))
    spart = lambda p: pl.BlockSpec((rows.slots, 1, cn), lambda i, j: (sblk(i), 0, j + p * ncb))
    mupart = lambda p: pl.BlockSpec((1, cn), lambda i, j: (0, j + p * ncb))
    vec = pl.BlockSpec((1, cn), lambda i, j: (0, j))
    lora = lambda k: pl.BlockSpec((k, cn), lambda i, j: (0, j))
    out = pl.BlockSpec((rows.rb, cn), lambda i, j: (i, j))
    kern = functools.partial(_rwkv_prep_kernel, dt=rows.dt, prompt_blocks=rows.prompt_blocks,
                             blocks_per_seq=rows.blocks_per_seq,
                             w_lora=w2.shape[0], a_lora=a2.shape[0], g_lora=g_lora, hd=hd)
    shift3 = s_shift[:, None, :]
    mu2 = mu.reshape(1, -1)
    return pl.pallas_call(
        kern,
        grid=(rows.n_blocks, ncb),
        in_specs=[part(0), part(1), part(2), pl.BlockSpec((rows.rb, tw), lambda i, j: (i, tb)),
                  spart(0), spart(1), spart(2), pl.BlockSpec((rows.slots, 1, tw), lambda i, j: (sblk(i), 0, tb)),
                  mupart(0), mupart(1), mupart(2), pl.BlockSpec((1, tw), lambda i, j: (0, tb)),
                  vec, lora(w2.shape[0]), lora(w2.shape[0]), vec, lora(a2.shape[0]), lora(a2.shape[0]),
                  lora(g2p.shape[0]), lora(g2p.shape[0]), vec, vec, vec],
        out_specs=[out] * 8,
        out_shape=[jax.ShapeDtypeStruct((n, rw), F32)] * 8,
        scratch_shapes=[pltpu.VMEM((3, ncb, 8, cn), F32), pltpu.VMEM((8, tw), F32)],
        compiler_params=_params(("arbitrary", "arbitrary")),
        name="rwkv_prep",
    )(proj, proj, proj, proj, shift3, shift3, shift3, shift3, mu2, mu2, mu2, mu2,
      w0, *_hi_lo(w2), a0, *_hi_lo(a2), *_hi_lo(g2p), k_k, k_a, r_k_flat)


def _rwkv_scan_kernel(r_ref, w_ref, k_ref, v_ref, a_ref, b_ref, *rest, hd, has_state):
    if has_state:
        s0_ref, o_ref, sout_ref, st_ref, ob_ref, vs_ref, row_ref = rest
    else:
        o_ref, sout_ref, st_ref, ob_ref, vs_ref, row_ref = rest
        s0_ref = None
    tb = pl.program_id(1)
    tc, width = r_ref.shape
    pw = 2 * hd
    npairs = width // pw

    @pl.when(tb == 0)
    def _():
        for p in range(npairs):
            if has_state:
                st_ref[p] = jnp.concatenate([s0_ref[0, 2 * p], s0_ref[0, 2 * p + 1]], axis=1)
            else:
                st_ref[p] = jnp.zeros((hd, pw), F32)

    half = npairs // 2
    ones2 = _block_ones(2 * pw, hd, BF16)
    sub = lax.broadcasted_iota(jnp.int32, (hd, pw), 0)
    lane = lax.broadcasted_iota(jnp.int32, (hd, pw), 1)
    diags = [(((lane % hd) - sub + hd) % hd == i).astype(F32).astype(BF16) for i in range(3)]
    lane8 = lax.broadcasted_iota(jnp.int32, (8, pw), 1)
    row16 = lax.broadcasted_iota(jnp.int32, (8, 2 * pw), 0)
    lane16 = lax.broadcasted_iota(jnp.int32, (8, 2 * pw), 1)
    sel2 = row16 == (lane16 // hd)

    def shift_in_head(x, i):
        return jnp.where(lane8 % hd >= i, pltpu.roll(x, i, 1), pltpu.roll(x, pw - hd + i, 1))

    def group(tg, carry):
        rows8 = pl.ds(pl.multiple_of(tg * 8, 8), 8)
        for idx, ref in enumerate((a_ref, w_ref, k_ref, b_ref, r_ref)):
            row_ref[idx] = ref[rows8, :]
        for p in range(npairs):
            cols = pl.ds(p * pw, pw)
            v8 = v_ref[rows8, cols]
            v1 = v8.astype(BF16).astype(F32)
            v2 = (v8 - v1).astype(BF16).astype(F32)
            vs_ref[0, :, cols] = v1
            vs_ref[1, :, cols] = shift_in_head(v2, 1)
            vs_ref[2, :, cols] = shift_in_head((v8 - v1) - v2, 2)

        def row(idx, p, j):
            return row_ref[idx, j:j + 1, pl.ds(p * pw, pw)]

        def side_by_side(tiles):
            return jnp.concatenate([jnp.concatenate([tiles[p], tiles[p + half]], axis=1) for p in range(half)],
                                   axis=0)

        def split(x, p):
            q, c = p % half, p // half
            return x[q * hd:(q + 1) * hd, c * pw:(c + 1) * pw]

        def moved_v(j):
            xs = []
            for p in range(npairs):
                x = None
                for i in range(3):
                    term = diags[i] * vs_ref[i, j:j + 1, pl.ds(p * pw, pw)].astype(BF16)
                    x = term if x is None else x + term
                xs.append(x)
            return _dot(side_by_side(xs), ones2)

        def emit_out(j, states):
            for p in range(half):
                r2 = jnp.concatenate([row(4, p, j), row(4, p + half, j)], axis=1)
                rsel = jnp.where(sel2, r2, 0.0).astype(BF16)
                o8 = _dot_nt(rsel, jnp.concatenate([states[p], states[p + half]], axis=1))
                ob_ref[p, j:j + 1, 0:hd] = o8[0:1, :]
                ob_ref[p, j:j + 1, hd:pw] = o8[1:2, :]
                ob_ref[p + half, j:j + 1, 0:hd] = o8[2:3, :]
                ob_ref[p + half, j:j + 1, hd:pw] = o8[3:4, :]

        vc_next = moved_v(0)
        states = None
        for j in range(8):
            old = [st_ref[p] for p in range(npairs)]
            sa = _dot(side_by_side([old[p] * row(0, p, j) for p in range(npairs)]).astype(BF16), ones2)
            if states is not None:
                emit_out(j - 1, states)
            vc = vc_next
            if j < 7:
                vc_next = moved_v(j + 1)
            states = []
            for p in range(npairs):
                s = old[p] * row(1, p, j) + split(sa, p) * row(3, p, j) + split(vc, p) * row(2, p, j)
                st_ref[p] = s
                states.append(s.astype(BF16))
        emit_out(7, states)
        for p in range(npairs):
            o_ref[rows8, pl.ds(p * pw, pw)] = ob_ref[p]
        return carry

    lax.fori_loop(0, tc // 8, group, 0)

    @pl.when(tb == pl.num_programs(1) - 1)
    def _():
        for p in range(npairs):
            s = st_ref[p]
            sout_ref[0, 2 * p] = s[:, :hd]
            sout_ref[0, 2 * p + 1] = s[:, hd:]


def _rwkv_scan(ops, nseq, t, row0, tc, heads, hd, state):
    width = heads * hd
    assert t % tc == 0 and row0 % tc == 0
    nt = t // tc
    blk = pl.BlockSpec((tc, width), lambda s, j: (row0 // tc + s * nt + j, 0))
    st = pl.BlockSpec((1, heads, hd, hd), lambda s, j: (s, 0, 0, 0))
    in_specs = [blk] * 6
    args = list(ops)
    if state is not None:
        in_specs.append(st)
        args.append(state)
    return pl.pallas_call(
        functools.partial(_rwkv_scan_kernel, hd=hd, has_state=state is not None),
        grid=(nseq, nt),
        in_specs=in_specs,
        out_specs=[pl.BlockSpec((tc, width), lambda s, j: (s * nt + j, 0)), st],
        out_shape=[jax.ShapeDtypeStruct((nseq * t, width), F32),
                   jax.ShapeDtypeStruct((nseq, heads, hd, hd), F32)],
        scratch_shapes=[pltpu.VMEM((heads // 2, hd, 2 * hd), F32), pltpu.VMEM((heads // 2, 8, 2 * hd), F32),
                        pltpu.VMEM((3, 8, width), F32), pltpu.VMEM((5, 8, width), F32)],
        compiler_params=_params(("parallel", "arbitrary")),
        name="rwkv_scan_state" if state is not None else "rwkv_scan",
    )(*args)


def _rwkv_lanes_kernel(r_ref, w_ref, k_ref, v_ref, a_ref, b_ref, s0_ref, o_ref, sout_ref, x_ref, ot_ref, *, dt, hd):
    nseq = s0_ref.shape[-1]
    nh = s0_ref.shape[0]
    for idx, ref in enumerate((a_ref, w_ref, k_ref, b_ref, r_ref, v_ref)):
        for t in range(dt):
            x_ref[idx, t] = ref[pl.ds(t, nseq, stride=dt), :].T

    for hh in range(nh):
        ch = slice(hh * hd, (hh + 1) * hd)

        def vgroup(vg, carry):
            v0 = pl.multiple_of(hh * hd + vg * 8, 8)
            v8 = [x_ref[5, t, pl.ds(v0, 8), :] for t in range(dt)]
            outs = [[None] * 8 for _ in range(dt)]
            for j in range(8):
                s = s0_ref[hh, vg * 8 + j]
                for t in range(dt):
                    sa = jnp.sum(s * x_ref[0, t, ch, :], axis=0, keepdims=True)
                    s = s * x_ref[1, t, ch, :] + sa * x_ref[3, t, ch, :] + v8[t][j:j + 1] * x_ref[2, t, ch, :]
                    outs[t][j] = jnp.sum(s * x_ref[4, t, ch, :], axis=0, keepdims=True)
                sout_ref[hh, vg * 8 + j] = s
            for t in range(dt):
                ot_ref[t, pl.ds(v0, 8), :] = jnp.concatenate(outs[t], axis=0)
            return carry

        lax.fori_loop(0, hd // 8, vgroup, 0)

    for t in range(dt):
        o_ref[pl.ds(t, nseq, stride=dt), :] = ot_ref[t].T


def _rwkv_scan_lanes(ops, nseq, dt, row0, heads, hd, state_lanes):
    width = heads * hd
    rows = nseq * dt
    nh = LANES // hd
    assert row0 % rows == 0 and heads % nh == 0
    blk = pl.BlockSpec((rows, nh * hd), lambda g: (row0 // rows, g))
    st = pl.BlockSpec((nh, hd, hd, nseq), lambda g: (g, 0, 0, 0))
    return pl.pallas_call(
        functools.partial(_rwkv_lanes_kernel, dt=dt, hd=hd),
        grid=(heads // nh,),
        in_specs=[blk] * 6 + [st],
        out_specs=[pl.BlockSpec((rows, nh * hd), lambda g: (0, g)), st],
        out_shape=[jax.ShapeDtypeStruct((rows, width), F32),
                   jax.ShapeDtypeStruct((heads, hd, hd, nseq), F32)],
        scratch_shapes=[pltpu.VMEM((6, dt, nh * hd, nseq), F32), pltpu.VMEM((dt, nh * hd, nseq), F32)],
        compiler_params=_params(("parallel",)),
        name="rwkv_scan_lanes",
    )(*ops, state_lanes)


def _rwkv_post_kernel(op_ref, os_ref, bonus_ref, g_ref, lg_ref, lb_ref, out_ref, *, hd, first_blocks):
    cn = out_ref.shape[1]
    seg = _block_ones(LANES, hd, BF16)

    def segmean(x):
        return jnp.concatenate([_dot_exact01(x[:, c:c + LANES], seg) for c in range(0, cn, LANES)],
                               axis=1) * (1.0 / hd)

    def body(o):
        mean = segmean(o)
        cen = o - mean
        var = segmean(cen * cen)
        y = cen * lax.rsqrt(var + LNX_EPS) * lg_ref[...] + lb_ref[...]
        out_ref[...] = ((y + bonus_ref[...]) * g_ref[...]).astype(out_ref.dtype)

    _from_either(pl.program_id(0), first_blocks, op_ref, os_ref, body)


def _rwkv_post(o_raw_p, o_raw_s, bonus, g, lnx_g, lnx_b, hd, rb=256, cn=512):
    n, rw = bonus.shape
    rb, cn = _pick(math.gcd(o_raw_p.shape[0], o_raw_s.shape[0]), rb), _pick(rw, cn)
    first_blocks = o_raw_p.shape[0] // rb
    blk = pl.BlockSpec((rb, cn), lambda i, j: (i, j))
    vec = pl.BlockSpec((1, cn), lambda i, j: (0, j))
    pair = [pl.BlockSpec((rb, cn), lambda i, j: (jnp.minimum(i, first_blocks - 1), j)),
            pl.BlockSpec((rb, cn), lambda i, j: (jnp.maximum(i - first_blocks, 0), j))]
    return pl.pallas_call(
        functools.partial(_rwkv_post_kernel, hd=hd, first_blocks=first_blocks),
        grid=(n // rb, rw // cn),
        in_specs=pair + [blk, blk, vec, vec],
        out_specs=blk,
        out_shape=jax.ShapeDtypeStruct((n, rw), BF16),
        compiler_params=_params(("parallel", "parallel")),
        name="rwkv_post",
    )(o_raw_p, o_raw_s, bonus, g, lnx_g, lnx_b)


def _mmt_kernel(wt_ref, x_ref, o_ref):
    o_ref[...] = _dot(wt_ref[...].astype(BF16), x_ref[...])


def _matmul_t(wt, xt, tm=512, tn=1024, name="matmul_t"):
    m, k = wt.shape
    n = xt.shape[1]
    tm, tn = _pick(m, tm), _pick(n, tn)
    return pl.pallas_call(
        _mmt_kernel,
        grid=(n // tn, m // tm),
        in_specs=[pl.BlockSpec((tm, k), lambda j, i: (i, 0)),
                  pl.BlockSpec((k, tn), lambda j, i: (0, j))],
        out_specs=pl.BlockSpec((tm, tn), lambda j, i: (i, j)),
        out_shape=jax.ShapeDtypeStruct((m, n), F32),
        compiler_params=_params(("parallel", "parallel")),
        name=name,
    )(wt, xt)


def _take_top(s, k):
    n = s.shape[0]
    row = lax.broadcasted_iota(jnp.int32, s.shape, 0).astype(F32)
    out = []
    for _ in range(k):
        m = jnp.max(s, axis=0, keepdims=True)
        first = jnp.min(jnp.where(s == m, row, float(n)), axis=0, keepdims=True)
        s = jnp.where(row == first, -jnp.inf, s)
        out.append(m)
    return out


def _peer_topk_kernel(q_ref, keys_ref, s0_ref, s1_ref, e0_ref, e1_ref, thr_ref, *, topk):
    kd = keys_ref.shape[3]
    cols = q_ref.shape[1]
    s0 = _dot(keys_ref[0, 0], q_ref[:kd, :], HI)
    s1 = _dot(keys_ref[0, 1], q_ref[kd:, :], HI)
    top0 = _take_top(s0, topk)
    top1 = _take_top(s1, topk)
    width = [topk // (a + 1) for a in range(topk)]
    start = [sum(width[:a]) for a in range(topk)]
    n_cand = sum(width)
    rows = -(-n_cand // 8) * 8
    r = lax.broadcasted_iota(jnp.int32, (rows, cols), 0)
    t0 = jnp.broadcast_to(top0[0], (rows, cols))
    first = jnp.zeros((rows, cols), jnp.int32)
    for a in range(1, topk):
        t0 = jnp.where(r >= start[a], top0[a], t0)
        first = jnp.where(r >= start[a], start[a], first)
    b_of_row = r - first
    t1 = jnp.broadcast_to(top1[0], (rows, cols))
    for b in range(1, topk):
        t1 = jnp.where(b_of_row == b, top1[b], t1)
    cand = jnp.where(r < n_cand, t0 + t1, -jnp.inf)
    best = _take_top(cand, topk)
    z = jnp.zeros((1, cols), F32)
    for b in best:
        z = z + jnp.exp(b - best[0])
    s0_ref[...] = s0
    s1_ref[...] = s1
    e0_ref[...] = jnp.exp(s0 - top0[0]) / z
    e1_ref[...] = jnp.exp(s1 - top1[0])
    thr_ref[0] = jnp.broadcast_to(best[topk - 1], thr_ref.shape[1:])


def _peer_topk(qt, sub_keys, tc=256):
    n = qt.shape[1]
    tc = _pick(n, tc)
    heads, _, nk, kd = sub_keys.shape
    blk = pl.BlockSpec((nk, tc), lambda i, h: (h, i))
    shp = jax.ShapeDtypeStruct((heads * nk, n), F32)
    return pl.pallas_call(
        functools.partial(_peer_topk_kernel, topk=PEER_TOPK),
        grid=(n // tc, heads),
        in_specs=[pl.BlockSpec((2 * kd, tc), lambda i, h: (h, i)),
                  pl.BlockSpec((1, 2, nk, kd), lambda i, h: (h, 0, 0, 0))],
        out_specs=[blk] * 4 + [pl.BlockSpec((1, 8, tc), lambda i, h: (h, 0, i))],
        out_shape=[shp] * 4 + [jax.ShapeDtypeStruct((heads, 8, n), F32)],
        compiler_params=_params(("parallel", "parallel")),
        name="peer_topk",
    )(qt, sub_keys)


def _gelu(x):
    return 0.5 * x * (1.0 + lax.erf(x * (2.0 ** -0.5)))


def _peer_dense_kernel(u_ref, ht_ref, vt_ref, s0_ref, e0_ref, s1_ref, e1_ref, thr_ref, o_ref, act_ref, coef_ref,
                       *, heads, nk, jc):
    e = pl.program_id(1)
    eb = u_ref.shape[0]
    d = vt_ref.shape[0]
    ni = eb // nk
    dq = d // ni

    @pl.when(e == 0)
    def _():
        o_ref[...] = jnp.zeros_like(o_ref)
        coef_ref[1] = jnp.zeros(coef_ref.shape[1:], coef_ref.dtype)

    act_ref[...] = _dot(u_ref[...], ht_ref[...])
    cur = e % 2
    prev = (e + 1) % 2
    thr = thr_ref[...]

    def one_i(ii, carry):
        orow = pl.ds(pl.multiple_of(ii * dq, dq), dq)
        o_ref[orow, :] += _dot(vt_ref[orow, :], coef_ref[prev])
        i_abs = jnp.minimum(e, pl.num_programs(1) - 2) * ni + ii
        grp = pl.multiple_of((i_abs // 8) * 8, 8)
        keep = lax.broadcasted_iota(jnp.int32, (8, s0_ref.shape[1]), 0) == i_abs % 8

        def pick(ref):
            return jnp.concatenate(
                [jnp.sum(jnp.where(keep, ref[pl.ds(h * nk + grp, 8), :], 0.0), axis=0, keepdims=True)
                 for h in range(heads)], axis=0)

        s0 = pick(s0_ref)
        e0 = pick(e0_ref)
        for j0 in range(0, nk, jc):
            g = None
            for h in range(heads):
                jrows = slice(h * nk + j0, h * nk + j0 + jc)
                hit = (s0[h:h + 1] + s1_ref[jrows, :]) >= thr[h:h + 1]
                term = jnp.where(hit, e0[h:h + 1] * e1_ref[jrows, :], 0.0)
                g = term if g is None else g + term
            arow = pl.ds(pl.multiple_of(ii * nk + j0, jc), jc)
            coef_ref[cur, arow, :] = (g * _gelu(act_ref[arow, :])).astype(coef_ref.dtype)
        return carry

    lax.fori_loop(0, ni, one_i, 0)


def _peer_dense(ht, u, vt, s0, e0, s1, e1, thr, heads, nk, tp=512, eb=512):
    d, n = ht.shape
    tp = _pick(n, tp)
    nblk = u.shape[0] // eb
    ni = eb // nk
    once = pl.Buffered(1)
    tok = pl.BlockSpec((heads * nk, tp), lambda i, e: (0, i), pipeline_mode=once)
    return pl.pallas_call(
        functools.partial(_peer_dense_kernel, heads=heads, nk=nk, jc=PEER_MASK_ROWS),
        grid=(n // tp, nblk + 1),
        in_specs=[pl.BlockSpec((eb, d), lambda i, e: (jnp.minimum(e, nblk - 1), 0)),
                  pl.BlockSpec((d, tp), lambda i, e: (0, i), pipeline_mode=once),
                  pl.BlockSpec((d, eb), lambda i, e: (0, jnp.maximum(e - 1, 0))),
                  tok, tok, tok, tok, pl.BlockSpec((heads, tp), lambda i, e: (0, i))],
        out_specs=pl.BlockSpec((d, tp), lambda i, e: (0, i)),
        out_shape=jax.ShapeDtypeStruct((d, n), F32),
        scratch_shapes=[pltpu.VMEM((eb, tp), F32), pltpu.VMEM((2, eb, tp), BF16)],
        compiler_params=_params(("parallel", "arbitrary")),
        name="peer_dense",
    )(u, ht, vt, s0, e0, s1, e1, thr)


def _layer(xp, xs, s_ret, s_rwkv, s_shift, cp, cs, p):
    nb, t, d = xp.shape
    db, dt, _ = xs.shape
    rows = _Rows(nb, t, db, dt)
    ret_heads, ret_hd = s_ret.shape[1], s_ret.shape[2]
    rw_heads, rw_hd = s_rwkv.shape[1], s_rwkv.shape[2]
    ret_w, rw = ret_heads * ret_hd, rw_heads * rw_hd
    w_lora, a_lora, g_lora = p['w2'].shape[0], p['a2'].shape[0], p['g2'].shape[0]
    rwkv_in = s_shift.shape[-1]
    ret_in = 4 * ret_w
    assert p['w_in'].shape[1] == ret_in + rwkv_in and rwkv_in == 3 * rw + w_lora + a_lora + g_lora

    xp, xs = xp.reshape(nb * t, d), xs.reshape(db * dt, d)
    mod = _ada(jnp.concatenate([cp, cs], axis=0), p['ada_w'], p['ada_b'])
    mod_ext = rows.extend(mod)

    h = _prenorm(rows, xp, xs, p['pre_mix_g'], mod_ext, d)

    tm = _pick(rows.n, 1024)
    w_in_t = p['w_in'].T
    proj_ret = _matmul(h, w_in_t, col_block0=0, n_out=ret_in, tm=tm, tn=512, name="in_proj_ret")
    proj_rw = _matmul(h, w_in_t, col_block0=ret_in // 512, n_out=rwkv_in, tm=tm, tn=512, name="in_proj_rwkv")

    o_ret_p, o_ret_s, sr_p, sr_s = _retention(rows, proj_ret, s_ret, ret_heads, ret_hd)

    tail_w = -(-(w_lora + a_lora + g_lora) // LANES) * LANES
    g2p = jnp.pad(p['g2'], ((0, tail_w - w_lora - a_lora - g_lora), (0, 0)))
    vec = lambda a: a.reshape(1, rw)
    r_, w_, k_, v_, a_, b_, g_, bonus = _rwkv_prep(
        rows, proj_rw, s_shift, p['shift_mu'], vec(p['w0']), p['w2'], vec(p['a0']), p['a2'], g2p,
        vec(p['k_k']), vec(p['k_a']), vec(p['r_k']), rw, rw_hd, g_lora)
    ops = (r_, w_, k_, v_, a_, b_)
    o_rw_p, sw_p = _rwkv_scan(ops, nb, t, 0, 256, rw_heads, rw_hd, None)
    o_rw_s, sw_lanes = _rwkv_scan_lanes(ops, db, dt, nb * t, rw_heads, rw_hd, jnp.transpose(s_rwkv, (1, 2, 3, 0)))
    sw_s = jnp.transpose(sw_lanes, (3, 0, 1, 2))
    o_rw = _rwkv_post(o_rw_p, o_rw_s, bonus, g_, vec(p['lnx_g']), vec(p['lnx_b']), rw_hd)

    mix = _matmul_out(o_ret_p, o_ret_s, o_rw, p['w_out'])
    x1, h2t = _postmix(rows, xp, xs, mix, p['post_mix_g'], p['pre_ffn_g'], mod_ext, d)

    heads, _, nk, kd = p['peer_sub_keys'].shape
    qt = _matmul_t(p['peer_wq'].T, h2t, name="peer_q")
    s0, s1, e0, e1, thr = _peer_topk(qt, p['peer_sub_keys'])
    eb = 512
    ni = eb // nk
    peer_t = _peer_dense(h2t, p['peer_u'].astype(BF16), p['peer_v'].T.astype(BF16),
                         s0, e0, s1, e1, thr[:, 0, :], heads, nk, eb=eb)
    yp = _final(rows, x1, peer_t, p['post_ffn_g'], mod_ext, d, 0, rows.prompt_blocks)
    ys = _final(rows, x1, peer_t, p['post_ffn_g'], mod_ext, d, rows.prompt_blocks, rows.n_blocks - rows.prompt_blocks)

    n_p = nb * t
    last_p = slice(t - 1, n_p, t)
    last_s = slice(n_p + dt - 1, None, dt)
    return (yp.reshape(nb, t, d), ys.reshape(db, dt, d), sr_p, sr_s, sw_p, sw_s, proj_rw[last_p], proj_rw[last_s])


def kernel(x_prompt, x_sample, state_ret, state_rwkv, state_shift, c_prompt, c_sample, ada_w, ada_b, pre_mix_g, post_mix_g, pre_ffn_g, post_ffn_g, w_in, shift_mu, w0, w2, a0, a2, g2, k_k, k_a, r_k, lnx_g, lnx_b, w_out, peer_wq, peer_sub_keys, peer_u, peer_v):
    depth = ada_w.shape[0]
    assert depth == 1, "prompt and sample tokens are stacked per layer; deeper stacks need per-layer restacking"
    prm = dict(ada_w=ada_w[0], ada_b=ada_b[0], pre_mix_g=pre_mix_g[0], post_mix_g=post_mix_g[0],
               pre_ffn_g=pre_ffn_g[0], post_ffn_g=post_ffn_g[0], w_in=w_in[0], shift_mu=shift_mu[0],
               w0=w0[0], w2=w2[0], a0=a0[0], a2=a2[0], g2=g2[0], k_k=k_k[0], k_a=k_a[0], r_k=r_k[0],
               lnx_g=lnx_g[0], lnx_b=lnx_b[0], w_out=w_out[0], peer_wq=peer_wq[0],
               peer_sub_keys=peer_sub_keys[0], peer_u=peer_u[0], peer_v=peer_v[0])
    yp, ys, sr_p, sr_s, sw_p, sw_s, ss_p, ss_s = _layer(
        x_prompt, x_sample, state_ret[0], state_rwkv[0], state_shift[0], c_prompt, c_sample, prm)
    sd, wd, hd = state_ret.dtype, state_rwkv.dtype, state_shift.dtype
    return (yp, ys, sr_p[None].astype(sd), sr_s[None].astype(sd), sw_p[None].astype(wd), sw_s[None].astype(wd),
            ss_p[None].astype(hd), ss_s[None].astype(hd))
```

```python
import functools
import math

import jax
import jax.numpy as jnp
from jax import lax
from jax.experimental import pallas as pl
from jax.experimental.pallas import tpu as pltpu

PAST_LEN = 16384
RET_CHUNK = 128
ROPE_BASE = 10000.0
LNX_EPS = 64e-5
NORM_EPS = 1e-6
PEER_TOPK = 16
PEER_MASK_ROWS = 16
RET_HEADS_PER_STEP = 4

LANES = 128
ROW_BLOCK = 128
VMEM_LIMIT = 56 * 1024 * 1024

HI = lax.Precision.HIGHEST
F32 = jnp.float32
BF16 = jnp.bfloat16


def _pick(n, target, mult=LANES):
    best = None
    for c in range(mult, min(n, target) + 1, mult):
        if n % c == 0:
            best = c
    assert best is not None, (n, target, mult)
    return best


def _params(sem, vmem=VMEM_LIMIT):
    return pltpu.CompilerParams(dimension_semantics=sem, vmem_limit_bytes=vmem)


def _dot(a, b, precision=None):
    return jnp.dot(a, b, preferred_element_type=F32, precision=precision)


def _dot_nt(a, b, precision=None):
    return lax.dot_general(a, b, (((1,), (1,)), ((), ())), preferred_element_type=F32, precision=precision)


def _dot_tn(a, b, precision=None):
    return lax.dot_general(a, b, (((0,), (0,)), ((), ())), preferred_element_type=F32, precision=precision)


def _split_bf16(x, terms):
    out = []
    for _ in range(terms - 1):
        t = x.astype(BF16)
        out.append(t)
        x = x - t.astype(F32)
    return out + [x.astype(BF16)]


def _dot_split(x, w_terms):
    xh, xl = _split_bf16(x, 2)
    wh, wl = w_terms
    return _dot(xh, wh) + (_dot(xl, wh) + _dot(xh, wl))


def _dot_exact01(x, ones):
    return sum(_dot(t, ones) for t in _split_bf16(x, 3))


def _block_ones(n, seg, dtype):
    r = lax.broadcasted_iota(jnp.int32, (n, n), 0) // seg
    c = lax.broadcasted_iota(jnp.int32, (n, n), 1) // seg
    return (r == c).astype(dtype)


def _ada_kernel(c_ref, w_ref, b_ref, o_ref):
    c = c_ref[...]
    sc = (c * jax.nn.sigmoid(c)).astype(BF16)
    o_ref[...] = _dot(sc, w_ref[...].astype(BF16)) + b_ref[...]


def _ada(c_all, ada_w, ada_b, tn=512):
    m, d = c_all.shape
    n = ada_w.shape[1]
    return pl.pallas_call(
        _ada_kernel,
        grid=(n // tn,),
        in_specs=[pl.BlockSpec((m, d), lambda j: (0, 0)),
                  pl.BlockSpec((d, tn), lambda j: (0, j)),
                  pl.BlockSpec((1, tn), lambda j: (0, j))],
        out_specs=pl.BlockSpec((m, tn), lambda j: (0, j)),
        out_shape=jax.ShapeDtypeStruct((m, n), F32),
        compiler_params=_params(("arbitrary",)),
        name="ada",
    )(c_all, ada_w, ada_b.reshape(1, n))


def _expand_slots(m, rows):
    slots, _, d = m.shape
    return jnp.broadcast_to(m, (slots, rows // slots, d)).reshape(rows, d)


def _rms(x, g):
    return x * lax.rsqrt(jnp.mean(x * x, axis=-1, keepdims=True) + NORM_EPS) * g


def _from_either(i, first_blocks, a_ref, b_ref, body):
    @pl.when(i < first_blocks)
    def _():
        body(a_ref[...])

    @pl.when(i >= first_blocks)
    def _():
        body(b_ref[...])


def _two_specs(block, first_blocks):
    return [pl.BlockSpec(block, lambda i, *_: (jnp.minimum(i, first_blocks - 1), 0)),
            pl.BlockSpec(block, lambda i, *_: (jnp.maximum(i - first_blocks, 0), 0))]


def _prenorm_kernel(xp_ref, xs_ref, g_ref, sh_ref, sc_ref, h_ref, *, prompt_blocks):
    def body(x):
        rows = x.shape[0]
        y = _rms(x, g_ref[...])
        h = y * (1.0 + _expand_slots(sc_ref[...], rows)) + _expand_slots(sh_ref[...], rows)
        h_ref[...] = h.astype(h_ref.dtype)

    _from_either(pl.program_id(0), prompt_blocks, xp_ref, xs_ref, body)


def _postmix_kernel(xp_ref, xs_ref, m_ref, gpost_ref, gpre_ref, gate_ref, sh_ref, sc_ref, x1_ref, ht_ref,
                    *, prompt_blocks):
    def body(x):
        rows = x.shape[0]
        x1 = x + _expand_slots(gate_ref[...], rows) * _rms(m_ref[...], gpost_ref[...])
        x1_ref[...] = x1
        y = _rms(x1, gpre_ref[...])
        h = y * (1.0 + _expand_slots(sc_ref[...], rows)) + _expand_slots(sh_ref[...], rows)
        ht_ref[...] = h.T.astype(ht_ref.dtype)

    _from_either(pl.program_id(0), prompt_blocks, xp_ref, xs_ref, body)


def _final_kernel(x_ref, mt_ref, gpost_ref, gate_ref, y_ref):
    x = x_ref[...]
    rows = x.shape[0]
    y_ref[...] = x + _expand_slots(gate_ref[...], rows) * _rms(mt_ref[...].T, gpost_ref[...])


class _Rows:
    def __init__(self, nb, t, db, dt):
        self.n_prompt = nb * t
        self.n_sample = db * dt
        self.n = self.n_prompt + self.n_sample
        self.rb = ROW_BLOCK
        assert t % self.rb == 0 and self.rb % dt == 0 and self.n_sample % self.rb == 0
        self.slots = self.rb // dt
        self.prompt_blocks = self.n_prompt // self.rb
        self.blocks_per_seq = t // self.rb
        self.n_blocks = self.n // self.rb
        self.nb, self.t, self.db, self.dt = nb, t, db, dt

    def slot_block(self, i):
        return jnp.where(i < self.prompt_blocks, i // self.blocks_per_seq, i - self.prompt_blocks + self.nb)

    def extend(self, m):
        mp = jnp.repeat(m[: self.nb], self.slots, axis=0)
        return jnp.concatenate([mp, m[self.nb:]], axis=0)[:, None, :]


def _mod_spec(rows, d, chunk, block0=0):
    return pl.BlockSpec((rows.slots, 1, d), lambda i: (rows.slot_block(i + block0), 0, chunk))


def _prenorm(rows, xp, xs, g, mod_ext, d):
    row = pl.BlockSpec((rows.rb, d), lambda i: (i, 0))
    vec = pl.BlockSpec((1, d), lambda i: (0, 0))
    return pl.pallas_call(
        functools.partial(_prenorm_kernel, prompt_blocks=rows.prompt_blocks),
        grid=(rows.n_blocks,),
        in_specs=_two_specs((rows.rb, d), rows.prompt_blocks) + [vec, _mod_spec(rows, d, 0), _mod_spec(rows, d, 1)],
        out_specs=row,
        out_shape=jax.ShapeDtypeStruct((rows.n, d), BF16),
        compiler_params=_params(("parallel",)),
        name="prenorm",
    )(xp, xs, g.reshape(1, d), mod_ext, mod_ext)


def _postmix(rows, xp, xs, mix, gpost, gpre, mod_ext, d):
    row = pl.BlockSpec((rows.rb, d), lambda i: (i, 0))
    vec = pl.BlockSpec((1, d), lambda i: (0, 0))
    return pl.pallas_call(
        functools.partial(_postmix_kernel, prompt_blocks=rows.prompt_blocks),
        grid=(rows.n_blocks,),
        in_specs=_two_specs((rows.rb, d), rows.prompt_blocks)
        + [row, vec, vec, _mod_spec(rows, d, 2), _mod_spec(rows, d, 3), _mod_spec(rows, d, 4)],
        out_specs=[row, pl.BlockSpec((d, rows.rb), lambda i: (0, i))],
        out_shape=[jax.ShapeDtypeStruct((rows.n, d), F32), jax.ShapeDtypeStruct((d, rows.n), BF16)],
        compiler_params=_params(("parallel",)),
        name="postmix",
    )(xp, xs, mix, gpost.reshape(1, d), gpre.reshape(1, d), mod_ext, mod_ext, mod_ext)


def _final(rows, x1, peer_t, gpost, mod_ext, d, block0, n_blocks):
    vec = pl.BlockSpec((1, d), lambda i: (0, 0))
    return pl.pallas_call(
        _final_kernel,
        grid=(n_blocks,),
        in_specs=[pl.BlockSpec((rows.rb, d), lambda i: (i + block0, 0)),
                  pl.BlockSpec((d, rows.rb), lambda i: (0, i + block0)),
                  vec, _mod_spec(rows, d, 5, block0)],
        out_specs=pl.BlockSpec((rows.rb, d), lambda i: (i, 0)),
        out_shape=jax.ShapeDtypeStruct((n_blocks * rows.rb, d), F32),
        compiler_params=_params(("parallel",)),
        name="final",
    )(x1, peer_t, gpost.reshape(1, d), mod_ext)


def _mm_kernel(a_ref, wt_ref, o_ref, wbf_ref):
    @pl.when(pl.program_id(1) == 0)
    def _():
        wbf_ref[...] = wt_ref[...].astype(BF16)

    o_ref[...] = _dot_nt(a_ref[...], wbf_ref[...]).astype(o_ref.dtype)


def _matmul(a, wt, *, col_block0, n_out, tm, tn, name):
    m, k = a.shape
    return pl.pallas_call(
        _mm_kernel,
        grid=(pl.cdiv(n_out, tn), m // tm),
        in_specs=[pl.BlockSpec((tm, k), lambda j, i: (i, 0)),
                  pl.BlockSpec((tn, k), lambda j, i: (j + col_block0, 0))],
        out_specs=pl.BlockSpec((tm, tn), lambda j, i: (i, j)),
        out_shape=jax.ShapeDtypeStruct((m, n_out), F32),
        scratch_shapes=[pltpu.VMEM((tn, k), BF16)],
        compiler_params=_params(("parallel", "arbitrary")),
        name=name,
    )(a, wt)


def _mm2_kernel(ap_ref, as_ref, b_ref, wa_ref, wb_ref, o_ref, *, first_blocks):
    def body(a):
        o_ref[...] = _dot(a, wa_ref[...].astype(BF16)) + _dot(b_ref[...], wb_ref[...].astype(BF16))

    _from_either(pl.program_id(1), first_blocks, ap_ref, as_ref, body)


def _matmul_out(o_ret_p, o_ret_s, o_rw, w_out, tn=512):
    m, kb = o_rw.shape
    ka = o_ret_p.shape[1]
    tm = _pick(math.gcd(o_ret_p.shape[0], o_ret_s.shape[0]), 1024)
    first_blocks = o_ret_p.shape[0] // tm
    n = w_out.shape[1]
    tn = _pick(n, tn)
    assert ka == kb
    pair = [pl.BlockSpec((tm, ka), lambda j, i: (jnp.minimum(i, first_blocks - 1), 0)),
            pl.BlockSpec((tm, ka), lambda j, i: (jnp.maximum(i - first_blocks, 0), 0))]
    return pl.pallas_call(
        functools.partial(_mm2_kernel, first_blocks=first_blocks),
        grid=(n // tn, m // tm),
        in_specs=pair + [pl.BlockSpec((tm, kb), lambda j, i: (i, 0)),
                         pl.BlockSpec((ka, tn), lambda j, i: (0, j)),
                         pl.BlockSpec((kb, tn), lambda j, i: (1, j))],
        out_specs=pl.BlockSpec((tm, tn), lambda j, i: (i, j)),
        out_shape=jax.ShapeDtypeStruct((m, n), F32),
        compiler_params=_params(("parallel", "parallel")),
        name="out_proj",
    )(o_ret_p, o_ret_s, o_rw, w_out, w_out)


def _rope_tables(pos, hd):
    half = hd // 2
    inv = ROPE_BASE ** (-jnp.arange(half, dtype=F32) / half)
    ang = pos.astype(F32)[:, None] * inv[None, :]
    cos, sin = jnp.cos(ang), jnp.sin(ang)
    return jnp.concatenate([cos, cos], axis=-1), jnp.concatenate([-sin, sin], axis=-1)


def _rope(x, cos, sin_signed):
    half = x.shape[-1] // 2
    return x * cos + pltpu.roll(x, half, 1) * sin_signed


def _ret_finish(o, gate):
    o = o * lax.rsqrt(jnp.mean(o * o, axis=-1, keepdims=True) + NORM_EPS)
    return o * (gate * jax.nn.sigmoid(gate))


def _ret_prompt_kernel(q_ref, k_ref, v_ref, g_ref, cos_ref, sin_ref, lg_ref, o_ref, s_ref, *, chunk, hd):
    t = q_ref.shape[0]
    nh = q_ref.shape[1] // hd
    ri = lax.broadcasted_iota(jnp.int32, (chunk, chunk), 0)
    ci = lax.broadcasted_iota(jnp.int32, (chunk, chunk), 1)
    causal = ri >= ci
    diff = jnp.where(causal, ri - ci, 0).astype(F32)
    rowf = lax.broadcasted_iota(jnp.int32, (chunk, hd), 0).astype(F32)
    lgs = [lg_ref[i, 0:1, :] for i in range(nh)]
    masks = [jnp.where(causal, jnp.exp(diff * lg), 0.0) for lg in lgs]
    q_decs = [jnp.exp((rowf + 1.0) * lg) for lg in lgs]
    k_decs = [jnp.exp((chunk - 1.0 - rowf) * lg) for lg in lgs]
    c_decs = [jnp.exp(float(chunk) * lg) for lg in lgs]

    def step(c, states):
        r0 = pl.multiple_of(c * chunk, chunk)
        rows = pl.ds(r0, chunk)
        cos, sin = cos_ref[rows, :], sin_ref[rows, :]
        cols = [slice(i * hd, (i + 1) * hd) for i in range(nh)]
        qs = [_rope(q_ref[rows, cl], cos, sin) for cl in cols]
        ks = [_rope(k_ref[rows, cl], cos, sin) * (hd ** -0.5) for cl in cols]
        vs = [v_ref[rows, cl] for cl in cols]
        atts = [_dot_nt(q.astype(BF16), k.astype(BF16)) * m for q, k, m in zip(qs, ks, masks)]
        cross = [_dot((q * qd).astype(BF16), s.astype(BF16)) for q, qd, s in zip(qs, q_decs, states)]
        inner = [_dot(a.astype(BF16), v.astype(BF16)) for a, v in zip(atts, vs)]
        new = [s * cd + _dot_tn(k * kd, v, HI) for s, cd, k, kd, v in zip(states, c_decs, ks, k_decs, vs)]
        for i, cl in enumerate(cols):
            o_ref[rows, cl] = _ret_finish(inner[i] + cross[i], g_ref[rows, cl]).astype(o_ref.dtype)
        return tuple(new)

    final = lax.fori_loop(0, t // chunk, step, tuple(jnp.zeros((hd, hd), F32) for _ in range(nh)))
    for i in range(nh):
        s_ref[0, i] = final[i]


def _ret_sample_kernel(q_ref, k_ref, v_ref, g_ref, cos_ref, sin_ref, lg_ref, s0_ref, o_ref, s_ref, *, dt):
    rows, hd = q_ref.shape
    nseq = rows // dt
    lg = lg_ref[0, 0:1, :]
    ri = lax.broadcasted_iota(jnp.int32, (rows, rows), 0)
    ci = lax.broadcasted_iota(jnp.int32, (rows, rows), 1)
    ok = (ri >= ci) & ((ri // dt) == (ci // dt))
    diff = jnp.where(ok, ri - ci, 0).astype(F32)
    mask = jnp.where(ok, jnp.exp(diff * lg), 0.0)
    rown = lax.broadcasted_iota(jnp.int32, (rows, hd), 0)
    posf = (rown % dt).astype(F32)
    q_dec = jnp.exp((posf + 1.0) * lg)
    k_dec = jnp.exp((dt - 1.0 - posf) * lg)
    c_dec = jnp.exp(float(dt) * lg)
    cos, sin = cos_ref[...], sin_ref[...]
    qc = _rope(q_ref[...], cos, sin)
    kc = _rope(k_ref[...], cos, sin) * (hd ** -0.5)
    vc = v_ref[...]
    att = _dot_nt(qc.astype(BF16), kc.astype(BF16)) * mask
    inner = _dot(att.astype(BF16), vc.astype(BF16))
    qd = qc * q_dec
    kd = kc * k_dec
    seq = rown // dt
    cross = jnp.zeros((rows, hd), F32)
    for s in range(nseq):
        s0 = s0_ref[s, 0]
        mine = seq == s
        cross = cross + _dot(jnp.where(mine, qd, 0.0).astype(BF16), s0.astype(BF16))
        s_ref[s, 0] = s0 * c_dec + _dot_tn(jnp.where(mine, kd, 0.0), vc, HI)
    o_ref[...] = _ret_finish(inner + cross, g_ref[...]).astype(o_ref.dtype)


def _retention(rows, proj, state_ret, ret_heads, hd):
    nb, t, db, dt = rows.nb, rows.t, rows.db, rows.dt
    lg = jnp.log1p(-jnp.exp2(-5.0 - jnp.arange(ret_heads, dtype=F32)))
    lg_tab = jnp.broadcast_to(lg[:, None, None], (ret_heads, 8, hd))
    lg_spec = pl.BlockSpec((1, 8, hd), lambda b, h: (h, 0, 0))
    assert hd == RET_CHUNK and t % RET_CHUNK == 0 and dt % RET_CHUNK != 0

    cos_p, sin_p = _rope_tables(jnp.arange(t, dtype=jnp.int32), hd)
    nh = RET_HEADS_PER_STEP if ret_heads % RET_HEADS_PER_STEP == 0 else 1
    hsteps = ret_heads // nh
    col = lambda part: pl.BlockSpec((t, nh * hd), lambda b, h: (b, h + part * hsteps))
    tab = pl.BlockSpec((t, hd), lambda b, h: (0, 0))
    o_p, s_p = pl.pallas_call(
        functools.partial(_ret_prompt_kernel, chunk=RET_CHUNK, hd=hd),
        grid=(nb, hsteps),
        in_specs=[col(0), col(1), col(2), col(3), tab, tab, pl.BlockSpec((nh, 8, hd), lambda b, h: (h, 0, 0))],
        out_specs=[pl.BlockSpec((t, nh * hd), lambda b, h: (b, h)),
                   pl.BlockSpec((1, nh, hd, hd), lambda b, h: (b, h, 0, 0))],
        out_shape=[jax.ShapeDtypeStruct((nb * t, ret_heads * hd), BF16),
                   jax.ShapeDtypeStruct((nb, ret_heads, hd, hd), F32)],
        compiler_params=_params(("parallel", "parallel")),
        name="ret_prompt",
    )(proj, proj, proj, proj, cos_p, sin_p, lg_tab)

    rb = RET_CHUNK
    nseq = rb // dt
    cos_s, sin_s = _rope_tables(PAST_LEN + jnp.arange(dt, dtype=jnp.int32), hd)
    cos_s, sin_s = jnp.tile(cos_s, (nseq, 1)), jnp.tile(sin_s, (nseq, 1))
    off = (nb * t) // rb
    col = lambda part: pl.BlockSpec((rb, hd), lambda b, h: (b + off, h + part * ret_heads))
    tab = pl.BlockSpec((rb, hd), lambda b, h: (0, 0))
    st = pl.BlockSpec((nseq, 1, hd, hd), lambda b, h: (b, h, 0, 0))
    o_s, s_s = pl.pallas_call(
        functools.partial(_ret_sample_kernel, dt=dt),
        grid=(db // nseq, ret_heads),
        in_specs=[col(0), col(1), col(2), col(3), tab, tab, lg_spec, st],
        out_specs=[pl.BlockSpec((rb, hd), lambda b, h: (b, h)), st],
        out_shape=[jax.ShapeDtypeStruct((db * dt, ret_heads * hd), BF16),
                   jax.ShapeDtypeStruct((db, ret_heads, hd, hd), F32)],
        compiler_params=_params(("parallel", "parallel")),
        name="ret_sample",
    )(proj, proj, proj, proj, cos_s, sin_s, lg_tab, state_ret)
    return o_p, o_s, s_p, s_s


def _shifted(x, first, tseq_mask):
    prev = pltpu.roll(x, 1, 0)
    return jnp.where(tseq_mask, first, prev)


def _rwkv_prep_kernel(pr_ref, pk_ref, pv_ref, pt_ref, sr_ref, sk_ref, sv_ref, st_ref,
                      mur_ref, muk_ref, muv_ref, mut_ref, w0_ref, w2h_ref, w2l_ref, a0_ref, a2h_ref, a2l_ref,
                      g2h_ref, g2l_ref,
                      kk_ref, ka_ref, rk_ref,
                      r_out, w_out, k_out, v_out, a_out, b_out, g_out, bonus_out, carry_ref, tcarry_ref,
                      *, dt, prompt_blocks, blocks_per_seq, w_lora, a_lora, g_lora, hd):
    i, j = pl.program_id(0), pl.program_id(1)
    rows, cn = pr_ref.shape
    rown = lax.broadcasted_iota(jnp.int32, (rows, 1), 0)
    sample_block = jnp.zeros((rows, 1), jnp.int32) + (i >= prompt_blocks).astype(jnp.int32)
    is_first = (rown == 0) | ((rown % dt == 0) & (sample_block == 1))
    seq_start = i % blocks_per_seq == 0

    @pl.when(i == 0)
    def _():
        for part in range(3):
            carry_ref[part, j] = jnp.zeros(carry_ref.shape[2:], F32)
        tcarry_ref[...] = jnp.zeros_like(tcarry_ref)

    def lerp(p_ref, s_ref, mu_ref, carried):
        p = p_ref[...]
        from_prompt = jnp.broadcast_to(jnp.where(seq_start, 0.0, carried), p.shape)
        first = jnp.where(sample_block == 1, _expand_slots(s_ref[...], rows), from_prompt)
        prev = _shifted(p, first, is_first)
        return p + mu_ref[...] * (prev - p)

    r = lerp(pr_ref, sr_ref, mur_ref, carry_ref[0, j, 0:1, :])
    kw = lerp(pk_ref, sk_ref, muk_ref, carry_ref[1, j, 0:1, :])
    vw = lerp(pv_ref, sv_ref, muv_ref, carry_ref[2, j, 0:1, :])
    tail = lerp(pt_ref, st_ref, mut_ref, tcarry_ref[0:1, :])
    for part, p_ref in enumerate((pr_ref, pk_ref, pv_ref)):
        carry_ref[part, j, 0:1, :] = p_ref[rows - 1:rows, :]

    @pl.when(j == pl.num_programs(1) - 1)
    def _():
        tcarry_ref[0:1, :] = pt_ref[rows - 1:rows, :]

    tw = tail.shape[1]
    wl = tail[:, :w_lora]
    al = tail[:, w_lora:w_lora + a_lora]
    glane = lax.broadcasted_iota(jnp.int32, (rows, tw - w_lora - a_lora), 1)
    gl = jnp.where(glane < g_lora, jax.nn.sigmoid(tail[:, w_lora + a_lora:]), 0.0)

    w_log = -jax.nn.softplus(-(w0_ref[...] + _dot_split(jnp.tanh(wl), (w2h_ref[...], w2l_ref[...])))) - 0.5
    decay = jnp.exp(-jnp.exp(w_log))
    a = jax.nn.sigmoid(a0_ref[...] + _dot_split(al, (a2h_ref[...], a2l_ref[...])))
    g = _dot_split(gl, (g2h_ref[...], g2l_ref[...]))

    seg = _block_ones(LANES, hd, BF16)

    def segsum(x):
        return jnp.concatenate([_dot_exact01(x[:, c:c + LANES], seg) for c in range(0, cn, LANES)], axis=1)

    kk = kw * kk_ref[...]
    kk = kk / jnp.maximum(jnp.sqrt(segsum(kk * kk)), 1e-12)
    kn = kw * (1.0 + (a - 1.0) * ka_ref[...])
    r_out[...] = r
    w_out[...] = decay
    k_out[...] = kn
    v_out[...] = vw
    a_out[...] = -kk
    b_out[...] = kk * a
    g_out[...] = g
    bonus_out[...] = segsum(r * kn * rk_ref[...]) * vw


def _hi_lo(w):
    hi = w.astype(BF16)
    return hi, (w - hi.astype(F32)).astype(BF16)


def _rwkv_prep(rows, proj, s_shift, mu, w0, w2, a0, a2, g2p, k_k, k_a, r_k_flat, rw, hd, g_lora, cn=512):
    n = rows.n
    cn = _pick(rw, cn)
    ncb = rw // cn
    tw = g2p.shape[0] + w2.shape[0] + a2.shape[0]
    assert (3 * rw) % tw == 0
    tb = (3 * rw) // tw
    sblk = lambda i: jnp.maximum(i - rows.prompt_blocks, 0)
    part = lambda p: pl.BlockSpec((rows.rb, cn), lambda i, j: (i, j + p * ncb))
    spart = lambda p: pl.BlockSpec((rows.slots, 1, cn), lambda i, j: (sblk(i), 0, j + p * ncb))
    mupart = lambda p: pl.BlockSpec((1, cn), lambda i, j: (0, j + p * ncb))
    vec = pl.BlockSpec((1, cn), lambda i, j: (0, j))
    lora = lambda k: pl.BlockSpec((k, cn), lambda i, j: (0, j))
    out = pl.BlockSpec((rows.rb, cn), lambda i, j: (i, j))
    kern = functools.partial(_rwkv_prep_kernel, dt=rows.dt, prompt_blocks=rows.prompt_blocks,
                             blocks_per_seq=rows.blocks_per_seq,
                             w_lora=w2.shape[0], a_lora=a2.shape[0], g_lora=g_lora, hd=hd)
    shift3 = s_shift[:, None, :]
    mu2 = mu.reshape(1, -1)
    return pl.pallas_call(
        kern,
        grid=(rows.n_blocks, ncb),
        in_specs=[part(0), part(1), part(2), pl.BlockSpec((rows.rb, tw), lambda i, j: (i, tb)),
                  spart(0), spart(1), spart(2), pl.BlockSpec((rows.slots, 1, tw), lambda i, j: (sblk(i), 0, tb)),
                  mupart(0), mupart(1), mupart(2), pl.BlockSpec((1, tw), lambda i, j: (0, tb)),
                  vec, lora(w2.shape[0]), lora(w2.shape[0]), vec, lora(a2.shape[0]), lora(a2.shape[0]),
                  lora(g2p.shape[0]), lora(g2p.shape[0]), vec, vec, vec],
        out_specs=[out] * 8,
        out_shape=[jax.ShapeDtypeStruct((n, rw), F32)] * 8,
        scratch_shapes=[pltpu.VMEM((3, ncb, 8, cn), F32), pltpu.VMEM((8, tw), F32)],
        compiler_params=_params(("arbitrary", "arbitrary")),
        name="rwkv_prep",
    )(proj, proj, proj, proj, shift3, shift3, shift3, shift3, mu2, mu2, mu2, mu2,
      w0, *_hi_lo(w2), a0, *_hi_lo(a2), *_hi_lo(g2p), k_k, k_a, r_k_flat)


def _rwkv_scan_kernel(r_ref, w_ref, k_ref, v_ref, a_ref, b_ref, o_ref, sout_ref, st_ref, ob_ref, vs_ref, row_ref,
                      *, hd):
    tb = pl.program_id(1)
    tc, width = r_ref.shape
    pw = 2 * hd
    npairs = width // pw

    @pl.when(tb == 0)
    def _():
        st_ref[...] = jnp.zeros_like(st_ref)

    half = npairs // 2
    ones2 = _block_ones(2 * pw, hd, BF16)
    sub = lax.broadcasted_iota(jnp.int32, (hd, pw), 0)
    lane = lax.broadcasted_iota(jnp.int32, (hd, pw), 1)
    diags = [(((lane % hd) - sub + hd) % hd == i).astype(F32).astype(BF16) for i in range(3)]
    lane8 = lax.broadcasted_iota(jnp.int32, (8, pw), 1)
    row16 = lax.broadcasted_iota(jnp.int32, (8, 2 * pw), 0)
    lane16 = lax.broadcasted_iota(jnp.int32, (8, 2 * pw), 1)
    sel2 = row16 == (lane16 // hd)

    def shift_in_head(x, i):
        return jnp.where(lane8 % hd >= i, pltpu.roll(x, i, 1), pltpu.roll(x, pw - hd + i, 1))

    def group(tg, carry):
        rows8 = pl.ds(pl.multiple_of(tg * 8, 8), 8)
        for idx, ref in enumerate((a_ref, w_ref, k_ref, b_ref, r_ref)):
            row_ref[idx] = ref[rows8, :]
        for p in range(npairs):
            cols = pl.ds(p * pw, pw)
            v8 = v_ref[rows8, cols]
            v1 = v8.astype(BF16).astype(F32)
            v2 = (v8 - v1).astype(BF16).astype(F32)
            vs_ref[0, :, cols] = v1
            vs_ref[1, :, cols] = shift_in_head(v2, 1)
            vs_ref[2, :, cols] = shift_in_head((v8 - v1) - v2, 2)

        def row(idx, p, j):
            return row_ref[idx, j:j + 1, pl.ds(p * pw, pw)]

        def side_by_side(tiles):
            return jnp.concatenate([jnp.concatenate([tiles[p], tiles[p + half]], axis=1) for p in range(half)],
                                   axis=0)

        def split(x, p):
            q, c = p % half, p // half
            return x[q * hd:(q + 1) * hd, c * pw:(c + 1) * pw]

        def moved_v(j):
            xs = []
            for p in range(npairs):
                x = None
                for i in range(3):
                    term = diags[i] * vs_ref[i, j:j + 1, pl.ds(p * pw, pw)].astype(BF16)
                    x = term if x is None else x + term
                xs.append(x)
            return _dot(side_by_side(xs), ones2)

        def emit_out(j, states):
            for p in range(half):
                r2 = jnp.concatenate([row(4, p, j), row(4, p + half, j)], axis=1)
                rsel = jnp.where(sel2, r2, 0.0).astype(BF16)
                o8 = _dot_nt(rsel, jnp.concatenate([states[p], states[p + half]], axis=1))
                ob_ref[p, j:j + 1, 0:hd] = o8[0:1, :]
                ob_ref[p, j:j + 1, hd:pw] = o8[1:2, :]
                ob_ref[p + half, j:j + 1, 0:hd] = o8[2:3, :]
                ob_ref[p + half, j:j + 1, hd:pw] = o8[3:4, :]

        vc_next = moved_v(0)
        states = None
        for j in range(8):
            old = [st_ref[p] for p in range(npairs)]
            sa = _dot(side_by_side([old[p] * row(0, p, j) for p in range(npairs)]).astype(BF16), ones2)
            if states is not None:
                emit_out(j - 1, states)
            vc = vc_next
            if j < 7:
                vc_next = moved_v(j + 1)
            states = []
            for p in range(npairs):
                s = old[p] * row(1, p, j) + split(sa, p) * row(3, p, j) + split(vc, p) * row(2, p, j)
                st_ref[p] = s
                states.append(s.astype(BF16))
        emit_out(7, states)
        for p in range(npairs):
            o_ref[rows8, pl.ds(p * pw, pw)] = ob_ref[p]
        return carry

    lax.fori_loop(0, tc // 8, group, 0)

    @pl.when(tb == pl.num_programs(1) - 1)
    def _():
        for p in range(npairs):
            s = st_ref[p]
            sout_ref[0, 2 * p] = s[:, :hd]
            sout_ref[0, 2 * p + 1] = s[:, hd:]


def _rwkv_scan(ops, nseq, t, row0, tc, heads, hd):
    width = heads * hd
    assert t % tc == 0 and row0 % tc == 0
    nt = t // tc
    blk = pl.BlockSpec((tc, width), lambda s, j: (row0 // tc + s * nt + j, 0))
    st = pl.BlockSpec((1, heads, hd, hd), lambda s, j: (s, 0, 0, 0))
    return pl.pallas_call(
        functools.partial(_rwkv_scan_kernel, hd=hd),
        grid=(nseq, nt),
        in_specs=[blk] * 6,
        out_specs=[pl.BlockSpec((tc, width), lambda s, j: (s * nt + j, 0)), st],
        out_shape=[jax.ShapeDtypeStruct((nseq * t, width), F32),
                   jax.ShapeDtypeStruct((nseq, heads, hd, hd), F32)],
        scratch_shapes=[pltpu.VMEM((heads // 2, hd, 2 * hd), F32), pltpu.VMEM((heads // 2, 8, 2 * hd), F32),
                        pltpu.VMEM((3, 8, width), F32), pltpu.VMEM((5, 8, width), F32)],
        compiler_params=_params(("parallel", "arbitrary")),
        name="rwkv_scan",
    )(*ops)


def _rwkv_lanes_kernel(r_ref, w_ref, k_ref, v_ref, a_ref, b_ref, s0_ref, o_ref, sout_ref, x_ref, ot_ref, *, dt, hd):
    nseq = s0_ref.shape[-1]
    nh = s0_ref.shape[0]
    for idx, ref in enumerate((a_ref, w_ref, k_ref, b_ref, r_ref, v_ref)):
        for t in range(dt):
            x_ref[idx, t] = ref[pl.ds(t, nseq, stride=dt), :].T

    for hh in range(nh):
        ch = slice(hh * hd, (hh + 1) * hd)

        def vgroup(vg, carry):
            v0 = pl.multiple_of(hh * hd + vg * 8, 8)
            v8 = [x_ref[5, t, pl.ds(v0, 8), :] for t in range(dt)]
            outs = [[None] * 8 for _ in range(dt)]
            for j in range(8):
                s = s0_ref[hh, vg * 8 + j]
                for t in range(dt):
                    sa = jnp.sum(s * x_ref[0, t, ch, :], axis=0, keepdims=True)
                    s = s * x_ref[1, t, ch, :] + sa * x_ref[3, t, ch, :] + v8[t][j:j + 1] * x_ref[2, t, ch, :]
                    outs[t][j] = jnp.sum(s * x_ref[4, t, ch, :], axis=0, keepdims=True)
                sout_ref[hh, vg * 8 + j] = s
            for t in range(dt):
                ot_ref[t, pl.ds(v0, 8), :] = jnp.concatenate(outs[t], axis=0)
            return carry

        lax.fori_loop(0, hd // 8, vgroup, 0)

    for t in range(dt):
        o_ref[pl.ds(t, nseq, stride=dt), :] = ot_ref[t].T


def _rwkv_scan_lanes(ops, nseq, dt, row0, heads, hd, state_lanes):
    width = heads * hd
    rows = nseq * dt
    nh = LANES // hd
    assert row0 % rows == 0 and heads % nh == 0
    blk = pl.BlockSpec((rows, nh * hd), lambda g: (row0 // rows, g))
    st = pl.BlockSpec((nh, hd, hd, nseq), lambda g: (g, 0, 0, 0))
    return pl.pallas_call(
        functools.partial(_rwkv_lanes_kernel, dt=dt, hd=hd),
        grid=(heads // nh,),
        in_specs=[blk] * 6 + [st],
        out_specs=[pl.BlockSpec((rows, nh * hd), lambda g: (0, g)), st],
        out_shape=[jax.ShapeDtypeStruct((rows, width), F32),
                   jax.ShapeDtypeStruct((heads, hd, hd, nseq), F32)],
        scratch_shapes=[pltpu.VMEM((6, dt, nh * hd, nseq), F32), pltpu.VMEM((dt, nh * hd, nseq), F32)],
        compiler_params=_params(("parallel",)),
        name="rwkv_scan_lanes",
    )(*ops, state_lanes)


def _rwkv_post_kernel(op_ref, os_ref, bonus_ref, g_ref, lg_ref, lb_ref, out_ref, *, hd, first_blocks):
    cn = out_ref.shape[1]
    seg = _block_ones(LANES, hd, BF16)

    def segmean(x):
        return jnp.concatenate([_dot_exact01(x[:, c:c + LANES], seg) for c in range(0, cn, LANES)],
                               axis=1) * (1.0 / hd)

    def body(o):
        mean = segmean(o)
        cen = o - mean
        var = segmean(cen * cen)
        y = cen * lax.rsqrt(var + LNX_EPS) * lg_ref[...] + lb_ref[...]
        out_ref[...] = ((y + bonus_ref[...]) * g_ref[...]).astype(out_ref.dtype)

    _from_either(pl.program_id(0), first_blocks, op_ref, os_ref, body)


def _rwkv_post(o_raw_p, o_raw_s, bonus, g, lnx_g, lnx_b, hd, rb=256, cn=512):
    n, rw = bonus.shape
    rb, cn = _pick(math.gcd(o_raw_p.shape[0], o_raw_s.shape[0]), rb), _pick(rw, cn)
    first_blocks = o_raw_p.shape[0] // rb
    blk = pl.BlockSpec((rb, cn), lambda i, j: (i, j))
    vec = pl.BlockSpec((1, cn), lambda i, j: (0, j))
    pair = [pl.BlockSpec((rb, cn), lambda i, j: (jnp.minimum(i, first_blocks - 1), j)),
            pl.BlockSpec((rb, cn), lambda i, j: (jnp.maximum(i - first_blocks, 0), j))]
    return pl.pallas_call(
        functools.partial(_rwkv_post_kernel, hd=hd, first_blocks=first_blocks),
        grid=(n // rb, rw // cn),
        in_specs=pair + [blk, blk, vec, vec],
        out_specs=blk,
        out_shape=jax.ShapeDtypeStruct((n, rw), BF16),
        compiler_params=_params(("parallel", "parallel")),
        name="rwkv_post",
    )(o_raw_p, o_raw_s, bonus, g, lnx_g, lnx_b)


def _mmt_kernel(wt_ref, x_ref, o_ref):
    o_ref[...] = _dot(wt_ref[...].astype(BF16), x_ref[...])


def _matmul_t(wt, xt, tm=512, tn=1024, name="matmul_t"):
    m, k = wt.shape
    n = xt.shape[1]
    tm, tn = _pick(m, tm), _pick(n, tn)
    return pl.pallas_call(
        _mmt_kernel,
        grid=(n // tn, m // tm),
        in_specs=[pl.BlockSpec((tm, k), lambda j, i: (i, 0)),
                  pl.BlockSpec((k, tn), lambda j, i: (0, j))],
        out_specs=pl.BlockSpec((tm, tn), lambda j, i: (i, j)),
        out_shape=jax.ShapeDtypeStruct((m, n), F32),
        compiler_params=_params(("parallel", "parallel")),
        name=name,
    )(wt, xt)


def _take_top(s, k):
    n = s.shape[0]
    row = lax.broadcasted_iota(jnp.int32, s.shape, 0).astype(F32)
    out = []
    for _ in range(k):
        m = jnp.max(s, axis=0, keepdims=True)
        first = jnp.min(jnp.where(s == m, row, float(n)), axis=0, keepdims=True)
        s = jnp.where(row == first, -jnp.inf, s)
        out.append(m)
    return out


def _peer_topk_kernel(q_ref, keys_ref, s0_ref, s1_ref, e0_ref, e1_ref, thr_ref, *, topk):
    kd = keys_ref.shape[3]
    cols = q_ref.shape[1]
    s0 = _dot(keys_ref[0, 0], q_ref[:kd, :], HI)
    s1 = _dot(keys_ref[0, 1], q_ref[kd:, :], HI)
    top0 = _take_top(s0, topk)
    top1 = _take_top(s1, topk)
    width = [topk // (a + 1) for a in range(topk)]
    start = [sum(width[:a]) for a in range(topk)]
    n_cand = sum(width)
    rows = -(-n_cand // 8) * 8
    r = lax.broadcasted_iota(jnp.int32, (rows, cols), 0)
    t0 = jnp.broadcast_to(top0[0], (rows, cols))
    first = jnp.zeros((rows, cols), jnp.int32)
    for a in range(1, topk):
        t0 = jnp.where(r >= start[a], top0[a], t0)
        first = jnp.where(r >= start[a], start[a], first)
    b_of_row = r - first
    t1 = jnp.broadcast_to(top1[0], (rows, cols))
    for b in range(1, topk):
        t1 = jnp.where(b_of_row == b, top1[b], t1)
    cand = jnp.where(r < n_cand, t0 + t1, -jnp.inf)
    best = _take_top(cand, topk)
    z = jnp.zeros((1, cols), F32)
    for b in best:
        z = z + jnp.exp(b - best[0])
    s0_ref[...] = s0
    s1_ref[...] = s1
    e0_ref[...] = jnp.exp(s0 - top0[0]) / z
    e1_ref[...] = jnp.exp(s1 - top1[0])
    thr_ref[0] = jnp.broadcast_to(best[topk - 1], thr_ref.shape[1:])


def _peer_topk(qt, sub_keys, tc=256):
    n = qt.shape[1]
    tc = _pick(n, tc)
    heads, _, nk, kd = sub_keys.shape
    blk = pl.BlockSpec((nk, tc), lambda i, h: (h, i))
    shp = jax.ShapeDtypeStruct((heads * nk, n), F32)
    return pl.pallas_call(
        functools.partial(_peer_topk_kernel, topk=PEER_TOPK),
        grid=(n // tc, heads),
        in_specs=[pl.BlockSpec((2 * kd, tc), lambda i, h: (h, i)),
                  pl.BlockSpec((1, 2, nk, kd), lambda i, h: (h, 0, 0, 0))],
        out_specs=[blk] * 4 + [pl.BlockSpec((1, 8, tc), lambda i, h: (h, 0, i))],
        out_shape=[shp] * 4 + [jax.ShapeDtypeStruct((heads, 8, n), F32)],
        compiler_params=_params(("parallel", "parallel")),
        name="peer_topk",
    )(qt, sub_keys)


def _gelu(x):
    return 0.5 * x * (1.0 + lax.erf(x * (2.0 ** -0.5)))


def _peer_dense_kernel(u_ref, ht_ref, vt_ref, s0_ref, e0_ref, s1_ref, e1_ref, thr_ref, o_ref, act_ref, coef_ref,
                       *, heads, nk, jc):
    e = pl.program_id(1)
    eb = u_ref.shape[0]
    d = vt_ref.shape[0]
    ni = eb // nk
    dq = d // ni

    @pl.when(e == 0)
    def _():
        o_ref[...] = jnp.zeros_like(o_ref)
        coef_ref[1] = jnp.zeros(coef_ref.shape[1:], coef_ref.dtype)

    act_ref[...] = _dot(u_ref[...], ht_ref[...])
    cur = e % 2
    prev = (e + 1) % 2
    thr = thr_ref[...]

    def one_i(ii, carry):
        orow = pl.ds(pl.multiple_of(ii * dq, dq), dq)
        o_ref[orow, :] += _dot(vt_ref[orow, :], coef_ref[prev])
        i_abs = jnp.minimum(e, pl.num_programs(1) - 2) * ni + ii
        grp = pl.multiple_of((i_abs // 8) * 8, 8)
        keep = lax.broadcasted_iota(jnp.int32, (8, s0_ref.shape[1]), 0) == i_abs % 8

        def pick(ref):
            return jnp.concatenate(
                [jnp.sum(jnp.where(keep, ref[pl.ds(h * nk + grp, 8), :], 0.0), axis=0, keepdims=True)
                 for h in range(heads)], axis=0)

        s0 = pick(s0_ref)
        e0 = pick(e0_ref)
        for j0 in range(0, nk, jc):
            g = None
            for h in range(heads):
                jrows = slice(h * nk + j0, h * nk + j0 + jc)
                hit = (s0[h:h + 1] + s1_ref[jrows, :]) >= thr[h:h + 1]
                term = jnp.where(hit, e0[h:h + 1] * e1_ref[jrows, :], 0.0)
                g = term if g is None else g + term
            arow = pl.ds(pl.multiple_of(ii * nk + j0, jc), jc)
            coef_ref[cur, arow, :] = (g * _gelu(act_ref[arow, :])).astype(coef_ref.dtype)
        return carry

    lax.fori_loop(0, ni, one_i, 0)


def _peer_dense(ht, u, vt, s0, e0, s1, e1, thr, heads, nk, tp=512, eb=512):
    d, n = ht.shape
    tp = _pick(n, tp)
    nblk = u.shape[0] // eb
    once = pl.Buffered(1)
    tok = pl.BlockSpec((heads * nk, tp), lambda i, e: (0, i), pipeline_mode=once)
    return pl.pallas_call(
        functools.partial(_peer_dense_kernel, heads=heads, nk=nk, jc=PEER_MASK_ROWS),
        grid=(n // tp, nblk + 1),
        in_specs=[pl.BlockSpec((eb, d), lambda i, e: (jnp.minimum(e, nblk - 1), 0)),
                  pl.BlockSpec((d, tp), lambda i, e: (0, i), pipeline_mode=once),
                  pl.BlockSpec((d, eb), lambda i, e: (0, jnp.maximum(e - 1, 0))),
                  tok, tok, tok, tok, pl.BlockSpec((heads, tp), lambda i, e: (0, i))],
        out_specs=pl.BlockSpec((d, tp), lambda i, e: (0, i)),
        out_shape=jax.ShapeDtypeStruct((d, n), F32),
        scratch_shapes=[pltpu.VMEM((eb, tp), F32), pltpu.VMEM((2, eb, tp), BF16)],
        compiler_params=_params(("parallel", "arbitrary")),
        name="peer_dense",
    )(u, ht, vt, s0, e0, s1, e1, thr)


def _layer(xp, xs, s_ret, s_rwkv, s_shift, cp, cs, p):
    nb, t, d = xp.shape
    db, dt, _ = xs.shape
    rows = _Rows(nb, t, db, dt)
    ret_heads, ret_hd = s_ret.shape[1], s_ret.shape[2]
    rw_heads, rw_hd = s_rwkv.shape[1], s_rwkv.shape[2]
    ret_w, rw = ret_heads * ret_hd, rw_heads * rw_hd
    w_lora, a_lora, g_lora = p['w2'].shape[0], p['a2'].shape[0], p['g2'].shape[0]
    rwkv_in = s_shift.shape[-1]
    ret_in = 4 * ret_w
    assert p['w_in'].shape[1] == ret_in + rwkv_in and rwkv_in == 3 * rw + w_lora + a_lora + g_lora

    xp, xs = xp.reshape(nb * t, d), xs.reshape(db * dt, d)
    mod = _ada(jnp.concatenate([cp, cs], axis=0), p['ada_w'], p['ada_b'])
    mod_ext = rows.extend(mod)

    h = _prenorm(rows, xp, xs, p['pre_mix_g'], mod_ext, d)

    tm = _pick(rows.n, 1024)
    w_in_t = p['w_in'].T
    proj_ret = _matmul(h, w_in_t, col_block0=0, n_out=ret_in, tm=tm, tn=512, name="in_proj_ret")
    proj_rw = _matmul(h, w_in_t, col_block0=ret_in // 512, n_out=rwkv_in, tm=tm, tn=512, name="in_proj_rwkv")

    o_ret_p, o_ret_s, sr_p, sr_s = _retention(rows, proj_ret, s_ret, ret_heads, ret_hd)

    tail_w = -(-(w_lora + a_lora + g_lora) // LANES) * LANES
    g2p = jnp.pad(p['g2'], ((0, tail_w - w_lora - a_lora - g_lora), (0, 0)))
    vec = lambda a: a.reshape(1, rw)
    r_, w_, k_, v_, a_, b_, g_, bonus = _rwkv_prep(
        rows, proj_rw, s_shift, p['shift_mu'], vec(p['w0']), p['w2'], vec(p['a0']), p['a2'], g2p,
        vec(p['k_k']), vec(p['k_a']), vec(p['r_k']), rw, rw_hd, g_lora)
    ops = (r_, w_, k_, v_, a_, b_)
    o_rw_p, sw_p = _rwkv_scan(ops, nb, t, 0, 256, rw_heads, rw_hd)
    o_rw_s, sw_lanes = _rwkv_scan_lanes(ops, db, dt, nb * t, rw_heads, rw_hd, jnp.transpose(s_rwkv, (1, 2, 3, 0)))
    sw_s = jnp.transpose(sw_lanes, (3, 0, 1, 2))
    o_rw = _rwkv_post(o_rw_p, o_rw_s, bonus, g_, vec(p['lnx_g']), vec(p['lnx_b']), rw_hd)

    mix = _matmul_out(o_ret_p, o_ret_s, o_rw, p['w_out'])
    x1, h2t = _postmix(rows, xp, xs, mix, p['post_mix_g'], p['pre_ffn_g'], mod_ext, d)

    heads, _, nk, _ = p['peer_sub_keys'].shape
    qt = _matmul_t(p['peer_wq'].T, h2t, name="peer_q")
    s0, s1, e0, e1, thr = _peer_topk(qt, p['peer_sub_keys'])
    peer_t = _peer_dense(h2t, p['peer_u'].astype(BF16), p['peer_v'].T.astype(BF16),
                         s0, e0, s1, e1, thr[:, 0, :], heads, nk)
    yp = _final(rows, x1, peer_t, p['post_ffn_g'], mod_ext, d, 0, rows.prompt_blocks)
    ys = _final(rows, x1, peer_t, p['post_ffn_g'], mod_ext, d, rows.prompt_blocks, rows.n_blocks - rows.prompt_blocks)

    n_p = nb * t
    last_p = slice(t - 1, n_p, t)
    last_s = slice(n_p + dt - 1, None, dt)
    return (yp.reshape(nb, t, d), ys.reshape(db, dt, d), sr_p, sr_s, sw_p, sw_s, proj_rw[last_p], proj_rw[last_s])


def kernel(x_prompt, x_sample, state_ret, state_rwkv, state_shift, c_prompt, c_sample, ada_w, ada_b, pre_mix_g, post_mix_g, pre_ffn_g, post_ffn_g, w_in, shift_mu, w0, w2, a0, a2, g2, k_k, k_a, r_k, lnx_g, lnx_b, w_out, peer_wq, peer_sub_keys, peer_u, peer_v):
    depth = ada_w.shape[0]
    assert depth == 1, "prompt and sample tokens are stacked per layer; deeper stacks need per-layer restacking"
    prm = dict(ada_w=ada_w[0], ada_b=ada_b[0], pre_mix_g=pre_mix_g[0], post_mix_g=post_mix_g[0],
               pre_ffn_g=pre_ffn_g[0], post_ffn_g=post_ffn_g[0], w_in=w_in[0], shift_mu=shift_mu[0],
               w0=w0[0], w2=w2[0], a0=a0[0], a2=a2[0], g2=g2[0], k_k=k_k[0], k_a=k_a[0], r_k=r_k[0],
               lnx_g=lnx_g[0], lnx_b=lnx_b[0], w_out=w_out[0], peer_wq=peer_wq[0],
               peer_sub_keys=peer_sub_keys[0], peer_u=peer_u[0], peer_v=peer_v[0])
    yp, ys, sr_p, sr_s, sw_p, sw_s, ss_p, ss_s = _layer(
        x_prompt, x_sample, state_ret[0], state_rwkv[0], state_shift[0], c_prompt, c_sample, prm)
    sd, wd, hd = state_ret.dtype, state_rwkv.dtype, state_shift.dtype
    return (yp, ys, sr_p[None].astype(sd), sr_s[None].astype(sd), sw_p[None].astype(wd), sw_s[None].astype(wd),
            ss_p[None].astype(hd), ss_s[None].astype(hd))
```

```python
import functools
import math

import jax
import jax.numpy as jnp
from jax import lax
from jax.experimental import pallas as pl
from jax.experimental.pallas import tpu as pltpu

PAST_LEN = 16384
RET_CHUNK = 128
ROPE_BASE = 10000.0
LNX_EPS = 64e-5
NORM_EPS = 1e-6
PEER_TOPK = 16
PEER_MASK_ROWS = 16
RET_HEADS_PER_STEP = 4

LANES = 128
ROW_BLOCK = 128
VMEM_LIMIT = 56 * 1024 * 1024

HI = lax.Precision.HIGHEST
F32 = jnp.float32
BF16 = jnp.bfloat16


def _pick(n, target, mult=LANES):
    best = None
    for c in range(mult, min(n, target) + 1, mult):
        if n % c == 0:
            best = c
    assert best is not None, (n, target, mult)
    return best


def _params(sem, vmem=VMEM_LIMIT):
    return pltpu.CompilerParams(dimension_semantics=sem, vmem_limit_bytes=vmem)


def _dot(a, b, precision=None):
    return jnp.dot(a, b, preferred_element_type=F32, precision=precision)


def _dot_nt(a, b, precision=None):
    return lax.dot_general(a, b, (((1,), (1,)), ((), ())), preferred_element_type=F32, precision=precision)


def _dot_tn(a, b, precision=None):
    return lax.dot_general(a, b, (((0,), (0,)), ((), ())), preferred_element_type=F32, precision=precision)


def _split_bf16(x, terms):
    out = []
    for _ in range(terms - 1):
        t = x.astype(BF16)
        out.append(t)
        x = x - t.astype(F32)
    return out + [x.astype(BF16)]


def _dot_split(x, w_terms):
    xh, xl = _split_bf16(x, 2)
    wh, wl = w_terms
    return _dot(xh, wh) + (_dot(xl, wh) + _dot(xh, wl))


def _dot_exact01(x, ones):
    return sum(_dot(t, ones) for t in _split_bf16(x, 3))


def _block_ones(n, seg, dtype):
    r = lax.broadcasted_iota(jnp.int32, (n, n), 0) // seg
    c = lax.broadcasted_iota(jnp.int32, (n, n), 1) // seg
    return (r == c).astype(dtype)


def _ada_kernel(c_ref, w_ref, b_ref, o_ref):
    c = c_ref[...]
    sc = (c * jax.nn.sigmoid(c)).astype(BF16)
    o_ref[...] = _dot(sc, w_ref[...].astype(BF16)) + b_ref[...]


def _ada(c_all, ada_w, ada_b, tn=512):
    m, d = c_all.shape
    n = ada_w.shape[1]
    return pl.pallas_call(
        _ada_kernel,
        grid=(n // tn,),
        in_specs=[pl.BlockSpec((m, d), lambda j: (0, 0)),
                  pl.BlockSpec((d, tn), lambda j: (0, j)),
                  pl.BlockSpec((1, tn), lambda j: (0, j))],
        out_specs=pl.BlockSpec((m, tn), lambda j: (0, j)),
        out_shape=jax.ShapeDtypeStruct((m, n), F32),
        compiler_params=_params(("arbitrary",)),
        name="ada",
    )(c_all, ada_w, ada_b.reshape(1, n))


def _expand_slots(m, rows):
    slots, _, d = m.shape
    return jnp.broadcast_to(m, (slots, rows // slots, d)).reshape(rows, d)


def _rms(x, g):
    return x * lax.rsqrt(jnp.mean(x * x, axis=-1, keepdims=True) + NORM_EPS) * g


def _from_either(i, first_blocks, a_ref, b_ref, body):
    @pl.when(i < first_blocks)
    def _():
        body(a_ref[...])

    @pl.when(i >= first_blocks)
    def _():
        body(b_ref[...])


def _two_specs(block, first_blocks):
    return [pl.BlockSpec(block, lambda i, *_: (jnp.minimum(i, first_blocks - 1), 0)),
            pl.BlockSpec(block, lambda i, *_: (jnp.maximum(i - first_blocks, 0), 0))]


def _prenorm_kernel(xp_ref, xs_ref, g_ref, sh_ref, sc_ref, h_ref, *, prompt_blocks):
    def body(x):
        rows = x.shape[0]
        y = _rms(x, g_ref[...])
        h = y * (1.0 + _expand_slots(sc_ref[...], rows)) + _expand_slots(sh_ref[...], rows)
        h_ref[...] = h.astype(h_ref.dtype)

    _from_either(pl.program_id(0), prompt_blocks, xp_ref, xs_ref, body)


def _postmix_kernel(xp_ref, xs_ref, m_ref, gpost_ref, gpre_ref, gate_ref, sh_ref, sc_ref, x1_ref, ht_ref,
                    *, prompt_blocks):
    def body(x):
        rows = x.shape[0]
        x1 = x + _expand_slots(gate_ref[...], rows) * _rms(m_ref[...], gpost_ref[...])
        x1_ref[...] = x1
        y = _rms(x1, gpre_ref[...])
        h = y * (1.0 + _expand_slots(sc_ref[...], rows)) + _expand_slots(sh_ref[...], rows)
        ht_ref[...] = h.T.astype(ht_ref.dtype)

    _from_either(pl.program_id(0), prompt_blocks, xp_ref, xs_ref, body)


def _final_kernel(x_ref, mt_ref, gpost_ref, gate_ref, y_ref):
    x = x_ref[...]
    rows = x.shape[0]
    y_ref[...] = x + _expand_slots(gate_ref[...], rows) * _rms(mt_ref[...].T, gpost_ref[...])


class _Rows:
    def __init__(self, nb, t, db, dt):
        self.n_prompt = nb * t
        self.n_sample = db * dt
        self.n = self.n_prompt + self.n_sample
        self.rb = ROW_BLOCK
        assert t % self.rb == 0 and self.rb % dt == 0 and self.n_sample % self.rb == 0
        self.slots = self.rb // dt
        self.prompt_blocks = self.n_prompt // self.rb
        self.blocks_per_seq = t // self.rb
        self.n_blocks = self.n // self.rb
        self.nb, self.t, self.db, self.dt = nb, t, db, dt

    def slot_block(self, i):
        return jnp.where(i < self.prompt_blocks, i // self.blocks_per_seq, i - self.prompt_blocks + self.nb)

    def extend(self, m):
        mp = jnp.repeat(m[: self.nb], self.slots, axis=0)
        return jnp.concatenate([mp, m[self.nb:]], axis=0)[:, None, :]


def _mod_spec(rows, d, chunk, block0=0):
    return pl.BlockSpec((rows.slots, 1, d), lambda i: (rows.slot_block(i + block0), 0, chunk))


def _prenorm(rows, xp, xs, g, mod_ext, d):
    row = pl.BlockSpec((rows.rb, d), lambda i: (i, 0))
    vec = pl.BlockSpec((1, d), lambda i: (0, 0))
    return pl.pallas_call(
        functools.partial(_prenorm_kernel, prompt_blocks=rows.prompt_blocks),
        grid=(rows.n_blocks,),
        in_specs=_two_specs((rows.rb, d), rows.prompt_blocks) + [vec, _mod_spec(rows, d, 0), _mod_spec(rows, d, 1)],
        out_specs=row,
        out_shape=jax.ShapeDtypeStruct((rows.n, d), BF16),
        compiler_params=_params(("parallel",)),
        name="prenorm",
    )(xp, xs, g.reshape(1, d), mod_ext, mod_ext)


def _postmix(rows, xp, xs, mix, gpost, gpre, mod_ext, d):
    row = pl.BlockSpec((rows.rb, d), lambda i: (i, 0))
    vec = pl.BlockSpec((1, d), lambda i: (0, 0))
    return pl.pallas_call(
        functools.partial(_postmix_kernel, prompt_blocks=rows.prompt_blocks),
        grid=(rows.n_blocks,),
        in_specs=_two_specs((rows.rb, d), rows.prompt_blocks)
        + [row, vec, vec, _mod_spec(rows, d, 2), _mod_spec(rows, d, 3), _mod_spec(rows, d, 4)],
        out_specs=[row, pl.BlockSpec((d, rows.rb), lambda i: (0, i))],
        out_shape=[jax.ShapeDtypeStruct((rows.n, d), F32), jax.ShapeDtypeStruct((d, rows.n), BF16)],
        compiler_params=_params(("parallel",)),
        name="postmix",
    )(xp, xs, mix, gpost.reshape(1, d), gpre.reshape(1, d), mod_ext, mod_ext, mod_ext)


def _final(rows, x1, peer_t, gpost, mod_ext, d, block0, n_blocks):
    vec = pl.BlockSpec((1, d), lambda i: (0, 0))
    return pl.pallas_call(
        _final_kernel,
        grid=(n_blocks,),
        in_specs=[pl.BlockSpec((rows.rb, d), lambda i: (i + block0, 0)),
                  pl.BlockSpec((d, rows.rb), lambda i: (0, i + block0)),
                  vec, _mod_spec(rows, d, 5, block0)],
        out_specs=pl.BlockSpec((rows.rb, d), lambda i: (i, 0)),
        out_shape=jax.ShapeDtypeStruct((n_blocks * rows.rb, d), F32),
        compiler_params=_params(("parallel",)),
        name="final",
    )(x1, peer_t, gpost.reshape(1, d), mod_ext)


def _mm_kernel(a_ref, wt_ref, o_ref, wbf_ref):
    @pl.when(pl.program_id(1) == 0)
    def _():
        wbf_ref[...] = wt_ref[...].astype(BF16)

    o_ref[...] = _dot_nt(a_ref[...], wbf_ref[...]).astype(o_ref.dtype)


def _matmul(a, wt, *, col_block0, n_out, tm, tn, name):
    m, k = a.shape
    return pl.pallas_call(
        _mm_kernel,
        grid=(pl.cdiv(n_out, tn), m // tm),
        in_specs=[pl.BlockSpec((tm, k), lambda j, i: (i, 0)),
                  pl.BlockSpec((tn, k), lambda j, i: (j + col_block0, 0))],
        out_specs=pl.BlockSpec((tm, tn), lambda j, i: (i, j)),
        out_shape=jax.ShapeDtypeStruct((m, n_out), F32),
        scratch_shapes=[pltpu.VMEM((tn, k), BF16)],
        compiler_params=_params(("parallel", "arbitrary")),
        name=name,
    )(a, wt)


def _mm2_kernel(ap_ref, as_ref, b_ref, wa_ref, wb_ref, o_ref, *, first_blocks):
    def body(a):
        o_ref[...] = _dot(a, wa_ref[...].astype(BF16)) + _dot(b_ref[...], wb_ref[...].astype(BF16))

    _from_either(pl.program_id(1), first_blocks, ap_ref, as_ref, body)


def _matmul_out(o_ret_p, o_ret_s, o_rw, w_out, tn=512):
    m, kb = o_rw.shape
    ka = o_ret_p.shape[1]
    tm = _pick(math.gcd(o_ret_p.shape[0], o_ret_s.shape[0]), 1024)
    first_blocks = o_ret_p.shape[0] // tm
    n = w_out.shape[1]
    tn = _pick(n, tn)
    assert ka == kb
    pair = [pl.BlockSpec((tm, ka), lambda j, i: (jnp.minimum(i, first_blocks - 1), 0)),
            pl.BlockSpec((tm, ka), lambda j, i: (jnp.maximum(i - first_blocks, 0), 0))]
    return pl.pallas_call(
        functools.partial(_mm2_kernel, first_blocks=first_blocks),
        grid=(n // tn, m // tm),
        in_specs=pair + [pl.BlockSpec((tm, kb), lambda j, i: (i, 0)),
                         pl.BlockSpec((ka, tn), lambda j, i: (0, j)),
                         pl.BlockSpec((kb, tn), lambda j, i: (1, j))],
        out_specs=pl.BlockSpec((tm, tn), lambda j, i: (i, j)),
        out_shape=jax.ShapeDtypeStruct((m, n), F32),
        compiler_params=_params(("parallel", "parallel")),
        name="out_proj",
    )(o_ret_p, o_ret_s, o_rw, w_out, w_out)


def _rope_tables(pos, hd):
    half = hd // 2
    inv = ROPE_BASE ** (-jnp.arange(half, dtype=F32) / half)
    ang = pos.astype(F32)[:, None] * inv[None, :]
    cos, sin = jnp.cos(ang), jnp.sin(ang)
    return jnp.concatenate([cos, cos], axis=-1), jnp.concatenate([-sin, sin], axis=-1)


def _rope(x, cos, sin_signed):
    half = x.shape[-1] // 2
    return x * cos + pltpu.roll(x, half, 1) * sin_signed


def _ret_finish(o, gate):
    o = o * lax.rsqrt(jnp.mean(o * o, axis=-1, keepdims=True) + NORM_EPS)
    return o * (gate * jax.nn.sigmoid(gate))


def _ret_prompt_kernel(q_ref, k_ref, v_ref, g_ref, cos_ref, sin_ref, lg_ref, o_ref, s_ref, *, chunk, hd):
    t = q_ref.shape[0]
    nh = q_ref.shape[1] // hd
    ri = lax.broadcasted_iota(jnp.int32, (chunk, chunk), 0)
    ci = lax.broadcasted_iota(jnp.int32, (chunk, chunk), 1)
    causal = ri >= ci
    diff = jnp.where(causal, ri - ci, 0).astype(F32)
    rowf = lax.broadcasted_iota(jnp.int32, (chunk, hd), 0).astype(F32)
    lgs = [lg_ref[i, 0:1, :] for i in range(nh)]
    masks = [jnp.where(causal, jnp.exp(diff * lg), 0.0) for lg in lgs]
    q_decs = [jnp.exp((rowf + 1.0) * lg) for lg in lgs]
    k_decs = [jnp.exp((chunk - 1.0 - rowf) * lg) for lg in lgs]
    c_decs = [jnp.exp(float(chunk) * lg) for lg in lgs]

    def step(c, states):
        r0 = pl.multiple_of(c * chunk, chunk)
        rows = pl.ds(r0, chunk)
        cos, sin = cos_ref[rows, :], sin_ref[rows, :]
        cols = [slice(i * hd, (i + 1) * hd) for i in range(nh)]
        qs = [_rope(q_ref[rows, cl], cos, sin) for cl in cols]
        ks = [_rope(k_ref[rows, cl], cos, sin) * (hd ** -0.5) for cl in cols]
        vs = [v_ref[rows, cl] for cl in cols]
        atts = [_dot_nt(q.astype(BF16), k.astype(BF16)) * m for q, k, m in zip(qs, ks, masks)]
        cross = [_dot((q * qd).astype(BF16), s.astype(BF16)) for q, qd, s in zip(qs, q_decs, states)]
        inner = [_dot(a.astype(BF16), v.astype(BF16)) for a, v in zip(atts, vs)]
        new = [s * cd + _dot_tn(k * kd, v, HI) for s, cd, k, kd, v in zip(states, c_decs, ks, k_decs, vs)]
        for i, cl in enumerate(cols):
            o_ref[rows, cl] = _ret_finish(inner[i] + cross[i], g_ref[rows, cl]).astype(o_ref.dtype)
        return tuple(new)

    final = lax.fori_loop(0, t // chunk, step, tuple(jnp.zeros((hd, hd), F32) for _ in range(nh)))
    for i in range(nh):
        s_ref[0, i] = final[i]


def _ret_sample_kernel(q_ref, k_ref, v_ref, g_ref, cos_ref, sin_ref, lg_ref, s0_ref, o_ref, s_ref, *, dt):
    rows, hd = q_ref.shape
    nseq = rows // dt
    lg = lg_ref[0, 0:1, :]
    ri = lax.broadcasted_iota(jnp.int32, (rows, rows), 0)
    ci = lax.broadcasted_iota(jnp.int32, (rows, rows), 1)
    ok = (ri >= ci) & ((ri // dt) == (ci // dt))
    diff = jnp.where(ok, ri - ci, 0).astype(F32)
    mask = jnp.where(ok, jnp.exp(diff * lg), 0.0)
    rown = lax.broadcasted_iota(jnp.int32, (rows, hd), 0)
    posf = (rown % dt).astype(F32)
    q_dec = jnp.exp((posf + 1.0) * lg)
    k_dec = jnp.exp((dt - 1.0 - posf) * lg)
    c_dec = jnp.exp(float(dt) * lg)
    cos, sin = cos_ref[...], sin_ref[...]
    qc = _rope(q_ref[...], cos, sin)
    kc = _rope(k_ref[...], cos, sin) * (hd ** -0.5)
    vc = v_ref[...]
    att = _dot_nt(qc.astype(BF16), kc.astype(BF16)) * mask
    inner = _dot(att.astype(BF16), vc.astype(BF16))
    qd = qc * q_dec
    kd = kc * k_dec
    seq = rown // dt
    cross = jnp.zeros((rows, hd), F32)
    for s in range(nseq):
        s0 = s0_ref[s, 0]
        mine = seq == s
        cross = cross + _dot(jnp.where(mine, qd, 0.0).astype(BF16), s0.astype(BF16))
        s_ref[s, 0] = s0 * c_dec + _dot_tn(jnp.where(mine, kd, 0.0), vc, HI)
    o_ref[...] = _ret_finish(inner + cross, g_ref[...]).astype(o_ref.dtype)


def _retention(rows, proj, state_ret, ret_heads, hd):
    nb, t, db, dt = rows.nb, rows.t, rows.db, rows.dt
    lg = jnp.log1p(-jnp.exp2(-5.0 - jnp.arange(ret_heads, dtype=F32)))
    lg_tab = jnp.broadcast_to(lg[:, None, None], (ret_heads, 8, hd))
    lg_spec = pl.BlockSpec((1, 8, hd), lambda b, h: (h, 0, 0))
    assert hd == RET_CHUNK and t % RET_CHUNK == 0 and dt % RET_CHUNK != 0

    cos_p, sin_p = _rope_tables(jnp.arange(t, dtype=jnp.int32), hd)
    nh = RET_HEADS_PER_STEP if ret_heads % RET_HEADS_PER_STEP == 0 else 1
    hsteps = ret_heads // nh
    col = lambda part: pl.BlockSpec((t, nh * hd), lambda b, h: (b, h + part * hsteps))
    tab = pl.BlockSpec((t, hd), lambda b, h: (0, 0))
    o_p, s_p = pl.pallas_call(
        functools.partial(_ret_prompt_kernel, chunk=RET_CHUNK, hd=hd),
        grid=(nb, hsteps),
        in_specs=[col(0), col(1), col(2), col(3), tab, tab, pl.BlockSpec((nh, 8, hd), lambda b, h: (h, 0, 0))],
        out_specs=[pl.BlockSpec((t, nh * hd), lambda b, h: (b, h)),
                   pl.BlockSpec((1, nh, hd, hd), lambda b, h: (b, h, 0, 0))],
        out_shape=[jax.ShapeDtypeStruct((nb * t, ret_heads * hd), BF16),
                   jax.ShapeDtypeStruct((nb, ret_heads, hd, hd), F32)],
        compiler_params=_params(("parallel", "parallel")),
        name="ret_prompt",
    )(proj, proj, proj, proj, cos_p, sin_p, lg_tab)

    rb = RET_CHUNK
    nseq = rb // dt
    cos_s, sin_s = _rope_tables(PAST_LEN + jnp.arange(dt, dtype=jnp.int32), hd)
    cos_s, sin_s = jnp.tile(cos_s, (nseq, 1)), jnp.tile(sin_s, (nseq, 1))
    off = (nb * t) // rb
    col = lambda part: pl.BlockSpec((rb, hd), lambda b, h: (b + off, h + part * ret_heads))
    tab = pl.BlockSpec((rb, hd), lambda b, h: (0, 0))
    st = pl.BlockSpec((nseq, 1, hd, hd), lambda b, h: (b, h, 0, 0))
    o_s, s_s = pl.pallas_call(
        functools.partial(_ret_sample_kernel, dt=dt),
        grid=(db // nseq, ret_heads),
        in_specs=[col(0), col(1), col(2), col(3), tab, tab, lg_spec, st],
        out_specs=[pl.BlockSpec((rb, hd), lambda b, h: (b, h)), st],
        out_shape=[jax.ShapeDtypeStruct((db * dt, ret_heads * hd), BF16),
                   jax.ShapeDtypeStruct((db, ret_heads, hd, hd), F32)],
        compiler_params=_params(("parallel", "parallel")),
        name="ret_sample",
    )(proj, proj, proj, proj, cos_s, sin_s, lg_tab, state_ret)
    return o_p, o_s, s_p, s_s


def _shifted(x, first, tseq_mask):
    prev = pltpu.roll(x, 1, 0)
    return jnp.where(tseq_mask, first, prev)


def _rwkv_prep_kernel(pr_ref, pk_ref, pv_ref, pt_ref, sr_ref, sk_ref, sv_ref, st_ref,
                      mur_ref, muk_ref, muv_ref, mut_ref, w0_ref, w2h_ref, w2l_ref, a0_ref, a2h_ref, a2l_ref,
                      g2h_ref, g2l_ref,
                      kk_ref, ka_ref, rk_ref,
                      r_out, w_out, k_out, v_out, a_out, b_out, g_out, bonus_out, carry_ref, tcarry_ref,
                      *, dt, prompt_blocks, blocks_per_seq, w_lora, a_lora, g_lora, hd):
    i, j = pl.program_id(0), pl.program_id(1)
    rows, cn = pr_ref.shape
    rown = lax.broadcasted_iota(jnp.int32, (rows, 1), 0)
    sample_block = jnp.zeros((rows, 1), jnp.int32) + (i >= prompt_blocks).astype(jnp.int32)
    is_first = (rown == 0) | ((rown % dt == 0) & (sample_block == 1))
    seq_start = i % blocks_per_seq == 0

    @pl.when(i == 0)
    def _():
        for part in range(3):
            carry_ref[part, j] = jnp.zeros(carry_ref.shape[2:], F32)
        tcarry_ref[...] = jnp.zeros_like(tcarry_ref)

    def lerp(p_ref, s_ref, mu_ref, carried):
        p = p_ref[...]
        from_prompt = jnp.broadcast_to(jnp.where(seq_start, 0.0, carried), p.shape)
        first = jnp.where(sample_block == 1, _expand_slots(s_ref[...], rows), from_prompt)
        prev = _shifted(p, first, is_first)
        return p + mu_ref[...] * (prev - p)

    r = lerp(pr_ref, sr_ref, mur_ref, carry_ref[0, j, 0:1, :])
    kw = lerp(pk_ref, sk_ref, muk_ref, carry_ref[1, j, 0:1, :])
    vw = lerp(pv_ref, sv_ref, muv_ref, carry_ref[2, j, 0:1, :])
    tail = lerp(pt_ref, st_ref, mut_ref, tcarry_ref[0:1, :])
    for part, p_ref in enumerate((pr_ref, pk_ref, pv_ref)):
        carry_ref[part, j, 0:1, :] = p_ref[rows - 1:rows, :]

    @pl.when(j == pl.num_programs(1) - 1)
    def _():
        tcarry_ref[0:1, :] = pt_ref[rows - 1:rows, :]

    tw = tail.shape[1]
    wl = tail[:, :w_lora]
    al = tail[:, w_lora:w_lora + a_lora]
    glane = lax.broadcasted_iota(jnp.int32, (rows, tw - w_lora - a_lora), 1)
    gl = jnp.where(glane < g_lora, jax.nn.sigmoid(tail[:, w_lora + a_lora:]), 0.0)

    w_log = -jax.nn.softplus(-(w0_ref[...] + _dot_split(jnp.tanh(wl), (w2h_ref[...], w2l_ref[...])))) - 0.5
    decay = jnp.exp(-jnp.exp(w_log))
    a = jax.nn.sigmoid(a0_ref[...] + _dot_split(al, (a2h_ref[...], a2l_ref[...])))
    g = _dot_split(gl, (g2h_ref[...], g2l_ref[...]))

    seg = _block_ones(LANES, hd, BF16)

    def segsum(x):
        return jnp.concatenate([_dot_exact01(x[:, c:c + LANES], seg) for c in range(0, cn, LANES)], axis=1)

    kk = kw * kk_ref[...]
    kk = kk / jnp.maximum(jnp.sqrt(segsum(kk * kk)), 1e-12)
    kn = kw * (1.0 + (a - 1.0) * ka_ref[...])
    r_out[...] = r
    w_out[...] = decay
    k_out[...] = kn
    v_out[...] = vw
    a_out[...] = -kk
    b_out[...] = kk * a
    g_out[...] = g
    bonus_out[...] = segsum(r * kn * rk_ref[...]) * vw


def _hi_lo(w):
    hi = w.astype(BF16)
    return hi, (w - hi.astype(F32)).astype(BF16)


def _rwkv_prep(rows, proj, s_shift, mu, w0, w2, a0, a2, g2p, k_k, k_a, r_k_flat, rw, hd, g_lora, cn=512):
    n = rows.n
    cn = _pick(rw, cn)
    ncb = rw // cn
    tw = g2p.shape[0] + w2.shape[0] + a2.shape[0]
    assert (3 * rw) % tw == 0
    tb = (3 * rw) // tw
    sblk = lambda i: jnp.maximum(i - rows.prompt_blocks, 0)
    part = lambda p: pl.BlockSpec((rows.rb, cn), lambda i, j: (i, j + p * ncb))
    spart = lambda p: pl.BlockSpec((rows.slots, 1, cn), lambda i, j: (sblk(i), 0, j + p * ncb))
    mupart = lambda p: pl.BlockSpec((1, cn), lambda i, j: (0, j + p * ncb))
    vec = pl.BlockSpec((1, cn), lambda i, j: (0, j))
    lora = lambda k: pl.BlockSpec((k, cn), lambda i, j: (0, j))
    out = pl.BlockSpec((rows.rb, cn), lambda i, j: (i, j))
    kern = functools.partial(_rwkv_prep_kernel, dt=rows.dt, prompt_blocks=rows.prompt_blocks,
                             blocks_per_seq=rows.blocks_per_seq,
                             w_lora=w2.shape[0], a_lora=a2.shape[0], g_lora=g_lora, hd=hd)
    shift3 = s_shift[:, None, :]
    mu2 = mu.reshape(1, -1)
    return pl.pallas_call(
        kern,
        grid=(rows.n_blocks, ncb),
        in_specs=[part(0), part(1), part(2), pl.BlockSpec((rows.rb, tw), lambda i, j: (i, tb)),
                  spart(0), spart(1), spart(2), pl.BlockSpec((rows.slots, 1, tw), lambda i, j: (sblk(i), 0, tb)),
                  mupart(0), mupart(1), mupart(2), pl.BlockSpec((1, tw), lambda i, j: (0, tb)),
                  vec, lora(w2.shape[0]), lora(w2.shape[0]), vec, lora(a2.shape[0]), lora(a2.shape[0]),
                  lora(g2p.shape[0]), lora(g2p.shape[0]), vec, vec, vec],
        out_specs=[out] * 8,
        out_shape=[jax.ShapeDtypeStruct((n, rw), F32)] * 8,
        scratch_shapes=[pltpu.VMEM((3, ncb, 8, cn), F32), pltpu.VMEM((8, tw), F32)],
        compiler_params=_params(("arbitrary", "arbitrary")),
        name="rwkv_prep",
    )(proj, proj, proj, proj, shift3, shift3, shift3, shift3, mu2, mu2, mu2, mu2,
      w0, *_hi_lo(w2), a0, *_hi_lo(a2), *_hi_lo(g2p), k_k, k_a, r_k_flat)


def _rwkv_scan_kernel(r_ref, w_ref, k_ref, v_ref, a_ref, b_ref, o_ref, sout_ref, st_ref, ob_ref, vs_ref, row_ref,
                      *, hd):
    tb = pl.program_id(1)
    tc, width = r_ref.shape
    pw = 2 * hd
    npairs = width // pw

    @pl.when(tb == 0)
    def _():
        st_ref[...] = jnp.zeros_like(st_ref)

    half = npairs // 2
    ones2 = _block_ones(2 * pw, hd, BF16)
    sub = lax.broadcasted_iota(jnp.int32, (hd, pw), 0)
    lane = lax.broadcasted_iota(jnp.int32, (hd, pw), 1)
    diags = [(((lane % hd) - sub + hd) % hd == i).astype(F32).astype(BF16) for i in range(3)]
    lane8 = lax.broadcasted_iota(jnp.int32, (8, pw), 1)
    row16 = lax.broadcasted_iota(jnp.int32, (8, 2 * pw), 0)
    lane16 = lax.broadcasted_iota(jnp.int32, (8, 2 * pw), 1)
    sel2 = row16 == (lane16 // hd)

    def shift_in_head(x, i):
        return jnp.where(lane8 % hd >= i, pltpu.roll(x, i, 1), pltpu.roll(x, pw - hd + i, 1))

    def group(tg, carry):
        rows8 = pl.ds(pl.multiple_of(tg * 8, 8), 8)
        for idx, ref in enumerate((a_ref, w_ref, k_ref, b_ref, r_ref)):
            row_ref[idx] = ref[rows8, :]
        for p in range(npairs):
            cols = pl.ds(p * pw, pw)
            v8 = v_ref[rows8, cols]
            v1 = v8.astype(BF16).astype(F32)
            v2 = (v8 - v1).astype(BF16).astype(F32)
            vs_ref[0, :, cols] = v1
            vs_ref[1, :, cols] = shift_in_head(v2, 1)
            vs_ref[2, :, cols] = shift_in_head((v8 - v1) - v2, 2)

        def row(idx, p, j):
            return row_ref[idx, j:j + 1, pl.ds(p * pw, pw)]

        def side_by_side(tiles):
            return jnp.concatenate([jnp.concatenate([tiles[p], tiles[p + half]], axis=1) for p in range(half)],
                                   axis=0)

        def split(x, p):
            q, c = p % half, p // half
            return x[q * hd:(q + 1) * hd, c * pw:(c + 1) * pw]

        def moved_v(j):
            xs = []
            for p in range(npairs):
                x = None
                for i in range(3):
                    term = diags[i] * vs_ref[i, j:j + 1, pl.ds(p * pw, pw)].astype(BF16)
                    x = term if x is None else x + term
                xs.append(x)
            return _dot(side_by_side(xs), ones2)

        def emit_out(j, states):
            for p in range(half):
                r2 = jnp.concatenate([row(4, p, j), row(4, p + half, j)], axis=1)
                rsel = jnp.where(sel2, r2, 0.0).astype(BF16)
                o8 = _dot_nt(rsel, jnp.concatenate([states[p], states[p + half]], axis=1))
                ob_ref[p, j:j + 1, 0:hd] = o8[0:1, :]
                ob_ref[p, j:j + 1, hd:pw] = o8[1:2, :]
                ob_ref[p + half, j:j + 1, 0:hd] = o8[2:3, :]
                ob_ref[p + half, j:j + 1, hd:pw] = o8[3:4, :]

        vc_next = moved_v(0)
        states = None
        for j in range(8):
            old = [st_ref[p] for p in range(npairs)]
            sa = _dot(side_by_side([old[p] * row(0, p, j) for p in range(npairs)]).astype(BF16), ones2)
            if states is not None:
                emit_out(j - 1, states)
            vc = vc_next
            if j < 7:
                vc_next = moved_v(j + 1)
            states = []
            for p in range(npairs):
                s = old[p] * row(1, p, j) + split(sa, p) * row(3, p, j) + split(vc, p) * row(2, p, j)
                st_ref[p] = s
                states.append(s.astype(BF16))
        emit_out(7, states)
        for p in range(npairs):
            o_ref[rows8, pl.ds(p * pw, pw)] = ob_ref[p]
        return carry

    lax.fori_loop(0, tc // 8, group, 0)

    @pl.when(tb == pl.num_programs(1) - 1)
    def _():
        for p in range(npairs):
            s = st_ref[p]
            sout_ref[0, 2 * p] = s[:, :hd]
            sout_ref[0, 2 * p + 1] = s[:, hd:]


def _rwkv_scan(ops, nseq, t, row0, tc, heads, hd):
    width = heads * hd
    assert t % tc == 0 and row0 % tc == 0
    nt = t // tc
    blk = pl.BlockSpec((tc, width), lambda s, j: (row0 // tc + s * nt + j, 0))
    st = pl.BlockSpec((1, heads, hd, hd), lambda s, j: (s, 0, 0, 0))
    return pl.pallas_call(
        functools.partial(_rwkv_scan_kernel, hd=hd),
        grid=(nseq, nt),
        in_specs=[blk] * 6,
        out_specs=[pl.BlockSpec((tc, width), lambda s, j: (s * nt + j, 0)), st],
        out_shape=[jax.ShapeDtypeStruct((nseq * t, width), F32),
                   jax.ShapeDtypeStruct((nseq, heads, hd, hd), F32)],
        scratch_shapes=[pltpu.VMEM((heads // 2, hd, 2 * hd), F32), pltpu.VMEM((heads // 2, 8, 2 * hd), F32),
                        pltpu.VMEM((3, 8, width), F32), pltpu.VMEM((5, 8, width), F32)],
        compiler_params=_params(("parallel", "arbitrary")),
        name="rwkv_scan",
    )(*ops)


def _rwkv_lanes_kernel(r_ref, w_ref, k_ref, v_ref, a_ref, b_ref, s0_ref, o_ref, sout_ref, x_ref, ot_ref, *, dt, hd):
    nseq = s0_ref.shape[-1]
    nh = s0_ref.shape[0]
    for idx, ref in enumerate((a_ref, w_ref, k_ref, b_ref, r_ref, v_ref)):
        for t in range(dt):
            x_ref[idx, t] = ref[pl.ds(t, nseq, stride=dt), :].T

    for hh in range(nh):
        ch = slice(hh * hd, (hh + 1) * hd)

        def vgroup(vg, carry):
            v0 = pl.multiple_of(hh * hd + vg * 8, 8)
            v8 = [x_ref[5, t, pl.ds(v0, 8), :] for t in range(dt)]
            outs = [[None] * 8 for _ in range(dt)]
            for j in range(8):
                s = s0_ref[hh, vg * 8 + j]
                for t in range(dt):
                    sa = jnp.sum(s * x_ref[0, t, ch, :], axis=0, keepdims=True)
                    s = s * x_ref[1, t, ch, :] + sa * x_ref[3, t, ch, :] + v8[t][j:j + 1] * x_ref[2, t, ch, :]
                    outs[t][j] = jnp.sum(s * x_ref[4, t, ch, :], axis=0, keepdims=True)
                sout_ref[hh, vg * 8 + j] = s
            for t in range(dt):
                ot_ref[t, pl.ds(v0, 8), :] = jnp.concatenate(outs[t], axis=0)
            return carry

        lax.fori_loop(0, hd // 8, vgroup, 0)

    for t in range(dt):
        o_ref[pl.ds(t, nseq, stride=dt), :] = ot_ref[t].T


def _rwkv_scan_lanes(ops, nseq, dt, row0, heads, hd, state_lanes):
    width = heads * hd
    rows = nseq * dt
    nh = LANES // hd
    assert row0 % rows == 0 and heads % nh == 0
    blk = pl.BlockSpec((rows, nh * hd), lambda g: (row0 // rows, g))
    st = pl.BlockSpec((nh, hd, hd, nseq), lambda g: (g, 0, 0, 0))
    return pl.pallas_call(
        functools.partial(_rwkv_lanes_kernel, dt=dt, hd=hd),
        grid=(heads // nh,),
        in_specs=[blk] * 6 + [st],
        out_specs=[pl.BlockSpec((rows, nh * hd), lambda g: (0, g)), st],
        out_shape=[jax.ShapeDtypeStruct((rows, width), F32),
                   jax.ShapeDtypeStruct((heads, hd, hd, nseq), F32)],
        scratch_shapes=[pltpu.VMEM((6, dt, nh * hd, nseq), F32), pltpu.VMEM((dt, nh * hd, nseq), F32)],
        compiler_params=_params(("parallel",)),
        name="rwkv_scan_lanes",
    )(*ops, state_lanes)


def _rwkv_post_kernel(op_ref, os_ref, bonus_ref, g_ref, lg_ref, lb_ref, out_ref, *, hd, first_blocks):
    cn = out_ref.shape[1]
    seg = _block_ones(LANES, hd, BF16)

    def segmean(x):
        return jnp.concatenate([_dot_exact01(x[:, c:c + LANES], seg) for c in range(0, cn, LANES)],
                               axis=1) * (1.0 / hd)

    def body(o):
        mean = segmean(o)
        cen = o - mean
        var = segmean(cen * cen)
        y = cen * lax.rsqrt(var + LNX_EPS) * lg_ref[...] + lb_ref[...]
        out_ref[...] = ((y + bonus_ref[...]) * g_ref[...]).astype(out_ref.dtype)

    _from_either(pl.program_id(0), first_blocks, op_ref, os_ref, body)


def _rwkv_post(o_raw_p, o_raw_s, bonus, g, lnx_g, lnx_b, hd, rb=256, cn=512):
    n, rw = bonus.shape
    rb, cn = _pick(math.gcd(o_raw_p.shape[0], o_raw_s.shape[0]), rb), _pick(rw, cn)
    first_blocks = o_raw_p.shape[0] // rb
    blk = pl.BlockSpec((rb, cn), lambda i, j: (i, j))
    vec = pl.BlockSpec((1, cn), lambda i, j: (0, j))
    pair = [pl.BlockSpec((rb, cn), lambda i, j: (jnp.minimum(i, first_blocks - 1), j)),
            pl.BlockSpec((rb, cn), lambda i, j: (jnp.maximum(i - first_blocks, 0), j))]
    return pl.pallas_call(
        functools.partial(_rwkv_post_kernel, hd=hd, first_blocks=first_blocks),
        grid=(n // rb, rw // cn),
        in_specs=pair + [blk, blk, vec, vec],
        out_specs=blk,
        out_shape=jax.ShapeDtypeStruct((n, rw), BF16),
        compiler_params=_params(("parallel", "parallel")),
        name="rwkv_post",
    )(o_raw_p, o_raw_s, bonus, g, lnx_g, lnx_b)


def _mmt_kernel(wt_ref, x_ref, o_ref):
    o_ref[...] = _dot(wt_ref[...].astype(BF16), x_ref[...])


def _matmul_t(wt, xt, tm=512, tn=1024, name="matmul_t"):
    m, k = wt.shape
    n = xt.shape[1]
    tm, tn = _pick(m, tm), _pick(n, tn)
    return pl.pallas_call(
        _mmt_kernel,
        grid=(n // tn, m // tm),
        in_specs=[pl.BlockSpec((tm, k), lambda j, i: (i, 0)),
                  pl.BlockSpec((k, tn), lambda j, i: (0, j))],
        out_specs=pl.BlockSpec((tm, tn), lambda j, i: (i, j)),
        out_shape=jax.ShapeDtypeStruct((m, n), F32),
        compiler_params=_params(("parallel", "parallel")),
        name=name,
    )(wt, xt)


def _take_top(s, k):
    n = s.shape[0]
    row = lax.broadcasted_iota(jnp.int32, s.shape, 0).astype(F32)
    out = []
    for _ in range(k):
        m = jnp.max(s, axis=0, keepdims=True)
        first = jnp.min(jnp.where(s == m, row, float(n)), axis=0, keepdims=True)
        s = jnp.where(row == first, -jnp.inf, s)
        out.append(m)
    return out


def _peer_topk_kernel(q_ref, keys_ref, s0_ref, s1_ref, e0_ref, e1_ref, thr_ref, *, topk):
    kd = keys_ref.shape[3]
    cols = q_ref.shape[1]
    s0 = _dot(keys_ref[0, 0], q_ref[:kd, :], HI)
    s1 = _dot(keys_ref[0, 1], q_ref[kd:, :], HI)
    top0 = _take_top(s0, topk)
    top1 = _take_top(s1, topk)
    width = [topk // (a + 1) for a in range(topk)]
    start = [sum(width[:a]) for a in range(topk)]
    n_cand = sum(width)
    rows = -(-n_cand // 8) * 8
    r = lax.broadcasted_iota(jnp.int32, (rows, cols), 0)
    t0 = jnp.broadcast_to(top0[0], (rows, cols))
    first = jnp.zeros((rows, cols), jnp.int32)
    for a in range(1, topk):
        t0 = jnp.where(r >= start[a], top0[a], t0)
        first = jnp.where(r >= start[a], start[a], first)
    b_of_row = r - first
    t1 = jnp.broadcast_to(top1[0], (rows, cols))
    for b in range(1, topk):
        t1 = jnp.where(b_of_row == b, top1[b], t1)
    cand = jnp.where(r < n_cand, t0 + t1, -jnp.inf)
    best = _take_top(cand, topk)
    z = jnp.zeros((1, cols), F32)
    for b in best:
        z = z + jnp.exp(b - best[0])
    s0_ref[...] = s0
    s1_ref[...] = s1
    e0_ref[...] = jnp.exp(s0 - top0[0]) / z
    e1_ref[...] = jnp.exp(s1 - top1[0])
    thr_ref[0] = jnp.broadcast_to(best[topk - 1], thr_ref.shape[1:])


def _peer_topk(qt, sub_keys, tc=256):
    n = qt.shape[1]
    tc = _pick(n, tc)
    heads, _, nk, kd = sub_keys.shape
    blk = pl.BlockSpec((nk, tc), lambda i, h: (h, i))
    shp = jax.ShapeDtypeStruct((heads * nk, n), F32)
    return pl.pallas_call(
        functools.partial(_peer_topk_kernel, topk=PEER_TOPK),
        grid=(n // tc, heads),
        in_specs=[pl.BlockSpec((2 * kd, tc), lambda i, h: (h, i)),
                  pl.BlockSpec((1, 2, nk, kd), lambda i, h: (h, 0, 0, 0))],
        out_specs=[blk] * 4 + [pl.BlockSpec((1, 8, tc), lambda i, h: (h, 0, i))],
        out_shape=[shp] * 4 + [jax.ShapeDtypeStruct((heads, 8, n), F32)],
        compiler_params=_params(("parallel", "parallel")),
        name="peer_topk",
    )(qt, sub_keys)


def _gelu(x):
    return 0.5 * x * (1.0 + lax.erf(x * (2.0 ** -0.5)))


def _peer_dense_kernel(u_ref, ht_ref, vt_ref, s0_ref, e0_ref, s1_ref, e1_ref, thr_ref, o_ref, act_ref, coef_ref,
                       *, heads, nk, jc):
    e = pl.program_id(1)
    eb = u_ref.shape[0]
    d = vt_ref.shape[0]
    ni = eb // nk
    dq = d // ni

    last = pl.num_programs(1) - 1
    cur = e % 2
    prev = (e + 1) % 2

    def run(score, finish):
        if score:
            act_ref[...] = _dot(u_ref[...], ht_ref[...])
            thr = thr_ref[...]

        def one_i(ii, carry):
            if finish:
                orow = pl.ds(pl.multiple_of(ii * dq, dq), dq)
                o_ref[orow, :] += _dot(vt_ref[orow, :], coef_ref[prev])
            if not score:
                return carry
            i_abs = e * ni + ii
            grp = pl.multiple_of((i_abs // 8) * 8, 8)
            keep = lax.broadcasted_iota(jnp.int32, (8, s0_ref.shape[1]), 0) == i_abs % 8

            def pick(ref):
                return jnp.concatenate(
                    [jnp.sum(jnp.where(keep, ref[pl.ds(h * nk + grp, 8), :], 0.0), axis=0, keepdims=True)
                     for h in range(heads)], axis=0)

            s0 = pick(s0_ref)
            e0 = pick(e0_ref)
            for j0 in range(0, nk, jc):
                g = None
                for h in range(heads):
                    jrows = slice(h * nk + j0, h * nk + j0 + jc)
                    hit = (s0[h:h + 1] + s1_ref[jrows, :]) >= thr[h:h + 1]
                    term = jnp.where(hit, e0[h:h + 1] * e1_ref[jrows, :], 0.0)
                    g = term if g is None else g + term
                arow = pl.ds(pl.multiple_of(ii * nk + j0, jc), jc)
                coef_ref[cur, arow, :] = (g * _gelu(act_ref[arow, :])).astype(coef_ref.dtype)
            return carry

        lax.fori_loop(0, ni, one_i, 0)

    @pl.when(e == 0)
    def _():
        o_ref[...] = jnp.zeros_like(o_ref)
        run(True, False)

    @pl.when((e > 0) & (e < last))
    def _():
        run(True, True)

    @pl.when(e == last)
    def _():
        run(False, True)


def _peer_dense(ht, u, vt, s0, e0, s1, e1, thr, heads, nk, tp=512, eb=512):
    d, n = ht.shape
    tp = _pick(n, tp)
    nblk = u.shape[0] // eb
    once = pl.Buffered(1)
    tok = pl.BlockSpec((heads * nk, tp), lambda i, e: (0, i), pipeline_mode=once)
    return pl.pallas_call(
        functools.partial(_peer_dense_kernel, heads=heads, nk=nk, jc=PEER_MASK_ROWS),
        grid=(n // tp, nblk + 1),
        in_specs=[pl.BlockSpec((eb, d), lambda i, e: (jnp.minimum(e, nblk - 1), 0)),
                  pl.BlockSpec((d, tp), lambda i, e: (0, i), pipeline_mode=once),
                  pl.BlockSpec((d, eb), lambda i, e: (0, jnp.maximum(e - 1, 0))),
                  tok, tok, tok, tok, pl.BlockSpec((heads, tp), lambda i, e: (0, i))],
        out_specs=pl.BlockSpec((d, tp), lambda i, e: (0, i)),
        out_shape=jax.ShapeDtypeStruct((d, n), F32),
        scratch_shapes=[pltpu.VMEM((eb, tp), F32), pltpu.VMEM((2, eb, tp), BF16)],
        compiler_params=_params(("parallel", "arbitrary")),
        name="peer_dense",
    )(u, ht, vt, s0, e0, s1, e1, thr)


def _layer(xp, xs, s_ret, s_rwkv, s_shift, cp, cs, p):
    nb, t, d = xp.shape
    db, dt, _ = xs.shape
    rows = _Rows(nb, t, db, dt)
    ret_heads, ret_hd = s_ret.shape[1], s_ret.shape[2]
    rw_heads, rw_hd = s_rwkv.shape[1], s_rwkv.shape[2]
    ret_w, rw = ret_heads * ret_hd, rw_heads * rw_hd
    w_lora, a_lora, g_lora = p['w2'].shape[0], p['a2'].shape[0], p['g2'].shape[0]
    rwkv_in = s_shift.shape[-1]
    ret_in = 4 * ret_w
    assert p['w_in'].shape[1] == ret_in + rwkv_in and rwkv_in == 3 * rw + w_lora + a_lora + g_lora

    xp, xs = xp.reshape(nb * t, d), xs.reshape(db * dt, d)
    mod = _ada(jnp.concatenate([cp, cs], axis=0), p['ada_w'], p['ada_b'])
    mod_ext = rows.extend(mod)

    h = _prenorm(rows, xp, xs, p['pre_mix_g'], mod_ext, d)

    tm = _pick(rows.n, 1024)
    w_in_t = p['w_in'].T
    proj_ret = _matmul(h, w_in_t, col_block0=0, n_out=ret_in, tm=tm, tn=512, name="in_proj_ret")
    proj_rw = _matmul(h, w_in_t, col_block0=ret_in // 512, n_out=rwkv_in, tm=tm, tn=512, name="in_proj_rwkv")

    o_ret_p, o_ret_s, sr_p, sr_s = _retention(rows, proj_ret, s_ret, ret_heads, ret_hd)

    tail_w = -(-(w_lora + a_lora + g_lora) // LANES) * LANES
    g2p = jnp.pad(p['g2'], ((0, tail_w - w_lora - a_lora - g_lora), (0, 0)))
    vec = lambda a: a.reshape(1, rw)
    r_, w_, k_, v_, a_, b_, g_, bonus = _rwkv_prep(
        rows, proj_rw, s_shift, p['shift_mu'], vec(p['w0']), p['w2'], vec(p['a0']), p['a2'], g2p,
        vec(p['k_k']), vec(p['k_a']), vec(p['r_k']), rw, rw_hd, g_lora)
    ops = (r_, w_, k_, v_, a_, b_)
    o_rw_p, sw_p = _rwkv_scan(ops, nb, t, 0, 256, rw_heads, rw_hd)
    o_rw_s, sw_lanes = _rwkv_scan_lanes(ops, db, dt, nb * t, rw_heads, rw_hd, jnp.transpose(s_rwkv, (1, 2, 3, 0)))
    sw_s = jnp.transpose(sw_lanes, (3, 0, 1, 2))
    o_rw = _rwkv_post(o_rw_p, o_rw_s, bonus, g_, vec(p['lnx_g']), vec(p['lnx_b']), rw_hd)

    mix = _matmul_out(o_ret_p, o_ret_s, o_rw, p['w_out'])
    x1, h2t = _postmix(rows, xp, xs, mix, p['post_mix_g'], p['pre_ffn_g'], mod_ext, d)

    heads, _, nk, _ = p['peer_sub_keys'].shape
    qt = _matmul_t(p['peer_wq'].T, h2t, name="peer_q")
    s0, s1, e0, e1, thr = _peer_topk(qt, p['peer_sub_keys'])
    peer_t = _peer_dense(h2t, p['peer_u'].astype(BF16), p['peer_v'].T.astype(BF16),
                         s0, e0, s1, e1, thr[:, 0, :], heads, nk)
    yp = _final(rows, x1, peer_t, p['post_ffn_g'], mod_ext, d, 0, rows.prompt_blocks)
    ys = _final(rows, x1, peer_t, p['post_ffn_g'], mod_ext, d, rows.prompt_blocks, rows.n_blocks - rows.prompt_blocks)

    n_p = nb * t
    last_p = slice(t - 1, n_p, t)
    last_s = slice(n_p + dt - 1, None, dt)
    return (yp.reshape(nb, t, d), ys.reshape(db, dt, d), sr_p, sr_s, sw_p, sw_s, proj_rw[last_p], proj_rw[last_s])


def kernel(x_prompt, x_sample, state_ret, state_rwkv, state_shift, c_prompt, c_sample, ada_w, ada_b, pre_mix_g, post_mix_g, pre_ffn_g, post_ffn_g, w_in, shift_mu, w0, w2, a0, a2, g2, k_k, k_a, r_k, lnx_g, lnx_b, w_out, peer_wq, peer_sub_keys, peer_u, peer_v):
    depth = ada_w.shape[0]
    assert depth == 1, "prompt and sample tokens are stacked per layer; deeper stacks need per-layer restacking"
    prm = dict(ada_w=ada_w[0], ada_b=ada_b[0], pre_mix_g=pre_mix_g[0], post_mix_g=post_mix_g[0],
               pre_ffn_g=pre_ffn_g[0], post_ffn_g=post_ffn_g[0], w_in=w_in[0], shift_mu=shift_mu[0],
               w0=w0[0], w2=w2[0], a0=a0[0], a2=a2[0], g2=g2[0], k_k=k_k[0], k_a=k_a[0], r_k=r_k[0],
               lnx_g=lnx_g[0], lnx_b=lnx_b[0], w_out=w_out[0], peer_wq=peer_wq[0],
               peer_sub_keys=peer_sub_keys[0], peer_u=peer_u[0], peer_v=peer_v[0])
    yp, ys, sr_p, sr_s, sw_p, sw_s, ss_p, ss_s = _layer(
        x_prompt, x_sample, state_ret[0], state_rwkv[0], state_shift[0], c_prompt, c_sample, prm)
    sd, wd, hd = state_ret.dtype, state_rwkv.dtype, state_shift.dtype
    return (yp, ys, sr_p[None].astype(sd), sr_s[None].astype(sd), sw_p[None].astype(wd), sw_s[None].astype(wd),
            ss_p[None].astype(hd), ss_s[None].astype(hd))
```

```python
import functools
import math

import jax
import jax.numpy as jnp
from jax import lax
from jax.experimental import pallas as pl
from jax.experimental.pallas import tpu as pltpu

PAST_LEN = 16384
RET_CHUNK = 128
ROPE_BASE = 10000.0
LNX_EPS = 64e-5
NORM_EPS = 1e-6
PEER_TOPK = 16
PEER_MASK_ROWS = 16
RET_HEADS_PER_STEP = 4

LANES = 128
ROW_BLOCK = 128
VMEM_LIMIT = 56 * 1024 * 1024

HI = lax.Precision.HIGHEST
F32 = jnp.float32
BF16 = jnp.bfloat16


def _pick(n, target, mult=LANES):
    best = None
    for c in range(mult, min(n, target) + 1, mult):
        if n % c == 0:
            best = c
    assert best is not None, (n, target, mult)
    return best


def _params(sem, vmem=VMEM_LIMIT):
    return pltpu.CompilerParams(dimension_semantics=sem, vmem_limit_bytes=vmem)


def _dot(a, b, precision=None):
    return jnp.dot(a, b, preferred_element_type=F32, precision=precision)


def _dot_nt(a, b, precision=None):
    return lax.dot_general(a, b, (((1,), (1,)), ((), ())), preferred_element_type=F32, precision=precision)


def _dot_tn(a, b, precision=None):
    return lax.dot_general(a, b, (((0,), (0,)), ((), ())), preferred_element_type=F32, precision=precision)


def _split_bf16(x, terms):
    out = []
    for _ in range(terms - 1):
        t = x.astype(BF16)
        out.append(t)
        x = x - t.astype(F32)
    return out + [x.astype(BF16)]


def _dot_split(x, w_terms):
    xh, xl = _split_bf16(x, 2)
    wh, wl = w_terms
    return _dot(xh, wh) + (_dot(xl, wh) + _dot(xh, wl))


def _dot_exact01(x, ones):
    return sum(_dot(t, ones) for t in _split_bf16(x, 3))


def _block_ones(n, seg, dtype):
    r = lax.broadcasted_iota(jnp.int32, (n, n), 0) // seg
    c = lax.broadcasted_iota(jnp.int32, (n, n), 1) // seg
    return (r == c).astype(dtype)


def _ada_kernel(c_ref, w_ref, b_ref, o_ref):
    c = c_ref[...]
    sc = (c * jax.nn.sigmoid(c)).astype(BF16)
    o_ref[...] = _dot(sc, w_ref[...].astype(BF16)) + b_ref[...]


def _ada(c_all, ada_w, ada_b, tn=512):
    m, d = c_all.shape
    n = ada_w.shape[1]
    return pl.pallas_call(
        _ada_kernel,
        grid=(n // tn,),
        in_specs=[pl.BlockSpec((m, d), lambda j: (0, 0)),
                  pl.BlockSpec((d, tn), lambda j: (0, j)),
                  pl.BlockSpec((1, tn), lambda j: (0, j))],
        out_specs=pl.BlockSpec((m, tn), lambda j: (0, j)),
        out_shape=jax.ShapeDtypeStruct((m, n), F32),
        compiler_params=_params(("arbitrary",)),
        name="ada",
    )(c_all, ada_w, ada_b.reshape(1, n))


def _expand_slots(m, rows):
    slots, _, d = m.shape
    return jnp.broadcast_to(m, (slots, rows // slots, d)).reshape(rows, d)


def _rms(x, g):
    return x * lax.rsqrt(jnp.mean(x * x, axis=-1, keepdims=True) + NORM_EPS) * g


def _from_either(i, first_blocks, a_ref, b_ref, body):
    @pl.when(i < first_blocks)
    def _():
        body(a_ref[...])

    @pl.when(i >= first_blocks)
    def _():
        body(b_ref[...])


def _two_specs(block, first_blocks):
    return [pl.BlockSpec(block, lambda i, *_: (jnp.minimum(i, first_blocks - 1), 0)),
            pl.BlockSpec(block, lambda i, *_: (jnp.maximum(i - first_blocks, 0), 0))]


def _prenorm_kernel(xp_ref, xs_ref, g_ref, sh_ref, sc_ref, h_ref, *, prompt_blocks):
    def body(x):
        rows = x.shape[0]
        y = _rms(x, g_ref[...])
        h = y * (1.0 + _expand_slots(sc_ref[...], rows)) + _expand_slots(sh_ref[...], rows)
        h_ref[...] = h.astype(h_ref.dtype)

    _from_either(pl.program_id(0), prompt_blocks, xp_ref, xs_ref, body)


def _postmix_kernel(xp_ref, xs_ref, m_ref, gpost_ref, gpre_ref, gate_ref, sh_ref, sc_ref, x1_ref, ht_ref,
                    *, prompt_blocks):
    def body(x):
        rows = x.shape[0]
        x1 = x + _expand_slots(gate_ref[...], rows) * _rms(m_ref[...], gpost_ref[...])
        x1_ref[...] = x1
        y = _rms(x1, gpre_ref[...])
        h = y * (1.0 + _expand_slots(sc_ref[...], rows)) + _expand_slots(sh_ref[...], rows)
        ht_ref[...] = h.T.astype(ht_ref.dtype)

    _from_either(pl.program_id(0), prompt_blocks, xp_ref, xs_ref, body)


def _final_kernel(x_ref, mt_ref, gpost_ref, gate_ref, y_ref):
    x = x_ref[...]
    rows = x.shape[0]
    y_ref[...] = x + _expand_slots(gate_ref[...], rows) * _rms(mt_ref[...].T, gpost_ref[...])


class _Rows:
    def __init__(self, nb, t, db, dt):
        self.n_prompt = nb * t
        self.n_sample = db * dt
        self.n = self.n_prompt + self.n_sample
        self.rb = ROW_BLOCK
        assert t % self.rb == 0 and self.rb % dt == 0 and self.n_sample % self.rb == 0
        self.slots = self.rb // dt
        self.prompt_blocks = self.n_prompt // self.rb
        self.blocks_per_seq = t // self.rb
        self.n_blocks = self.n // self.rb
        self.nb, self.t, self.db, self.dt = nb, t, db, dt

    def slot_block(self, i):
        return jnp.where(i < self.prompt_blocks, i // self.blocks_per_seq, i - self.prompt_blocks + self.nb)

    def extend(self, m):
        mp = jnp.repeat(m[: self.nb], self.slots, axis=0)
        return jnp.concatenate([mp, m[self.nb:]], axis=0)[:, None, :]


def _mod_spec(rows, d, chunk, block0=0):
    return pl.BlockSpec((rows.slots, 1, d), lambda i: (rows.slot_block(i + block0), 0, chunk))


def _prenorm(rows, xp, xs, g, mod_ext, d):
    row = pl.BlockSpec((rows.rb, d), lambda i: (i, 0))
    vec = pl.BlockSpec((1, d), lambda i: (0, 0))
    return pl.pallas_call(
        functools.partial(_prenorm_kernel, prompt_blocks=rows.prompt_blocks),
        grid=(rows.n_blocks,),
        in_specs=_two_specs((rows.rb, d), rows.prompt_blocks) + [vec, _mod_spec(rows, d, 0), _mod_spec(rows, d, 1)],
        out_specs=row,
        out_shape=jax.ShapeDtypeStruct((rows.n, d), BF16),
        compiler_params=_params(("parallel",)),
        name="prenorm",
    )(xp, xs, g.reshape(1, d), mod_ext, mod_ext)


def _postmix(rows, xp, xs, mix, gpost, gpre, mod_ext, d):
    row = pl.BlockSpec((rows.rb, d), lambda i: (i, 0))
    vec = pl.BlockSpec((1, d), lambda i: (0, 0))
    return pl.pallas_call(
        functools.partial(_postmix_kernel, prompt_blocks=rows.prompt_blocks),
        grid=(rows.n_blocks,),
        in_specs=_two_specs((rows.rb, d), rows.prompt_blocks)
        + [row, vec, vec, _mod_spec(rows, d, 2), _mod_spec(rows, d, 3), _mod_spec(rows, d, 4)],
        out_specs=[row, pl.BlockSpec((d, rows.rb), lambda i: (0, i))],
        out_shape=[jax.ShapeDtypeStruct((rows.n, d), F32), jax.ShapeDtypeStruct((d, rows.n), BF16)],
        compiler_params=_params(("parallel",)),
        name="postmix",
    )(xp, xs, mix, gpost.reshape(1, d), gpre.reshape(1, d), mod_ext, mod_ext, mod_ext)


def _final(rows, x1, peer_t, gpost, mod_ext, d, block0, n_blocks):
    vec = pl.BlockSpec((1, d), lambda i: (0, 0))
    return pl.pallas_call(
        _final_kernel,
        grid=(n_blocks,),
        in_specs=[pl.BlockSpec((rows.rb, d), lambda i: (i + block0, 0)),
                  pl.BlockSpec((d, rows.rb), lambda i: (0, i + block0)),
                  vec, _mod_spec(rows, d, 5, block0)],
        out_specs=pl.BlockSpec((rows.rb, d), lambda i: (i, 0)),
        out_shape=jax.ShapeDtypeStruct((n_blocks * rows.rb, d), F32),
        compiler_params=_params(("parallel",)),
        name="final",
    )(x1, peer_t, gpost.reshape(1, d), mod_ext)


def _mm_kernel(a_ref, wt_ref, o_ref, wbf_ref):
    @pl.when(pl.program_id(1) == 0)
    def _():
        wbf_ref[...] = wt_ref[...].astype(BF16)

    o_ref[...] = _dot_nt(a_ref[...], wbf_ref[...]).astype(o_ref.dtype)


def _matmul(a, wt, *, col_block0, n_out, tm, tn, name):
    m, k = a.shape
    return pl.pallas_call(
        _mm_kernel,
        grid=(pl.cdiv(n_out, tn), m // tm),
        in_specs=[pl.BlockSpec((tm, k), lambda j, i: (i, 0)),
                  pl.BlockSpec((tn, k), lambda j, i: (j + col_block0, 0))],
        out_specs=pl.BlockSpec((tm, tn), lambda j, i: (i, j)),
        out_shape=jax.ShapeDtypeStruct((m, n_out), F32),
        scratch_shapes=[pltpu.VMEM((tn, k), BF16)],
        compiler_params=_params(("parallel", "arbitrary")),
        name=name,
    )(a, wt)


def _mm2_kernel(ap_ref, as_ref, b_ref, wa_ref, wb_ref, o_ref, *, first_blocks):
    def body(a):
        o_ref[...] = _dot(a, wa_ref[...].astype(BF16)) + _dot(b_ref[...], wb_ref[...].astype(BF16))

    _from_either(pl.program_id(1), first_blocks, ap_ref, as_ref, body)


def _matmul_out(o_ret_p, o_ret_s, o_rw, w_out, tn=512):
    m, kb = o_rw.shape
    ka = o_ret_p.shape[1]
    tm = _pick(math.gcd(o_ret_p.shape[0], o_ret_s.shape[0]), 1024)
    first_blocks = o_ret_p.shape[0] // tm
    n = w_out.shape[1]
    tn = _pick(n, tn)
    assert ka == kb
    pair = [pl.BlockSpec((tm, ka), lambda j, i: (jnp.minimum(i, first_blocks - 1), 0)),
            pl.BlockSpec((tm, ka), lambda j, i: (jnp.maximum(i - first_blocks, 0), 0))]
    return pl.pallas_call(
        functools.partial(_mm2_kernel, first_blocks=first_blocks),
        grid=(n // tn, m // tm),
        in_specs=pair + [pl.BlockSpec((tm, kb), lambda j, i: (i, 0)),
                         pl.BlockSpec((ka, tn), lambda j, i: (0, j)),
                         pl.BlockSpec((kb, tn), lambda j, i: (1, j))],
        out_specs=pl.BlockSpec((tm, tn), lambda j, i: (i, j)),
        out_shape=jax.ShapeDtypeStruct((m, n), F32),
        compiler_params=_params(("parallel", "parallel")),
        name="out_proj",
    )(o_ret_p, o_ret_s, o_rw, w_out, w_out)


def _rope_tables(pos, hd):
    half = hd // 2
    inv = ROPE_BASE ** (-jnp.arange(half, dtype=F32) / half)
    ang = pos.astype(F32)[:, None] * inv[None, :]
    cos, sin = jnp.cos(ang), jnp.sin(ang)
    return jnp.concatenate([cos, cos], axis=-1), jnp.concatenate([-sin, sin], axis=-1)


def _rope(x, cos, sin_signed):
    half = x.shape[-1] // 2
    return x * cos + pltpu.roll(x, half, 1) * sin_signed


def _ret_finish(o, gate):
    o = o * lax.rsqrt(jnp.mean(o * o, axis=-1, keepdims=True) + NORM_EPS)
    return o * (gate * jax.nn.sigmoid(gate))


def _ret_prompt_kernel(q_ref, k_ref, v_ref, g_ref, cos_ref, sin_ref, lg_ref, o_ref, s_ref, *, chunk, hd):
    t = q_ref.shape[0]
    nh = q_ref.shape[1] // hd
    ri = lax.broadcasted_iota(jnp.int32, (chunk, chunk), 0)
    ci = lax.broadcasted_iota(jnp.int32, (chunk, chunk), 1)
    causal = ri >= ci
    diff = jnp.where(causal, ri - ci, 0).astype(F32)
    rowf = lax.broadcasted_iota(jnp.int32, (chunk, hd), 0).astype(F32)
    lgs = [lg_ref[i, 0:1, :] for i in range(nh)]
    masks = [jnp.where(causal, jnp.exp(diff * lg), 0.0) for lg in lgs]
    q_decs = [jnp.exp((rowf + 1.0) * lg) for lg in lgs]
    k_decs = [jnp.exp((chunk - 1.0 - rowf) * lg) for lg in lgs]
    c_decs = [jnp.exp(float(chunk) * lg) for lg in lgs]

    def step(c, states):
        r0 = pl.multiple_of(c * chunk, chunk)
        rows = pl.ds(r0, chunk)
        cos, sin = cos_ref[rows, :], sin_ref[rows, :]
        cols = [slice(i * hd, (i + 1) * hd) for i in range(nh)]
        qs = [_rope(q_ref[rows, cl], cos, sin) for cl in cols]
        ks = [_rope(k_ref[rows, cl], cos, sin) * (hd ** -0.5) for cl in cols]
        vs = [v_ref[rows, cl] for cl in cols]
        atts = [_dot_nt(q.astype(BF16), k.astype(BF16)) * m for q, k, m in zip(qs, ks, masks)]
        cross = [_dot((q * qd).astype(BF16), s.astype(BF16)) for q, qd, s in zip(qs, q_decs, states)]
        inner = [_dot(a.astype(BF16), v.astype(BF16)) for a, v in zip(atts, vs)]
        new = [s * cd + _dot_tn(k * kd, v, HI) for s, cd, k, kd, v in zip(states, c_decs, ks, k_decs, vs)]
        for i, cl in enumerate(cols):
            o_ref[rows, cl] = _ret_finish(inner[i] + cross[i], g_ref[rows, cl]).astype(o_ref.dtype)
        return tuple(new)

    final = lax.fori_loop(0, t // chunk, step, tuple(jnp.zeros((hd, hd), F32) for _ in range(nh)))
    for i in range(nh):
        s_ref[0, i] = final[i]


def _ret_sample_kernel(q_ref, k_ref, v_ref, g_ref, cos_ref, sin_ref, lg_ref, s0_ref, o_ref, s_ref, *, dt, hd):
    rows = q_ref.shape[0]
    nh = q_ref.shape[1] // hd
    nseq = rows // dt
    ri = lax.broadcasted_iota(jnp.int32, (rows, rows), 0)
    ci = lax.broadcasted_iota(jnp.int32, (rows, rows), 1)
    ok = (ri >= ci) & ((ri // dt) == (ci // dt))
    diff = jnp.where(ok, ri - ci, 0).astype(F32)
    posf = (lax.broadcasted_iota(jnp.int32, (rows, hd), 0) % dt).astype(F32)
    cos, sin = cos_ref[...], sin_ref[...]
    inner, qd, kd, vs, c_decs = [], [], [], [], []
    for i in range(nh):
        cl = slice(i * hd, (i + 1) * hd)
        lg = lg_ref[i, 0:1, :]
        mask = jnp.where(ok, jnp.exp(diff * lg), 0.0)
        qc = _rope(q_ref[:, cl], cos, sin)
        kc = _rope(k_ref[:, cl], cos, sin) * (hd ** -0.5)
        vc = v_ref[:, cl]
        att = _dot_nt(qc.astype(BF16), kc.astype(BF16)) * mask
        inner.append(_dot(att.astype(BF16), vc.astype(BF16)))
        qd.append(qc * jnp.exp((posf + 1.0) * lg))
        kd.append(kc * jnp.exp((dt - 1.0 - posf) * lg))
        vs.append(vc)
        c_decs.append(jnp.exp(float(dt) * lg))
    cross = [[] for _ in range(nh)]
    for s in range(nseq):
        own = slice(s * dt, (s + 1) * dt)
        for i in range(nh):
            s0 = s0_ref[s, i]
            cross[i].append(_dot(qd[i][own].astype(BF16), s0.astype(BF16)))
            s_ref[s, i] = s0 * c_decs[i] + _dot_tn(kd[i][own], vs[i][own], HI)
    for i in range(nh):
        cl = slice(i * hd, (i + 1) * hd)
        o = inner[i] + jnp.concatenate(cross[i], axis=0)
        o_ref[:, cl] = _ret_finish(o, g_ref[:, cl]).astype(o_ref.dtype)


def _retention(rows, proj, state_ret, ret_heads, hd):
    nb, t, db, dt = rows.nb, rows.t, rows.db, rows.dt
    lg = jnp.log1p(-jnp.exp2(-5.0 - jnp.arange(ret_heads, dtype=F32)))
    lg_tab = jnp.broadcast_to(lg[:, None, None], (ret_heads, 8, hd))
    assert hd == RET_CHUNK and t % RET_CHUNK == 0 and dt % RET_CHUNK != 0

    cos_p, sin_p = _rope_tables(jnp.arange(t, dtype=jnp.int32), hd)
    nh = RET_HEADS_PER_STEP if ret_heads % RET_HEADS_PER_STEP == 0 else 1
    hsteps = ret_heads // nh
    col = lambda part: pl.BlockSpec((t, nh * hd), lambda b, h: (b, h + part * hsteps))
    tab = pl.BlockSpec((t, hd), lambda b, h: (0, 0))
    o_p, s_p = pl.pallas_call(
        functools.partial(_ret_prompt_kernel, chunk=RET_CHUNK, hd=hd),
        grid=(nb, hsteps),
        in_specs=[col(0), col(1), col(2), col(3), tab, tab, pl.BlockSpec((nh, 8, hd), lambda b, h: (h, 0, 0))],
        out_specs=[pl.BlockSpec((t, nh * hd), lambda b, h: (b, h)),
                   pl.BlockSpec((1, nh, hd, hd), lambda b, h: (b, h, 0, 0))],
        out_shape=[jax.ShapeDtypeStruct((nb * t, ret_heads * hd), BF16),
                   jax.ShapeDtypeStruct((nb, ret_heads, hd, hd), F32)],
        compiler_params=_params(("parallel", "parallel")),
        name="ret_prompt",
    )(proj, proj, proj, proj, cos_p, sin_p, lg_tab)

    rb = RET_CHUNK
    nseq = rb // dt
    cos_s, sin_s = _rope_tables(PAST_LEN + jnp.arange(dt, dtype=jnp.int32), hd)
    cos_s, sin_s = jnp.tile(cos_s, (nseq, 1)), jnp.tile(sin_s, (nseq, 1))
    off = (nb * t) // rb
    col = lambda part: pl.BlockSpec((rb, nh * hd), lambda b, h: (b + off, h + part * hsteps))
    tab = pl.BlockSpec((rb, hd), lambda b, h: (0, 0))
    st = pl.BlockSpec((nseq, nh, hd, hd), lambda b, h: (b, h, 0, 0))
    o_s, s_s = pl.pallas_call(
        functools.partial(_ret_sample_kernel, dt=dt, hd=hd),
        grid=(db // nseq, hsteps),
        in_specs=[col(0), col(1), col(2), col(3), tab, tab, pl.BlockSpec((nh, 8, hd), lambda b, h: (h, 0, 0)), st],
        out_specs=[pl.BlockSpec((rb, nh * hd), lambda b, h: (b, h)), st],
        out_shape=[jax.ShapeDtypeStruct((db * dt, ret_heads * hd), BF16),
                   jax.ShapeDtypeStruct((db, ret_heads, hd, hd), F32)],
        compiler_params=_params(("parallel", "parallel")),
        name="ret_sample",
    )(proj, proj, proj, proj, cos_s, sin_s, lg_tab, state_ret)
    return o_p, o_s, s_p, s_s


def _shifted(x, first, tseq_mask):
    prev = pltpu.roll(x, 1, 0)
    return jnp.where(tseq_mask, first, prev)


def _rwkv_prep_kernel(pr_ref, pk_ref, pv_ref, pt_ref, sr_ref, sk_ref, sv_ref, st_ref,
                      mur_ref, muk_ref, muv_ref, mut_ref, w0_ref, w2h_ref, w2l_ref, a0_ref, a2h_ref, a2l_ref,
                      g2h_ref, g2l_ref,
                      kk_ref, ka_ref, rk_ref,
                      r_out, w_out, k_out, v_out, a_out, b_out, g_out, bonus_out, carry_ref, tcarry_ref,
                      *, dt, prompt_blocks, blocks_per_seq, w_lora, a_lora, g_lora, hd):
    i, j = pl.program_id(0), pl.program_id(1)
    rows, cn = pr_ref.shape
    rown = lax.broadcasted_iota(jnp.int32, (rows, 1), 0)
    sample_block = jnp.zeros((rows, 1), jnp.int32) + (i >= prompt_blocks).astype(jnp.int32)
    is_first = (rown == 0) | ((rown % dt == 0) & (sample_block == 1))
    seq_start = i % blocks_per_seq == 0

    @pl.when(i == 0)
    def _():
        for part in range(3):
            carry_ref[part, j] = jnp.zeros(carry_ref.shape[2:], F32)
        tcarry_ref[...] = jnp.zeros_like(tcarry_ref)

    def lerp(p_ref, s_ref, mu_ref, carried):
        p = p_ref[...]
        from_prompt = jnp.broadcast_to(jnp.where(seq_start, 0.0, carried), p.shape)
        first = jnp.where(sample_block == 1, _expand_slots(s_ref[...], rows), from_prompt)
        prev = _shifted(p, first, is_first)
        return p + mu_ref[...] * (prev - p)

    r = lerp(pr_ref, sr_ref, mur_ref, carry_ref[0, j, 0:1, :])
    kw = lerp(pk_ref, sk_ref, muk_ref, carry_ref[1, j, 0:1, :])
    vw = lerp(pv_ref, sv_ref, muv_ref, carry_ref[2, j, 0:1, :])
    tail = lerp(pt_ref, st_ref, mut_ref, tcarry_ref[0:1, :])
    for part, p_ref in enumerate((pr_ref, pk_ref, pv_ref)):
        carry_ref[part, j, 0:1, :] = p_ref[rows - 1:rows, :]

    @pl.when(j == pl.num_programs(1) - 1)
    def _():
        tcarry_ref[0:1, :] = pt_ref[rows - 1:rows, :]

    tw = tail.shape[1]
    wl = tail[:, :w_lora]
    al = tail[:, w_lora:w_lora + a_lora]
    glane = lax.broadcasted_iota(jnp.int32, (rows, tw - w_lora - a_lora), 1)
    gl = jnp.where(glane < g_lora, jax.nn.sigmoid(tail[:, w_lora + a_lora:]), 0.0)

    w_log = -jax.nn.softplus(-(w0_ref[...] + _dot_split(jnp.tanh(wl), (w2h_ref[...], w2l_ref[...])))) - 0.5
    decay = jnp.exp(-jnp.exp(w_log))
    a = jax.nn.sigmoid(a0_ref[...] + _dot_split(al, (a2h_ref[...], a2l_ref[...])))
    g = _dot_split(gl, (g2h_ref[...], g2l_ref[...]))

    seg = _block_ones(LANES, hd, BF16)

    def segsum(x):
        return jnp.concatenate([_dot_exact01(x[:, c:c + LANES], seg) for c in range(0, cn, LANES)], axis=1)

    kk = kw * kk_ref[...]
    kk = kk / jnp.maximum(jnp.sqrt(segsum(kk * kk)), 1e-12)
    kn = kw * (1.0 + (a - 1.0) * ka_ref[...])
    r_out[...] = r
    w_out[...] = decay
    k_out[...] = kn
    v_out[...] = vw
    a_out[...] = -kk
    b_out[...] = kk * a
    g_out[...] = g
    bonus_out[...] = segsum(r * kn * rk_ref[...]) * vw


def _hi_lo(w):
    hi = w.astype(BF16)
    return hi, (w - hi.astype(F32)).astype(BF16)


def _rwkv_prep(rows, proj, s_shift, mu, w0, w2, a0, a2, g2p, k_k, k_a, r_k_flat, rw, hd, g_lora, cn=512):
    n = rows.n
    cn = _pick(rw, cn)
    ncb = rw // cn
    tw = g2p.shape[0] + w2.shape[0] + a2.shape[0]
    assert (3 * rw) % tw == 0
    tb = (3 * rw) // tw
    sblk = lambda i: jnp.maximum(i - rows.prompt_blocks, 0)
    part = lambda p: pl.BlockSpec((rows.rb, cn), lambda i, j: (i, j + p * ncb))
    spart = lambda p: pl.BlockSpec((rows.slots, 1, cn), lambda i, j: (sblk(i), 0, j + p * ncb))
    mupart = lambda p: pl.BlockSpec((1, cn), lambda i, j: (0, j + p * ncb))
    vec = pl.BlockSpec((1, cn), lambda i, j: (0, j))
    lora = lambda k: pl.BlockSpec((k, cn), lambda i, j: (0, j))
    out = pl.BlockSpec((rows.rb, cn), lambda i, j: (i, j))
    kern = functools.partial(_rwkv_prep_kernel, dt=rows.dt, prompt_blocks=rows.prompt_blocks,
                             blocks_per_seq=rows.blocks_per_seq,
                             w_lora=w2.shape[0], a_lora=a2.shape[0], g_lora=g_lora, hd=hd)
    shift3 = s_shift[:, None, :]
    mu2 = mu.reshape(1, -1)
    return pl.pallas_call(
        kern,
        grid=(rows.n_blocks, ncb),
        in_specs=[part(0), part(1), part(2), pl.BlockSpec((rows.rb, tw), lambda i, j: (i, tb)),
                  spart(0), spart(1), spart(2), pl.BlockSpec((rows.slots, 1, tw), lambda i, j: (sblk(i), 0, tb)),
                  mupart(0), mupart(1), mupart(2), pl.BlockSpec((1, tw), lambda i, j: (0, tb)),
                  vec, lora(w2.shape[0]), lora(w2.shape[0]), vec, lora(a2.shape[0]), lora(a2.shape[0]),
                  lora(g2p.shape[0]), lora(g2p.shape[0]), vec, vec, vec],
        out_specs=[out] * 8,
        out_shape=[jax.ShapeDtypeStruct((n, rw), F32)] * 8,
        scratch_shapes=[pltpu.VMEM((3, ncb, 8, cn), F32), pltpu.VMEM((8, tw), F32)],
        compiler_params=_params(("arbitrary", "arbitrary")),
        name="rwkv_prep",
    )(proj, proj, proj, proj, shift3, shift3, shift3, shift3, mu2, mu2, mu2, mu2,
      w0, *_hi_lo(w2), a0, *_hi_lo(a2), *_hi_lo(g2p), k_k, k_a, r_k_flat)


def _rwkv_scan_kernel(r_ref, w_ref, k_ref, v_ref, a_ref, b_ref, o_ref, sout_ref, st_ref, ob_ref, vs_ref, row_ref,
                      *, hd):
    tb = pl.program_id(1)
    tc, width = r_ref.shape
    pw = 2 * hd
    npairs = width // pw

    @pl.when(tb == 0)
    def _():
        st_ref[...] = jnp.zeros_like(st_ref)

    half = npairs // 2
    ones2 = _block_ones(2 * pw, hd, BF16)
    sub = lax.broadcasted_iota(jnp.int32, (hd, pw), 0)
    lane = lax.broadcasted_iota(jnp.int32, (hd, pw), 1)
    diags = [(((lane % hd) - sub + hd) % hd == i).astype(F32).astype(BF16) for i in range(3)]
    lane8 = lax.broadcasted_iota(jnp.int32, (8, pw), 1)
    row16 = lax.broadcasted_iota(jnp.int32, (8, 2 * pw), 0)
    lane16 = lax.broadcasted_iota(jnp.int32, (8, 2 * pw), 1)
    sel2 = row16 == (lane16 // hd)

    def shift_in_head(x, i):
        return jnp.where(lane8 % hd >= i, pltpu.roll(x, i, 1), pltpu.roll(x, pw - hd + i, 1))

    def group(tg, carry):
        rows8 = pl.ds(pl.multiple_of(tg * 8, 8), 8)
        for idx, ref in enumerate((a_ref, w_ref, k_ref, b_ref, r_ref)):
            row_ref[idx] = ref[rows8, :]
        for p in range(npairs):
            cols = pl.ds(p * pw, pw)
            v8 = v_ref[rows8, cols]
            v1 = v8.astype(BF16).astype(F32)
            v2 = (v8 - v1).astype(BF16).astype(F32)
            vs_ref[0, :, cols] = v1
            vs_ref[1, :, cols] = shift_in_head(v2, 1)
            vs_ref[2, :, cols] = shift_in_head((v8 - v1) - v2, 2)

        def row(idx, p, j):
            return row_ref[idx, j:j + 1, pl.ds(p * pw, pw)]

        def side_by_side(tiles):
            return jnp.concatenate([jnp.concatenate([tiles[p], tiles[p + half]], axis=1) for p in range(half)],
                                   axis=0)

        def split(x, p):
            q, c = p % half, p // half
            return x[q * hd:(q + 1) * hd, c * pw:(c + 1) * pw]

        def moved_v(j):
            xs = []
            for p in range(npairs):
                x = None
                for i in range(3):
                    term = diags[i] * vs_ref[i, j:j + 1, pl.ds(p * pw, pw)].astype(BF16)
                    x = term if x is None else x + term
                xs.append(x)
            return _dot(side_by_side(xs), ones2)

        def emit_out(j, states):
            for p in range(half):
                r2 = jnp.concatenate([row(4, p, j), row(4, p + half, j)], axis=1)
                rsel = jnp.where(sel2, r2, 0.0).astype(BF16)
                o8 = _dot_nt(rsel, jnp.concatenate([states[p], states[p + half]], axis=1))
                ob_ref[p, j:j + 1, 0:hd] = o8[0:1, :]
                ob_ref[p, j:j + 1, hd:pw] = o8[1:2, :]
                ob_ref[p + half, j:j + 1, 0:hd] = o8[2:3, :]
                ob_ref[p + half, j:j + 1, hd:pw] = o8[3:4, :]

        vc_next = moved_v(0)
        states = None
        for j in range(8):
            old = [st_ref[p] for p in range(npairs)]
            sa = _dot(side_by_side([old[p] * row(0, p, j) for p in range(npairs)]).astype(BF16), ones2)
            if states is not None:
                emit_out(j - 1, states)
            vc = vc_next
            if j < 7:
                vc_next = moved_v(j + 1)
            states = []
            for p in range(npairs):
                s = old[p] * row(1, p, j) + split(sa, p) * row(3, p, j) + split(vc, p) * row(2, p, j)
                st_ref[p] = s
                states.append(s.astype(BF16))
        emit_out(7, states)
        for p in range(npairs):
            o_ref[rows8, pl.ds(p * pw, pw)] = ob_ref[p]
        return carry

    lax.fori_loop(0, tc // 8, group, 0)

    @pl.when(tb == pl.num_programs(1) - 1)
    def _():
        for p in range(npairs):
            s = st_ref[p]
            sout_ref[0, 2 * p] = s[:, :hd]
            sout_ref[0, 2 * p + 1] = s[:, hd:]


def _rwkv_scan(ops, nseq, t, row0, tc, heads, hd):
    width = heads * hd
    assert t % tc == 0 and row0 % tc == 0
    nt = t // tc
    blk = pl.BlockSpec((tc, width), lambda s, j: (row0 // tc + s * nt + j, 0))
    st = pl.BlockSpec((1, heads, hd, hd), lambda s, j: (s, 0, 0, 0))
    return pl.pallas_call(
        functools.partial(_rwkv_scan_kernel, hd=hd),
        grid=(nseq, nt),
        in_specs=[blk] * 6,
        out_specs=[pl.BlockSpec((tc, width), lambda s, j: (s * nt + j, 0)), st],
        out_shape=[jax.ShapeDtypeStruct((nseq * t, width), F32),
                   jax.ShapeDtypeStruct((nseq, heads, hd, hd), F32)],
        scratch_shapes=[pltpu.VMEM((heads // 2, hd, 2 * hd), F32), pltpu.VMEM((heads // 2, 8, 2 * hd), F32),
                        pltpu.VMEM((3, 8, width), F32), pltpu.VMEM((5, 8, width), F32)],
        compiler_params=_params(("parallel", "arbitrary")),
        name="rwkv_scan",
    )(*ops)


def _rwkv_lanes_kernel(r_ref, w_ref, k_ref, v_ref, a_ref, b_ref, s0_ref, o_ref, sout_ref, x_ref, ot_ref, *, dt, hd):
    nseq = s0_ref.shape[-1]
    nh = s0_ref.shape[0]
    for idx, ref in enumerate((a_ref, w_ref, k_ref, b_ref, r_ref, v_ref)):
        for t in range(dt):
            x_ref[idx, t] = ref[pl.ds(t, nseq, stride=dt), :].T

    for hh in range(nh):
        ch = slice(hh * hd, (hh + 1) * hd)

        def vgroup(vg, carry):
            v0 = pl.multiple_of(hh * hd + vg * 8, 8)
            v8 = [x_ref[5, t, pl.ds(v0, 8), :] for t in range(dt)]
            outs = [[None] * 8 for _ in range(dt)]
            for j in range(8):
                s = s0_ref[hh, vg * 8 + j]
                for t in range(dt):
                    sa = jnp.sum(s * x_ref[0, t, ch, :], axis=0, keepdims=True)
                    s = s * x_ref[1, t, ch, :] + sa * x_ref[3, t, ch, :] + v8[t][j:j + 1] * x_ref[2, t, ch, :]
                    outs[t][j] = jnp.sum(s * x_ref[4, t, ch, :], axis=0, keepdims=True)
                sout_ref[hh, vg * 8 + j] = s
            for t in range(dt):
                ot_ref[t, pl.ds(v0, 8), :] = jnp.concatenate(outs[t], axis=0)
            return carry

        lax.fori_loop(0, hd // 8, vgroup, 0)

    for t in range(dt):
        o_ref[pl.ds(t, nseq, stride=dt), :] = ot_ref[t].T


def _rwkv_scan_lanes(ops, nseq, dt, row0, heads, hd, state_lanes):
    width = heads * hd
    rows = nseq * dt
    nh = LANES // hd
    assert row0 % rows == 0 and heads % nh == 0
    blk = pl.BlockSpec((rows, nh * hd), lambda g: (row0 // rows, g))
    st = pl.BlockSpec((nh, hd, hd, nseq), lambda g: (g, 0, 0, 0))
    return pl.pallas_call(
        functools.partial(_rwkv_lanes_kernel, dt=dt, hd=hd),
        grid=(heads // nh,),
        in_specs=[blk] * 6 + [st],
        out_specs=[pl.BlockSpec((rows, nh * hd), lambda g: (0, g)), st],
        out_shape=[jax.ShapeDtypeStruct((rows, width), F32),
                   jax.ShapeDtypeStruct((heads, hd, hd, nseq), F32)],
        scratch_shapes=[pltpu.VMEM((6, dt, nh * hd, nseq), F32), pltpu.VMEM((dt, nh * hd, nseq), F32)],
        compiler_params=_params(("parallel",)),
        name="rwkv_scan_lanes",
    )(*ops, state_lanes)


def _rwkv_post_kernel(op_ref, os_ref, bonus_ref, g_ref, lg_ref, lb_ref, out_ref, *, hd, first_blocks):
    cn = out_ref.shape[1]
    seg = _block_ones(LANES, hd, BF16)

    def segmean(x):
        return jnp.concatenate([_dot_exact01(x[:, c:c + LANES], seg) for c in range(0, cn, LANES)],
                               axis=1) * (1.0 / hd)

    def body(o):
        mean = segmean(o)
        cen = o - mean
        var = segmean(cen * cen)
        y = cen * lax.rsqrt(var + LNX_EPS) * lg_ref[...] + lb_ref[...]
        out_ref[...] = ((y + bonus_ref[...]) * g_ref[...]).astype(out_ref.dtype)

    _from_either(pl.program_id(0), first_blocks, op_ref, os_ref, body)


def _rwkv_post(o_raw_p, o_raw_s, bonus, g, lnx_g, lnx_b, hd, rb=256, cn=512):
    n, rw = bonus.shape
    rb, cn = _pick(math.gcd(o_raw_p.shape[0], o_raw_s.shape[0]), rb), _pick(rw, cn)
    first_blocks = o_raw_p.shape[0] // rb
    blk = pl.BlockSpec((rb, cn), lambda i, j: (i, j))
    vec = pl.BlockSpec((1, cn), lambda i, j: (0, j))
    pair = [pl.BlockSpec((rb, cn), lambda i, j: (jnp.minimum(i, first_blocks - 1), j)),
            pl.BlockSpec((rb, cn), lambda i, j: (jnp.maximum(i - first_blocks, 0), j))]
    return pl.pallas_call(
        functools.partial(_rwkv_post_kernel, hd=hd, first_blocks=first_blocks),
        grid=(n // rb, rw // cn),
        in_specs=pair + [blk, blk, vec, vec],
        out_specs=blk,
        out_shape=jax.ShapeDtypeStruct((n, rw), BF16),
        compiler_params=_params(("parallel", "parallel")),
        name="rwkv_post",
    )(o_raw_p, o_raw_s, bonus, g, lnx_g, lnx_b)


def _mmt_kernel(wt_ref, x_ref, o_ref):
    o_ref[...] = _dot(wt_ref[...].astype(BF16), x_ref[...])


def _matmul_t(wt, xt, tm=512, tn=1024, name="matmul_t"):
    m, k = wt.shape
    n = xt.shape[1]
    tm, tn = _pick(m, tm), _pick(n, tn)
    return pl.pallas_call(
        _mmt_kernel,
        grid=(n // tn, m // tm),
        in_specs=[pl.BlockSpec((tm, k), lambda j, i: (i, 0)),
                  pl.BlockSpec((k, tn), lambda j, i: (0, j))],
        out_specs=pl.BlockSpec((tm, tn), lambda j, i: (i, j)),
        out_shape=jax.ShapeDtypeStruct((m, n), F32),
        compiler_params=_params(("parallel", "parallel")),
        name=name,
    )(wt, xt)


def _take_top(s, k):
    n = s.shape[0]
    row = lax.broadcasted_iota(jnp.int32, s.shape, 0).astype(F32)
    out = []
    for _ in range(k):
        m = jnp.max(s, axis=0, keepdims=True)
        first = jnp.min(jnp.where(s == m, row, float(n)), axis=0, keepdims=True)
        s = jnp.where(row == first, -jnp.inf, s)
        out.append(m)
    return out


def _peer_topk_kernel(q_ref, keys_ref, s0_ref, s1_ref, e0_ref, e1_ref, thr_ref, *, topk):
    kd = keys_ref.shape[3]
    cols = q_ref.shape[1]
    s0 = _dot(keys_ref[0, 0], q_ref[:kd, :], HI)
    s1 = _dot(keys_ref[0, 1], q_ref[kd:, :], HI)
    top0 = _take_top(s0, topk)
    top1 = _take_top(s1, topk)
    width = [topk // (a + 1) for a in range(topk)]
    start = [sum(width[:a]) for a in range(topk)]
    n_cand = sum(width)
    rows = -(-n_cand // 8) * 8
    r = lax.broadcasted_iota(jnp.int32, (rows, cols), 0)
    t0 = jnp.broadcast_to(top0[0], (rows, cols))
    first = jnp.zeros((rows, cols), jnp.int32)
    for a in range(1, topk):
        t0 = jnp.where(r >= start[a], top0[a], t0)
        first = jnp.where(r >= start[a], start[a], first)
    b_of_row = r - first
    t1 = jnp.broadcast_to(top1[0], (rows, cols))
    for b in range(1, topk):
        t1 = jnp.where(b_of_row == b, top1[b], t1)
    cand = jnp.where(r < n_cand, t0 + t1, -jnp.inf)
    best = _take_top(cand, topk)
    z = jnp.zeros((1, cols), F32)
    for b in best:
        z = z + jnp.exp(b - best[0])
    s0_ref[...] = s0
    s1_ref[...] = s1
    e0_ref[...] = jnp.exp(s0 - top0[0]) / z
    e1_ref[...] = jnp.exp(s1 - top1[0])
    thr_ref[0] = jnp.broadcast_to(best[topk - 1], thr_ref.shape[1:])


def _peer_topk(qt, sub_keys, tc=256):
    n = qt.shape[1]
    tc = _pick(n, tc)
    heads, _, nk, kd = sub_keys.shape
    blk = pl.BlockSpec((nk, tc), lambda i, h: (h, i))
    shp = jax.ShapeDtypeStruct((heads * nk, n), F32)
    return pl.pallas_call(
        functools.partial(_peer_topk_kernel, topk=PEER_TOPK),
        grid=(n // tc, heads),
        in_specs=[pl.BlockSpec((2 * kd, tc), lambda i, h: (h, i)),
                  pl.BlockSpec((1, 2, nk, kd), lambda i, h: (h, 0, 0, 0))],
        out_specs=[blk] * 4 + [pl.BlockSpec((1, 8, tc), lambda i, h: (h, 0, i))],
        out_shape=[shp] * 4 + [jax.ShapeDtypeStruct((heads, 8, n), F32)],
        compiler_params=_params(("parallel", "parallel")),
        name="peer_topk",
    )(qt, sub_keys)


def _gelu(x):
    return 0.5 * x * (1.0 + lax.erf(x * (2.0 ** -0.5)))


def _peer_dense_kernel(u_ref, ht_ref, vt_ref, s0_ref, e0_ref, s1_ref, e1_ref, thr_ref, o_ref, act_ref, coef_ref,
                       *, heads, nk, jc):
    e = pl.program_id(1)
    eb = u_ref.shape[0]
    d = vt_ref.shape[0]
    ni = eb // nk
    dq = d // ni

    last = pl.num_programs(1) - 1
    cur = e % 2
    prev = (e + 1) % 2

    def run(score, finish):
        if score:
            act_ref[...] = _dot(u_ref[...], ht_ref[...])
            thr = thr_ref[...]

        def one_i(ii, carry):
            if finish:
                orow = pl.ds(pl.multiple_of(ii * dq, dq), dq)
                o_ref[orow, :] += _dot(vt_ref[orow, :], coef_ref[prev])
            if not score:
                return carry
            i_abs = e * ni + ii
            grp = pl.multiple_of((i_abs // 8) * 8, 8)
            keep = lax.broadcasted_iota(jnp.int32, (8, s0_ref.shape[1]), 0) == i_abs % 8

            def pick(ref):
                return jnp.concatenate(
                    [jnp.sum(jnp.where(keep, ref[pl.ds(h * nk + grp, 8), :], 0.0), axis=0, keepdims=True)
                     for h in range(heads)], axis=0)

            s0 = pick(s0_ref)
            e0 = pick(e0_ref)
            for j0 in range(0, nk, jc):
                g = None
                for h in range(heads):
                    jrows = slice(h * nk + j0, h * nk + j0 + jc)
                    hit = (s0[h:h + 1] + s1_ref[jrows, :]) >= thr[h:h + 1]
                    term = jnp.where(hit, e0[h:h + 1] * e1_ref[jrows, :], 0.0)
                    g = term if g is None else g + term
                arow = pl.ds(pl.multiple_of(ii * nk + j0, jc), jc)
                coef_ref[cur, arow, :] = (g * _gelu(act_ref[arow, :])).astype(coef_ref.dtype)
            return carry

        lax.fori_loop(0, ni, one_i, 0)

    @pl.when(e == 0)
    def _():
        o_ref[...] = jnp.zeros_like(o_ref)
        run(True, False)

    @pl.when((e > 0) & (e < last))
    def _():
        run(True, True)

    @pl.when(e == last)
    def _():
        run(False, True)


def _peer_dense(ht, u, vt, s0, e0, s1, e1, thr, heads, nk, tp=512, eb=512):
    d, n = ht.shape
    tp = _pick(n, tp)
    nblk = u.shape[0] // eb
    once = pl.Buffered(1)
    tok = pl.BlockSpec((heads * nk, tp), lambda i, e: (0, i), pipeline_mode=once)
    return pl.pallas_call(
        functools.partial(_peer_dense_kernel, heads=heads, nk=nk, jc=PEER_MASK_ROWS),
        grid=(n // tp, nblk + 1),
        in_specs=[pl.BlockSpec((eb, d), lambda i, e: (jnp.minimum(e, nblk - 1), 0)),
                  pl.BlockSpec((d, tp), lambda i, e: (0, i), pipeline_mode=once),
                  pl.BlockSpec((d, eb), lambda i, e: (0, jnp.maximum(e - 1, 0))),
                  tok, tok, tok, tok, pl.BlockSpec((heads, tp), lambda i, e: (0, i))],
        out_specs=pl.BlockSpec((d, tp), lambda i, e: (0, i)),
        out_shape=jax.ShapeDtypeStruct((d, n), F32),
        scratch_shapes=[pltpu.VMEM((eb, tp), F32), pltpu.VMEM((2, eb, tp), BF16)],
        compiler_params=_params(("parallel", "arbitrary")),
        name="peer_dense",
    )(u, ht, vt, s0, e0, s1, e1, thr)


def _layer(xp, xs, s_ret, s_rwkv, s_shift, cp, cs, p):
    nb, t, d = xp.shape
    db, dt, _ = xs.shape
    rows = _Rows(nb, t, db, dt)
    ret_heads, ret_hd = s_ret.shape[1], s_ret.shape[2]
    rw_heads, rw_hd = s_rwkv.shape[1], s_rwkv.shape[2]
    ret_w, rw = ret_heads * ret_hd, rw_heads * rw_hd
    w_lora, a_lora, g_lora = p['w2'].shape[0], p['a2'].shape[0], p['g2'].shape[0]
    rwkv_in = s_shift.shape[-1]
    ret_in = 4 * ret_w
    assert p['w_in'].shape[1] == ret_in + rwkv_in and rwkv_in == 3 * rw + w_lora + a_lora + g_lora

    xp, xs = xp.reshape(nb * t, d), xs.reshape(db * dt, d)
    mod = _ada(jnp.concatenate([cp, cs], axis=0), p['ada_w'], p['ada_b'])
    mod_ext = rows.extend(mod)

    h = _prenorm(rows, xp, xs, p['pre_mix_g'], mod_ext, d)

    tm = _pick(rows.n, 1024)
    w_in_t = p['w_in'].T
    proj_ret = _matmul(h, w_in_t, col_block0=0, n_out=ret_in, tm=tm, tn=512, name="in_proj_ret")
    proj_rw = _matmul(h, w_in_t, col_block0=ret_in // 512, n_out=rwkv_in, tm=tm, tn=512, name="in_proj_rwkv")

    o_ret_p, o_ret_s, sr_p, sr_s = _retention(rows, proj_ret, s_ret, ret_heads, ret_hd)

    tail_w = -(-(w_lora + a_lora + g_lora) // LANES) * LANES
    g2p = jnp.pad(p['g2'], ((0, tail_w - w_lora - a_lora - g_lora), (0, 0)))
    vec = lambda a: a.reshape(1, rw)
    r_, w_, k_, v_, a_, b_, g_, bonus = _rwkv_prep(
        rows, proj_rw, s_shift, p['shift_mu'], vec(p['w0']), p['w2'], vec(p['a0']), p['a2'], g2p,
        vec(p['k_k']), vec(p['k_a']), vec(p['r_k']), rw, rw_hd, g_lora)
    ops = (r_, w_, k_, v_, a_, b_)
    o_rw_p, sw_p = _rwkv_scan(ops, nb, t, 0, 256, rw_heads, rw_hd)
    o_rw_s, sw_lanes = _rwkv_scan_lanes(ops, db, dt, nb * t, rw_heads, rw_hd, jnp.transpose(s_rwkv, (1, 2, 3, 0)))
    sw_s = jnp.transpose(sw_lanes, (3, 0, 1, 2))
    o_rw = _rwkv_post(o_rw_p, o_rw_s, bonus, g_, vec(p['lnx_g']), vec(p['lnx_b']), rw_hd)

    mix = _matmul_out(o_ret_p, o_ret_s, o_rw, p['w_out'])
    x1, h2t = _postmix(rows, xp, xs, mix, p['post_mix_g'], p['pre_ffn_g'], mod_ext, d)

    heads, _, nk, _ = p['peer_sub_keys'].shape
    qt = _matmul_t(p['peer_wq'].T, h2t, name="peer_q")
    s0, s1, e0, e1, thr = _peer_topk(qt, p['peer_sub_keys'])
    peer_t = _peer_dense(h2t, p['peer_u'].astype(BF16), p['peer_v'].T.astype(BF16),
                         s0, e0, s1, e1, thr[:, 0, :], heads, nk)
    yp = _final(rows, x1, peer_t, p['post_ffn_g'], mod_ext, d, 0, rows.prompt_blocks)
    ys = _final(rows, x1, peer_t, p['post_ffn_g'], mod_ext, d, rows.prompt_blocks, rows.n_blocks - rows.prompt_blocks)

    n_p = nb * t
    last_p = slice(t - 1, n_p, t)
    last_s = slice(n_p + dt - 1, None, dt)
    return (yp.reshape(nb, t, d), ys.reshape(db, dt, d), sr_p, sr_s, sw_p, sw_s, proj_rw[last_p], proj_rw[last_s])


def kernel(x_prompt, x_sample, state_ret, state_rwkv, state_shift, c_prompt, c_sample, ada_w, ada_b, pre_mix_g, post_mix_g, pre_ffn_g, post_ffn_g, w_in, shift_mu, w0, w2, a0, a2, g2, k_k, k_a, r_k, lnx_g, lnx_b, w_out, peer_wq, peer_sub_keys, peer_u, peer_v):
    depth = ada_w.shape[0]
    assert depth == 1, "prompt and sample tokens are stacked per layer; deeper stacks need per-layer restacking"
    prm = dict(ada_w=ada_w[0], ada_b=ada_b[0], pre_mix_g=pre_mix_g[0], post_mix_g=post_mix_g[0],
               pre_ffn_g=pre_ffn_g[0], post_ffn_g=post_ffn_g[0], w_in=w_in[0], shift_mu=shift_mu[0],
               w0=w0[0], w2=w2[0], a0=a0[0], a2=a2[0], g2=g2[0], k_k=k_k[0], k_a=k_a[0], r_k=r_k[0],
               lnx_g=lnx_g[0], lnx_b=lnx_b[0], w_out=w_out[0], peer_wq=peer_wq[0],
               peer_sub_keys=peer_sub_keys[0], peer_u=peer_u[0], peer_v=peer_v[0])
    yp, ys, sr_p, sr_s, sw_p, sw_s, ss_p, ss_s = _layer(
        x_prompt, x_sample, state_ret[0], state_rwkv[0], state_shift[0], c_prompt, c_sample, prm)
    sd, wd, hd = state_ret.dtype, state_rwkv.dtype, state_shift.dtype
    return (yp, ys, sr_p[None].astype(sd), sr_s[None].astype(sd), sw_p[None].astype(wd), sw_s[None].astype(wd),
            ss_p[None].astype(hd), ss_s[None].astype(hd))
```
